```python
import jax, jax.numpy as jnp
from jax import lax
import numpy as np

D_MODEL = 1024
BATCH = 16
SEQ = 256
DEPTH = 2
DEC_BATCH = 4
DEC_SEQ = 1024
PAST_LEN = 512

F32 = jnp.float32
GRID_W = 64
N_MIXERS = 4
W_GRP = D_MODEL // N_MIXERS
N_HEADS = 4
HEAD_V = W_GRP // N_HEADS
HGRN_DK = HEAD_V
GLA_DK = HEAD_V // 2
GLA_RANK = 16
GLA_NORMALIZER = 16.0
RG_BLOCKS = N_HEADS
RG_BLOCK = W_GRP // RG_BLOCKS
RG_CONV = 4
RG_C = 8.0
SCONV_W = 3
D_FF = 4 * D_MODEL
CHUNK = 32
EPS = 1e-6
F_FLOOR = 1e-20
SPLIT_SIZES = (
    N_HEADS * HGRN_DK, N_HEADS * HEAD_V, N_HEADS * HGRN_DK, N_HEADS * HGRN_DK, W_GRP,
    N_HEADS * GLA_DK, N_HEADS * GLA_DK, W_GRP, W_GRP, GLA_RANK, GLA_RANK,
    W_GRP, W_GRP,
    W_GRP, W_GRP, W_GRP,
)
PROJ_W = sum(SPLIT_SIZES)

kernel_name = 'hybrid_parallel_heads_diffusion_step'


def rmsnorm(x, g):
    xf = x.astype(F32)
    y = xf * lax.rsqrt(jnp.mean(xf * xf, axis=-1, keepdims=True) + EPS)
    return (y * g.astype(F32)).astype(x.dtype)


def to_heads(t):
    b, l, _ = t.shape
    return t.reshape(b, l, N_HEADS, -1).transpose(0, 2, 1, 3)


def head_norm_gate(o, g, gain):
    b, h, l, dv = o.shape
    o = o.transpose(0, 2, 1, 3)
    o = o * lax.rsqrt(jnp.mean(o * o, axis=-1, keepdims=True) + EPS)
    o = o.reshape(b, l, h * dv) * gain.astype(F32)
    return (o * jax.nn.silu(g.astype(F32))).astype(g.dtype)


def chunk_gla(q, k, v, log_a, s0):
    b, h, l, dk = q.shape
    dv = v.shape[-1]
    n = l // CHUNK
    q = q.reshape(b, h, n, CHUNK, dk)
    k = k.reshape(b, h, n, CHUNK, dk)
    v = v.reshape(b, h, n, CHUNK, dv)
    cum = jnp.cumsum(log_a.reshape(b, h, n, CHUNK, dk), axis=3)
    last = cum[:, :, :, -1:, :]
    causal = jnp.tril(jnp.ones((CHUNK, CHUNK), dtype=bool))[:, :, None]
    diff = cum[:, :, :, :, None, :] - cum[:, :, :, None, :, :]
    decay = jnp.where(causal, jnp.exp(jnp.minimum(diff, 0.0)), 0.0)
    scores = jnp.einsum('bhntd,bhnsd,bhntsd->bhnts', q, k, decay)
    o_intra = jnp.einsum('bhnts,bhnse->bhnte', scores, v)
    ds = jnp.einsum('bhncd,bhnce->bhnde', k * jnp.exp(last - cum), v)
    a_chunk = jnp.exp(last[:, :, :, 0, :])

    def step(s, inp):
        a_n, ds_n = inp
        return a_n[..., None] * s + ds_n, s

    s_final, s_in = lax.scan(step, s0, (jnp.moveaxis(a_chunk, 2, 0), jnp.moveaxis(ds, 2, 0)))
    s_in = jnp.moveaxis(s_in, 0, 2)
    o_inter = jnp.einsum('bhntd,bhnde->bhnte', q * jnp.exp(cum), s_in)
    return (o_intra + o_inter).reshape(b, h, l, dv), s_final


def bidirectional_gla(q, v, ks, log_as, s0):
    outs, finals = [], []
    for d in range(2):
        qd, kd, vd, ad = q, ks[d], v, log_as[d]
        if d == 1:
            qd, kd, vd, ad = (jnp.flip(t, axis=2) for t in (qd, kd, vd, ad))
        o, sf = chunk_gla(qd, kd, vd, ad, s0[:, d].astype(F32))
        outs.append(jnp.flip(o, axis=2) if d == 1 else o)
        finals.append(sf)
    return outs[0] + outs[1], jnp.stack(finals, axis=1)


def hgrn2_mixer(q, i, f_fwd, f_bwd, g, lb, norm_g, s0):
    qh = to_heads(q.astype(F32))
    vh = to_heads(i.astype(F32))
    lb = lb.astype(F32)
    ks, las = [], []
    for d, fl in enumerate((f_fwd, f_bwd)):
        f = lb[d] + (1.0 - lb[d]) * jax.nn.sigmoid(fl.astype(F32))
        log_f = jnp.log(jnp.maximum(f, F_FLOOR))
        ks.append(to_heads(1.0 - f))
        las.append(to_heads(log_f))
    o, sf = bidirectional_gla(qh, vh, ks, las, s0)
    return head_norm_gate(o, g, norm_g), sf


def gla_mixer(q, k, v, g, a_fwd, a_bwd, wa2, ba2, norm_g, s0):
    qh = to_heads(q.astype(F32)) * (GLA_DK ** -0.5)
    kh = to_heads(k.astype(F32))
    vh = to_heads(v.astype(F32))
    las = [to_heads(jax.nn.log_sigmoid((a @ wa2[d] + ba2[d]).astype(F32)) / GLA_NORMALIZER)
           for d, a in enumerate((a_fwd, a_bwd))]
    o, sf = bidirectional_gla(qh, vh, [kh, kh], las, s0)
    return head_norm_gate(o, g, norm_g), sf


def causal_conv(u, w, bias):
    l = u.shape[1]
    kw = w.shape[0]
    up = jnp.pad(u, ((0, 0), (kw - 1, 0), (0, 0)))
    out = bias
    for j in range(kw):
        out = out + up[:, j:j + l] * w[j]
    return out


def block_diag(x, w, bias):
    b, l, _ = x.shape
    y = jnp.einsum('blnc,ncd->blnd', x.reshape(b, l, RG_BLOCKS, RG_BLOCK), w.astype(F32))
    return y.reshape(b, l, W_GRP) + bias.astype(F32)


def rglru_scan(x, r, ig, lam, h0):
    log_a = -RG_C * r * jax.nn.softplus(-lam)
    a = jnp.exp(log_a)
    bx = jnp.sqrt(-jnp.expm1(2.0 * log_a)) * (ig * x)
    bx = bx.at[:, 0].add(a[:, 0] * h0)

    def combine(e1, e2):
        a1, b1 = e1
        a2, b2 = e2
        return a1 * a2, a2 * b1 + b2

    _, h = lax.associative_scan(combine, (a, bx), axis=1)
    return h, h[:, -1]


def rglru_mixer(u, gate, conv_w, conv_b, w_r, b_r, w_i, b_i, lam, h0):
    u = u.astype(F32)
    outs, finals = [], []
    for d in range(2):
        ud = jnp.flip(u, axis=1) if d == 1 else u
        xc = causal_conv(ud, conv_w[d].astype(F32), conv_b[d].astype(F32))
        r = jax.nn.sigmoid(block_diag(xc, w_r[d], b_r[d]))
        ig = jax.nn.sigmoid(block_diag(xc, w_i[d], b_i[d]))
        h, hf = rglru_scan(xc, r, ig, lam[d].astype(F32), h0[:, d].astype(F32))
        outs.append(jnp.flip(h, axis=1) if d == 1 else h)
        finals.append(hf)
    y = (outs[0] + outs[1]) * jax.nn.gelu(gate.astype(F32))
    return y.astype(gate.dtype), jnp.stack(finals, axis=1)


def short_conv_mixer(bg, cg, v, w, rows):
    u = cg * v
    b, l, ch = u.shape
    u = u.reshape(b, rows, l // rows, ch)
    pad = SCONV_W // 2
    up = jnp.pad(u, ((0, 0), (0, 0), (pad, pad), (0, 0)))
    seg = l // rows
    y = up[:, :, 0:seg] * w[0]
    for j in range(1, SCONV_W):
        y = y + up[:, :, j:j + seg] * w[j]
    return bg * y.reshape(b, l, ch)


def trunk_layer(x, mod, rows, s_hgrn, s_gla, s_rg, lb, norm1_g, norm2_g, w_in, w_out,
                hgrn_norm_g, gla_wa2, gla_ba2, gla_norm_g, rg_conv_w, rg_conv_b,
                rg_w_r, rg_b_r, rg_w_i, rg_b_i, rg_lambda, sconv_w, mlp_w1, mlp_w2):
    shift1, scale1, gate1, shift2, scale2, gate2 = jnp.split(mod[:, None, :], 6, axis=-1)
    h = rmsnorm(x, norm1_g) * (1 + scale1) + shift1
    proj = h @ w_in
    points = np.cumsum(SPLIT_SIZES)[:-1].tolist()
    (a_q, a_i, a_ff, a_fb, a_g, b_q, b_k, b_v, b_g, b_af, b_ab,
     c_x, c_g, d_b, d_c, d_v) = jnp.split(proj, points, axis=-1)
    o_a, sf_a = hgrn2_mixer(a_q, a_i, a_ff, a_fb, a_g, lb, hgrn_norm_g, s_hgrn)
    o_b, sf_b = gla_mixer(b_q, b_k, b_v, b_g, b_af, b_ab, gla_wa2, gla_ba2, gla_norm_g, s_gla)
    o_c, sf_c = rglru_mixer(c_x, c_g, rg_conv_w, rg_conv_b, rg_w_r, rg_b_r, rg_w_i, rg_b_i,
                            rg_lambda, s_rg)
    o_d = short_conv_mixer(d_b, d_c, d_v, sconv_w, rows)
    mix = jnp.concatenate([o_a, o_b, o_c, o_d], axis=-1) @ w_out
    x = x + gate1 * mix
    h2 = rmsnorm(x, norm2_g) * (1 + scale2) + shift2
    ff = jnp.square(jax.nn.relu(h2 @ mlp_w1)) @ mlp_w2
    x = x + gate2 * ff
    return x, sf_a, sf_b, sf_c


def setup_inputs(seed: int = 0) -> dict:
    key = jax.random.key(seed)
    ks = jax.random.split(key, 32)
    D = D_MODEL

    def nrm(k, shape, s):
        return jax.random.normal(k, shape, F32) * s

    def gain(k, shape):
        return 1.0 + 0.1 * jax.random.normal(k, shape, F32)

    u = jax.random.uniform(ks[29], (DEPTH, 2, W_GRP), F32, 0.9, 0.999)
    p_a = u ** (1.0 / RG_C)
    rg_lambda = jnp.log(p_a) - jnp.log1p(-p_a)
    return {
        'x_prompt': nrm(ks[0], (BATCH, SEQ, D), 1.0),
        'x_sample': nrm(ks[1], (DEC_BATCH, DEC_SEQ, D), 1.0),
        'state_hgrn': nrm(ks[2], (DEC_BATCH, DEPTH, 2, N_HEADS, HGRN_DK, HEAD_V), 0.5),
        'state_gla': nrm(ks[3], (DEC_BATCH, DEPTH, 2, N_HEADS, GLA_DK, HEAD_V), 0.5),
        'state_rglru': nrm(ks[4], (DEC_BATCH, DEPTH, 2, W_GRP), 0.5),
        'c': nrm(ks[5], (DEC_BATCH, D), 1.0),
        'c_ctx': nrm(ks[6], (D,), 1.0),
        'norm1_g': gain(ks[7], (DEPTH, D)),
        'norm2_g': gain(ks[8], (DEPTH, D)),
        'ada_w': nrm(ks[9], (DEPTH, D, 6 * D), 0.3 * D ** -0.5),
        'ada_b': nrm(ks[10], (DEPTH, 6 * D), 0.02),
        'w_in': nrm(ks[11], (DEPTH, D, PROJ_W), D ** -0.5),
        'w_out': nrm(ks[12], (DEPTH, D, D), D ** -0.5),
        'hgrn_lb_logits': nrm(ks[13], (DEPTH, 2, W_GRP), 1.0),
        'hgrn_norm_g': gain(ks[14], (DEPTH, W_GRP)),
        'gla_wa2': nrm(ks[15], (DEPTH, 2, GLA_RANK, N_HEADS * GLA_DK), GLA_RANK ** -0.5),
        'gla_ba2': nrm(ks[16], (DEPTH, 2, N_HEADS * GLA_DK), 0.1),
        'gla_norm_g': gain(ks[17], (DEPTH, W_GRP)),
        'rg_conv_w': nrm(ks[18], (DEPTH, 2, RG_CONV, W_GRP), RG_CONV ** -0.5),
        'rg_conv_b': nrm(ks[19], (DEPTH, 2, W_GRP), 0.02),
        'rg_w_r': nrm(ks[20], (DEPTH, 2, RG_BLOCKS, RG_BLOCK, RG_BLOCK), RG_BLOCK ** -0.5),
        'rg_b_r': nrm(ks[21], (DEPTH, 2, W_GRP), 0.02),
        'rg_w_i': nrm(ks[22], (DEPTH, 2, RG_BLOCKS, RG_BLOCK, RG_BLOCK), RG_BLOCK ** -0.5),
        'rg_b_i': nrm(ks[23], (DEPTH, 2, W_GRP), 0.02),
        'rg_lambda': rg_lambda,
        'sconv_w': nrm(ks[24], (DEPTH, SCONV_W, W_GRP), SCONV_W ** -0.5),
        'mlp_w1': nrm(ks[25], (DEPTH, D, D_FF), D ** -0.5),
        'mlp_w2': nrm(ks[26], (DEPTH, D_FF, D), D_FF ** -0.5),
        'final_norm_g': gain(ks[27], (D,)),
    }


def reference(x_prompt, x_sample, state_hgrn, state_gla, state_rglru, c, c_ctx,
              norm1_g, norm2_g, ada_w, ada_b, w_in, w_out, hgrn_lb_logits, hgrn_norm_g,
              gla_wa2, gla_ba2, gla_norm_g, rg_conv_w, rg_conv_b, rg_w_r, rg_b_r,
              rg_w_i, rg_b_i, rg_lambda, sconv_w, mlp_w1, mlp_w2, final_norm_g):
    sm = jax.nn.softmax(hgrn_lb_logits.astype(F32), axis=0)
    lower_bounds = jnp.cumsum(sm, axis=0) - sm[0]
    b_ctx = x_prompt.shape[0]
    rows = x_sample.shape[1] // GRID_W
    zero_h = jnp.zeros((b_ctx, 2, N_HEADS, HGRN_DK, HEAD_V), F32)
    zero_g = jnp.zeros((b_ctx, 2, N_HEADS, GLA_DK, HEAD_V), F32)
    zero_r = jnp.zeros((b_ctx, 2, W_GRP), F32)
    xp, xs = x_prompt, x_sample
    new_h, new_g, new_r = [], [], []
    for l in range(DEPTH):
        lw = (norm1_g[l], norm2_g[l], w_in[l], w_out[l], hgrn_norm_g[l], gla_wa2[l], gla_ba2[l],
              gla_norm_g[l], rg_conv_w[l], rg_conv_b[l], rg_w_r[l], rg_b_r[l], rg_w_i[l],
              rg_b_i[l], rg_lambda[l], sconv_w[l], mlp_w1[l], mlp_w2[l])
        mod_ctx = (jax.nn.silu(c_ctx) @ ada_w[l] + ada_b[l])[None, :]
        mod_lat = jax.nn.silu(c) @ ada_w[l] + ada_b[l]
        xp, sh, sg, sr = trunk_layer(xp, mod_ctx, 1, zero_h, zero_g, zero_r, lower_bounds[l], *lw)
        xs, _, _, _ = trunk_layer(xs, mod_lat, rows, state_hgrn[:, l], state_gla[:, l],
                                  state_rglru[:, l], lower_bounds[l], *lw)
        new_h.append(sh)
        new_g.append(sg)
        new_r.append(sr)
    y_prompt = rmsnorm(xp, final_norm_g)
    y_sample = rmsnorm(xs, final_norm_g)
    new_state_hgrn = jnp.stack(new_h, axis=1).astype(x_prompt.dtype)
    new_state_gla = jnp.stack(new_g, axis=1).astype(x_prompt.dtype)
    new_state_rglru = jnp.stack(new_r, axis=1).astype(x_prompt.dtype)
    return (y_prompt, y_sample, new_state_hgrn, new_state_gla, new_state_rglru)
```

```python
import functools

import numpy as np
import jax
import jax.numpy as jnp
from jax import lax
from jax.experimental import pallas as pl
from jax.experimental.pallas import tpu as pltpu

F32 = jnp.float32
BF16 = jnp.bfloat16

D_MODEL = 1024
DEPTH = 2
GRID_W = 64
N_HEADS = 4
W_GRP = 256
HEAD_V = 64
HGRN_DK = 64
GLA_DK = 32
GLA_RANK = 16
GLA_NORMALIZER = 16.0
RG_C = 8.0
D_FF = 4 * D_MODEL
EPS = 1e-6
F_FLOOR = 1e-20

PROJ_COLS = 3584

CHUNK = 64
HALF = CHUNK // 2
assert CHUNK == HEAD_V
SAFE_RANGE = 80.0
TOK_TILE = 1024
PROJ_TN = 512
FF_TILE = 512
ROW_BLOCK = 256
VMEM_LIMIT = 56 * 1024 * 1024


def _dot(a, b):
    return jnp.dot(a.astype(BF16), b.astype(BF16), preferred_element_type=F32)


def _dot_nt(a, b):
    return lax.dot_general(a.astype(BF16), b.astype(BF16), (((1,), (1,)), ((), ())),
                           preferred_element_type=F32)


def _dot_tn(a, b):
    return lax.dot_general(a.astype(BF16), b.astype(BF16), (((0,), (0,)), ((), ())),
                           preferred_element_type=F32)


def _sigmoid(x):
    return 1.0 / (1.0 + jnp.exp(-x))


def _silu(x):
    return x * _sigmoid(x)


def _log1p(y):
    u = 1.0 + y
    return jnp.where(u == 1.0, y, jnp.log(u) * (y / (u - 1.0)))


def _softplus(x):
    return jnp.maximum(x, 0.0) + _log1p(jnp.exp(-jnp.abs(x)))


def _rms(x):
    return x * lax.rsqrt(jnp.mean(x * x, axis=-1, keepdims=True) + EPS)


def _params(sem):
    return pltpu.CompilerParams(dimension_semantics=sem, vmem_limit_bytes=VMEM_LIMIT)


def _mod_kernel(c_ref, w_ref, b_ref, o_ref):
    s = _silu(c_ref[...])
    o_ref[0] = _dot(s, w_ref[0]) + b_ref[0]


def _modulation(cvec, ada_w, ada_b):
    tn = 512
    return pl.pallas_call(
        _mod_kernel,
        grid=(DEPTH, 6 * D_MODEL // tn),
        in_specs=[pl.BlockSpec((8, D_MODEL), lambda l, j: (0, 0)),
                  pl.BlockSpec((1, D_MODEL, tn), lambda l, j: (l, 0, j)),
                  pl.BlockSpec((1, 1, tn), lambda l, j: (l, 0, j))],
        out_specs=pl.BlockSpec((1, 8, tn), lambda l, j: (l, 0, j)),
        out_shape=jax.ShapeDtypeStruct((DEPTH, 8, 6 * D_MODEL), F32),
        compiler_params=_params(("arbitrary", "arbitrary")),
        name="adaln_modulation",
    )(cvec, ada_w, ada_b.reshape(DEPTH, 1, 6 * D_MODEL))


def _fold_kernel(a_ref, b_ref, o_ref):
    o_ref[0, 0] = jnp.dot(a_ref[0, 0], b_ref[0, 0], preferred_element_type=F32,
                          precision=lax.Precision.HIGHEST)


def _fold_lowrank(w_a, wa2):
    n = N_HEADS * GLA_DK
    return pl.pallas_call(
        _fold_kernel,
        grid=(DEPTH, 2),
        in_specs=[pl.BlockSpec((1, 1, D_MODEL, GLA_RANK), lambda l, d: (l, d, 0, 0)),
                  pl.BlockSpec((1, 1, GLA_RANK, n), lambda l, d: (l, d, 0, 0))],
        out_specs=pl.BlockSpec((1, 1, D_MODEL, n), lambda l, d: (l, d, 0, 0)),
        out_shape=jax.ShapeDtypeStruct((DEPTH, 2, D_MODEL, n), F32),
        compiler_params=_params(("arbitrary", "arbitrary")),
        name="gla_lowrank_fold",
    )(w_a, wa2)


def _proj_kernel(x_ref, mod_ref, g_ref, w_ref, o_ref, h_scr):
    @pl.when(pl.program_id(1) == 0)
    def _():
        shift = mod_ref[0, 0:1, :]
        scale = mod_ref[0, 1:2, :]
        h = _rms(x_ref[...]) * g_ref[...] * (1.0 + scale) + shift
        h_scr[...] = h.astype(BF16)

    o_ref[...] = jnp.dot(h_scr[...], w_ref[...], preferred_element_type=F32)


def _project(x, mod, norm_g, w_cat, seq_len):
    ntok = x.shape[0]
    nmod = mod.shape[0]
    tiles_per_mod = (ntok // TOK_TILE) // nmod if nmod > 1 else 1
    mod_idx = (lambda i, j: (i // tiles_per_mod, 0, 0)) if nmod > 1 else (lambda i, j: (0, 0, 0))
    del seq_len
    return pl.pallas_call(
        _proj_kernel,
        grid=(ntok // TOK_TILE, PROJ_COLS // PROJ_TN),
        in_specs=[pl.BlockSpec((TOK_TILE, D_MODEL), lambda i, j: (i, 0)),
                  pl.BlockSpec((1, 6, D_MODEL), mod_idx),
                  pl.BlockSpec((1, D_MODEL), lambda i, j: (0, 0)),
                  pl.BlockSpec((D_MODEL, PROJ_TN), lambda i, j: (0, j))],
        out_specs=pl.BlockSpec((TOK_TILE, PROJ_TN), lambda i, j: (i, j)),
        out_shape=jax.ShapeDtypeStruct((ntok, PROJ_COLS), F32),
        scratch_shapes=[pltpu.VMEM((TOK_TILE, D_MODEL), BF16)],
        compiler_params=_params(("arbitrary", "arbitrary")),
        name="norm_in_proj",
    )(x, mod, norm_g.reshape(1, D_MODEL), w_cat)


def _mlp_kernel(x_ref, oa_ref, ob_ref, oc_ref, od_ref, mod_ref, g2_ref, fg_ref, wout_ref,
                w1_ref, w2_ref, out_ref, h2_scr, *, final_norm):
    j = pl.program_id(1)

    @pl.when(j == 0)
    def _():
        mix = jnp.dot(oa_ref[...], wout_ref[0:W_GRP, :], preferred_element_type=F32)
        mix += jnp.dot(ob_ref[...], wout_ref[W_GRP:2 * W_GRP, :], preferred_element_type=F32)
        mix += jnp.dot(oc_ref[...], wout_ref[2 * W_GRP:3 * W_GRP, :], preferred_element_type=F32)
        mix += jnp.dot(od_ref[...], wout_ref[3 * W_GRP:4 * W_GRP, :], preferred_element_type=F32)
        x1 = x_ref[...] + mod_ref[0, 2:3, :] * mix
        out_ref[...] = x1
        h2 = _rms(x1) * g2_ref[...] * (1.0 + mod_ref[0, 4:5, :]) + mod_ref[0, 3:4, :]
        h2_scr[...] = h2.astype(BF16)

    t = jnp.dot(h2_scr[...], w1_ref[...], preferred_element_type=F32)
    t = jnp.square(jnp.maximum(t, 0.0))
    out_ref[...] += mod_ref[0, 5:6, :] * _dot(t, w2_ref[...])

    if final_norm:
        @pl.when(j == pl.num_programs(1) - 1)
        def _():
            out_ref[...] = _rms(out_ref[...]) * fg_ref[...]


def _out_mlp(x, o_a, o_b, o_c, o_d, mod, norm2_g, final_g, w_out, w1, w2, final_norm):
    ntok = x.shape[0]
    nmod = mod.shape[0]
    tiles_per_mod = (ntok // TOK_TILE) // nmod if nmod > 1 else 1
    mod_idx = (lambda i, j: (i // tiles_per_mod, 0, 0)) if nmod > 1 else (lambda i, j: (0, 0, 0))
    tok = lambda i, j: (i, 0)
    fixed = lambda i, j: (0, 0)
    return pl.pallas_call(
        functools.partial(_mlp_kernel, final_norm=final_norm),
        grid=(ntok // TOK_TILE, D_FF // FF_TILE),
        in_specs=[pl.BlockSpec((TOK_TILE, D_MODEL), tok),
                  pl.BlockSpec((TOK_TILE, W_GRP), tok),
                  pl.BlockSpec((TOK_TILE, W_GRP), tok),
                  pl.BlockSpec((TOK_TILE, W_GRP), tok),
                  pl.BlockSpec((TOK_TILE, W_GRP), tok),
                  pl.BlockSpec((1, 6, D_MODEL), mod_idx),
                  pl.BlockSpec((1, D_MODEL), fixed),
                  pl.BlockSpec((1, D_MODEL), fixed),
                  pl.BlockSpec((D_MODEL, D_MODEL), fixed),
                  pl.BlockSpec((D_MODEL, FF_TILE), lambda i, j: (0, j)),
                  pl.BlockSpec((FF_TILE, D_MODEL), lambda i, j: (j, 0))],
        out_specs=pl.BlockSpec((TOK_TILE, D_MODEL), tok),
        out_shape=jax.ShapeDtypeStruct((ntok, D_MODEL), F32),
        scratch_shapes=[pltpu.VMEM((TOK_TILE, D_MODEL), BF16)],
        compiler_params=_params(("arbitrary", "arbitrary")),
        name="out_proj_mlp",
    )(x, o_a, o_b, o_c, o_d, mod, norm2_g.reshape(1, D_MODEL), final_g.reshape(1, D_MODEL),
      w_out, w1, w2)


def _split3(x):
    hi = x.astype(BF16)
    r1 = x - hi.astype(F32)
    mid = r1.astype(BF16)
    lo = (r1 - mid.astype(F32)).astype(BF16)
    return hi, mid, lo


def _chunk_cumsum(tri, g):
    hi, mid, lo = _split3(g)
    s = jnp.dot(tri, hi, preferred_element_type=F32)
    s += jnp.dot(tri, mid, preferred_element_type=F32)
    s += jnp.dot(tri, lo, preferred_element_type=F32)
    return s


def _half_ranges(cum, reverse):
    if reverse:
        second = -cum[HALF:HALF + 1, :]
        first = -(cum[0:1, :] - cum[HALF:HALF + 1, :])
    else:
        first = -cum[HALF - 1:HALF, :]
        second = -(cum[CHUNK - 1:CHUNK, :] - cum[HALF - 1:HALF, :])
    return jnp.maximum(first, second)


def _recurrence_direction(d, fast, q_at, k_at, v_ref, cum_scr, st_scr, oacc, bdw_ref, bdv_ref,
                          cmask_ref, bmat_ref, *, seq_len, width):
    reverse = d == 1
    nchunk = seq_len // CHUNK
    mid_row = HALF if reverse else HALF - 1
    last_row = 0 if reverse else CHUNK - 1

    def body(ci, carry):
        c = (nchunk - 1 - ci) if reverse else ci
        r0 = pl.multiple_of(c * CHUNK, CHUNK)
        q = q_at(r0)
        k = k_at(d, r0)
        v = v_ref[pl.ds(r0, CHUNK), :]
        cum = cum_scr[d, pl.ds(r0, CHUNK), :]
        tot = cum[last_row:last_row + 1, :]
        st = st_scr[...]
        o = _dot_nt(q * jnp.exp(cum), st)
        if fast:
            cm = cum[mid_row:mid_row + 1, :]
            qm = q * jnp.exp(cum - cm)
            km = k * jnp.exp(cm - cum)
            kbd = jnp.concatenate([km] * N_HEADS, axis=0) * bdw_ref[...]
            sc = _dot_nt(qm, kbd)
            a = jnp.where(cmask_ref[d] > 0.5, sc, 0.0)
            vexp = jnp.concatenate([v] * N_HEADS, axis=0) * bdv_ref[...]
            o = o + _dot(a, vexp)
        else:
            rows = lax.broadcasted_iota(jnp.int32, (CHUNK, width), 0)

            def key_row(j, acc):
                krow = k_at(d, r0 + j, 1)
                crow = cum_scr[d, pl.ds(r0 + j, 1), :]
                vrow = v_ref[pl.ds(r0 + j, 1), :]
                keep = (rows <= j) if reverse else (rows >= j)
                e = jnp.exp(jnp.minimum(cum - crow, 0.0))
                p = jnp.where(keep, q * krow * e, 0.0)
                return acc + _dot(p, bmat_ref[...]) * vrow

            o = lax.fori_loop(0, CHUNK, key_row, o)
        ke = k * jnp.exp(tot - cum)
        st_scr[...] = st * jnp.exp(tot) + _dot_tn(v, ke) * bdw_ref[...]
        if d == 0:
            oacc[pl.ds(r0, CHUNK), :] = o
        else:
            oacc[pl.ds(r0, CHUNK), :] += o
        return carry

    lax.fori_loop(0, nchunk, body, 0)


def _recurrence_both(ok, q_at, k_at, v_ref, cum_scr, st_scr, oacc, s0_ref, sfin_ref, bdw_ref,
                     bdv_ref, cmask_ref, bmat_ref, *, seq_len, width):
    for d in (0, 1):
        if s0_ref is None:
            st_scr[...] = jnp.zeros_like(st_scr)
        else:
            st_scr[...] = s0_ref[0, d]
        run = functools.partial(_recurrence_direction, d, q_at=q_at, k_at=k_at, v_ref=v_ref,
                                cum_scr=cum_scr, st_scr=st_scr, oacc=oacc, bdw_ref=bdw_ref,
                                bdv_ref=bdv_ref, cmask_ref=cmask_ref, bmat_ref=bmat_ref,
                                seq_len=seq_len, width=width)
        lax.cond(ok, lambda: run(True), lambda: run(False))
        if sfin_ref is not None:
            sfin_ref[0, d] = st_scr[...]


def _head_norm_gate(oacc, gate_ref, gain_ref, bdv_ref, o_ref, seq_len):
    for rb in range(seq_len // ROW_BLOCK):
        rows = slice(rb * ROW_BLOCK, (rb + 1) * ROW_BLOCK)
        o = oacc[rows, :]
        ms = _dot(o * o, bdv_ref[...]) * (1.0 / HEAD_V)
        y = o * lax.rsqrt(ms + EPS) * gain_ref[...] * _silu(gate_ref[rows, :])
        o_ref[rows, :] = y.astype(o_ref.dtype)


def _hgrn_kernel(*refs, seq_len, layer, has_s0, want_final):
    (q_ref, i_ref, ff_ref, fb_ref, g_ref, lbl_ref, gain_ref, tri_ref, bdw_ref, bdv_ref,
     cmask_ref, bmat_ref) = refs[:12]
    pos = 12
    s0_ref = None
    if has_s0:
        s0_ref = refs[pos]
        pos += 1
    o_ref = refs[pos]
    pos += 1
    sfin_ref = None
    if want_final:
        sfin_ref = refs[pos]
        pos += 1
    k_scr, cum_scr, oacc, st_scr = refs[pos:]

    lg = lbl_ref[...]
    mx = lg[0]
    for i in range(1, DEPTH):
        mx = jnp.maximum(mx, lg[i])
    ex = [jnp.exp(lg[i] - mx) for i in range(DEPTH)]
    den = ex[0]
    for i in range(1, DEPTH):
        den = den + ex[i]
    sm = [e / den for e in ex]
    csum = sm[0]
    for i in range(1, layer + 1):
        csum = csum + sm[i]
    lb = csum - sm[0]

    nchunk = seq_len // CHUNK
    rmax = jnp.zeros((1, W_GRP), F32)
    for d in (0, 1):
        f_ref = ff_ref if d == 0 else fb_ref
        lbd = lb[d:d + 1, :]

        def pre(ci, rm, d=d, f_ref=f_ref, lbd=lbd):
            r0 = pl.multiple_of(ci * CHUNK, CHUNK)
            f = lbd + (1.0 - lbd) * _sigmoid(f_ref[pl.ds(r0, CHUNK), :])
            g = jnp.log(jnp.maximum(f, F_FLOOR))
            k_scr[d, pl.ds(r0, CHUNK), :] = 1.0 - f
            cum = _chunk_cumsum(tri_ref[d], g)
            cum_scr[d, pl.ds(r0, CHUNK), :] = cum
            return jnp.maximum(rm, _half_ranges(cum, d == 1))

        rmax = lax.fori_loop(0, nchunk, pre, rmax)
    ok = jnp.max(rmax) < SAFE_RANGE

    q_at = lambda r0: q_ref[pl.ds(r0, CHUNK), :]
    k_at = lambda d, r0, n=CHUNK: k_scr[d, pl.ds(r0, n), :]
    _recurrence_both(ok, q_at, k_at, i_ref, cum_scr, st_scr, oacc, s0_ref, sfin_ref, bdw_ref,
                     bdv_ref, cmask_ref, bmat_ref, seq_len=seq_len, width=W_GRP)
    _head_norm_gate(oacc, g_ref, gain_ref, bdv_ref, o_ref, seq_len)


def _gla_kernel(*refs, seq_len, has_s0, want_final):
    (q_ref, k_ref, v_ref, g_ref, zf_ref, zb_ref, ba_ref, gain_ref, tri_ref, bdw_ref, bdv_ref,
     cmask_ref, bmat_ref) = refs[:13]
    pos = 13
    s0_ref = None
    if has_s0:
        s0_ref = refs[pos]
        pos += 1
    o_ref = refs[pos]
    pos += 1
    sfin_ref = None
    if want_final:
        sfin_ref = refs[pos]
        pos += 1
    cum_scr, oacc, st_scr = refs[pos:]
    width = N_HEADS * GLA_DK

    nchunk = seq_len // CHUNK
    rmax = jnp.zeros((1, width), F32)
    for d in (0, 1):
        z_ref = zf_ref if d == 0 else zb_ref
        bias = ba_ref[d:d + 1, :]

        def pre(ci, rm, d=d, z_ref=z_ref, bias=bias):
            r0 = pl.multiple_of(ci * CHUNK, CHUNK)
            z = z_ref[pl.ds(r0, CHUNK), :] + bias
            g = -_softplus(-z) * (1.0 / GLA_NORMALIZER)
            cum = _chunk_cumsum(tri_ref[d], g)
            cum_scr[d, pl.ds(r0, CHUNK), :] = cum
            return jnp.maximum(rm, _half_ranges(cum, d == 1))

        rmax = lax.fori_loop(0, nchunk, pre, rmax)
    ok = jnp.max(rmax) < SAFE_RANGE

    q_at = lambda r0: q_ref[pl.ds(r0, CHUNK), :] * (GLA_DK ** -0.5)
    k_at = lambda d, r0, n=CHUNK: k_ref[pl.ds(r0, n), :]
    _recurrence_both(ok, q_at, k_at, v_ref, cum_scr, st_scr, oacc, s0_ref, sfin_ref, bdw_ref,
                     bdv_ref, cmask_ref, bmat_ref, seq_len=seq_len, width=width)
    _head_norm_gate(oacc, g_ref, gain_ref, bdv_ref, o_ref, seq_len)


def _recurrence_constants(width):
    dk = width // N_HEADS
    r = np.arange(N_HEADS * CHUNK)[:, None]
    tri = np.tril(np.ones((CHUNK, CHUNK), np.float32))
    tri = np.stack([tri, tri.T])
    bdw = (r // CHUNK == np.arange(width)[None, :] // dk).astype(np.float32)
    bdv = (r // CHUNK == np.arange(W_GRP)[None, :] // HEAD_V).astype(np.float32)
    t = np.arange(CHUNK)[:, None]
    s = np.arange(N_HEADS * CHUNK)[None, :] % CHUNK
    cmask = np.stack([(s <= t), (s >= t)]).astype(np.float32)
    bmat = (np.arange(width)[:, None] // dk == np.arange(W_GRP)[None, :] // HEAD_V)
    return (jnp.asarray(tri, BF16), jnp.asarray(bdw), jnp.asarray(bdv), jnp.asarray(cmask),
            jnp.asarray(bmat.astype(np.float32), BF16))


def _full_spec(a):
    zeros = (0,) * a.ndim
    return pl.BlockSpec(a.shape, lambda b: zeros)


def _hgrn_mixer(proj, lb_logits, gain, s0, nseq, seq_len, layer, want_final):
    consts = _recurrence_constants(W_GRP)
    col = lambda c: pl.BlockSpec((seq_len, W_GRP), lambda b: (b, c))
    st_spec = pl.BlockSpec((1, 2, W_GRP, W_GRP), lambda b: (b, 0, 0, 0))
    args = [proj] * 5 + [lb_logits, gain.reshape(1, W_GRP)] + list(consts)
    in_specs = [col(0), col(1), col(2), col(3), col(4), _full_spec(lb_logits),
                pl.BlockSpec((1, W_GRP), lambda b: (0, 0))] + [_full_spec(c) for c in consts]
    if s0 is not None:
        args.append(s0)
        in_specs.append(st_spec)
    out_shape = [jax.ShapeDtypeStruct((nseq * seq_len, W_GRP), BF16)]
    out_specs = [pl.BlockSpec((seq_len, W_GRP), lambda b: (b, 0))]
    if want_final:
        out_shape.append(jax.ShapeDtypeStruct((nseq, 2, W_GRP, W_GRP), F32))
        out_specs.append(st_spec)
    res = pl.pallas_call(
        functools.partial(_hgrn_kernel, seq_len=seq_len, layer=layer, has_s0=s0 is not None,
                          want_final=want_final),
        grid=(nseq,),
        in_specs=in_specs,
        out_specs=out_specs,
        out_shape=out_shape,
        scratch_shapes=[pltpu.VMEM((2, seq_len, W_GRP), F32), pltpu.VMEM((2, seq_len, W_GRP), F32),
                        pltpu.VMEM((seq_len, W_GRP), F32), pltpu.VMEM((W_GRP, W_GRP), F32)],
        compiler_params=_params(("arbitrary",)),
        name="hgrn2_mixer",
    )(*args)
    return res[0], (res[1] if want_final else None)


def _gla_mixer(proj, ba2, gain, s0, nseq, seq_len, want_final):
    width = N_HEADS * GLA_DK
    consts = _recurrence_constants(width)
    col256 = lambda c: pl.BlockSpec((seq_len, W_GRP), lambda b: (b, c))
    col128 = lambda c: pl.BlockSpec((seq_len, width), lambda b: (b, c))
    st_spec = pl.BlockSpec((1, 2, W_GRP, width), lambda b: (b, 0, 0, 0))
    args = [proj] * 6 + [ba2, gain.reshape(1, W_GRP)] + list(consts)
    in_specs = [col128(10), col128(11), col256(6), col256(7), col128(16), col128(17),
                _full_spec(ba2), pl.BlockSpec((1, W_GRP), lambda b: (0, 0))]
    in_specs += [_full_spec(c) for c in consts]
    if s0 is not None:
        args.append(s0)
        in_specs.append(st_spec)
    out_shape = [jax.ShapeDtypeStruct((nseq * seq_len, W_GRP), BF16)]
    out_specs = [pl.BlockSpec((seq_len, W_GRP), lambda b: (b, 0))]
    if want_final:
        out_shape.append(jax.ShapeDtypeStruct((nseq, 2, W_GRP, width), F32))
        out_specs.append(st_spec)
    res = pl.pallas_call(
        functools.partial(_gla_kernel, seq_len=seq_len, has_s0=s0 is not None,
                          want_final=want_final),
        grid=(nseq,),
        in_specs=in_specs,
        out_specs=out_specs,
        out_shape=out_shape,
        scratch_shapes=[pltpu.VMEM((2, seq_len, width), F32), pltpu.VMEM((seq_len, W_GRP), F32),
                        pltpu.VMEM((W_GRP, width), F32)],
        compiler_params=_params(("arbitrary",)),
        name="gla_mixer",
    )(*args)
    return res[0], (res[1] if want_final else None)


PAD = 8
SCAN_TILE = 8


def _rgsc_kernel(*refs, seq_len, seg, has_h0, want_final):
    (cx_ref, cg_ref, db_ref, dc_ref, dv_ref, convw_ref, convb_ref, wr_ref, br_ref, wi_ref,
     bi_ref, lam_ref, sw_ref) = refs[:13]
    pos = 13
    h0_ref = None
    if has_h0:
        h0_ref = refs[pos]
        pos += 1
    oc_ref, od_ref = refs[pos:pos + 2]
    pos += 2
    hfin_ref = None
    if want_final:
        hfin_ref = refs[pos]
        pos += 1
    upad, a_scr, b_scr, h_scr = refs[pos:]

    zpad = jnp.zeros((PAD, W_GRP), F32)
    upad[0:PAD, :] = zpad
    upad[PAD + seq_len:2 * PAD + seq_len, :] = zpad
    upad[PAD:PAD + seq_len, :] = cx_ref[...]
    rows8 = lax.broadcasted_iota(jnp.int32, (SCAN_TILE, W_GRP), 0)
    ntile = seq_len // SCAN_TILE

    for d in (0, 1):
        reverse = d == 1
        w = convw_ref[d]
        offs = [PAD + 3, PAD + 2, PAD + 1, PAD] if reverse else [PAD - 3, PAD - 2, PAD - 1, PAD]
        sp = _softplus(-lam_ref[d])
        for rb in range(seq_len // ROW_BLOCK):
            base = rb * ROW_BLOCK
            xc = convb_ref[d]
            for j in range(4):
                xc = xc + w[j:j + 1, :] * upad[offs[j] + base:offs[j] + base + ROW_BLOCK, :]
            r = _sigmoid(_dot(xc, wr_ref[d]) + br_ref[d])
            ig = _sigmoid(_dot(xc, wi_ref[d]) + bi_ref[d])
            log_a = -RG_C * r * sp
            a = jnp.exp(log_a)
            a_scr[base:base + ROW_BLOCK, :] = a
            b_scr[base:base + ROW_BLOCK, :] = jnp.sqrt(1.0 - a * a) * (ig * xc)

        h0 = h0_ref[0, d:d + 1, :] if has_h0 else jnp.zeros((1, W_GRP), F32)

        def step(ti, carry, d=d, reverse=reverse):
            t = (ntile - 1 - ti) if reverse else ti
            r0 = pl.multiple_of(t * SCAN_TILE, SCAN_TILE)
            a = a_scr[pl.ds(r0, SCAN_TILE), :]
            b = b_scr[pl.ds(r0, SCAN_TILE), :]
            for s in (1, 2, 4):
                if reverse:
                    keep = rows8 <= SCAN_TILE - 1 - s
                    shift = SCAN_TILE - s
                else:
                    keep = rows8 >= s
                    shift = s
                a_s = jnp.where(keep, pltpu.roll(a, shift, 0), 1.0)
                b_s = jnp.where(keep, pltpu.roll(b, shift, 0), 0.0)
                b = b + a * b_s
                a = a * a_s
            h = a * carry + b
            if d == 0:
                h_scr[pl.ds(r0, SCAN_TILE), :] = h
            else:
                h_scr[pl.ds(r0, SCAN_TILE), :] += h
            return h[0:1, :] if reverse else h[SCAN_TILE - 1:SCAN_TILE, :]

        hlast = lax.fori_loop(0, ntile, step, h0, unroll=4)
        if want_final:
            hfin_ref[0, d:d + 1, :] = hlast

    sw = sw_ref[...]
    for rb in range(seq_len // ROW_BLOCK):
        rows = slice(rb * ROW_BLOCK, (rb + 1) * ROW_BLOCK)
        x = cg_ref[rows, :]
        gelu = 0.5 * x * (1.0 + jnp.tanh(0.7978845608028654 * (x + 0.044715 * (x * x * x))))
        oc_ref[rows, :] = (h_scr[rows, :] * gelu).astype(oc_ref.dtype)
        upad[PAD + rb * ROW_BLOCK:PAD + (rb + 1) * ROW_BLOCK, :] = dc_ref[rows, :] * dv_ref[rows, :]
    for rb in range(seq_len // ROW_BLOCK):
        base = rb * ROW_BLOCK
        rows = slice(base, base + ROW_BLOCK)
        posn = (lax.broadcasted_iota(jnp.int32, (ROW_BLOCK, W_GRP), 0) + base) % seg
        left = jnp.where(posn != 0, upad[PAD - 1 + base:PAD - 1 + base + ROW_BLOCK, :], 0.0)
        right = jnp.where(posn != seg - 1, upad[PAD + 1 + base:PAD + 1 + base + ROW_BLOCK, :], 0.0)
        y = sw[0:1, :] * left + sw[1:2, :] * upad[PAD + base:PAD + base + ROW_BLOCK, :] + sw[2:3, :] * right
        od_ref[rows, :] = (db_ref[rows, :] * y).astype(od_ref.dtype)


def _rgsc_mixer(proj, conv_w, conv_b, wr_bd, b_r, wi_bd, b_i, lam, sconv_w, h0, nseq, seq_len, seg,
                want_final):
    col = lambda c: pl.BlockSpec((seq_len, W_GRP), lambda b: (b, c))
    params = [conv_w, conv_b.reshape(2, 1, W_GRP), wr_bd, b_r.reshape(2, 1, W_GRP), wi_bd,
              b_i.reshape(2, 1, W_GRP), lam.reshape(2, 1, W_GRP), sconv_w]
    args = [proj] * 5 + params
    in_specs = [col(9), col(10), col(11), col(12), col(13)] + [_full_spec(p) for p in params]
    h_spec = pl.BlockSpec((1, 2, W_GRP), lambda b: (b, 0, 0))
    if h0 is not None:
        args.append(h0)
        in_specs.append(h_spec)
    out_shape = [jax.ShapeDtypeStruct((nseq * seq_len, W_GRP), BF16)] * 2
    out_specs = [pl.BlockSpec((seq_len, W_GRP), lambda b: (b, 0))] * 2
    if want_final:
        out_shape.append(jax.ShapeDtypeStruct((nseq, 2, W_GRP), F32))
        out_specs.append(h_spec)
    res = pl.pallas_call(
        functools.partial(_rgsc_kernel, seq_len=seq_len, seg=seg, has_h0=h0 is not None,
                          want_final=want_final),
        grid=(nseq,),
        in_specs=in_specs,
        out_specs=out_specs,
        out_shape=out_shape,
        scratch_shapes=[pltpu.VMEM((seq_len + 2 * PAD, W_GRP), F32), pltpu.VMEM((seq_len, W_GRP), F32),
                        pltpu.VMEM((seq_len, W_GRP), F32), pltpu.VMEM((seq_len, W_GRP), F32)],
        compiler_params=_params(("arbitrary",)),
        name="rglru_sconv_mixer",
    )(*args)
    return res[0], res[1], (res[2] if want_final else None)


def _state_to_kernel(s):
    b, two, h, dk, dv = s.shape
    eye = jnp.eye(h, dtype=s.dtype)
    t = jnp.swapaxes(s, 3, 4)
    full = t[:, :, :, :, None, :] * eye[None, None, :, None, :, None]
    return full.reshape(b, two, h * dv, h * dk)


def _state_from_kernel(st, dk):
    b = st.shape[0]
    blocks = st.reshape(b, 2, N_HEADS, HEAD_V, N_HEADS, dk)
    return jnp.einsum('bdhehk->bdhke', blocks)


def _block_diag(w):
    two, h, c, _ = w.shape
    eye = jnp.eye(h, dtype=w.dtype)
    full = w[:, :, :, None, :] * eye[None, :, None, :, None]
    return full.reshape(two, h * c, h * c)


def kernel(x_prompt, x_sample, state_hgrn, state_gla, state_rglru, c, c_ctx, norm1_g, norm2_g, ada_w, ada_b, w_in, w_out, hgrn_lb_logits, hgrn_norm_g, gla_wa2, gla_ba2, gla_norm_g, rg_conv_w, rg_conv_b, rg_w_r, rg_b_r, rg_w_i, rg_b_i, rg_lambda, sconv_w, mlp_w1, mlp_w2, final_norm_g):
    b_ctx, l_ctx, _ = x_prompt.shape
    b_lat, l_lat, _ = x_sample.shape

    cvec = jnp.concatenate([c, c_ctx[None, :], jnp.zeros((8 - b_lat - 1, D_MODEL), F32)], axis=0)
    mod = _modulation(cvec, ada_w, ada_b).reshape(DEPTH, 8, 6, D_MODEL)

    a0 = 2048
    w_a = jnp.stack([w_in[:, :, a0:a0 + GLA_RANK], w_in[:, :, a0 + GLA_RANK:a0 + 2 * GLA_RANK]], axis=1)
    w_z = _fold_lowrank(w_a, gla_wa2)
    w_cat = jnp.concatenate([w_in[:, :, :a0], w_z[:, 0], w_z[:, 1], w_in[:, :, a0 + 2 * GLA_RANK:]],
                            axis=-1).astype(BF16)
    w_out_b = w_out.astype(BF16)
    w1_b = mlp_w1.astype(BF16)
    w2_b = mlp_w2.astype(BF16)

    xp = x_prompt.reshape(b_ctx * l_ctx, D_MODEL)
    xs = x_sample.reshape(b_lat * l_lat, D_MODEL)
    new_h, new_g, new_r = [], [], []
    for l in range(DEPTH):
        wr_bd = _block_diag(rg_w_r[l]).astype(BF16)
        wi_bd = _block_diag(rg_w_i[l]).astype(BF16)
        mod_ctx = mod[l, b_lat:b_lat + 1]
        mod_lat = mod[l, :b_lat]
        last = l == DEPTH - 1
        streams = (
            (xp, mod_ctx, b_ctx, l_ctx, l_ctx, None, None, None, True),
            (xs, mod_lat, b_lat, l_lat, GRID_W, _state_to_kernel(state_hgrn[:, l]),
             _state_to_kernel(state_gla[:, l]), state_rglru[:, l], False),
        )
        outs = []
        for (x, m, nseq, seq_len, seg, s_h, s_g, s_r, want_final) in streams:
            proj = _project(x, m, norm1_g[l], w_cat[l], seq_len)
            o_a, f_h = _hgrn_mixer(proj, hgrn_lb_logits, hgrn_norm_g[l], s_h, nseq, seq_len, l, want_final)
            o_b, f_g = _gla_mixer(proj, gla_ba2[l], gla_norm_g[l], s_g, nseq, seq_len, want_final)
            o_c, o_d, f_r = _rgsc_mixer(proj, rg_conv_w[l], rg_conv_b[l], wr_bd, rg_b_r[l], wi_bd,
                                        rg_b_i[l], rg_lambda[l], sconv_w[l], s_r, nseq, seq_len, seg,
                                        want_final)
            x_new = _out_mlp(x, o_a, o_b, o_c, o_d, m, norm2_g[l], final_norm_g, w_out_b[l], w1_b[l],
                             w2_b[l], last)
            outs.append((x_new, f_h, f_g, f_r))
        xp, f_h, f_g, f_r = outs[0]
        xs = outs[1][0]
        new_h.append(_state_from_kernel(f_h, HGRN_DK))
        new_g.append(_state_from_kernel(f_g, GLA_DK))
        new_r.append(f_r)
    y_prompt = xp.reshape(b_ctx, l_ctx, D_MODEL)
    y_sample = xs.reshape(b_lat, l_lat, D_MODEL)
    return (y_prompt, y_sample, jnp.stack(new_h, axis=1), jnp.stack(new_g, axis=1),
            jnp.stack(new_r, axis=1))
```

```python
import functools

import numpy as np
import jax
import jax.numpy as jnp
from jax import lax
from jax.experimental import pallas as pl
from jax.experimental.pallas import tpu as pltpu

F32 = jnp.float32
BF16 = jnp.bfloat16

D_MODEL = 1024
DEPTH = 2
GRID_W = 64
N_HEADS = 4
W_GRP = 256
HEAD_V = 64
HGRN_DK = 64
GLA_DK = 32
GLA_RANK = 16
GLA_NORMALIZER = 16.0
RG_C = 8.0
D_FF = 4 * D_MODEL
EPS = 1e-6
F_FLOOR = 1e-20

PROJ_COLS = 3584

CHUNK = 64
HALF = CHUNK // 2
assert CHUNK == HEAD_V
GROUP = 4
GROWS = GROUP * CHUNK
SAFE_RANGE = 80.0
TOK_TILE = 1024
PROJ_TN = 512
FF_TILE = 512
ROW_BLOCK = 256
VMEM_LIMIT = 56 * 1024 * 1024

_NT = (((1,), (1,)), ((), ()))
_TN = (((0,), (0,)), ((), ()))


def _dot(a, b):
    return jnp.dot(a.astype(BF16), b.astype(BF16), preferred_element_type=F32)


def _dot_nt(a, b):
    return lax.dot_general(a.astype(BF16), b.astype(BF16), _NT, preferred_element_type=F32)


def _dot_tn(a, b):
    return lax.dot_general(a.astype(BF16), b.astype(BF16), _TN, preferred_element_type=F32)


def _sigmoid(x):
    return 1.0 / (1.0 + jnp.exp(-x))


def _silu(x):
    return x * _sigmoid(x)


def _log1p(y):
    u = 1.0 + y
    return jnp.where(u == 1.0, y, jnp.log(u) * (y / (u - 1.0)))


def _softplus(x):
    return jnp.maximum(x, 0.0) + _log1p(jnp.exp(-jnp.abs(x)))


def _rms(x):
    return x * lax.rsqrt(jnp.mean(x * x, axis=-1, keepdims=True) + EPS)


def _params(sem):
    return pltpu.CompilerParams(dimension_semantics=sem, vmem_limit_bytes=VMEM_LIMIT)


def _full_spec(a):
    zeros = (0,) * a.ndim
    return pl.BlockSpec(a.shape, lambda *_: zeros)


def _mod_kernel(c_ref, w_ref, b_ref, o_ref):
    s = _silu(c_ref[...])
    o_ref[0] = _dot(s, w_ref[0]) + b_ref[0]


def _modulation(cvec, ada_w, ada_b):
    tn = 512
    return pl.pallas_call(
        _mod_kernel,
        grid=(DEPTH, 6 * D_MODEL // tn),
        in_specs=[pl.BlockSpec((8, D_MODEL), lambda l, j: (0, 0)),
                  pl.BlockSpec((1, D_MODEL, tn), lambda l, j: (l, 0, j)),
                  pl.BlockSpec((1, 1, tn), lambda l, j: (l, 0, j))],
        out_specs=pl.BlockSpec((1, 8, tn), lambda l, j: (l, 0, j)),
        out_shape=jax.ShapeDtypeStruct((DEPTH, 8, 6 * D_MODEL), F32),
        compiler_params=_params(("arbitrary", "arbitrary")),
        name="adaln_modulation",
    )(cvec, ada_w, ada_b.reshape(DEPTH, 1, 6 * D_MODEL))


def _fold_kernel(a_ref, b_ref, o_ref):
    o_ref[0, 0] = jnp.dot(a_ref[0, 0], b_ref[0, 0], preferred_element_type=F32,
                          precision=lax.Precision.HIGHEST)


def _fold_lowrank(w_a, wa2):
    n = N_HEADS * GLA_DK
    return pl.pallas_call(
        _fold_kernel,
        grid=(DEPTH, 2),
        in_specs=[pl.BlockSpec((1, 1, D_MODEL, GLA_RANK), lambda l, d: (l, d, 0, 0)),
                  pl.BlockSpec((1, 1, GLA_RANK, n), lambda l, d: (l, d, 0, 0))],
        out_specs=pl.BlockSpec((1, 1, D_MODEL, n), lambda l, d: (l, d, 0, 0)),
        out_shape=jax.ShapeDtypeStruct((DEPTH, 2, D_MODEL, n), F32),
        compiler_params=_params(("arbitrary", "arbitrary")),
        name="gla_lowrank_fold",
    )(w_a, wa2)


def _proj_kernel(x_ref, mod_ref, g_ref, w_ref, o_ref, h_scr):
    @pl.when(pl.program_id(1) == 0)
    def _():
        shift = mod_ref[0, 0:1, :]
        scale = mod_ref[0, 1:2, :]
        h = _rms(x_ref[...]) * g_ref[0] * (1.0 + scale) + shift
        h_scr[...] = h.astype(BF16)

    o_ref[...] = jnp.dot(h_scr[...], w_ref[0], preferred_element_type=F32)


def _mod_index(ntok, nmod):
    tiles_per_mod = (ntok // TOK_TILE) // nmod if nmod > 1 else 1
    if nmod > 1:
        return lambda i, j: (i // tiles_per_mod, 0, 0)
    return lambda i, j: (0, 0, 0)


def _project(x, mod, norm_g, w_cat, layer):
    ntok = x.shape[0]
    return pl.pallas_call(
        _proj_kernel,
        grid=(ntok // TOK_TILE, PROJ_COLS // PROJ_TN),
        in_specs=[pl.BlockSpec((TOK_TILE, D_MODEL), lambda i, j: (i, 0)),
                  pl.BlockSpec((1, 6, D_MODEL), _mod_index(ntok, mod.shape[0])),
                  pl.BlockSpec((1, 1, D_MODEL), lambda i, j: (layer, 0, 0)),
                  pl.BlockSpec((1, D_MODEL, PROJ_TN), lambda i, j: (layer, 0, j))],
        out_specs=pl.BlockSpec((TOK_TILE, PROJ_TN), lambda i, j: (i, j)),
        out_shape=jax.ShapeDtypeStruct((ntok, PROJ_COLS), F32),
        scratch_shapes=[pltpu.VMEM((TOK_TILE, D_MODEL), BF16)],
        compiler_params=_params(("arbitrary", "arbitrary")),
        name="norm_in_proj",
    )(x, mod, norm_g.reshape(DEPTH, 1, D_MODEL), w_cat)


def _mlp_kernel(x_ref, oa_ref, ob_ref, oc_ref, od_ref, mod_ref, g2_ref, fg_ref, wout_ref,
                w1_ref, w2_ref, out_ref, h2_scr, *, final_norm):
    j = pl.program_id(1)

    @pl.when(j == 0)
    def _():
        mix = jnp.dot(oa_ref[...], wout_ref[0, 0:W_GRP, :], preferred_element_type=F32)
        mix += jnp.dot(ob_ref[...], wout_ref[0, W_GRP:2 * W_GRP, :], preferred_element_type=F32)
        mix += jnp.dot(oc_ref[...], wout_ref[0, 2 * W_GRP:3 * W_GRP, :], preferred_element_type=F32)
        mix += jnp.dot(od_ref[...], wout_ref[0, 3 * W_GRP:4 * W_GRP, :], preferred_element_type=F32)
        x1 = x_ref[...] + mod_ref[0, 2:3, :] * mix
        out_ref[...] = x1
        h2 = _rms(x1) * g2_ref[0] * (1.0 + mod_ref[0, 4:5, :]) + mod_ref[0, 3:4, :]
        h2_scr[...] = h2.astype(BF16)

    t = jnp.dot(h2_scr[...], w1_ref[0], preferred_element_type=F32)
    t = jnp.square(jnp.maximum(t, 0.0))
    out_ref[...] += mod_ref[0, 5:6, :] * _dot(t, w2_ref[0])

    if final_norm:
        @pl.when(j == pl.num_programs(1) - 1)
        def _():
            out_ref[...] = _rms(out_ref[...]) * fg_ref[...]


def _out_mlp(x, o_a, o_b, o_c, o_d, mod, norm2_g, final_g, w_out, w1, w2, layer, final_norm):
    ntok = x.shape[0]
    tok = lambda i, j: (i, 0)
    return pl.pallas_call(
        functools.partial(_mlp_kernel, final_norm=final_norm),
        grid=(ntok // TOK_TILE, D_FF // FF_TILE),
        in_specs=[pl.BlockSpec((TOK_TILE, D_MODEL), tok),
                  pl.BlockSpec((TOK_TILE, W_GRP), tok),
                  pl.BlockSpec((TOK_TILE, W_GRP), tok),
                  pl.BlockSpec((TOK_TILE, W_GRP), tok),
                  pl.BlockSpec((TOK_TILE, W_GRP), tok),
                  pl.BlockSpec((1, 6, D_MODEL), _mod_index(ntok, mod.shape[0])),
                  pl.BlockSpec((1, 1, D_MODEL), lambda i, j: (layer, 0, 0)),
                  pl.BlockSpec((1, D_MODEL), lambda i, j: (0, 0)),
                  pl.BlockSpec((1, D_MODEL, D_MODEL), lambda i, j: (layer, 0, 0)),
                  pl.BlockSpec((1, D_MODEL, FF_TILE), lambda i, j: (layer, 0, j)),
                  pl.BlockSpec((1, FF_TILE, D_MODEL), lambda i, j: (layer, j, 0))],
        out_specs=pl.BlockSpec((TOK_TILE, D_MODEL), tok),
        out_shape=jax.ShapeDtypeStruct((ntok, D_MODEL), F32),
        scratch_shapes=[pltpu.VMEM((TOK_TILE, D_MODEL), BF16)],
        compiler_params=_params(("arbitrary", "arbitrary")),
        name="out_proj_mlp",
    )(x, o_a, o_b, o_c, o_d, mod, norm2_g.reshape(DEPTH, 1, D_MODEL), final_g.reshape(1, D_MODEL),
      w_out, w1, w2)


def _split3(x):
    hi = x.astype(BF16)
    r1 = x - hi.astype(F32)
    mid = r1.astype(BF16)
    lo = (r1 - mid.astype(F32)).astype(BF16)
    return hi, mid, lo


def _dot_exact_rhs(x, m):
    hi, mid, lo = _split3(x)
    s = jnp.dot(hi, m, preferred_element_type=F32)
    s += jnp.dot(mid, m, preferred_element_type=F32)
    s += jnp.dot(lo, m, preferred_element_type=F32)
    return s


def _block_cumsum(tri, g):
    hi, mid, lo = _split3(g)
    s = jnp.dot(tri, hi, preferred_element_type=F32)
    s += jnp.dot(tri, mid, preferred_element_type=F32)
    s += jnp.dot(tri, lo, preferred_element_type=F32)
    return s


def _half_ranges(cum, reverse):
    if reverse:
        second = -cum[HALF:HALF + 1, :]
        first = -(cum[0:1, :] - cum[HALF:HALF + 1, :])
    else:
        first = -cum[HALF - 1:HALF, :]
        second = -(cum[CHUNK - 1:CHUNK, :] - cum[HALF - 1:HALF, :])
    return jnp.maximum(first, second)


def _block_ranges(cum, reverse, rm):
    for c in range(GROUP):
        rm = jnp.maximum(rm, _half_ranges(cum[c * CHUNK:(c + 1) * CHUNK, :], reverse))
    return rm


def _row_start(c):
    r0 = c * CHUNK
    return r0 if isinstance(r0, int) else pl.multiple_of(r0, CHUNK)


class _Rec:
    def __init__(self, q_at, k_at, v_ref, cum_scr, st_scr, oacc, consts, width):
        self.q_at, self.k_at, self.v_ref = q_at, k_at, v_ref
        self.cum_scr, self.st_scr, self.oacc = cum_scr, st_scr, oacc
        (self.tri_ref, self.bdw_b, self.bdw_f, self.bdv_b, self.cmask, self.bmat,
         self.tile_t, self.tile_tt) = consts
        self.width = width


def _chunk_step(rec, d, c, fast):
    reverse = d == 1
    mid_row = HALF if reverse else HALF - 1
    last_row = 0 if reverse else CHUNK - 1
    r0 = _row_start(c)
    rows = pl.ds(r0, CHUNK)
    q = rec.q_at(r0)
    k = rec.k_at(d, r0)
    v_b = rec.v_ref[rows, :].astype(BF16)
    cum = rec.cum_scr[d, rows, :]
    tot = cum[last_row:last_row + 1, :]
    st = rec.st_scr[d]
    o = _dot_nt(q * jnp.exp(cum), st)
    if fast:
        cm = cum[mid_row:mid_row + 1, :]
        qm = q * jnp.exp(cum - cm)
        km_b = (k * jnp.exp(cm - cum)).astype(BF16)
        kbd = jnp.concatenate([km_b] * N_HEADS, axis=0) * rec.bdw_b[...]
        sc = lax.dot_general(qm.astype(BF16), kbd, _NT, preferred_element_type=F32)
        a = jnp.where(rec.cmask[d] > 0.5, sc, 0.0)
        vexp = jnp.concatenate([v_b] * N_HEADS, axis=0) * rec.bdv_b[...]
        o = o + jnp.dot(a.astype(BF16), vexp, preferred_element_type=F32)
    else:
        row_id = lax.broadcasted_iota(jnp.int32, (CHUNK, rec.width), 0)

        def key_row(j, acc):
            krow = rec.k_at(d, r0 + j, 1)
            crow = rec.cum_scr[d, pl.ds(r0 + j, 1), :]
            vrow = rec.v_ref[pl.ds(r0 + j, 1), :]
            keep = (row_id <= j) if reverse else (row_id >= j)
            e = jnp.exp(jnp.minimum(cum - crow, 0.0))
            p = jnp.where(keep, q * krow * e, 0.0)
            return acc + _dot(p, rec.bmat[...]) * vrow

        o = lax.fori_loop(0, CHUNK, key_row, o)
    ke_b = (k * jnp.exp(tot - cum)).astype(BF16)
    ds = lax.dot_general(v_b, ke_b, _TN, preferred_element_type=F32)
    rec.st_scr[d] = st * jnp.exp(tot) + ds * rec.bdw_f[...]
    rec.oacc[d, rows, :] = o


def _load_state(rec, s0_ref, d):
    x = jnp.concatenate([s0_ref[0, 0, d, h] for h in range(N_HEADS)], axis=0)
    y = _dot_exact_rhs(x, rec.tile_t[...]) * rec.bmat[...].astype(F32)
    rec.st_scr[d] = y.T


def _store_state(rec, sfin_ref, d):
    y = rec.st_scr[d].T
    x = _dot_exact_rhs(y, rec.tile_tt[...])
    dk = rec.width // N_HEADS
    for h in range(N_HEADS):
        sfin_ref[0, d, h] = x[h * dk:(h + 1) * dk, :]


def _run_recurrence(rec, ok, s0_ref, sfin_ref, seq_len):
    nchunk = seq_len // CHUNK
    ngroup = nchunk // GROUP
    for d in (0, 1):
        if s0_ref is None:
            rec.st_scr[d] = jnp.zeros(rec.st_scr.shape[1:], F32)
        else:
            _load_state(rec, s0_ref, d)

    def fast_all():
        def group(gi, carry):
            for u in range(GROUP):
                s = gi * GROUP + u
                _chunk_step(rec, 0, s, True)
                _chunk_step(rec, 1, nchunk - 1 - s, True)
            return carry

        if ngroup == 1:
            group(0, 0)
        else:
            lax.fori_loop(0, ngroup, group, 0)

    def direct_all():
        def one(s, carry):
            _chunk_step(rec, 0, s, False)
            _chunk_step(rec, 1, nchunk - 1 - s, False)
            return carry

        lax.fori_loop(0, nchunk, one, 0)

    lax.cond(ok, fast_all, direct_all)
    if sfin_ref is not None:
        for d in (0, 1):
            _store_state(rec, sfin_ref, d)


def _head_norm_gate(rec, gate_ref, gain_ref, o_ref, seq_len):
    for rb in range(seq_len // ROW_BLOCK):
        rows = slice(rb * ROW_BLOCK, (rb + 1) * ROW_BLOCK)
        o = rec.oacc[0, rows, :] + rec.oacc[1, rows, :]
        ms = jnp.dot((o * o).astype(BF16), rec.bdv_b[...], preferred_element_type=F32) * (1.0 / HEAD_V)
        y = o * lax.rsqrt(ms + EPS) * gain_ref[0] * _silu(gate_ref[rows, :])
        o_ref[rows, :] = y.astype(o_ref.dtype)


def _split_refs(refs, n_in, has_s0, want_final):
    ins = refs[:n_in]
    pos = n_in
    s0_ref = None
    if has_s0:
        s0_ref = refs[pos]
        pos += 1
    o_ref = refs[pos]
    pos += 1
    sfin_ref = None
    if want_final:
        sfin_ref = refs[pos]
        pos += 1
    return ins, s0_ref, o_ref, sfin_ref, refs[pos:]


def _hgrn_kernel(*refs, seq_len, layer, has_s0, want_final):
    ins, s0_ref, o_ref, sfin_ref, scr = _split_refs(refs, 15, has_s0, want_final)
    q_ref, i_ref, ff_ref, fb_ref, g_ref, lbl_ref, gain_ref = ins[:7]
    k_scr, cum_scr, oacc, st_scr = scr
    q_at = lambda r0: q_ref[pl.ds(r0, CHUNK), :]
    k_at = lambda d, r0, n=CHUNK: k_scr[d, pl.ds(r0, n), :]
    rec = _Rec(q_at, k_at, i_ref, cum_scr, st_scr, oacc, ins[7:], W_GRP)

    lg = lbl_ref[...]
    mx = lg[0]
    for i in range(1, DEPTH):
        mx = jnp.maximum(mx, lg[i])
    ex = [jnp.exp(lg[i] - mx) for i in range(DEPTH)]
    den = ex[0]
    for i in range(1, DEPTH):
        den = den + ex[i]
    sm = [e / den for e in ex]
    csum = sm[0]
    for i in range(1, layer + 1):
        csum = csum + sm[i]
    lb = csum - sm[0]

    def pre(bi, rm):
        r0 = bi * GROWS if isinstance(bi, int) else pl.multiple_of(bi * GROWS, GROWS)
        rows = pl.ds(r0, GROWS)
        for d, f_ref in ((0, ff_ref), (1, fb_ref)):
            lbd = lb[d:d + 1, :]
            f = lbd + (1.0 - lbd) * _sigmoid(f_ref[rows, :])
            g = jnp.log(jnp.maximum(f, F_FLOOR))
            k_scr[d, rows, :] = 1.0 - f
            cum = _block_cumsum(rec.tri_ref[d], g)
            cum_scr[d, rows, :] = cum
            rm = _block_ranges(cum, d == 1, rm)
        return rm

    rmax = jnp.zeros((1, W_GRP), F32)
    nblock = seq_len // GROWS
    rmax = pre(0, rmax) if nblock == 1 else lax.fori_loop(0, nblock, pre, rmax)
    ok = jnp.max(rmax) < SAFE_RANGE

    _run_recurrence(rec, ok, s0_ref, sfin_ref, seq_len)
    _head_norm_gate(rec, g_ref, gain_ref, o_ref, seq_len)


def _gla_kernel(*refs, seq_len, has_s0, want_final):
    ins, s0_ref, o_ref, sfin_ref, scr = _split_refs(refs, 16, has_s0, want_final)
    q_ref, k_ref, v_ref, g_ref, zf_ref, zb_ref, ba_ref, gain_ref = ins[:8]
    cum_scr, oacc, st_scr = scr
    width = N_HEADS * GLA_DK
    q_at = lambda r0: q_ref[pl.ds(r0, CHUNK), :] * (GLA_DK ** -0.5)
    k_at = lambda d, r0, n=CHUNK: k_ref[pl.ds(r0, n), :]
    rec = _Rec(q_at, k_at, v_ref, cum_scr, st_scr, oacc, ins[8:], width)

    def pre(bi, rm):
        r0 = bi * GROWS if isinstance(bi, int) else pl.multiple_of(bi * GROWS, GROWS)
        rows = pl.ds(r0, GROWS)
        for d, z_ref in ((0, zf_ref), (1, zb_ref)):
            z = z_ref[rows, :] + ba_ref[0, d:d + 1, :]
            g = -_softplus(-z) * (1.0 / GLA_NORMALIZER)
            cum = _block_cumsum(rec.tri_ref[d], g)
            cum_scr[d, rows, :] = cum
            rm = _block_ranges(cum, d == 1, rm)
        return rm

    rmax = jnp.zeros((1, width), F32)
    nblock = seq_len // GROWS
    rmax = pre(0, rmax) if nblock == 1 else lax.fori_loop(0, nblock, pre, rmax)
    ok = jnp.max(rmax) < SAFE_RANGE

    _run_recurrence(rec, ok, s0_ref, sfin_ref, seq_len)
    _head_norm_gate(rec, g_ref, gain_ref, o_ref, seq_len)


def _recurrence_constants(width):
    dk = width // N_HEADS
    r = np.arange(N_HEADS * CHUNK)[:, None]
    tri = np.kron(np.eye(GROUP, dtype=np.float32), np.tril(np.ones((CHUNK, CHUNK), np.float32)))
    tri = np.stack([tri, tri.T])
    bdw = (r // CHUNK == np.arange(width)[None, :] // dk).astype(np.float32)
    bdv = (r // CHUNK == np.arange(W_GRP)[None, :] // HEAD_V).astype(np.float32)
    t = np.arange(CHUNK)[:, None]
    s = np.arange(N_HEADS * CHUNK)[None, :] % CHUNK
    cmask = np.stack([(s <= t), (s >= t)]).astype(np.float32)
    bmat = (np.arange(width)[:, None] // dk == np.arange(W_GRP)[None, :] // HEAD_V).astype(np.float32)
    tile_t = np.tile(np.eye(HEAD_V, dtype=np.float32), (1, N_HEADS))
    return (jnp.asarray(tri, BF16), jnp.asarray(bdw, BF16), jnp.asarray(bdw), jnp.asarray(bdv, BF16),
            jnp.asarray(cmask), jnp.asarray(bmat, BF16), jnp.asarray(tile_t, BF16),
            jnp.asarray(tile_t.T, BF16))


def _recurrent_mixer(kernel_fn, name, width, args, in_specs, state, layer, nseq, seq_len, want_final,
                     extra_scratch):
    dk = width // N_HEADS
    consts = _recurrence_constants(width)
    args = list(args) + list(consts)
    in_specs = list(in_specs) + [_full_spec(c) for c in consts]
    if state is not None:
        args.append(state)
        in_specs.append(pl.BlockSpec((1, 1, 2, N_HEADS, dk, HEAD_V), lambda b: (b, layer, 0, 0, 0, 0)))
    out_shape = [jax.ShapeDtypeStruct((nseq * seq_len, W_GRP), BF16)]
    out_specs = [pl.BlockSpec((seq_len, W_GRP), lambda b: (b, 0))]
    if want_final:
        out_shape.append(jax.ShapeDtypeStruct((nseq, 2, N_HEADS, dk, HEAD_V), F32))
        out_specs.append(pl.BlockSpec((1, 2, N_HEADS, dk, HEAD_V), lambda b: (b, 0, 0, 0, 0)))
    res = pl.pallas_call(
        functools.partial(kernel_fn, seq_len=seq_len, has_s0=state is not None, want_final=want_final),
        grid=(nseq,),
        in_specs=in_specs,
        out_specs=out_specs,
        out_shape=out_shape,
        scratch_shapes=extra_scratch + [pltpu.VMEM((2, seq_len, width), F32),
                                        pltpu.VMEM((2, seq_len, W_GRP), F32),
                                        pltpu.VMEM((2, W_GRP, width), F32)],
        compiler_params=_params(("arbitrary",)),
        name=name,
    )(*args)
    return res[0], (res[1] if want_final else None)


def _hgrn_mixer(proj, lb_logits, gain, state, layer, nseq, seq_len, want_final):
    col = lambda c: pl.BlockSpec((seq_len, W_GRP), lambda b: (b, c))
    args = [proj] * 5 + [lb_logits, gain.reshape(DEPTH, 1, W_GRP)]
    in_specs = [col(0), col(1), col(2), col(3), col(4), _full_spec(lb_logits),
                pl.BlockSpec((1, 1, W_GRP), lambda b: (layer, 0, 0))]
    return _recurrent_mixer(functools.partial(_hgrn_kernel, layer=layer), "hgrn2_mixer", W_GRP, args,
                            in_specs, state, layer, nseq, seq_len, want_final,
                            [pltpu.VMEM((2, seq_len, W_GRP), F32)])


def _gla_mixer(proj, ba2, gain, state, layer, nseq, seq_len, want_final):
    width = N_HEADS * GLA_DK
    col256 = lambda c: pl.BlockSpec((seq_len, W_GRP), lambda b: (b, c))
    col128 = lambda c: pl.BlockSpec((seq_len, width), lambda b: (b, c))
    args = [proj] * 6 + [ba2, gain.reshape(DEPTH, 1, W_GRP)]
    in_specs = [col128(10), col128(11), col256(6), col256(7), col128(16), col128(17),
                pl.BlockSpec((1, 2, width), lambda b: (layer, 0, 0)),
                pl.BlockSpec((1, 1, W_GRP), lambda b: (layer, 0, 0))]
    return _recurrent_mixer(_gla_kernel, "gla_mixer", width, args, in_specs, state, layer, nseq, seq_len,
                            want_final, [])


PAD = 8
SCAN_TILE = 8


def _rgsc_kernel(*refs, seq_len, seg, has_h0, want_final):
    (cx_ref, cg_ref, db_ref, dc_ref, dv_ref, convw_ref, convb_ref, wr_ref, br_ref, wi_ref,
     bi_ref, lam_ref, sw_ref) = refs[:13]
    pos = 13
    h0_ref = None
    if has_h0:
        h0_ref = refs[pos]
        pos += 1
    oc_ref, od_ref = refs[pos:pos + 2]
    pos += 2
    hfin_ref = None
    if want_final:
        hfin_ref = refs[pos]
        pos += 1
    upad, a_scr, b_scr, h_scr = refs[pos:]

    zpad = jnp.zeros((PAD, W_GRP), F32)
    upad[0:PAD, :] = zpad
    upad[PAD + seq_len:2 * PAD + seq_len, :] = zpad
    upad[PAD:PAD + seq_len, :] = cx_ref[...]
    rows8 = lax.broadcasted_iota(jnp.int32, (SCAN_TILE, W_GRP), 0)
    ntile = seq_len // SCAN_TILE

    for d in (0, 1):
        reverse = d == 1
        w = convw_ref[0, d]
        offs = [PAD + 3, PAD + 2, PAD + 1, PAD] if reverse else [PAD - 3, PAD - 2, PAD - 1, PAD]
        sp = _softplus(-lam_ref[0, d:d + 1, :])
        for rb in range(seq_len // ROW_BLOCK):
            base = rb * ROW_BLOCK
            xc = convb_ref[0, d:d + 1, :]
            for j in range(4):
                xc = xc + w[j:j + 1, :] * upad[offs[j] + base:offs[j] + base + ROW_BLOCK, :]
            r = _sigmoid(_dot(xc, wr_ref[d]) + br_ref[0, d:d + 1, :])
            ig = _sigmoid(_dot(xc, wi_ref[d]) + bi_ref[0, d:d + 1, :])
            log_a = -RG_C * r * sp
            a = jnp.exp(log_a)
            a_scr[base:base + ROW_BLOCK, :] = a
            b_scr[base:base + ROW_BLOCK, :] = jnp.sqrt(1.0 - a * a) * (ig * xc)

        h0 = h0_ref[0, 0, d:d + 1, :] if has_h0 else jnp.zeros((1, W_GRP), F32)

        def step(ti, carry, d=d, reverse=reverse):
            t = (ntile - 1 - ti) if reverse else ti
            r0 = pl.multiple_of(t * SCAN_TILE, SCAN_TILE)
            a = a_scr[pl.ds(r0, SCAN_TILE), :]
            b = b_scr[pl.ds(r0, SCAN_TILE), :]
            for s in (1, 2, 4):
                if reverse:
                    keep = rows8 <= SCAN_TILE - 1 - s
                    shift = SCAN_TILE - s
                else:
                    keep = rows8 >= s
                    shift = s
                a_s = jnp.where(keep, pltpu.roll(a, shift, 0), 1.0)
                b_s = jnp.where(keep, pltpu.roll(b, shift, 0), 0.0)
                b = b + a * b_s
                a = a * a_s
            h = a * carry + b
            if d == 0:
                h_scr[pl.ds(r0, SCAN_TILE), :] = h
            else:
                h_scr[pl.ds(r0, SCAN_TILE), :] += h
            return h[0:1, :] if reverse else h[SCAN_TILE - 1:SCAN_TILE, :]

        hlast = lax.fori_loop(0, ntile, step, h0, unroll=4)
        if want_final:
            hfin_ref[0, d:d + 1, :] = hlast

    sw = sw_ref[0]
    for rb in range(seq_len // ROW_BLOCK):
        rows = slice(rb * ROW_BLOCK, (rb + 1) * ROW_BLOCK)
        x = cg_ref[rows, :]
        gelu = 0.5 * x * (1.0 + jnp.tanh(0.7978845608028654 * (x + 0.044715 * (x * x * x))))
        oc_ref[rows, :] = (h_scr[rows, :] * gelu).astype(oc_ref.dtype)
        upad[PAD + rb * ROW_BLOCK:PAD + (rb + 1) * ROW_BLOCK, :] = dc_ref[rows, :] * dv_ref[rows, :]
    for rb in range(seq_len // ROW_BLOCK):
        base = rb * ROW_BLOCK
        rows = slice(base, base + ROW_BLOCK)
        posn = (lax.broadcasted_iota(jnp.int32, (ROW_BLOCK, W_GRP), 0) + base) % seg
        left = jnp.where(posn != 0, upad[PAD - 1 + base:PAD - 1 + base + ROW_BLOCK, :], 0.0)
        right = jnp.where(posn != seg - 1, upad[PAD + 1 + base:PAD + 1 + base + ROW_BLOCK, :], 0.0)
        y = sw[0:1, :] * left + sw[1:2, :] * upad[PAD + base:PAD + base + ROW_BLOCK, :] + sw[2:3, :] * right
        od_ref[rows, :] = (db_ref[rows, :] * y).astype(od_ref.dtype)


def _rgsc_mixer(proj, conv_w, conv_b, wr_bd, b_r, wi_bd, b_i, lam, sconv_w, h0, layer, nseq, seq_len,
                seg, want_final):
    col = lambda c: pl.BlockSpec((seq_len, W_GRP), lambda b: (b, c))
    lay3 = lambda a: pl.BlockSpec((1,) + a.shape[1:], lambda b: (layer, 0, 0))
    lay4 = lambda a: pl.BlockSpec((1,) + a.shape[1:], lambda b: (layer, 0, 0, 0))
    params = [conv_w, conv_b, wr_bd, b_r, wi_bd, b_i, lam, sconv_w]
    args = [proj] * 5 + params
    in_specs = [col(9), col(10), col(11), col(12), col(13), lay4(conv_w), lay3(conv_b), _full_spec(wr_bd),
                lay3(b_r), _full_spec(wi_bd), lay3(b_i), lay3(lam), lay3(sconv_w)]
    if h0 is not None:
        args.append(h0)
        in_specs.append(pl.BlockSpec((1, 1, 2, W_GRP), lambda b: (b, layer, 0, 0)))
    out_shape = [jax.ShapeDtypeStruct((nseq * seq_len, W_GRP), BF16)] * 2
    out_specs = [pl.BlockSpec((seq_len, W_GRP), lambda b: (b, 0))] * 2
    if want_final:
        out_shape.append(jax.ShapeDtypeStruct((nseq, 2, W_GRP), F32))
        out_specs.append(pl.BlockSpec((1, 2, W_GRP), lambda b: (b, 0, 0)))
    res = pl.pallas_call(
        functools.partial(_rgsc_kernel, seq_len=seq_len, seg=seg, has_h0=h0 is not None,
                          want_final=want_final),
        grid=(nseq,),
        in_specs=in_specs,
        out_specs=out_specs,
        out_shape=out_shape,
        scratch_shapes=[pltpu.VMEM((seq_len + 2 * PAD, W_GRP), F32), pltpu.VMEM((seq_len, W_GRP), F32),
                        pltpu.VMEM((seq_len, W_GRP), F32), pltpu.VMEM((seq_len, W_GRP), F32)],
        compiler_params=_params(("arbitrary",)),
        name="rglru_sconv_mixer",
    )(*args)
    return res[0], res[1], (res[2] if want_final else None)


def _block_diag(w):
    two, h, c, _ = w.shape
    eye = jnp.eye(h, dtype=w.dtype)
    full = w[:, :, :, None, :] * eye[None, :, None, :, None]
    return full.reshape(two, h * c, h * c)


def kernel(x_prompt, x_sample, state_hgrn, state_gla, state_rglru, c, c_ctx, norm1_g, norm2_g, ada_w, ada_b, w_in, w_out, hgrn_lb_logits, hgrn_norm_g, gla_wa2, gla_ba2, gla_norm_g, rg_conv_w, rg_conv_b, rg_w_r, rg_b_r, rg_w_i, rg_b_i, rg_lambda, sconv_w, mlp_w1, mlp_w2, final_norm_g):
    b_ctx, l_ctx, _ = x_prompt.shape
    b_lat, l_lat, _ = x_sample.shape

    cvec = jnp.concatenate([c, c_ctx[None, :], jnp.zeros((8 - b_lat - 1, D_MODEL), F32)], axis=0)
    mod = _modulation(cvec, ada_w, ada_b).reshape(DEPTH, 8, 6, D_MODEL)

    a0 = 2048
    w_a = jnp.stack([w_in[:, :, a0:a0 + GLA_RANK], w_in[:, :, a0 + GLA_RANK:a0 + 2 * GLA_RANK]], axis=1)
    w_z = _fold_lowrank(w_a, gla_wa2)
    w_cat = jnp.concatenate([w_in[:, :, :a0], w_z[:, 0], w_z[:, 1], w_in[:, :, a0 + 2 * GLA_RANK:]],
                            axis=-1).astype(BF16)
    w_out_b = w_out.astype(BF16)
    w1_b = mlp_w1.astype(BF16)
    w2_b = mlp_w2.astype(BF16)

    xp = x_prompt.reshape(b_ctx * l_ctx, D_MODEL)
    xs = x_sample.reshape(b_lat * l_lat, D_MODEL)
    new_h, new_g, new_r = [], [], []
    for l in range(DEPTH):
        wr_bd = _block_diag(rg_w_r[l]).astype(BF16)
        wi_bd = _block_diag(rg_w_i[l]).astype(BF16)
        mod_ctx = mod[l, b_lat:b_lat + 1]
        mod_lat = mod[l, :b_lat]
        last = l == DEPTH - 1
        streams = (
            (xp, mod_ctx, b_ctx, l_ctx, l_ctx, None, None, None, True),
            (xs, mod_lat, b_lat, l_lat, GRID_W, state_hgrn, state_gla, state_rglru, False),
        )
        outs = []
        for (x, m, nseq, seq_len, seg, s_h, s_g, s_r, want_final) in streams:
            proj = _project(x, m, norm1_g, w_cat, l)
            o_a, f_h = _hgrn_mixer(proj, hgrn_lb_logits, hgrn_norm_g, s_h, l, nseq, seq_len, want_final)
            o_b, f_g = _gla_mixer(proj, gla_ba2, gla_norm_g, s_g, l, nseq, seq_len, want_final)
            o_c, o_d, f_r = _rgsc_mixer(proj, rg_conv_w, rg_conv_b, wr_bd, rg_b_r, wi_bd, rg_b_i,
                                        rg_lambda, sconv_w, s_r, l, nseq, seq_len, seg, want_final)
            x_new = _out_mlp(x, o_a, o_b, o_c, o_d, m, norm2_g, final_norm_g, w_out_b, w1_b, w2_b, l, last)
            outs.append((x_new, f_h, f_g, f_r))
        xp, f_h, f_g, f_r = outs[0]
        xs = outs[1][0]
        new_h.append(f_h)
        new_g.append(f_g)
        new_r.append(f_r)
    y_prompt = xp.reshape(b_ctx, l_ctx, D_MODEL)
    y_sample = xs.reshape(b_lat, l_lat, D_MODEL)
    return (y_prompt, y_sample, jnp.stack(new_h, axis=1), jnp.stack(new_g, axis=1),
            jnp.stack(new_r, axis=1))
```

```python
import functools

import numpy as np
import jax
import jax.numpy as jnp
from jax import lax
from jax.experimental import pallas as pl
from jax.experimental.pallas import tpu as pltpu

F32 = jnp.float32
BF16 = jnp.bfloat16

D_MODEL = 1024
DEPTH = 2
GRID_W = 64
N_HEADS = 4
W_GRP = 256
HEAD_V = 64
HGRN_DK = 64
GLA_DK = 32
GLA_RANK = 16
GLA_NORMALIZER = 16.0
RG_C = 8.0
D_FF = 4 * D_MODEL
EPS = 1e-6
F_FLOOR = 1e-20

PROJ_COLS = 3584

CHUNK = 64
HALF = CHUNK // 2
assert CHUNK == HEAD_V
GROUP = 4
GROWS = GROUP * CHUNK
SAFE_RANGE = 80.0
TOK_TILE = 1024
PROJ_TM = 512
PROJ_TN = 512
FF_TILE = 512
ROW_BLOCK = 256
VMEM_LIMIT = 56 * 1024 * 1024

_NT = (((1,), (1,)), ((), ()))
_TN = (((0,), (0,)), ((), ()))


def _dot(a, b):
    return jnp.dot(a.astype(BF16), b.astype(BF16), preferred_element_type=F32)


def _dot_nt(a, b):
    return lax.dot_general(a.astype(BF16), b.astype(BF16), _NT, preferred_element_type=F32)


def _dot_tn(a, b):
    return lax.dot_general(a.astype(BF16), b.astype(BF16), _TN, preferred_element_type=F32)


def _sigmoid(x):
    return 1.0 / (1.0 + jnp.exp(-x))


def _silu(x):
    return x * _sigmoid(x)


def _log1p(y):
    u = 1.0 + y
    return jnp.where(u == 1.0, y, jnp.log(u) * (y / (u - 1.0)))


def _softplus(x):
    return jnp.maximum(x, 0.0) + _log1p(jnp.exp(-jnp.abs(x)))


def _rms(x):
    return x * lax.rsqrt(jnp.mean(x * x, axis=-1, keepdims=True) + EPS)


def _params(sem):
    return pltpu.CompilerParams(dimension_semantics=sem, vmem_limit_bytes=VMEM_LIMIT)


def _full_spec(a):
    zeros = (0,) * a.ndim
    return pl.BlockSpec(a.shape, lambda *_: zeros)


def _mod_kernel(c_ref, w_ref, b_ref, o_ref):
    s = _silu(c_ref[...])
    o_ref[0] = _dot(s, w_ref[0]) + b_ref[0]


def _modulation(cvec, ada_w, ada_b):
    tn = 2048
    return pl.pallas_call(
        _mod_kernel,
        grid=(DEPTH, 6 * D_MODEL // tn),
        in_specs=[pl.BlockSpec((8, D_MODEL), lambda l, j: (0, 0)),
                  pl.BlockSpec((1, D_MODEL, tn), lambda l, j: (l, 0, j)),
                  pl.BlockSpec((1, 1, tn), lambda l, j: (l, 0, j))],
        out_specs=pl.BlockSpec((1, 8, tn), lambda l, j: (l, 0, j)),
        out_shape=jax.ShapeDtypeStruct((DEPTH, 8, 6 * D_MODEL), F32),
        compiler_params=_params(("arbitrary", "arbitrary")),
        name="adaln_modulation",
    )(cvec, ada_w, ada_b.reshape(DEPTH, 1, 6 * D_MODEL))


def _fold_kernel(a_ref, b_ref, o_ref):
    o_ref[0, 0] = jnp.dot(a_ref[0, 0], b_ref[0, 0], preferred_element_type=F32,
                          precision=lax.Precision.HIGHEST)


def _fold_lowrank(w_a, wa2):
    n = N_HEADS * GLA_DK
    return pl.pallas_call(
        _fold_kernel,
        grid=(DEPTH, 2),
        in_specs=[pl.BlockSpec((1, 1, D_MODEL, GLA_RANK), lambda l, d: (l, d, 0, 0)),
                  pl.BlockSpec((1, 1, GLA_RANK, n), lambda l, d: (l, d, 0, 0))],
        out_specs=pl.BlockSpec((1, 1, D_MODEL, n), lambda l, d: (l, d, 0, 0)),
        out_shape=jax.ShapeDtypeStruct((DEPTH, 2, D_MODEL, n), F32),
        compiler_params=_params(("arbitrary", "arbitrary")),
        name="gla_lowrank_fold",
    )(w_a, wa2)


def _proj_kernel(x_ref, mod_ref, g_ref, w_ref, o_ref):
    shift = mod_ref[0, 0:1, :]
    scale = mod_ref[0, 1:2, :]
    h = (_rms(x_ref[...]) * g_ref[0] * (1.0 + scale) + shift).astype(BF16)
    for n in range(PROJ_COLS // PROJ_TN):
        cols = slice(n * PROJ_TN, (n + 1) * PROJ_TN)
        o_ref[:, cols] = jnp.dot(h, w_ref[0, :, cols], preferred_element_type=F32)


def _mod_index(ntok, nmod, tile):
    tiles_per_mod = (ntok // tile) // nmod if nmod > 1 else 1
    if nmod > 1:
        return lambda i, *_: (i // tiles_per_mod, 0, 0)
    return lambda i, *_: (0, 0, 0)


def _project(x, mod, norm_g, w_cat, layer):
    ntok = x.shape[0]
    return pl.pallas_call(
        _proj_kernel,
        grid=(ntok // PROJ_TM,),
        in_specs=[pl.BlockSpec((PROJ_TM, D_MODEL), lambda i: (i, 0)),
                  pl.BlockSpec((1, 6, D_MODEL), _mod_index(ntok, mod.shape[0], PROJ_TM)),
                  pl.BlockSpec((1, 1, D_MODEL), lambda i: (layer, 0, 0)),
                  pl.BlockSpec((1, D_MODEL, PROJ_COLS), lambda i: (layer, 0, 0))],
        out_specs=pl.BlockSpec((PROJ_TM, PROJ_COLS), lambda i: (i, 0)),
        out_shape=jax.ShapeDtypeStruct((ntok, PROJ_COLS), F32),
        compiler_params=_params(("arbitrary",)),
        name="norm_in_proj",
    )(x, mod, norm_g.reshape(DEPTH, 1, D_MODEL), w_cat)


def _mlp_kernel(x_ref, oa_ref, ob_ref, oc_ref, od_ref, mod_ref, g2_ref, fg_ref, wout_ref,
                w1_ref, w2_ref, out_ref, h2_scr, *, final_norm):
    j = pl.program_id(1)

    @pl.when(j == 0)
    def _():
        mixed = jnp.concatenate([oa_ref[...], ob_ref[...], oc_ref[...], od_ref[...]], axis=1)
        mix = jnp.dot(mixed, wout_ref[0].astype(BF16), preferred_element_type=F32)
        x1 = x_ref[...] + mod_ref[0, 2:3, :] * mix
        out_ref[...] = x1
        h2 = _rms(x1) * g2_ref[0] * (1.0 + mod_ref[0, 4:5, :]) + mod_ref[0, 3:4, :]
        h2_scr[...] = h2.astype(BF16)

    t = jnp.dot(h2_scr[...], w1_ref[0].astype(BF16), preferred_element_type=F32)
    t = jnp.square(jnp.maximum(t, 0.0))
    out_ref[...] += mod_ref[0, 5:6, :] * _dot(t, w2_ref[0])

    if final_norm:
        @pl.when(j == pl.num_programs(1) - 1)
        def _():
            out_ref[...] = _rms(out_ref[...]) * fg_ref[...]


def _out_mlp(x, o_a, o_b, o_c, o_d, mod, norm2_g, final_g, w_out, w1, w2, layer, final_norm):
    ntok = x.shape[0]
    tok = lambda i, j: (i, 0)
    return pl.pallas_call(
        functools.partial(_mlp_kernel, final_norm=final_norm),
        grid=(ntok // TOK_TILE, D_FF // FF_TILE),
        in_specs=[pl.BlockSpec((TOK_TILE, D_MODEL), tok),
                  pl.BlockSpec((TOK_TILE, W_GRP), tok),
                  pl.BlockSpec((TOK_TILE, W_GRP), tok),
                  pl.BlockSpec((TOK_TILE, W_GRP), tok),
                  pl.BlockSpec((TOK_TILE, W_GRP), tok),
                  pl.BlockSpec((1, 6, D_MODEL), _mod_index(ntok, mod.shape[0], TOK_TILE)),
                  pl.BlockSpec((1, 1, D_MODEL), lambda i, j: (layer, 0, 0)),
                  pl.BlockSpec((1, D_MODEL), lambda i, j: (0, 0)),
                  pl.BlockSpec((1, D_MODEL, D_MODEL), lambda i, j: (layer, 0, 0)),
                  pl.BlockSpec((1, D_MODEL, FF_TILE), lambda i, j: (layer, 0, j)),
                  pl.BlockSpec((1, FF_TILE, D_MODEL), lambda i, j: (layer, j, 0))],
        out_specs=pl.BlockSpec((TOK_TILE, D_MODEL), tok),
        out_shape=jax.ShapeDtypeStruct((ntok, D_MODEL), F32),
        scratch_shapes=[pltpu.VMEM((TOK_TILE, D_MODEL), BF16)],
        compiler_params=_params(("arbitrary", "arbitrary")),
        name="out_proj_mlp",
    )(x, o_a, o_b, o_c, o_d, mod, norm2_g.reshape(DEPTH, 1, D_MODEL), final_g.reshape(1, D_MODEL),
      w_out, w1, w2)


def _split3(x):
    hi = x.astype(BF16)
    r1 = x - hi.astype(F32)
    mid = r1.astype(BF16)
    lo = (r1 - mid.astype(F32)).astype(BF16)
    return hi, mid, lo


def _dot_exact_rhs(x, m):
    hi, mid, lo = _split3(x)
    s = jnp.dot(hi, m, preferred_element_type=F32)
    s += jnp.dot(mid, m, preferred_element_type=F32)
    s += jnp.dot(lo, m, preferred_element_type=F32)
    return s


def _block_cumsum(tri, g):
    hi, mid, lo = _split3(g)
    s = jnp.dot(tri, hi, preferred_element_type=F32)
    s += jnp.dot(tri, mid, preferred_element_type=F32)
    s += jnp.dot(tri, lo, preferred_element_type=F32)
    return s


def _half_ranges(cum, reverse):
    if reverse:
        second = -cum[HALF:HALF + 1, :]
        first = -(cum[0:1, :] - cum[HALF:HALF + 1, :])
    else:
        first = -cum[HALF - 1:HALF, :]
        second = -(cum[CHUNK - 1:CHUNK, :] - cum[HALF - 1:HALF, :])
    return jnp.maximum(first, second)


def _block_ranges(cum, reverse, rm):
    for c in range(GROUP):
        rm = jnp.maximum(rm, _half_ranges(cum[c * CHUNK:(c + 1) * CHUNK, :], reverse))
    return rm


def _row_start(c):
    r0 = c * CHUNK
    return r0 if isinstance(r0, int) else pl.multiple_of(r0, CHUNK)


class _Rec:
    def __init__(self, q_at, k_at, v_ref, cum_scr, st_scr, oacc, consts, width):
        self.q_at, self.k_at, self.v_ref = q_at, k_at, v_ref
        self.cum_scr, self.st_scr, self.oacc = cum_scr, st_scr, oacc
        (self.tri_ref, self.bdw_b, self.bdw_f, self.bdv_b, self.cmask, self.bmat,
         self.tile_t, self.tile_tt) = consts
        self.width = width


def _chunk_step(rec, d, c, fast):
    reverse = d == 1
    mid_row = HALF if reverse else HALF - 1
    last_row = 0 if reverse else CHUNK - 1
    r0 = _row_start(c)
    rows = pl.ds(r0, CHUNK)
    q = rec.q_at(r0)
    k = rec.k_at(d, r0)
    v_b = rec.v_ref[rows, :].astype(BF16)
    cum = rec.cum_scr[d, rows, :]
    tot = cum[last_row:last_row + 1, :]
    st = rec.st_scr[d]
    o = _dot_nt(q * jnp.exp(cum), st)
    if fast:
        cm = cum[mid_row:mid_row + 1, :]
        qm = q * jnp.exp(cum - cm)
        km_b = (k * jnp.exp(cm - cum)).astype(BF16)
        kbd = jnp.concatenate([km_b] * N_HEADS, axis=0) * rec.bdw_b[...]
        sc = lax.dot_general(qm.astype(BF16), kbd, _NT, preferred_element_type=F32)
        a = jnp.where(rec.cmask[d] > 0.5, sc, 0.0)
        vexp = jnp.concatenate([v_b] * N_HEADS, axis=0) * rec.bdv_b[...]
        o = o + jnp.dot(a.astype(BF16), vexp, preferred_element_type=F32)
    else:
        row_id = lax.broadcasted_iota(jnp.int32, (CHUNK, rec.width), 0)

        def key_row(j, acc):
            krow = rec.k_at(d, r0 + j, 1)
            crow = rec.cum_scr[d, pl.ds(r0 + j, 1), :]
            vrow = rec.v_ref[pl.ds(r0 + j, 1), :]
            keep = (row_id <= j) if reverse else (row_id >= j)
            e = jnp.exp(jnp.minimum(cum - crow, 0.0))
            p = jnp.where(keep, q * krow * e, 0.0)
            return acc + _dot(p, rec.bmat[...]) * vrow

        o = lax.fori_loop(0, CHUNK, key_row, o)
    ke_b = (k * jnp.exp(tot - cum)).astype(BF16)
    ds = lax.dot_general(v_b, ke_b, _TN, preferred_element_type=F32)
    rec.st_scr[d] = st * jnp.exp(tot) + ds * rec.bdw_f[...]
    rec.oacc[d, rows, :] = o


def _load_state(rec, s0_ref, d):
    x = jnp.concatenate([s0_ref[0, 0, d, h] for h in range(N_HEADS)], axis=0)
    y = _dot_exact_rhs(x, rec.tile_t[...]) * rec.bmat[...].astype(F32)
    rec.st_scr[d] = y.T


def _store_state(rec, sfin_ref, d):
    y = rec.st_scr[d].T
    x = _dot_exact_rhs(y, rec.tile_tt[...])
    dk = rec.width // N_HEADS
    for h in range(N_HEADS):
        sfin_ref[0, d, h] = x[h * dk:(h + 1) * dk, :]


def _run_recurrence(rec, ok, s0_ref, sfin_ref, seq_len):
    nchunk = seq_len // CHUNK
    ngroup = nchunk // GROUP
    for d in (0, 1):
        if s0_ref is None:
            rec.st_scr[d] = jnp.zeros(rec.st_scr.shape[1:], F32)
        else:
            _load_state(rec, s0_ref, d)

    def fast_all():
        def group(gi, carry):
            for u in range(GROUP):
                s = gi * GROUP + u
                _chunk_step(rec, 0, s, True)
                _chunk_step(rec, 1, nchunk - 1 - s, True)
            return carry

        if ngroup == 1:
            group(0, 0)
        else:
            lax.fori_loop(0, ngroup, group, 0)

    def direct_all():
        def one(s, carry):
            _chunk_step(rec, 0, s, False)
            _chunk_step(rec, 1, nchunk - 1 - s, False)
            return carry

        lax.fori_loop(0, nchunk, one, 0)

    lax.cond(ok, fast_all, direct_all)
    if sfin_ref is not None:
        for d in (0, 1):
            _store_state(rec, sfin_ref, d)


def _head_norm_gate(rec, gate_ref, gain_ref, o_ref, seq_len):
    for rb in range(seq_len // ROW_BLOCK):
        rows = slice(rb * ROW_BLOCK, (rb + 1) * ROW_BLOCK)
        o = rec.oacc[0, rows, :] + rec.oacc[1, rows, :]
        ms = jnp.dot((o * o).astype(BF16), rec.bdv_b[...], preferred_element_type=F32) * (1.0 / HEAD_V)
        y = o * lax.rsqrt(ms + EPS) * gain_ref[0] * _silu(gate_ref[rows, :])
        o_ref[rows, :] = y.astype(o_ref.dtype)


def _split_refs(refs, n_in, has_s0, want_final):
    ins = refs[:n_in]
    pos = n_in
    s0_ref = None
    if has_s0:
        s0_ref = refs[pos]
        pos += 1
    o_ref = refs[pos]
    pos += 1
    sfin_ref = None
    if want_final:
        sfin_ref = refs[pos]
        pos += 1
    return ins, s0_ref, o_ref, sfin_ref, refs[pos:]


def _hgrn_kernel(*refs, seq_len, layer, has_s0, want_final):
    ins, s0_ref, o_ref, sfin_ref, scr = _split_refs(refs, 15, has_s0, want_final)
    q_ref, i_ref, ff_ref, fb_ref, g_ref, lbl_ref, gain_ref = ins[:7]
    k_scr, cum_scr, oacc, st_scr = scr
    q_at = lambda r0: q_ref[pl.ds(r0, CHUNK), :]
    k_at = lambda d, r0, n=CHUNK: k_scr[d, pl.ds(r0, n), :]
    rec = _Rec(q_at, k_at, i_ref, cum_scr, st_scr, oacc, ins[7:], W_GRP)

    lg = lbl_ref[...]
    mx = lg[0]
    for i in range(1, DEPTH):
        mx = jnp.maximum(mx, lg[i])
    ex = [jnp.exp(lg[i] - mx) for i in range(DEPTH)]
    den = ex[0]
    for i in range(1, DEPTH):
        den = den + ex[i]
    sm = [e / den for e in ex]
    csum = sm[0]
    for i in range(1, layer + 1):
        csum = csum + sm[i]
    lb = csum - sm[0]

    def pre(bi, rm):
        r0 = bi * GROWS if isinstance(bi, int) else pl.multiple_of(bi * GROWS, GROWS)
        rows = pl.ds(r0, GROWS)
        for d, f_ref in ((0, ff_ref), (1, fb_ref)):
            lbd = lb[d:d + 1, :]
            f = lbd + (1.0 - lbd) * _sigmoid(f_ref[rows, :])
            g = jnp.log(jnp.maximum(f, F_FLOOR))
            k_scr[d, rows, :] = 1.0 - f
            cum = _block_cumsum(rec.tri_ref[d], g)
            cum_scr[d, rows, :] = cum
            rm = _block_ranges(cum, d == 1, rm)
        return rm

    rmax = jnp.zeros((1, W_GRP), F32)
    nblock = seq_len // GROWS
    rmax = pre(0, rmax) if nblock == 1 else lax.fori_loop(0, nblock, pre, rmax)
    ok = jnp.max(rmax) < SAFE_RANGE

    _run_recurrence(rec, ok, s0_ref, sfin_ref, seq_len)
    _head_norm_gate(rec, g_ref, gain_ref, o_ref, seq_len)


def _gla_kernel(*refs, seq_len, has_s0, want_final):
    ins, s0_ref, o_ref, sfin_ref, scr = _split_refs(refs, 16, has_s0, want_final)
    q_ref, k_ref, v_ref, g_ref, zf_ref, zb_ref, ba_ref, gain_ref = ins[:8]
    cum_scr, oacc, st_scr = scr
    width = N_HEADS * GLA_DK
    q_at = lambda r0: q_ref[pl.ds(r0, CHUNK), :] * (GLA_DK ** -0.5)
    k_at = lambda d, r0, n=CHUNK: k_ref[pl.ds(r0, n), :]
    rec = _Rec(q_at, k_at, v_ref, cum_scr, st_scr, oacc, ins[8:], width)

    def pre(bi, rm):
        r0 = bi * GROWS if isinstance(bi, int) else pl.multiple_of(bi * GROWS, GROWS)
        rows = pl.ds(r0, GROWS)
        for d, z_ref in ((0, zf_ref), (1, zb_ref)):
            z = z_ref[rows, :] + ba_ref[0, d:d + 1, :]
            g = -_softplus(-z) * (1.0 / GLA_NORMALIZER)
            cum = _block_cumsum(rec.tri_ref[d], g)
            cum_scr[d, rows, :] = cum
            rm = _block_ranges(cum, d == 1, rm)
        return rm

    rmax = jnp.zeros((1, width), F32)
    nblock = seq_len // GROWS
    rmax = pre(0, rmax) if nblock == 1 else lax.fori_loop(0, nblock, pre, rmax)
    ok = jnp.max(rmax) < SAFE_RANGE

    _run_recurrence(rec, ok, s0_ref, sfin_ref, seq_len)
    _head_norm_gate(rec, g_ref, gain_ref, o_ref, seq_len)


def _recurrence_constants(width):
    dk = width // N_HEADS
    r = np.arange(N_HEADS * CHUNK)[:, None]
    tri = np.kron(np.eye(GROUP, dtype=np.float32), np.tril(np.ones((CHUNK, CHUNK), np.float32)))
    tri = np.stack([tri, tri.T])
    bdw = (r // CHUNK == np.arange(width)[None, :] // dk).astype(np.float32)
    bdv = (r // CHUNK == np.arange(W_GRP)[None, :] // HEAD_V).astype(np.float32)
    t = np.arange(CHUNK)[:, None]
    s = np.arange(N_HEADS * CHUNK)[None, :] % CHUNK
    cmask = np.stack([(s <= t), (s >= t)]).astype(np.float32)
    bmat = (np.arange(width)[:, None] // dk == np.arange(W_GRP)[None, :] // HEAD_V).astype(np.float32)
    tile_t = np.tile(np.eye(HEAD_V, dtype=np.float32), (1, N_HEADS))
    return (jnp.asarray(tri, BF16), jnp.asarray(bdw, BF16), jnp.asarray(bdw), jnp.asarray(bdv, BF16),
            jnp.asarray(cmask), jnp.asarray(bmat, BF16), jnp.asarray(tile_t, BF16),
            jnp.asarray(tile_t.T, BF16))


def _recurrent_mixer(kernel_fn, name, width, args, in_specs, state, layer, nseq, seq_len, want_final,
                     extra_scratch):
    dk = width // N_HEADS
    consts = _recurrence_constants(width)
    args = list(args) + list(consts)
    in_specs = list(in_specs) + [_full_spec(c) for c in consts]
    if state is not None:
        args.append(state)
        in_specs.append(pl.BlockSpec((1, 1, 2, N_HEADS, dk, HEAD_V), lambda b: (b, layer, 0, 0, 0, 0)))
    out_shape = [jax.ShapeDtypeStruct((nseq * seq_len, W_GRP), BF16)]
    out_specs = [pl.BlockSpec((seq_len, W_GRP), lambda b: (b, 0))]
    if want_final:
        out_shape.append(jax.ShapeDtypeStruct((nseq, 2, N_HEADS, dk, HEAD_V), F32))
        out_specs.append(pl.BlockSpec((1, 2, N_HEADS, dk, HEAD_V), lambda b: (b, 0, 0, 0, 0)))
    res = pl.pallas_call(
        functools.partial(kernel_fn, seq_len=seq_len, has_s0=state is not None, want_final=want_final),
        grid=(nseq,),
        in_specs=in_specs,
        out_specs=out_specs,
        out_shape=out_shape,
        scratch_shapes=extra_scratch + [pltpu.VMEM((2, seq_len, width), F32),
                                        pltpu.VMEM((2, seq_len, W_GRP), F32),
                                        pltpu.VMEM((2, W_GRP, width), F32)],
        compiler_params=_params(("arbitrary",)),
        name=name,
    )(*args)
    return res[0], (res[1] if want_final else None)


def _hgrn_mixer(proj, lb_logits, gain, state, layer, nseq, seq_len, want_final):
    col = lambda c: pl.BlockSpec((seq_len, W_GRP), lambda b: (b, c))
    args = [proj] * 5 + [lb_logits, gain.reshape(DEPTH, 1, W_GRP)]
    in_specs = [col(0), col(1), col(2), col(3), col(4), _full_spec(lb_logits),
                pl.BlockSpec((1, 1, W_GRP), lambda b: (layer, 0, 0))]
    return _recurrent_mixer(functools.partial(_hgrn_kernel, layer=layer), "hgrn2_mixer", W_GRP, args,
                            in_specs, state, layer, nseq, seq_len, want_final,
                            [pltpu.VMEM((2, seq_len, W_GRP), F32)])


def _gla_mixer(proj, ba2, gain, state, layer, nseq, seq_len, want_final):
    width = N_HEADS * GLA_DK
    col256 = lambda c: pl.BlockSpec((seq_len, W_GRP), lambda b: (b, c))
    col128 = lambda c: pl.BlockSpec((seq_len, width), lambda b: (b, c))
    args = [proj] * 6 + [ba2, gain.reshape(DEPTH, 1, W_GRP)]
    in_specs = [col128(10), col128(11), col256(6), col256(7), col128(16), col128(17),
                pl.BlockSpec((1, 2, width), lambda b: (layer, 0, 0)),
                pl.BlockSpec((1, 1, W_GRP), lambda b: (layer, 0, 0))]
    return _recurrent_mixer(_gla_kernel, "gla_mixer", width, args, in_specs, state, layer, nseq, seq_len,
                            want_final, [])


PAD = 8
SCAN_TILE = 8


def _rgsc_kernel(*refs, seq_len, seg, has_h0, want_final):
    (cx_ref, cg_ref, db_ref, dc_ref, dv_ref, convw_ref, convb_ref, wr_ref, br_ref, wi_ref,
     bi_ref, lam_ref, sw_ref) = refs[:13]
    pos = 13
    h0_ref = None
    if has_h0:
        h0_ref = refs[pos]
        pos += 1
    oc_ref, od_ref = refs[pos:pos + 2]
    pos += 2
    hfin_ref = None
    if want_final:
        hfin_ref = refs[pos]
        pos += 1
    upad, a_scr, b_scr, h_scr = refs[pos:]

    zpad = jnp.zeros((PAD, W_GRP), F32)
    upad[0:PAD, :] = zpad
    upad[PAD + seq_len:2 * PAD + seq_len, :] = zpad
    upad[PAD:PAD + seq_len, :] = cx_ref[...]
    rows8 = lax.broadcasted_iota(jnp.int32, (SCAN_TILE, W_GRP), 0)
    ntile = seq_len // SCAN_TILE

    for d in (0, 1):
        reverse = d == 1
        w = convw_ref[0, d]
        offs = [PAD + 3, PAD + 2, PAD + 1, PAD] if reverse else [PAD - 3, PAD - 2, PAD - 1, PAD]
        sp = _softplus(-lam_ref[0, d:d + 1, :])
        for rb in range(seq_len // ROW_BLOCK):
            base = rb * ROW_BLOCK
            xc = convb_ref[0, d:d + 1, :]
            for j in range(4):
                xc = xc + w[j:j + 1, :] * upad[offs[j] + base:offs[j] + base + ROW_BLOCK, :]
            r = _sigmoid(_dot(xc, wr_ref[d]) + br_ref[0, d:d + 1, :])
            ig = _sigmoid(_dot(xc, wi_ref[d]) + bi_ref[0, d:d + 1, :])
            log_a = -RG_C * r * sp
            a = jnp.exp(log_a)
            a_scr[base:base + ROW_BLOCK, :] = a
            b_scr[base:base + ROW_BLOCK, :] = jnp.sqrt(1.0 - a * a) * (ig * xc)

        h0 = h0_ref[0, 0, d:d + 1, :] if has_h0 else jnp.zeros((1, W_GRP), F32)

        def step(ti, carry, d=d, reverse=reverse):
            t = (ntile - 1 - ti) if reverse else ti
            r0 = pl.multiple_of(t * SCAN_TILE, SCAN_TILE)
            a = a_scr[pl.ds(r0, SCAN_TILE), :]
            b = b_scr[pl.ds(r0, SCAN_TILE), :]
            for s in (1, 2, 4):
                if reverse:
                    keep = rows8 <= SCAN_TILE - 1 - s
                    shift = SCAN_TILE - s
                else:
                    keep = rows8 >= s
                    shift = s
                a_s = jnp.where(keep, pltpu.roll(a, shift, 0), 1.0)
                b_s = jnp.where(keep, pltpu.roll(b, shift, 0), 0.0)
                b = b + a * b_s
                a = a * a_s
            h = a * carry + b
            if d == 0:
                h_scr[pl.ds(r0, SCAN_TILE), :] = h
            else:
                h_scr[pl.ds(r0, SCAN_TILE), :] += h
            return h[0:1, :] if reverse else h[SCAN_TILE - 1:SCAN_TILE, :]

        hlast = lax.fori_loop(0, ntile, step, h0, unroll=4)
        if want_final:
            hfin_ref[0, d:d + 1, :] = hlast

    sw = sw_ref[0]
    for rb in range(seq_len // ROW_BLOCK):
        rows = slice(rb * ROW_BLOCK, (rb + 1) * ROW_BLOCK)
        x = cg_ref[rows, :]
        gelu = 0.5 * x * (1.0 + jnp.tanh(0.7978845608028654 * (x + 0.044715 * (x * x * x))))
        oc_ref[rows, :] = (h_scr[rows, :] * gelu).astype(oc_ref.dtype)
        upad[PAD + rb * ROW_BLOCK:PAD + (rb + 1) * ROW_BLOCK, :] = dc_ref[rows, :] * dv_ref[rows, :]
    for rb in range(seq_len // ROW_BLOCK):
        base = rb * ROW_BLOCK
        rows = slice(base, base + ROW_BLOCK)
        posn = (lax.broadcasted_iota(jnp.int32, (ROW_BLOCK, W_GRP), 0) + base) % seg
        left = jnp.where(posn != 0, upad[PAD - 1 + base:PAD - 1 + base + ROW_BLOCK, :], 0.0)
        right = jnp.where(posn != seg - 1, upad[PAD + 1 + base:PAD + 1 + base + ROW_BLOCK, :], 0.0)
        y = sw[0:1, :] * left + sw[1:2, :] * upad[PAD + base:PAD + base + ROW_BLOCK, :] + sw[2:3, :] * right
        od_ref[rows, :] = (db_ref[rows, :] * y).astype(od_ref.dtype)


def _rgsc_mixer(proj, conv_w, conv_b, wr_bd, b_r, wi_bd, b_i, lam, sconv_w, h0, layer, nseq, seq_len,
                seg, want_final):
    col = lambda c: pl.BlockSpec((seq_len, W_GRP), lambda b: (b, c))
    lay3 = lambda a: pl.BlockSpec((1,) + a.shape[1:], lambda b: (layer, 0, 0))
    lay4 = lambda a: pl.BlockSpec((1,) + a.shape[1:], lambda b: (layer, 0, 0, 0))
    params = [conv_w, conv_b, wr_bd, b_r, wi_bd, b_i, lam, sconv_w]
    args = [proj] * 5 + params
    in_specs = [col(9), col(10), col(11), col(12), col(13), lay4(conv_w), lay3(conv_b), _full_spec(wr_bd),
                lay3(b_r), _full_spec(wi_bd), lay3(b_i), lay3(lam), lay3(sconv_w)]
    if h0 is not None:
        args.append(h0)
        in_specs.append(pl.BlockSpec((1, 1, 2, W_GRP), lambda b: (b, layer, 0, 0)))
    out_shape = [jax.ShapeDtypeStruct((nseq * seq_len, W_GRP), BF16)] * 2
    out_specs = [pl.BlockSpec((seq_len, W_GRP), lambda b: (b, 0))] * 2
    if want_final:
        out_shape.append(jax.ShapeDtypeStruct((nseq, 2, W_GRP), F32))
        out_specs.append(pl.BlockSpec((1, 2, W_GRP), lambda b: (b, 0, 0)))
    res = pl.pallas_call(
        functools.partial(_rgsc_kernel, seq_len=seq_len, seg=seg, has_h0=h0 is not None,
                          want_final=want_final),
        grid=(nseq,),
        in_specs=in_specs,
        out_specs=out_specs,
        out_shape=out_shape,
        scratch_shapes=[pltpu.VMEM((seq_len + 2 * PAD, W_GRP), F32), pltpu.VMEM((seq_len, W_GRP), F32),
                        pltpu.VMEM((seq_len, W_GRP), F32), pltpu.VMEM((seq_len, W_GRP), F32)],
        compiler_params=_params(("arbitrary",)),
        name="rglru_sconv_mixer",
    )(*args)
    return res[0], res[1], (res[2] if want_final else None)


def _block_diag(w):
    two, h, c, _ = w.shape
    eye = jnp.eye(h, dtype=w.dtype)
    full = w[:, :, :, None, :] * eye[None, :, None, :, None]
    return full.reshape(two, h * c, h * c)


def kernel(x_prompt, x_sample, state_hgrn, state_gla, state_rglru, c, c_ctx, norm1_g, norm2_g, ada_w, ada_b, w_in, w_out, hgrn_lb_logits, hgrn_norm_g, gla_wa2, gla_ba2, gla_norm_g, rg_conv_w, rg_conv_b, rg_w_r, rg_b_r, rg_w_i, rg_b_i, rg_lambda, sconv_w, mlp_w1, mlp_w2, final_norm_g):
    b_ctx, l_ctx, _ = x_prompt.shape
    b_lat, l_lat, _ = x_sample.shape

    cvec = jnp.concatenate([c, c_ctx[None, :], jnp.zeros((8 - b_lat - 1, D_MODEL), F32)], axis=0)
    mod = _modulation(cvec, ada_w, ada_b).reshape(DEPTH, 8, 6, D_MODEL)

    a0 = 2048
    w_a = jnp.stack([w_in[:, :, a0:a0 + GLA_RANK], w_in[:, :, a0 + GLA_RANK:a0 + 2 * GLA_RANK]], axis=1)
    w_z = _fold_lowrank(w_a, gla_wa2)
    w_cat = jnp.concatenate([w_in[:, :, :a0], w_z[:, 0], w_z[:, 1], w_in[:, :, a0 + 2 * GLA_RANK:]],
                            axis=-1).astype(BF16)

    xp = x_prompt.reshape(b_ctx * l_ctx, D_MODEL)
    xs = x_sample.reshape(b_lat * l_lat, D_MODEL)
    new_h, new_g, new_r = [], [], []
    for l in range(DEPTH):
        wr_bd = _block_diag(rg_w_r[l]).astype(BF16)
        wi_bd = _block_diag(rg_w_i[l]).astype(BF16)
        mod_ctx = mod[l, b_lat:b_lat + 1]
        mod_lat = mod[l, :b_lat]
        last = l == DEPTH - 1
        streams = (
            (xp, mod_ctx, b_ctx, l_ctx, l_ctx, None, None, None, True),
            (xs, mod_lat, b_lat, l_lat, GRID_W, state_hgrn, state_gla, state_rglru, False),
        )
        outs = []
        for (x, m, nseq, seq_len, seg, s_h, s_g, s_r, want_final) in streams:
            proj = _project(x, m, norm1_g, w_cat, l)
            o_a, f_h = _hgrn_mixer(proj, hgrn_lb_logits, hgrn_norm_g, s_h, l, nseq, seq_len, want_final)
            o_b, f_g = _gla_mixer(proj, gla_ba2, gla_norm_g, s_g, l, nseq, seq_len, want_final)
            o_c, o_d, f_r = _rgsc_mixer(proj, rg_conv_w, rg_conv_b, wr_bd, rg_b_r, wi_bd, rg_b_i,
                                        rg_lambda, sconv_w, s_r, l, nseq, seq_len, seg, want_final)
            x_new = _out_mlp(x, o_a, o_b, o_c, o_d, m, norm2_g, final_norm_g, w_out, mlp_w1, mlp_w2, l, last)
            outs.append((x_new, f_h, f_g, f_r))
        xp, f_h, f_g, f_r = outs[0]
        xs = outs[1][0]
        new_h.append(f_h)
        new_g.append(f_g)
        new_r.append(f_r)
    y_prompt = xp.reshape(b_ctx, l_ctx, D_MODEL)
    y_sample = xs.reshape(b_lat, l_lat, D_MODEL)
    return (y_prompt, y_sample, jnp.stack(new_h, axis=1), jnp.stack(new_g, axis=1),
            jnp.stack(new_r, axis=1))
```

```python
import functools

import numpy as np
import jax
import jax.numpy as jnp
from jax import lax
from jax.experimental import pallas as pl
from jax.experimental.pallas import tpu as pltpu

F32 = jnp.float32
BF16 = jnp.bfloat16

D_MODEL = 1024
DEPTH = 2
GRID_W = 64
N_HEADS = 4
W_GRP = 256
HEAD_V = 64
HGRN_DK = 64
GLA_DK = 32
GLA_RANK = 16
GLA_NORMALIZER = 16.0
RG_C = 8.0
D_FF = 4 * D_MODEL
EPS = 1e-6
F_FLOOR = 1e-20

PROJ_COLS = 3584

CHUNK = 64
HALF = CHUNK // 2
assert CHUNK == HEAD_V
GROUP = 4
GROWS = GROUP * CHUNK
MAIN_GROUP = 8
LANES = 128
SAFE_RANGE = 80.0
TOK_TILE = 1024
PROJ_TM = 512
PROJ_TN = 512
FF_TILE = 1024
MLP_ROW_SPLIT = 2
ROW_BLOCK = 256
VMEM_LIMIT = 56 * 1024 * 1024

_NT = (((1,), (1,)), ((), ()))
_TN = (((0,), (0,)), ((), ()))


def _dot(a, b):
    return jnp.dot(a.astype(BF16), b.astype(BF16), preferred_element_type=F32)


def _dot_nt(a, b):
    return lax.dot_general(a.astype(BF16), b.astype(BF16), _NT, preferred_element_type=F32)


def _dot_tn(a, b):
    return lax.dot_general(a.astype(BF16), b.astype(BF16), _TN, preferred_element_type=F32)


def _sigmoid(x):
    return 1.0 / (1.0 + jnp.exp(-x))


def _silu(x):
    return x * _sigmoid(x)


def _log1p(y):
    u = 1.0 + y
    return jnp.where(u == 1.0, y, jnp.log(u) * (y / (u - 1.0)))


def _softplus(x):
    return jnp.maximum(x, 0.0) + _log1p(jnp.exp(-jnp.abs(x)))


def _rms(x):
    return x * lax.rsqrt(jnp.mean(x * x, axis=-1, keepdims=True) + EPS)


def _params(sem):
    return pltpu.CompilerParams(dimension_semantics=sem, vmem_limit_bytes=VMEM_LIMIT)


def _full_spec(a):
    zeros = (0,) * a.ndim
    return pl.BlockSpec(a.shape, lambda *_: zeros)


def _mod_kernel(c_ref, w_ref, b_ref, o_ref):
    s = _silu(c_ref[...])
    o_ref[0] = _dot(s, w_ref[0]) + b_ref[0]


def _modulation(cvec, ada_w, ada_b):
    tn = 2048
    return pl.pallas_call(
        _mod_kernel,
        grid=(DEPTH, 6 * D_MODEL // tn),
        in_specs=[pl.BlockSpec((8, D_MODEL), lambda l, j: (0, 0)),
                  pl.BlockSpec((1, D_MODEL, tn), lambda l, j: (l, 0, j)),
                  pl.BlockSpec((1, 1, tn), lambda l, j: (l, 0, j))],
        out_specs=pl.BlockSpec((1, 8, tn), lambda l, j: (l, 0, j)),
        out_shape=jax.ShapeDtypeStruct((DEPTH, 8, 6 * D_MODEL), F32),
        compiler_params=_params(("arbitrary", "arbitrary")),
        name="adaln_modulation",
    )(cvec, ada_w, ada_b.reshape(DEPTH, 1, 6 * D_MODEL))


def _fold_kernel(a_ref, b_ref, o_ref):
    o_ref[0, 0] = jnp.dot(a_ref[0, 0], b_ref[0, 0], preferred_element_type=F32,
                          precision=lax.Precision.HIGHEST)


def _fold_lowrank(w_a, wa2):
    n = N_HEADS * GLA_DK
    return pl.pallas_call(
        _fold_kernel,
        grid=(DEPTH, 2),
        in_specs=[pl.BlockSpec((1, 1, D_MODEL, GLA_RANK), lambda l, d: (l, d, 0, 0)),
                  pl.BlockSpec((1, 1, GLA_RANK, n), lambda l, d: (l, d, 0, 0))],
        out_specs=pl.BlockSpec((1, 1, D_MODEL, n), lambda l, d: (l, d, 0, 0)),
        out_shape=jax.ShapeDtypeStruct((DEPTH, 2, D_MODEL, n), F32),
        compiler_params=_params(("arbitrary", "arbitrary")),
        name="gla_lowrank_fold",
    )(w_a, wa2)


def _proj_kernel(x_ref, mod_ref, g_ref, w_ref, o_ref):
    shift = mod_ref[0, 0:1, :]
    scale = mod_ref[0, 1:2, :]
    h = (_rms(x_ref[...]) * g_ref[0] * (1.0 + scale) + shift).astype(BF16)
    for n in range(PROJ_COLS // PROJ_TN):
        cols = slice(n * PROJ_TN, (n + 1) * PROJ_TN)
        o_ref[:, cols] = jnp.dot(h, w_ref[0, :, cols], preferred_element_type=F32)


def _mod_index(ntok, nmod, tile):
    tiles_per_mod = (ntok // tile) // nmod if nmod > 1 else 1
    if nmod > 1:
        return lambda i, *_: (i // tiles_per_mod, 0, 0)
    return lambda i, *_: (0, 0, 0)


def _project(x, mod, norm_g, w_cat, layer):
    ntok = x.shape[0]
    return pl.pallas_call(
        _proj_kernel,
        grid=(ntok // PROJ_TM,),
        in_specs=[pl.BlockSpec((PROJ_TM, D_MODEL), lambda i: (i, 0)),
                  pl.BlockSpec((1, 6, D_MODEL), _mod_index(ntok, mod.shape[0], PROJ_TM)),
                  pl.BlockSpec((1, 1, D_MODEL), lambda i: (layer, 0, 0)),
                  pl.BlockSpec((1, D_MODEL, PROJ_COLS), lambda i: (layer, 0, 0))],
        out_specs=pl.BlockSpec((PROJ_TM, PROJ_COLS), lambda i: (i, 0)),
        out_shape=jax.ShapeDtypeStruct((ntok, PROJ_COLS), F32),
        compiler_params=_params(("arbitrary",)),
        name="norm_in_proj",
    )(x, mod, norm_g.reshape(DEPTH, 1, D_MODEL), w_cat)


def _mlp_kernel(x_ref, oa_ref, ob_ref, oc_ref, od_ref, mod_ref, g2_ref, fg_ref, wout_ref,
                w1_ref, w2_ref, out_ref, h2_scr, *, final_norm):
    j = pl.program_id(1)

    @pl.when(j == 0)
    def _():
        mixed = jnp.concatenate([oa_ref[...], ob_ref[...], oc_ref[...], od_ref[...]], axis=1)
        mix = jnp.dot(mixed, wout_ref[0].astype(BF16), preferred_element_type=F32)
        x1 = x_ref[...] + mod_ref[0, 2:3, :] * mix
        out_ref[...] = x1
        h2 = _rms(x1) * g2_ref[0] * (1.0 + mod_ref[0, 4:5, :]) + mod_ref[0, 3:4, :]
        h2_scr[...] = h2.astype(BF16)

    w1 = w1_ref[0].astype(BF16)
    w2 = w2_ref[0].astype(BF16)
    for r in range(MLP_ROW_SPLIT):
        rows = slice(r * (TOK_TILE // MLP_ROW_SPLIT), (r + 1) * (TOK_TILE // MLP_ROW_SPLIT))
        t = jnp.dot(h2_scr[rows, :], w1, preferred_element_type=F32)
        t = jnp.square(jnp.maximum(t, 0.0)).astype(BF16)
        out_ref[rows, :] += mod_ref[0, 5:6, :] * jnp.dot(t, w2, preferred_element_type=F32)

    if final_norm:
        @pl.when(j == pl.num_programs(1) - 1)
        def _():
            out_ref[...] = _rms(out_ref[...]) * fg_ref[...]


def _out_mlp(x, o_a, o_b, o_c, o_d, mod, norm2_g, final_g, w_out, w1, w2, layer, final_norm):
    ntok = x.shape[0]
    tok = lambda i, j: (i, 0)
    return pl.pallas_call(
        functools.partial(_mlp_kernel, final_norm=final_norm),
        grid=(ntok // TOK_TILE, D_FF // FF_TILE),
        in_specs=[pl.BlockSpec((TOK_TILE, D_MODEL), tok),
                  pl.BlockSpec((TOK_TILE, W_GRP), tok),
                  pl.BlockSpec((TOK_TILE, W_GRP), tok),
                  pl.BlockSpec((TOK_TILE, W_GRP), tok),
                  pl.BlockSpec((TOK_TILE, W_GRP), tok),
                  pl.BlockSpec((1, 6, D_MODEL), _mod_index(ntok, mod.shape[0], TOK_TILE)),
                  pl.BlockSpec((1, 1, D_MODEL), lambda i, j: (layer, 0, 0)),
                  pl.BlockSpec((1, D_MODEL), lambda i, j: (0, 0)),
                  pl.BlockSpec((1, D_MODEL, D_MODEL), lambda i, j: (layer, 0, 0)),
                  pl.BlockSpec((1, D_MODEL, FF_TILE), lambda i, j: (layer, 0, j)),
                  pl.BlockSpec((1, FF_TILE, D_MODEL), lambda i, j: (layer, j, 0))],
        out_specs=pl.BlockSpec((TOK_TILE, D_MODEL), tok),
        out_shape=jax.ShapeDtypeStruct((ntok, D_MODEL), F32),
        scratch_shapes=[pltpu.VMEM((TOK_TILE, D_MODEL), BF16)],
        compiler_params=_params(("arbitrary", "arbitrary")),
        name="out_proj_mlp",
    )(x, o_a, o_b, o_c, o_d, mod, norm2_g.reshape(DEPTH, 1, D_MODEL), final_g.reshape(1, D_MODEL),
      w_out, w1, w2)


def _split3(x):
    hi = x.astype(BF16)
    r1 = x - hi.astype(F32)
    mid = r1.astype(BF16)
    lo = (r1 - mid.astype(F32)).astype(BF16)
    return hi, mid, lo


def _dot_exact_rhs(x, m):
    hi, mid, lo = _split3(x)
    s = jnp.dot(hi, m, preferred_element_type=F32)
    s += jnp.dot(mid, m, preferred_element_type=F32)
    s += jnp.dot(lo, m, preferred_element_type=F32)
    return s


def _block_cumsum(tri, g):
    hi, mid, lo = _split3(g)
    s = jnp.dot(tri, hi, preferred_element_type=F32)
    s += jnp.dot(tri, mid, preferred_element_type=F32)
    s += jnp.dot(tri, lo, preferred_element_type=F32)
    return s


def _half_ranges(cum, reverse):
    if reverse:
        second = -cum[HALF:HALF + 1, :]
        first = -(cum[0:1, :] - cum[HALF:HALF + 1, :])
    else:
        first = -cum[HALF - 1:HALF, :]
        second = -(cum[CHUNK - 1:CHUNK, :] - cum[HALF - 1:HALF, :])
    return jnp.maximum(first, second)


def _block_ranges(cum, reverse, rm):
    for c in range(GROUP):
        rm = jnp.maximum(rm, _half_ranges(cum[c * CHUNK:(c + 1) * CHUNK, :], reverse))
    return rm


def _row_start(c):
    r0 = c * CHUNK
    return r0 if isinstance(r0, int) else pl.multiple_of(r0, CHUNK)


class _Rec:
    def __init__(self, q_at, k_at, v_ref, cum_scr, st_scr, oacc, consts, width):
        self.q_at, self.k_at, self.v_ref = q_at, k_at, v_ref
        self.cum_scr, self.st_scr, self.oacc = cum_scr, st_scr, oacc
        (self.tri_ref, self.bdw_b, self.bdw_f, self.bdv_b, self.cmask, self.bmat,
         self.tile_t, self.tile_tt) = consts
        self.width = width


def _chunk_step(rec, d, c, fast):
    reverse = d == 1
    mid_row = HALF if reverse else HALF - 1
    last_row = 0 if reverse else CHUNK - 1
    r0 = _row_start(c)
    rows = pl.ds(r0, CHUNK)
    q = rec.q_at(r0)
    k = rec.k_at(d, r0)
    v_b = rec.v_ref[rows, :].astype(BF16)
    cum = rec.cum_scr[d, rows, :]
    tot = cum[last_row:last_row + 1, :]
    st = rec.st_scr[d]
    o = _dot_nt(q * jnp.exp(cum), st)
    if fast:
        cm = cum[mid_row:mid_row + 1, :]
        qm = q * jnp.exp(cum - cm)
        km_b = (k * jnp.exp(cm - cum)).astype(BF16)
        kbd = jnp.concatenate([km_b] * N_HEADS, axis=0) * rec.bdw_b[...]
        sc = lax.dot_general(qm.astype(BF16), kbd, _NT, preferred_element_type=F32)
        a = jnp.where(rec.cmask[d] > 0.5, sc, 0.0)
        vexp = jnp.concatenate([v_b] * N_HEADS, axis=0) * rec.bdv_b[...]
        o = o + jnp.dot(a.astype(BF16), vexp, preferred_element_type=F32)
    else:
        row_id = lax.broadcasted_iota(jnp.int32, (CHUNK, rec.width), 0)

        def key_row(j, acc):
            krow = rec.k_at(d, r0 + j, 1)
            crow = rec.cum_scr[d, pl.ds(r0 + j, 1), :]
            vrow = rec.v_ref[pl.ds(r0 + j, 1), :]
            keep = (row_id <= j) if reverse else (row_id >= j)
            e = jnp.exp(jnp.minimum(cum - crow, 0.0))
            p = jnp.where(keep, q * krow * e, 0.0)
            return acc + _dot(p, rec.bmat[...]) * vrow

        o = lax.fori_loop(0, CHUNK, key_row, o)
    ke_b = (k * jnp.exp(tot - cum)).astype(BF16)
    ds = lax.dot_general(v_b, ke_b, _TN, preferred_element_type=F32)
    rec.st_scr[d] = st * jnp.exp(tot) + ds * rec.bdw_f[...]
    rec.oacc[d, rows, :] = o


def _load_state(rec, s0_ref, d):
    x = jnp.concatenate([s0_ref[0, 0, d, h] for h in range(N_HEADS)], axis=0)
    y = _dot_exact_rhs(x, rec.tile_t[...]) * rec.bmat[...].astype(F32)
    rec.st_scr[d] = y.T


def _store_state(rec, sfin_ref, d):
    y = rec.st_scr[d].T
    x = _dot_exact_rhs(y, rec.tile_tt[...])
    dk = rec.width // N_HEADS
    for h in range(N_HEADS):
        sfin_ref[0, d, h] = x[h * dk:(h + 1) * dk, :]


def _run_recurrence(rec, ok, s0_ref, sfin_ref, seq_len):
    nchunk = seq_len // CHUNK
    gsize = min(MAIN_GROUP, nchunk)
    ngroup = nchunk // gsize
    for d in (0, 1):
        if s0_ref is None:
            rec.st_scr[d] = jnp.zeros(rec.st_scr.shape[1:], F32)
        else:
            _load_state(rec, s0_ref, d)

    def fast_all():
        def group(gi, carry):
            for u in range(gsize):
                s = gi * gsize + u
                _chunk_step(rec, 0, s, True)
                _chunk_step(rec, 1, nchunk - 1 - s, True)
            return carry

        if ngroup == 1:
            group(0, 0)
        else:
            lax.fori_loop(0, ngroup, group, 0)

    def direct_all():
        def one(s, carry):
            _chunk_step(rec, 0, s, False)
            _chunk_step(rec, 1, nchunk - 1 - s, False)
            return carry

        lax.fori_loop(0, nchunk, one, 0)

    lax.cond(ok, fast_all, direct_all)
    if sfin_ref is not None:
        for d in (0, 1):
            _store_state(rec, sfin_ref, d)


def _head_norm_gate(rec, gate_ref, gain_ref, o_ref, seq_len):
    for rb in range(seq_len // ROW_BLOCK):
        rows = slice(rb * ROW_BLOCK, (rb + 1) * ROW_BLOCK)
        o = rec.oacc[0, rows, :] + rec.oacc[1, rows, :]
        ms = jnp.dot((o * o).astype(BF16), rec.bdv_b[...], preferred_element_type=F32) * (1.0 / HEAD_V)
        y = o * lax.rsqrt(ms + EPS) * gain_ref[0] * _silu(gate_ref[rows, :])
        o_ref[rows, :] = y.astype(o_ref.dtype)


def _split_refs(refs, n_in, has_s0, want_final):
    ins = refs[:n_in]
    pos = n_in
    s0_ref = None
    if has_s0:
        s0_ref = refs[pos]
        pos += 1
    o_ref = refs[pos]
    pos += 1
    sfin_ref = None
    if want_final:
        sfin_ref = refs[pos]
        pos += 1
    return ins, s0_ref, o_ref, sfin_ref, refs[pos:]


def _hgrn_kernel(*refs, seq_len, layer, has_s0, want_final):
    ins, s0_ref, o_ref, sfin_ref, scr = _split_refs(refs, 15, has_s0, want_final)
    q_ref, i_ref, ff_ref, fb_ref, g_ref, lbl_ref, gain_ref = ins[:7]
    k_scr, cum_scr, oacc, st_scr = scr
    q_at = lambda r0: q_ref[pl.ds(r0, CHUNK), :]
    k_at = lambda d, r0, n=CHUNK: k_scr[d, pl.ds(r0, n), :]
    rec = _Rec(q_at, k_at, i_ref, cum_scr, st_scr, oacc, ins[7:], W_GRP)

    lg = lbl_ref[...]
    mx = lg[0]
    for i in range(1, DEPTH):
        mx = jnp.maximum(mx, lg[i])
    ex = [jnp.exp(lg[i] - mx) for i in range(DEPTH)]
    den = ex[0]
    for i in range(1, DEPTH):
        den = den + ex[i]
    sm = [e / den for e in ex]
    csum = sm[0]
    for i in range(1, layer + 1):
        csum = csum + sm[i]
    lb = csum - sm[0]

    def pre(bi, rm):
        r0 = bi * GROWS if isinstance(bi, int) else pl.multiple_of(bi * GROWS, GROWS)
        rows = pl.ds(r0, GROWS)
        for d, f_ref in ((0, ff_ref), (1, fb_ref)):
            lbd = lb[d:d + 1, :]
            f = lbd + (1.0 - lbd) * _sigmoid(f_ref[rows, :])
            g = jnp.log(jnp.maximum(f, F_FLOOR))
            k_scr[d, rows, :] = 1.0 - f
            cum = _block_cumsum(rec.tri_ref[d], g)
            cum_scr[d, rows, :] = cum
            rm = _block_ranges(cum, d == 1, rm)
        return rm

    rmax = jnp.zeros((1, W_GRP), F32)
    nblock = seq_len // GROWS
    rmax = pre(0, rmax) if nblock == 1 else lax.fori_loop(0, nblock, pre, rmax)
    ok = jnp.max(rmax) < SAFE_RANGE

    _run_recurrence(rec, ok, s0_ref, sfin_ref, seq_len)
    _head_norm_gate(rec, g_ref, gain_ref, o_ref, seq_len)


def _gla_kernel(*refs, seq_len, has_s0, want_final):
    ins, s0_ref, o_ref, sfin_ref, scr = _split_refs(refs, 16, has_s0, want_final)
    q_ref, k_ref, v_ref, g_ref, zf_ref, zb_ref, ba_ref, gain_ref = ins[:8]
    cum_scr, oacc, st_scr = scr
    width = N_HEADS * GLA_DK
    q_at = lambda r0: q_ref[pl.ds(r0, CHUNK), :] * (GLA_DK ** -0.5)
    k_at = lambda d, r0, n=CHUNK: k_ref[pl.ds(r0, n), :]
    rec = _Rec(q_at, k_at, v_ref, cum_scr, st_scr, oacc, ins[8:], width)

    def pre(bi, rm):
        r0 = bi * GROWS if isinstance(bi, int) else pl.multiple_of(bi * GROWS, GROWS)
        rows = pl.ds(r0, GROWS)
        for d, z_ref in ((0, zf_ref), (1, zb_ref)):
            z = z_ref[rows, :] + ba_ref[0, d:d + 1, :]
            g = -_softplus(-z) * (1.0 / GLA_NORMALIZER)
            cum = _block_cumsum(rec.tri_ref[d], g)
            cum_scr[d, rows, :] = cum
            rm = _block_ranges(cum, d == 1, rm)
        return rm

    rmax = jnp.zeros((1, width), F32)
    nblock = seq_len // GROWS
    rmax = pre(0, rmax) if nblock == 1 else lax.fori_loop(0, nblock, pre, rmax)
    ok = jnp.max(rmax) < SAFE_RANGE

    _run_recurrence(rec, ok, s0_ref, sfin_ref, seq_len)
    _head_norm_gate(rec, g_ref, gain_ref, o_ref, seq_len)


def _recurrence_constants(width):
    dk = width // N_HEADS
    r = np.arange(N_HEADS * CHUNK)[:, None]
    tri = np.kron(np.eye(GROUP, dtype=np.float32), np.tril(np.ones((CHUNK, CHUNK), np.float32)))
    tri = np.stack([tri, tri.T])
    bdw = (r // CHUNK == np.arange(width)[None, :] // dk).astype(np.float32)
    bdv = (r // CHUNK == np.arange(W_GRP)[None, :] // HEAD_V).astype(np.float32)
    t = np.arange(CHUNK)[:, None]
    s = np.arange(N_HEADS * CHUNK)[None, :] % CHUNK
    cmask = np.stack([(s <= t), (s >= t)]).astype(np.float32)
    bmat = (np.arange(width)[:, None] // dk == np.arange(W_GRP)[None, :] // HEAD_V).astype(np.float32)
    tile_t = np.tile(np.eye(HEAD_V, dtype=np.float32), (1, N_HEADS))
    return (jnp.asarray(tri, BF16), jnp.asarray(bdw, BF16), jnp.asarray(bdw), jnp.asarray(bdv, BF16),
            jnp.asarray(cmask), jnp.asarray(bmat, BF16), jnp.asarray(tile_t, BF16),
            jnp.asarray(tile_t.T, BF16))


def _recurrent_mixer(kernel_fn, name, width, args, in_specs, state, layer, nseq, seq_len, want_final,
                     extra_scratch):
    dk = width // N_HEADS
    consts = _recurrence_constants(width)
    args = list(args) + list(consts)
    in_specs = list(in_specs) + [_full_spec(c) for c in consts]
    if state is not None:
        args.append(state)
        in_specs.append(pl.BlockSpec((1, 1, 2, N_HEADS, dk, HEAD_V), lambda b: (b, layer, 0, 0, 0, 0)))
    out_shape = [jax.ShapeDtypeStruct((nseq * seq_len, W_GRP), BF16)]
    out_specs = [pl.BlockSpec((seq_len, W_GRP), lambda b: (b, 0))]
    if want_final:
        out_shape.append(jax.ShapeDtypeStruct((nseq, 2, N_HEADS, dk, HEAD_V), F32))
        out_specs.append(pl.BlockSpec((1, 2, N_HEADS, dk, HEAD_V), lambda b: (b, 0, 0, 0, 0)))
    res = pl.pallas_call(
        functools.partial(kernel_fn, seq_len=seq_len, has_s0=state is not None, want_final=want_final),
        grid=(nseq,),
        in_specs=in_specs,
        out_specs=out_specs,
        out_shape=out_shape,
        scratch_shapes=extra_scratch + [pltpu.VMEM((2, seq_len, width), F32),
                                        pltpu.VMEM((2, seq_len, W_GRP), F32),
                                        pltpu.VMEM((2, W_GRP, width), F32)],
        compiler_params=_params(("arbitrary",)),
        name=name,
    )(*args)
    return res[0], (res[1] if want_final else None)


def _hgrn_mixer(proj, lb_logits, gain, state, layer, nseq, seq_len, want_final):
    col = lambda c: pl.BlockSpec((seq_len, W_GRP), lambda b: (b, c))
    args = [proj] * 5 + [lb_logits, gain.reshape(DEPTH, 1, W_GRP)]
    in_specs = [col(0), col(1), col(2), col(3), col(4), _full_spec(lb_logits),
                pl.BlockSpec((1, 1, W_GRP), lambda b: (layer, 0, 0))]
    return _recurrent_mixer(functools.partial(_hgrn_kernel, layer=layer), "hgrn2_mixer", W_GRP, args,
                            in_specs, state, layer, nseq, seq_len, want_final,
                            [pltpu.VMEM((2, seq_len, W_GRP), F32)])


def _gla_mixer(proj, ba2, gain, state, layer, nseq, seq_len, want_final):
    width = N_HEADS * GLA_DK
    col256 = lambda c: pl.BlockSpec((seq_len, W_GRP), lambda b: (b, c))
    col128 = lambda c: pl.BlockSpec((seq_len, width), lambda b: (b, c))
    args = [proj] * 6 + [ba2, gain.reshape(DEPTH, 1, W_GRP)]
    in_specs = [col128(10), col128(11), col256(6), col256(7), col128(16), col128(17),
                pl.BlockSpec((1, 2, width), lambda b: (layer, 0, 0)),
                pl.BlockSpec((1, 1, W_GRP), lambda b: (layer, 0, 0))]
    return _recurrent_mixer(_gla_kernel, "gla_mixer", width, args, in_specs, state, layer, nseq, seq_len,
                            want_final, [])


PAD = 8
SCAN_TILE = 8


def _rgsc_kernel(*refs, seq_len, seg, has_h0, want_final):
    (cx_ref, cg_ref, db_ref, dc_ref, dv_ref, convw_ref, convb_ref, wr_ref, br_ref, wi_ref,
     bi_ref, lam_ref, sw_ref) = refs[:13]
    pos = 13
    h0_ref = None
    if has_h0:
        h0_ref = refs[pos]
        pos += 1
    oc_ref, od_ref = refs[pos:pos + 2]
    pos += 2
    hfin_ref = None
    if want_final:
        hfin_ref = refs[pos]
        pos += 1
    upad, a_scr, b_scr, h_scr = refs[pos:]

    zpad = jnp.zeros((PAD, W_GRP), F32)
    upad[0:PAD, :] = zpad
    upad[PAD + seq_len:2 * PAD + seq_len, :] = zpad
    upad[PAD:PAD + seq_len, :] = cx_ref[...]
    rows8 = lax.broadcasted_iota(jnp.int32, (SCAN_TILE, W_GRP), 0)
    ntile = seq_len // SCAN_TILE

    for d in (0, 1):
        reverse = d == 1
        w = convw_ref[0, d]
        offs = [PAD + 3, PAD + 2, PAD + 1, PAD] if reverse else [PAD - 3, PAD - 2, PAD - 1, PAD]
        sp = _softplus(-lam_ref[0, d:d + 1, :])
        for rb in range(seq_len // ROW_BLOCK):
            base = rb * ROW_BLOCK
            xc = convb_ref[0, d:d + 1, :]
            for j in range(4):
                xc = xc + w[j:j + 1, :] * upad[offs[j] + base:offs[j] + base + ROW_BLOCK, :]
            r = _sigmoid(_dot(xc, wr_ref[d]) + br_ref[0, d:d + 1, :])
            ig = _sigmoid(_dot(xc, wi_ref[d]) + bi_ref[0, d:d + 1, :])
            log_a = -RG_C * r * sp
            a = jnp.exp(log_a)
            a_scr[base:base + ROW_BLOCK, :] = a
            b_scr[base:base + ROW_BLOCK, :] = jnp.sqrt(1.0 - a * a) * (ig * xc)

        h0 = h0_ref[0, 0, d:d + 1, :] if has_h0 else jnp.zeros((1, W_GRP), F32)

        def step(ti, carry, d=d, reverse=reverse):
            t = (ntile - 1 - ti) if reverse else ti
            r0 = pl.multiple_of(t * SCAN_TILE, SCAN_TILE)
            a = a_scr[pl.ds(r0, SCAN_TILE), :]
            b = b_scr[pl.ds(r0, SCAN_TILE), :]
            for s in (1, 2, 4):
                if reverse:
                    keep = rows8 <= SCAN_TILE - 1 - s
                    shift = SCAN_TILE - s
                else:
                    keep = rows8 >= s
                    shift = s
                a_s = jnp.where(keep, pltpu.roll(a, shift, 0), 1.0)
                b_s = jnp.where(keep, pltpu.roll(b, shift, 0), 0.0)
                b = b + a * b_s
                a = a * a_s
            h = a * carry + b
            if d == 0:
                h_scr[pl.ds(r0, SCAN_TILE), :] = h
            else:
                h_scr[pl.ds(r0, SCAN_TILE), :] += h
            return h[0:1, :] if reverse else h[SCAN_TILE - 1:SCAN_TILE, :]

        hlast = lax.fori_loop(0, ntile, step, h0, unroll=4)
        if want_final:
            hfin_ref[0, d:d + 1, :] = hlast

    sw = sw_ref[0]
    for rb in range(seq_len // ROW_BLOCK):
        rows = slice(rb * ROW_BLOCK, (rb + 1) * ROW_BLOCK)
        x = cg_ref[rows, :]
        gelu = 0.5 * x * (1.0 + jnp.tanh(0.7978845608028654 * (x + 0.044715 * (x * x * x))))
        oc_ref[rows, :] = (h_scr[rows, :] * gelu).astype(oc_ref.dtype)
        upad[PAD + rb * ROW_BLOCK:PAD + (rb + 1) * ROW_BLOCK, :] = dc_ref[rows, :] * dv_ref[rows, :]
    for rb in range(seq_len // ROW_BLOCK):
        base = rb * ROW_BLOCK
        rows = slice(base, base + ROW_BLOCK)
        posn = (lax.broadcasted_iota(jnp.int32, (ROW_BLOCK, W_GRP), 0) + base) % seg
        left = jnp.where(posn != 0, upad[PAD - 1 + base:PAD - 1 + base + ROW_BLOCK, :], 0.0)
        right = jnp.where(posn != seg - 1, upad[PAD + 1 + base:PAD + 1 + base + ROW_BLOCK, :], 0.0)
        y = sw[0:1, :] * left + sw[1:2, :] * upad[PAD + base:PAD + base + ROW_BLOCK, :] + sw[2:3, :] * right
        od_ref[rows, :] = (db_ref[rows, :] * y).astype(od_ref.dtype)


def _rgsc_mixer(proj, conv_w, conv_b, wr_bd, b_r, wi_bd, b_i, lam, sconv_w, h0, layer, nseq, seq_len,
                seg, want_final):
    col = lambda c: pl.BlockSpec((seq_len, W_GRP), lambda b: (b, c))
    lay3 = lambda a: pl.BlockSpec((1,) + a.shape[1:], lambda b: (layer, 0, 0))
    lay4 = lambda a: pl.BlockSpec((1,) + a.shape[1:], lambda b: (layer, 0, 0, 0))
    params = [conv_w, conv_b, wr_bd, b_r, wi_bd, b_i, lam, sconv_w]
    args = [proj] * 5 + params
    in_specs = [col(9), col(10), col(11), col(12), col(13), lay4(conv_w), lay3(conv_b), _full_spec(wr_bd),
                lay3(b_r), _full_spec(wi_bd), lay3(b_i), lay3(lam), lay3(sconv_w)]
    if h0 is not None:
        args.append(h0)
        in_specs.append(pl.BlockSpec((1, 1, 2, W_GRP), lambda b: (b, layer, 0, 0)))
    out_shape = [jax.ShapeDtypeStruct((nseq * seq_len, W_GRP), BF16)] * 2
    out_specs = [pl.BlockSpec((seq_len, W_GRP), lambda b: (b, 0))] * 2
    if want_final:
        out_shape.append(jax.ShapeDtypeStruct((nseq, 2, W_GRP), F32))
        out_specs.append(pl.BlockSpec((1, 2, W_GRP), lambda b: (b, 0, 0)))
    res = pl.pallas_call(
        functools.partial(_rgsc_kernel, seq_len=seq_len, seg=seg, has_h0=h0 is not None,
                          want_final=want_final),
        grid=(nseq,),
        in_specs=in_specs,
        out_specs=out_specs,
        out_shape=out_shape,
        scratch_shapes=[pltpu.VMEM((seq_len + 2 * PAD, W_GRP), F32), pltpu.VMEM((seq_len, W_GRP), F32),
                        pltpu.VMEM((seq_len, W_GRP), F32), pltpu.VMEM((seq_len, W_GRP), F32)],
        compiler_params=_params(("arbitrary",)),
        name="rglru_sconv_mixer",
    )(*args)
    return res[0], res[1], (res[2] if want_final else None)


def _block_diag(w):
    two, h, c, _ = w.shape
    eye = jnp.eye(h, dtype=w.dtype)
    full = w[:, :, :, None, :] * eye[None, :, None, :, None]
    return full.reshape(two, h * c, h * c)


def kernel(x_prompt, x_sample, state_hgrn, state_gla, state_rglru, c, c_ctx, norm1_g, norm2_g, ada_w, ada_b, w_in, w_out, hgrn_lb_logits, hgrn_norm_g, gla_wa2, gla_ba2, gla_norm_g, rg_conv_w, rg_conv_b, rg_w_r, rg_b_r, rg_w_i, rg_b_i, rg_lambda, sconv_w, mlp_w1, mlp_w2, final_norm_g):
    b_ctx, l_ctx, _ = x_prompt.shape
    b_lat, l_lat, _ = x_sample.shape

    cvec = jnp.concatenate([c, c_ctx[None, :], jnp.zeros((8 - b_lat - 1, D_MODEL), F32)], axis=0)
    mod = _modulation(cvec, ada_w, ada_b).reshape(DEPTH, 8, 6, D_MODEL)

    a0 = 2048
    w_a = jnp.stack([w_in[:, :, a0:a0 + GLA_RANK], w_in[:, :, a0 + GLA_RANK:a0 + 2 * GLA_RANK]], axis=1)
    w_z = _fold_lowrank(w_a, gla_wa2)
    w_cat = jnp.concatenate([w_in[:, :, :a0], w_z[:, 0], w_z[:, 1], w_in[:, :, a0 + 2 * GLA_RANK:]],
                            axis=-1).astype(BF16)

    xp = x_prompt.reshape(b_ctx * l_ctx, D_MODEL)
    xs = x_sample.reshape(b_lat * l_lat, D_MODEL)
    new_h, new_g, new_r = [], [], []
    for l in range(DEPTH):
        wr_bd = _block_diag(rg_w_r[l]).astype(BF16)
        wi_bd = _block_diag(rg_w_i[l]).astype(BF16)
        mod_ctx = mod[l, b_lat:b_lat + 1]
        mod_lat = mod[l, :b_lat]
        last = l == DEPTH - 1
        streams = (
            (xp, mod_ctx, b_ctx, l_ctx, l_ctx, None, None, None, True),
            (xs, mod_lat, b_lat, l_lat, GRID_W, state_hgrn, state_gla, state_rglru, False),
        )
        outs = []
        for (x, m, nseq, seq_len, seg, s_h, s_g, s_r, want_final) in streams:
            proj = _project(x, m, norm1_g, w_cat, l)
            o_a, f_h = _hgrn_mixer(proj, hgrn_lb_logits, hgrn_norm_g, s_h, l, nseq, seq_len, want_final)
            o_b, f_g = _gla_mixer(proj, gla_ba2, gla_norm_g, s_g, l, nseq, seq_len, want_final)
            o_c, o_d, f_r = _rgsc_mixer(proj, rg_conv_w, rg_conv_b, wr_bd, rg_b_r, wi_bd, rg_b_i,
                                        rg_lambda, sconv_w, s_r, l, nseq, seq_len, seg, want_final)
            x_new = _out_mlp(x, o_a, o_b, o_c, o_d, m, norm2_g, final_norm_g, w_out, mlp_w1, mlp_w2, l, last)
            outs.append((x_new, f_h, f_g, f_r))
        xp, f_h, f_g, f_r = outs[0]
        xs = outs[1][0]
        new_h.append(f_h)
        new_g.append(f_g)
        new_r.append(f_r)
    y_prompt = xp.reshape(b_ctx, l_ctx, D_MODEL)
    y_sample = xs.reshape(b_lat, l_lat, D_MODEL)
    return (y_prompt, y_sample, jnp.stack(new_h, axis=1), jnp.stack(new_g, axis=1),
            jnp.stack(new_r, axis=1))
```

```python
import functools

import numpy as np
import jax
import jax.numpy as jnp
from jax import lax
from jax.experimental import pallas as pl
from jax.experimental.pallas import tpu as pltpu

F32 = jnp.float32
BF16 = jnp.bfloat16

D_MODEL = 1024
DEPTH = 2
GRID_W = 64
N_HEADS = 4
W_GRP = 256
HEAD_V = 64
HGRN_DK = 64
GLA_DK = 32
GLA_RANK = 16
GLA_NORMALIZER = 16.0
RG_C = 8.0
D_FF = 4 * D_MODEL
EPS = 1e-6
F_FLOOR = 1e-20

PROJ_COLS = 3584

CHUNK = 64
HALF = CHUNK // 2
assert CHUNK == HEAD_V
GROUP = 4
GROWS = GROUP * CHUNK
MAIN_GROUP = 8
BLOCK_ROWS = 1024
LANES = 128
SAFE_RANGE = 80.0
TOK_TILE = 1024
PROJ_TM = 512
PROJ_TN = 512
FF_TILE = 1024
MLP_ROW_SPLIT = 2
ROW_BLOCK = 256
VMEM_LIMIT = 56 * 1024 * 1024

_NT = (((1,), (1,)), ((), ()))
_TN = (((0,), (0,)), ((), ()))


def _dot(a, b):
    return jnp.dot(a.astype(BF16), b.astype(BF16), preferred_element_type=F32)


def _dot_nt(a, b):
    return lax.dot_general(a.astype(BF16), b.astype(BF16), _NT, preferred_element_type=F32)


def _dot_tn(a, b):
    return lax.dot_general(a.astype(BF16), b.astype(BF16), _TN, preferred_element_type=F32)


def _sigmoid(x):
    return 1.0 / (1.0 + jnp.exp(-x))


def _silu(x):
    return x * _sigmoid(x)


def _log1p(y):
    u = 1.0 + y
    return jnp.where(u == 1.0, y, jnp.log(u) * (y / (u - 1.0)))


def _softplus(x):
    return jnp.maximum(x, 0.0) + _log1p(jnp.exp(-jnp.abs(x)))


def _rms(x):
    return x * lax.rsqrt(jnp.mean(x * x, axis=-1, keepdims=True) + EPS)


def _params(sem):
    return pltpu.CompilerParams(dimension_semantics=sem, vmem_limit_bytes=VMEM_LIMIT)


def _full_spec(a):
    zeros = (0,) * a.ndim
    return pl.BlockSpec(a.shape, lambda *_: zeros)


def _mod_kernel(c_ref, w_ref, b_ref, o_ref):
    s = _silu(c_ref[...])
    o_ref[0] = _dot(s, w_ref[0]) + b_ref[0]


def _modulation(cvec, ada_w, ada_b):
    tn = 2048
    return pl.pallas_call(
        _mod_kernel,
        grid=(DEPTH, 6 * D_MODEL // tn),
        in_specs=[pl.BlockSpec((8, D_MODEL), lambda l, j: (0, 0)),
                  pl.BlockSpec((1, D_MODEL, tn), lambda l, j: (l, 0, j)),
                  pl.BlockSpec((1, 1, tn), lambda l, j: (l, 0, j))],
        out_specs=pl.BlockSpec((1, 8, tn), lambda l, j: (l, 0, j)),
        out_shape=jax.ShapeDtypeStruct((DEPTH, 8, 6 * D_MODEL), F32),
        compiler_params=_params(("arbitrary", "arbitrary")),
        name="adaln_modulation",
    )(cvec, ada_w, ada_b.reshape(DEPTH, 1, 6 * D_MODEL))


def _fold_kernel(a_ref, b_ref, o_ref):
    o_ref[0, 0] = jnp.dot(a_ref[0, 0], b_ref[0, 0], preferred_element_type=F32,
                          precision=lax.Precision.HIGHEST)


def _fold_lowrank(w_a, wa2):
    n = N_HEADS * GLA_DK
    return pl.pallas_call(
        _fold_kernel,
        grid=(DEPTH, 2),
        in_specs=[pl.BlockSpec((1, 1, D_MODEL, GLA_RANK), lambda l, d: (l, d, 0, 0)),
                  pl.BlockSpec((1, 1, GLA_RANK, n), lambda l, d: (l, d, 0, 0))],
        out_specs=pl.BlockSpec((1, 1, D_MODEL, n), lambda l, d: (l, d, 0, 0)),
        out_shape=jax.ShapeDtypeStruct((DEPTH, 2, D_MODEL, n), F32),
        compiler_params=_params(("arbitrary", "arbitrary")),
        name="gla_lowrank_fold",
    )(w_a, wa2)


def _proj_kernel(x_ref, mod_ref, g_ref, w_ref, o_ref):
    shift = mod_ref[0, 0:1, :]
    scale = mod_ref[0, 1:2, :]
    h = (_rms(x_ref[...]) * g_ref[0] * (1.0 + scale) + shift).astype(BF16)
    for n in range(PROJ_COLS // PROJ_TN):
        cols = slice(n * PROJ_TN, (n + 1) * PROJ_TN)
        o_ref[:, cols] = jnp.dot(h, w_ref[0, :, cols], preferred_element_type=F32)


def _mod_index(ntok, nmod, tile):
    tiles_per_mod = (ntok // tile) // nmod if nmod > 1 else 1
    if nmod > 1:
        return lambda i, *_: (i // tiles_per_mod, 0, 0)
    return lambda i, *_: (0, 0, 0)


def _project(x, mod, norm_g, w_cat, layer):
    ntok = x.shape[0]
    return pl.pallas_call(
        _proj_kernel,
        grid=(ntok // PROJ_TM,),
        in_specs=[pl.BlockSpec((PROJ_TM, D_MODEL), lambda i: (i, 0)),
                  pl.BlockSpec((1, 6, D_MODEL), _mod_index(ntok, mod.shape[0], PROJ_TM)),
                  pl.BlockSpec((1, 1, D_MODEL), lambda i: (layer, 0, 0)),
                  pl.BlockSpec((1, D_MODEL, PROJ_COLS), lambda i: (layer, 0, 0))],
        out_specs=pl.BlockSpec((PROJ_TM, PROJ_COLS), lambda i: (i, 0)),
        out_shape=jax.ShapeDtypeStruct((ntok, PROJ_COLS), F32),
        compiler_params=_params(("arbitrary",)),
        name="norm_in_proj",
    )(x, mod, norm_g.reshape(DEPTH, 1, D_MODEL), w_cat)


def _mlp_kernel(x_ref, oa_ref, ob_ref, oc_ref, od_ref, mod_ref, g2_ref, fg_ref, wout_ref,
                w1_ref, w2_ref, out_ref, h2_scr, *, final_norm):
    j = pl.program_id(1)

    @pl.when(j == 0)
    def _():
        mixed = jnp.concatenate([oa_ref[...], ob_ref[...], oc_ref[...], od_ref[...]], axis=1)
        mix = jnp.dot(mixed, wout_ref[0].astype(BF16), preferred_element_type=F32)
        x1 = x_ref[...] + mod_ref[0, 2:3, :] * mix
        out_ref[...] = x1
        h2 = _rms(x1) * g2_ref[0] * (1.0 + mod_ref[0, 4:5, :]) + mod_ref[0, 3:4, :]
        h2_scr[...] = h2.astype(BF16)

    w1 = w1_ref[0].astype(BF16)
    w2 = w2_ref[0].astype(BF16)
    for r in range(MLP_ROW_SPLIT):
        rows = slice(r * (TOK_TILE // MLP_ROW_SPLIT), (r + 1) * (TOK_TILE // MLP_ROW_SPLIT))
        t = jnp.dot(h2_scr[rows, :], w1, preferred_element_type=F32)
        t = jnp.square(jnp.maximum(t, 0.0)).astype(BF16)
        out_ref[rows, :] += mod_ref[0, 5:6, :] * jnp.dot(t, w2, preferred_element_type=F32)

    if final_norm:
        @pl.when(j == pl.num_programs(1) - 1)
        def _():
            out_ref[...] = _rms(out_ref[...]) * fg_ref[...]


def _out_mlp(x, o_a, o_b, o_c, o_d, mod, norm2_g, final_g, w_out, w1, w2, layer, final_norm):
    ntok = x.shape[0]
    tok = lambda i, j: (i, 0)
    return pl.pallas_call(
        functools.partial(_mlp_kernel, final_norm=final_norm),
        grid=(ntok // TOK_TILE, D_FF // FF_TILE),
        in_specs=[pl.BlockSpec((TOK_TILE, D_MODEL), tok),
                  pl.BlockSpec((TOK_TILE, W_GRP), tok),
                  pl.BlockSpec((TOK_TILE, W_GRP), tok),
                  pl.BlockSpec((TOK_TILE, W_GRP), tok),
                  pl.BlockSpec((TOK_TILE, W_GRP), tok),
                  pl.BlockSpec((1, 6, D_MODEL), _mod_index(ntok, mod.shape[0], TOK_TILE)),
                  pl.BlockSpec((1, 1, D_MODEL), lambda i, j: (layer, 0, 0)),
                  pl.BlockSpec((1, D_MODEL), lambda i, j: (0, 0)),
                  pl.BlockSpec((1, D_MODEL, D_MODEL), lambda i, j: (layer, 0, 0)),
                  pl.BlockSpec((1, D_MODEL, FF_TILE), lambda i, j: (layer, 0, j)),
                  pl.BlockSpec((1, FF_TILE, D_MODEL), lambda i, j: (layer, j, 0))],
        out_specs=pl.BlockSpec((TOK_TILE, D_MODEL), tok),
        out_shape=jax.ShapeDtypeStruct((ntok, D_MODEL), F32),
        scratch_shapes=[pltpu.VMEM((TOK_TILE, D_MODEL), BF16)],
        compiler_params=_params(("arbitrary", "arbitrary")),
        name="out_proj_mlp",
    )(x, o_a, o_b, o_c, o_d, mod, norm2_g.reshape(DEPTH, 1, D_MODEL), final_g.reshape(1, D_MODEL),
      w_out, w1, w2)


def _split3(x):
    hi = x.astype(BF16)
    r1 = x - hi.astype(F32)
    mid = r1.astype(BF16)
    lo = (r1 - mid.astype(F32)).astype(BF16)
    return hi, mid, lo


def _dot_exact_rhs(x, m):
    hi, mid, lo = _split3(x)
    s = jnp.dot(hi, m, preferred_element_type=F32)
    s += jnp.dot(mid, m, preferred_element_type=F32)
    s += jnp.dot(lo, m, preferred_element_type=F32)
    return s


def _block_cumsum(tri, g):
    hi, mid, lo = _split3(g)
    s = jnp.dot(tri, hi, preferred_element_type=F32)
    s += jnp.dot(tri, mid, preferred_element_type=F32)
    s += jnp.dot(tri, lo, preferred_element_type=F32)
    return s


def _half_ranges(cum, reverse):
    if reverse:
        second = -cum[HALF:HALF + 1, :]
        first = -(cum[0:1, :] - cum[HALF:HALF + 1, :])
    else:
        first = -cum[HALF - 1:HALF, :]
        second = -(cum[CHUNK - 1:CHUNK, :] - cum[HALF - 1:HALF, :])
    return jnp.maximum(first, second)


def _block_ranges(cum, reverse, rm):
    for c in range(GROUP):
        rm = jnp.maximum(rm, _half_ranges(cum[c * CHUNK:(c + 1) * CHUNK, :], reverse))
    return rm


def _row_start(c):
    r0 = c * CHUNK
    return r0 if isinstance(r0, int) else pl.multiple_of(r0, CHUNK)


class _Rec:
    def __init__(self, q_at, k_at, v_ref, cum_scr, st_scr, oacc, consts, width, seq_len, nsub):
        self.q_at, self.k_at, self.v_ref = q_at, k_at, v_ref
        self.cum_scr, self.st_scr, self.oacc = cum_scr, st_scr, oacc
        (self.tri_ref, self.bdw_b, self.bdw_f, self.bdv_b, self.cmask, self.bmat,
         self.tile_t, self.tile_tt) = consts
        self.width, self.seq_len, self.nsub = width, seq_len, nsub


def _chunk_step(rec, sub, d, c, fast):
    reverse = d == 1
    mid_row = HALF if reverse else HALF - 1
    last_row = 0 if reverse else CHUNK - 1
    sd = sub * 2 + d
    r0 = sub * rec.seq_len + c * CHUNK
    if not isinstance(r0, int):
        r0 = pl.multiple_of(r0, CHUNK)
    rows = pl.ds(r0, CHUNK)
    q = rec.q_at(r0)
    k = rec.k_at(d, r0)
    v_b = rec.v_ref[rows, :].astype(BF16)
    cum = rec.cum_scr[d, rows, :]
    tot = cum[last_row:last_row + 1, :]
    st = rec.st_scr[sd]
    o = _dot_nt(q * jnp.exp(cum), st)
    if fast:
        cm = cum[mid_row:mid_row + 1, :]
        qm = q * jnp.exp(cum - cm)
        km_b = (k * jnp.exp(cm - cum)).astype(BF16)
        kbd = jnp.concatenate([km_b] * N_HEADS, axis=0) * rec.bdw_b[...]
        sc = lax.dot_general(qm.astype(BF16), kbd, _NT, preferred_element_type=F32)
        a = jnp.where(rec.cmask[d] > 0.5, sc, 0.0)
        vexp = jnp.concatenate([v_b] * N_HEADS, axis=0) * rec.bdv_b[...]
        o = o + jnp.dot(a.astype(BF16), vexp, preferred_element_type=F32)
    else:
        row_id = lax.broadcasted_iota(jnp.int32, (CHUNK, rec.width), 0)

        def key_row(j, acc):
            krow = rec.k_at(d, r0 + j, 1)
            crow = rec.cum_scr[d, pl.ds(r0 + j, 1), :]
            vrow = rec.v_ref[pl.ds(r0 + j, 1), :]
            keep = (row_id <= j) if reverse else (row_id >= j)
            e = jnp.exp(jnp.minimum(cum - crow, 0.0))
            p = jnp.where(keep, q * krow * e, 0.0)
            return acc + _dot(p, rec.bmat[...]) * vrow

        o = lax.fori_loop(0, CHUNK, key_row, o)
    ke_b = (k * jnp.exp(tot - cum)).astype(BF16)
    ds = lax.dot_general(v_b, ke_b, _TN, preferred_element_type=F32)
    rec.st_scr[sd] = st * jnp.exp(tot) + ds * rec.bdw_f[...]
    rec.oacc[d, rows, :] = o


def _load_state(rec, s0_ref, sub, d):
    x = jnp.concatenate([s0_ref[sub, 0, d, h] for h in range(N_HEADS)], axis=0)
    y = _dot_exact_rhs(x, rec.tile_t[...]) * rec.bmat[...].astype(F32)
    rec.st_scr[sub * 2 + d] = y.T


def _store_state(rec, sfin_ref, sub, d):
    y = rec.st_scr[sub * 2 + d].T
    x = _dot_exact_rhs(y, rec.tile_tt[...])
    dk = rec.width // N_HEADS
    for h in range(N_HEADS):
        sfin_ref[sub, d, h] = x[h * dk:(h + 1) * dk, :]


def _run_recurrence(rec, ok, s0_ref, sfin_ref):
    nsub = rec.nsub
    nchunk = rec.seq_len // CHUNK
    gsize = min(MAIN_GROUP, nchunk)
    ngroup = nchunk // gsize
    spp = max(1, min(nsub, MAIN_GROUP // gsize))
    nsp = nsub // spp
    for sub in range(nsub):
        for d in (0, 1):
            if s0_ref is None:
                rec.st_scr[sub * 2 + d] = jnp.zeros(rec.st_scr.shape[1:], F32)
            else:
                _load_state(rec, s0_ref, sub, d)

    def fast_all():
        def step(i, carry):
            sp = i // ngroup if (nsp > 1 and ngroup > 1) else (i if nsp > 1 else 0)
            gi = i % ngroup if (nsp > 1 and ngroup > 1) else (i if ngroup > 1 else 0)
            for u in range(gsize):
                s = gi * gsize + u
                for j in range(spp):
                    sub = sp * spp + j
                    _chunk_step(rec, sub, 0, s, True)
                    _chunk_step(rec, sub, 1, nchunk - 1 - s, True)
            return carry

        if nsp * ngroup == 1:
            step(0, 0)
        else:
            lax.fori_loop(0, nsp * ngroup, step, 0)

    def direct_all():
        def one(i, carry):
            sub = i // nchunk if nsub > 1 else 0
            s = i % nchunk if nsub > 1 else i
            _chunk_step(rec, sub, 0, s, False)
            _chunk_step(rec, sub, 1, nchunk - 1 - s, False)
            return carry

        lax.fori_loop(0, nsub * nchunk, one, 0)

    lax.cond(ok, fast_all, direct_all)
    if sfin_ref is not None:
        for sub in range(nsub):
            for d in (0, 1):
                _store_state(rec, sfin_ref, sub, d)


def _head_norm_gate(rec, gate_ref, gain_ref, o_ref):
    for rb in range(rec.nsub * rec.seq_len // ROW_BLOCK):
        rows = slice(rb * ROW_BLOCK, (rb + 1) * ROW_BLOCK)
        o = rec.oacc[0, rows, :] + rec.oacc[1, rows, :]
        ms = jnp.dot((o * o).astype(BF16), rec.bdv_b[...], preferred_element_type=F32) * (1.0 / HEAD_V)
        y = o * lax.rsqrt(ms + EPS) * gain_ref[0] * _silu(gate_ref[rows, :])
        o_ref[rows, :] = y.astype(o_ref.dtype)


def _split_refs(refs, n_in, has_s0, want_final):
    ins = refs[:n_in]
    pos = n_in
    s0_ref = None
    if has_s0:
        s0_ref = refs[pos]
        pos += 1
    o_ref = refs[pos]
    pos += 1
    sfin_ref = None
    if want_final:
        sfin_ref = refs[pos]
        pos += 1
    return ins, s0_ref, o_ref, sfin_ref, refs[pos:]


def _hgrn_kernel(*refs, seq_len, nsub, layer, has_s0, want_final):
    ins, s0_ref, o_ref, sfin_ref, scr = _split_refs(refs, 15, has_s0, want_final)
    q_ref, i_ref, ff_ref, fb_ref, g_ref, lbl_ref, gain_ref = ins[:7]
    k_scr, cum_scr, oacc, st_scr = scr
    q_at = lambda r0: q_ref[pl.ds(r0, CHUNK), :]
    k_at = lambda d, r0, n=CHUNK: k_scr[d, pl.ds(r0, n), :]
    rec = _Rec(q_at, k_at, i_ref, cum_scr, st_scr, oacc, ins[7:], W_GRP, seq_len, nsub)

    lg = lbl_ref[...]
    mx = lg[0]
    for i in range(1, DEPTH):
        mx = jnp.maximum(mx, lg[i])
    ex = [jnp.exp(lg[i] - mx) for i in range(DEPTH)]
    den = ex[0]
    for i in range(1, DEPTH):
        den = den + ex[i]
    sm = [e / den for e in ex]
    csum = sm[0]
    for i in range(1, layer + 1):
        csum = csum + sm[i]
    lb = csum - sm[0]

    def pre(bi, rm):
        r0 = bi * GROWS if isinstance(bi, int) else pl.multiple_of(bi * GROWS, GROWS)
        rows = pl.ds(r0, GROWS)
        for d, f_ref in ((0, ff_ref), (1, fb_ref)):
            lbd = lb[d:d + 1, :]
            f = lbd + (1.0 - lbd) * _sigmoid(f_ref[rows, :])
            g = jnp.log(jnp.maximum(f, F_FLOOR))
            k_scr[d, rows, :] = 1.0 - f
            cum = _block_cumsum(rec.tri_ref[d], g)
            cum_scr[d, rows, :] = cum
            rm = _block_ranges(cum, d == 1, rm)
        return rm

    rmax = jnp.zeros((1, W_GRP), F32)
    nblock = nsub * seq_len // GROWS
    rmax = pre(0, rmax) if nblock == 1 else lax.fori_loop(0, nblock, pre, rmax)
    ok = jnp.max(rmax) < SAFE_RANGE

    _run_recurrence(rec, ok, s0_ref, sfin_ref)
    _head_norm_gate(rec, g_ref, gain_ref, o_ref)


def _gla_kernel(*refs, seq_len, nsub, has_s0, want_final):
    ins, s0_ref, o_ref, sfin_ref, scr = _split_refs(refs, 16, has_s0, want_final)
    q_ref, k_ref, v_ref, g_ref, zf_ref, zb_ref, ba_ref, gain_ref = ins[:8]
    cum_scr, oacc, st_scr = scr
    width = N_HEADS * GLA_DK
    q_at = lambda r0: q_ref[pl.ds(r0, CHUNK), :] * (GLA_DK ** -0.5)
    k_at = lambda d, r0, n=CHUNK: k_ref[pl.ds(r0, n), :]
    rec = _Rec(q_at, k_at, v_ref, cum_scr, st_scr, oacc, ins[8:], width, seq_len, nsub)

    def pre(bi, rm):
        r0 = bi * GROWS if isinstance(bi, int) else pl.multiple_of(bi * GROWS, GROWS)
        rows = pl.ds(r0, GROWS)
        for d, z_ref in ((0, zf_ref), (1, zb_ref)):
            z = z_ref[rows, :] + ba_ref[0, d:d + 1, :]
            g = -_softplus(-z) * (1.0 / GLA_NORMALIZER)
            cum = _block_cumsum(rec.tri_ref[d], g)
            cum_scr[d, rows, :] = cum
            rm = _block_ranges(cum, d == 1, rm)
        return rm

    rmax = jnp.zeros((1, width), F32)
    nblock = nsub * seq_len // GROWS
    rmax = pre(0, rmax) if nblock == 1 else lax.fori_loop(0, nblock, pre, rmax)
    ok = jnp.max(rmax) < SAFE_RANGE

    _run_recurrence(rec, ok, s0_ref, sfin_ref)
    _head_norm_gate(rec, g_ref, gain_ref, o_ref)


def _recurrence_constants(width):
    dk = width // N_HEADS
    r = np.arange(N_HEADS * CHUNK)[:, None]
    tri = np.kron(np.eye(GROUP, dtype=np.float32), np.tril(np.ones((CHUNK, CHUNK), np.float32)))
    tri = np.stack([tri, tri.T])
    bdw = (r // CHUNK == np.arange(width)[None, :] // dk).astype(np.float32)
    bdv = (r // CHUNK == np.arange(W_GRP)[None, :] // HEAD_V).astype(np.float32)
    t = np.arange(CHUNK)[:, None]
    s = np.arange(N_HEADS * CHUNK)[None, :] % CHUNK
    cmask = np.stack([(s <= t), (s >= t)]).astype(np.float32)
    bmat = (np.arange(width)[:, None] // dk == np.arange(W_GRP)[None, :] // HEAD_V).astype(np.float32)
    tile_t = np.tile(np.eye(HEAD_V, dtype=np.float32), (1, N_HEADS))
    return (jnp.asarray(tri, BF16), jnp.asarray(bdw, BF16), jnp.asarray(bdw), jnp.asarray(bdv, BF16),
            jnp.asarray(cmask), jnp.asarray(bmat, BF16), jnp.asarray(tile_t, BF16),
            jnp.asarray(tile_t.T, BF16))


def _recurrent_mixer(kernel_fn, name, width, args, in_specs, state, layer, nseq, seq_len, want_final,
                     extra_scratch):
    dk = width // N_HEADS
    nsub = _seqs_per_block(nseq, seq_len)
    rows = nsub * seq_len
    consts = _recurrence_constants(width)
    args = list(args) + list(consts)
    in_specs = list(in_specs) + [_full_spec(c) for c in consts]
    if state is not None:
        args.append(state)
        in_specs.append(pl.BlockSpec((nsub, 1, 2, N_HEADS, dk, HEAD_V), lambda b: (b, layer, 0, 0, 0, 0)))
    out_shape = [jax.ShapeDtypeStruct((nseq * seq_len, W_GRP), BF16)]
    out_specs = [pl.BlockSpec((rows, W_GRP), lambda b: (b, 0))]
    if want_final:
        out_shape.append(jax.ShapeDtypeStruct((nseq, 2, N_HEADS, dk, HEAD_V), F32))
        out_specs.append(pl.BlockSpec((nsub, 2, N_HEADS, dk, HEAD_V), lambda b: (b, 0, 0, 0, 0)))
    res = pl.pallas_call(
        functools.partial(kernel_fn, seq_len=seq_len, nsub=nsub, has_s0=state is not None,
                          want_final=want_final),
        grid=(nseq // nsub,),
        in_specs=in_specs,
        out_specs=out_specs,
        out_shape=out_shape,
        scratch_shapes=[pltpu.VMEM((2, rows, W_GRP), F32)] * extra_scratch + [
            pltpu.VMEM((2, rows, width), F32), pltpu.VMEM((2, rows, W_GRP), F32),
            pltpu.VMEM((2 * nsub, W_GRP, width), F32)],
        compiler_params=_params(("arbitrary",)),
        name=name,
    )(*args)
    return res[0], (res[1] if want_final else None)


def _seqs_per_block(nseq, seq_len):
    return max(1, min(nseq, BLOCK_ROWS // seq_len))


def _hgrn_mixer(proj, lb_logits, gain, state, layer, nseq, seq_len, want_final):
    rows = _seqs_per_block(nseq, seq_len) * seq_len
    col = lambda c: pl.BlockSpec((rows, W_GRP), lambda b: (b, c))
    args = [proj] * 5 + [lb_logits, gain.reshape(DEPTH, 1, W_GRP)]
    in_specs = [col(0), col(1), col(2), col(3), col(4), _full_spec(lb_logits),
                pl.BlockSpec((1, 1, W_GRP), lambda b: (layer, 0, 0))]
    return _recurrent_mixer(functools.partial(_hgrn_kernel, layer=layer), "hgrn2_mixer", W_GRP, args,
                            in_specs, state, layer, nseq, seq_len, want_final, 1)


def _gla_mixer(proj, ba2, gain, state, layer, nseq, seq_len, want_final):
    width = N_HEADS * GLA_DK
    rows = _seqs_per_block(nseq, seq_len) * seq_len
    col256 = lambda c: pl.BlockSpec((rows, W_GRP), lambda b: (b, c))
    col128 = lambda c: pl.BlockSpec((rows, width), lambda b: (b, c))
    args = [proj] * 6 + [ba2, gain.reshape(DEPTH, 1, W_GRP)]
    in_specs = [col128(10), col128(11), col256(6), col256(7), col128(16), col128(17),
                pl.BlockSpec((1, 2, width), lambda b: (layer, 0, 0)),
                pl.BlockSpec((1, 1, W_GRP), lambda b: (layer, 0, 0))]
    return _recurrent_mixer(_gla_kernel, "gla_mixer", width, args, in_specs, state, layer, nseq, seq_len,
                            want_final, 0)


PAD = 8
SCAN_TILE = 8


def _rgsc_kernel(*refs, seq_len, seg, has_h0, want_final):
    (cx_ref, cg_ref, db_ref, dc_ref, dv_ref, convw_ref, convb_ref, wr_ref, br_ref, wi_ref,
     bi_ref, lam_ref, sw_ref) = refs[:13]
    pos = 13
    h0_ref = None
    if has_h0:
        h0_ref = refs[pos]
        pos += 1
    oc_ref, od_ref = refs[pos:pos + 2]
    pos += 2
    hfin_ref = None
    if want_final:
        hfin_ref = refs[pos]
        pos += 1
    upad, a_scr, b_scr, h_scr = refs[pos:]

    zpad = jnp.zeros((PAD, W_GRP), F32)
    upad[0:PAD, :] = zpad
    upad[PAD + seq_len:2 * PAD + seq_len, :] = zpad
    upad[PAD:PAD + seq_len, :] = cx_ref[...]
    rows8 = lax.broadcasted_iota(jnp.int32, (SCAN_TILE, W_GRP), 0)
    ntile = seq_len // SCAN_TILE

    for d in (0, 1):
        reverse = d == 1
        w = convw_ref[0, d]
        offs = [PAD + 3, PAD + 2, PAD + 1, PAD] if reverse else [PAD - 3, PAD - 2, PAD - 1, PAD]
        sp = _softplus(-lam_ref[0, d:d + 1, :])
        for rb in range(seq_len // ROW_BLOCK):
            base = rb * ROW_BLOCK
            xc = convb_ref[0, d:d + 1, :]
            for j in range(4):
                xc = xc + w[j:j + 1, :] * upad[offs[j] + base:offs[j] + base + ROW_BLOCK, :]
            r = _sigmoid(_dot(xc, wr_ref[d]) + br_ref[0, d:d + 1, :])
            ig = _sigmoid(_dot(xc, wi_ref[d]) + bi_ref[0, d:d + 1, :])
            log_a = -RG_C * r * sp
            a = jnp.exp(log_a)
            a_scr[base:base + ROW_BLOCK, :] = a
            b_scr[base:base + ROW_BLOCK, :] = jnp.sqrt(1.0 - a * a) * (ig * xc)

        h0 = h0_ref[0, 0, d:d + 1, :] if has_h0 else jnp.zeros((1, W_GRP), F32)

        def step(ti, carry, d=d, reverse=reverse):
            t = (ntile - 1 - ti) if reverse else ti
            r0 = pl.multiple_of(t * SCAN_TILE, SCAN_TILE)
            a = a_scr[pl.ds(r0, SCAN_TILE), :]
            b = b_scr[pl.ds(r0, SCAN_TILE), :]
            for s in (1, 2, 4):
                if reverse:
                    keep = rows8 <= SCAN_TILE - 1 - s
                    shift = SCAN_TILE - s
                else:
                    keep = rows8 >= s
                    shift = s
                a_s = jnp.where(keep, pltpu.roll(a, shift, 0), 1.0)
                b_s = jnp.where(keep, pltpu.roll(b, shift, 0), 0.0)
                b = b + a * b_s
                a = a * a_s
            h = a * carry + b
            if d == 0:
                h_scr[pl.ds(r0, SCAN_TILE), :] = h
            else:
                h_scr[pl.ds(r0, SCAN_TILE), :] += h
            return h[0:1, :] if reverse else h[SCAN_TILE - 1:SCAN_TILE, :]

        hlast = lax.fori_loop(0, ntile, step, h0, unroll=4)
        if want_final:
            hfin_ref[0, d:d + 1, :] = hlast

    sw = sw_ref[0]
    for rb in range(seq_len // ROW_BLOCK):
        rows = slice(rb * ROW_BLOCK, (rb + 1) * ROW_BLOCK)
        x = cg_ref[rows, :]
        gelu = 0.5 * x * (1.0 + jnp.tanh(0.7978845608028654 * (x + 0.044715 * (x * x * x))))
        oc_ref[rows, :] = (h_scr[rows, :] * gelu).astype(oc_ref.dtype)
        upad[PAD + rb * ROW_BLOCK:PAD + (rb + 1) * ROW_BLOCK, :] = dc_ref[rows, :] * dv_ref[rows, :]
    for rb in range(seq_len // ROW_BLOCK):
        base = rb * ROW_BLOCK
        rows = slice(base, base + ROW_BLOCK)
        posn = (lax.broadcasted_iota(jnp.int32, (ROW_BLOCK, W_GRP), 0) + base) % seg
        left = jnp.where(posn != 0, upad[PAD - 1 + base:PAD - 1 + base + ROW_BLOCK, :], 0.0)
        right = jnp.where(posn != seg - 1, upad[PAD + 1 + base:PAD + 1 + base + ROW_BLOCK, :], 0.0)
        y = sw[0:1, :] * left + sw[1:2, :] * upad[PAD + base:PAD + base + ROW_BLOCK, :] + sw[2:3, :] * right
        od_ref[rows, :] = (db_ref[rows, :] * y).astype(od_ref.dtype)


def _rgsc_mixer(proj, conv_w, conv_b, wr_bd, b_r, wi_bd, b_i, lam, sconv_w, h0, layer, nseq, seq_len,
                seg, want_final):
    col = lambda c: pl.BlockSpec((seq_len, W_GRP), lambda b: (b, c))
    lay3 = lambda a: pl.BlockSpec((1,) + a.shape[1:], lambda b: (layer, 0, 0))
    lay4 = lambda a: pl.BlockSpec((1,) + a.shape[1:], lambda b: (layer, 0, 0, 0))
    params = [conv_w, conv_b, wr_bd, b_r, wi_bd, b_i, lam, sconv_w]
    args = [proj] * 5 + params
    in_specs = [col(9), col(10), col(11), col(12), col(13), lay4(conv_w), lay3(conv_b), _full_spec(wr_bd),
                lay3(b_r), _full_spec(wi_bd), lay3(b_i), lay3(lam), lay3(sconv_w)]
    if h0 is not None:
        args.append(h0)
        in_specs.append(pl.BlockSpec((1, 1, 2, W_GRP), lambda b: (b, layer, 0, 0)))
    out_shape = [jax.ShapeDtypeStruct((nseq * seq_len, W_GRP), BF16)] * 2
    out_specs = [pl.BlockSpec((seq_len, W_GRP), lambda b: (b, 0))] * 2
    if want_final:
        out_shape.append(jax.ShapeDtypeStruct((nseq, 2, W_GRP), F32))
        out_specs.append(pl.BlockSpec((1, 2, W_GRP), lambda b: (b, 0, 0)))
    res = pl.pallas_call(
        functools.partial(_rgsc_kernel, seq_len=seq_len, seg=seg, has_h0=h0 is not None,
                          want_final=want_final),
        grid=(nseq,),
        in_specs=in_specs,
        out_specs=out_specs,
        out_shape=out_shape,
        scratch_shapes=[pltpu.VMEM((seq_len + 2 * PAD, W_GRP), F32), pltpu.VMEM((seq_len, W_GRP), F32),
                        pltpu.VMEM((seq_len, W_GRP), F32), pltpu.VMEM((seq_len, W_GRP), F32)],
        compiler_params=_params(("arbitrary",)),
        name="rglru_sconv_mixer",
    )(*args)
    return res[0], res[1], (res[2] if want_final else None)


def _block_diag(w):
    two, h, c, _ = w.shape
    eye = jnp.eye(h, dtype=w.dtype)
    full = w[:, :, :, None, :] * eye[None, :, None, :, None]
    return full.reshape(two, h * c, h * c)


def kernel(x_prompt, x_sample, state_hgrn, state_gla, state_rglru, c, c_ctx, norm1_g, norm2_g, ada_w, ada_b, w_in, w_out, hgrn_lb_logits, hgrn_norm_g, gla_wa2, gla_ba2, gla_norm_g, rg_conv_w, rg_conv_b, rg_w_r, rg_b_r, rg_w_i, rg_b_i, rg_lambda, sconv_w, mlp_w1, mlp_w2, final_norm_g):
    b_ctx, l_ctx, _ = x_prompt.shape
    b_lat, l_lat, _ = x_sample.shape

    cvec = jnp.concatenate([c, c_ctx[None, :], jnp.zeros((8 - b_lat - 1, D_MODEL), F32)], axis=0)
    mod = _modulation(cvec, ada_w, ada_b).reshape(DEPTH, 8, 6, D_MODEL)

    a0 = 2048
    w_a = jnp.stack([w_in[:, :, a0:a0 + GLA_RANK], w_in[:, :, a0 + GLA_RANK:a0 + 2 * GLA_RANK]], axis=1)
    w_z = _fold_lowrank(w_a, gla_wa2)
    w_cat = jnp.concatenate([w_in[:, :, :a0], w_z[:, 0], w_z[:, 1], w_in[:, :, a0 + 2 * GLA_RANK:]],
                            axis=-1).astype(BF16)

    xp = x_prompt.reshape(b_ctx * l_ctx, D_MODEL)
    xs = x_sample.reshape(b_lat * l_lat, D_MODEL)
    new_h, new_g, new_r = [], [], []
    for l in range(DEPTH):
        wr_bd = _block_diag(rg_w_r[l]).astype(BF16)
        wi_bd = _block_diag(rg_w_i[l]).astype(BF16)
        mod_ctx = mod[l, b_lat:b_lat + 1]
        mod_lat = mod[l, :b_lat]
        last = l == DEPTH - 1
        streams = (
            (xp, mod_ctx, b_ctx, l_ctx, l_ctx, None, None, None, True),
            (xs, mod_lat, b_lat, l_lat, GRID_W, state_hgrn, state_gla, state_rglru, False),
        )
        outs = []
        for (x, m, nseq, seq_len, seg, s_h, s_g, s_r, want_final) in streams:
            proj = _project(x, m, norm1_g, w_cat, l)
            o_a, f_h = _hgrn_mixer(proj, hgrn_lb_logits, hgrn_norm_g, s_h, l, nseq, seq_len, want_final)
            o_b, f_g = _gla_mixer(proj, gla_ba2, gla_norm_g, s_g, l, nseq, seq_len, want_final)
            o_c, o_d, f_r = _rgsc_mixer(proj, rg_conv_w, rg_conv_b, wr_bd, rg_b_r, wi_bd, rg_b_i,
                                        rg_lambda, sconv_w, s_r, l, nseq, seq_len, seg, want_final)
            x_new = _out_mlp(x, o_a, o_b, o_c, o_d, m, norm2_g, final_norm_g, w_out, mlp_w1, mlp_w2, l, last)
            outs.append((x_new, f_h, f_g, f_r))
        xp, f_h, f_g, f_r = outs[0]
        xs = outs[1][0]
        new_h.append(f_h)
        new_g.append(f_g)
        new_r.append(f_r)
    y_prompt = xp.reshape(b_ctx, l_ctx, D_MODEL)
    y_sample = xs.reshape(b_lat, l_lat, D_MODEL)
    return (y_prompt, y_sample, jnp.stack(new_h, axis=1), jnp.stack(new_g, axis=1),
            jnp.stack(new_r, axis=1))
```

```python
import functools

import numpy as np
import jax
import jax.numpy as jnp
from jax import lax
from jax.experimental import pallas as pl
from jax.experimental.pallas import tpu as pltpu

F32 = jnp.float32
BF16 = jnp.bfloat16

D_MODEL = 1024
DEPTH = 2
GRID_W = 64
N_HEADS = 4
W_GRP = 256
HEAD_V = 64
HGRN_DK = 64
GLA_DK = 32
GLA_RANK = 16
GLA_NORMALIZER = 16.0
RG_C = 8.0
D_FF = 4 * D_MODEL
EPS = 1e-6
F_FLOOR = 1e-20

PROJ_COLS = 3584

CHUNK = 64
HALF = CHUNK // 2
assert CHUNK == HEAD_V
GROUP = 4
GROWS = GROUP * CHUNK
MAIN_GROUP = 8
BLOCK_ROWS = 1024
LANES = 128
SAFE_RANGE = 80.0
TOK_TILE = 1024
PROJ_TM = 512
PROJ_TN = 512
FF_TILE = 1024
MLP_ROW_SPLIT = 2
ROW_BLOCK = 256
VMEM_LIMIT = 56 * 1024 * 1024

_NT = (((1,), (1,)), ((), ()))
_TN = (((0,), (0,)), ((), ()))


def _dot(a, b):
    return jnp.dot(a.astype(BF16), b.astype(BF16), preferred_element_type=F32)


def _dot_nt(a, b):
    return lax.dot_general(a.astype(BF16), b.astype(BF16), _NT, preferred_element_type=F32)


def _dot_tn(a, b):
    return lax.dot_general(a.astype(BF16), b.astype(BF16), _TN, preferred_element_type=F32)


def _sigmoid(x):
    return 1.0 / (1.0 + jnp.exp(-x))


def _silu(x):
    return x * _sigmoid(x)


def _log1p(y):
    u = 1.0 + y
    return jnp.where(u == 1.0, y, jnp.log(u) * (y / (u - 1.0)))


def _softplus(x):
    return jnp.maximum(x, 0.0) + _log1p(jnp.exp(-jnp.abs(x)))


def _rms(x):
    return x * lax.rsqrt(jnp.mean(x * x, axis=-1, keepdims=True) + EPS)


def _params(sem):
    return pltpu.CompilerParams(dimension_semantics=sem, vmem_limit_bytes=VMEM_LIMIT)


def _full_spec(a):
    zeros = (0,) * a.ndim
    return pl.BlockSpec(a.shape, lambda *_: zeros)


def _mod_kernel(c_ref, w_ref, b_ref, o_ref):
    s = _silu(c_ref[...])
    o_ref[0] = _dot(s, w_ref[0]) + b_ref[0]


def _modulation(cvec, ada_w, ada_b):
    tn = 2048
    return pl.pallas_call(
        _mod_kernel,
        grid=(DEPTH, 6 * D_MODEL // tn),
        in_specs=[pl.BlockSpec((8, D_MODEL), lambda l, j: (0, 0)),
                  pl.BlockSpec((1, D_MODEL, tn), lambda l, j: (l, 0, j)),
                  pl.BlockSpec((1, 1, tn), lambda l, j: (l, 0, j))],
        out_specs=pl.BlockSpec((1, 8, tn), lambda l, j: (l, 0, j)),
        out_shape=jax.ShapeDtypeStruct((DEPTH, 8, 6 * D_MODEL), F32),
        compiler_params=_params(("arbitrary", "arbitrary")),
        name="adaln_modulation",
    )(cvec, ada_w, ada_b.reshape(DEPTH, 1, 6 * D_MODEL))


def _fold_kernel(a_ref, b_ref, o_ref):
    o_ref[0, 0] = jnp.dot(a_ref[0, 0], b_ref[0, 0], preferred_element_type=F32,
                          precision=lax.Precision.HIGHEST)


def _fold_lowrank(w_a, wa2):
    n = N_HEADS * GLA_DK
    return pl.pallas_call(
        _fold_kernel,
        grid=(DEPTH, 2),
        in_specs=[pl.BlockSpec((1, 1, D_MODEL, GLA_RANK), lambda l, d: (l, d, 0, 0)),
                  pl.BlockSpec((1, 1, GLA_RANK, n), lambda l, d: (l, d, 0, 0))],
        out_specs=pl.BlockSpec((1, 1, D_MODEL, n), lambda l, d: (l, d, 0, 0)),
        out_shape=jax.ShapeDtypeStruct((DEPTH, 2, D_MODEL, n), F32),
        compiler_params=_params(("arbitrary", "arbitrary")),
        name="gla_lowrank_fold",
    )(w_a, wa2)


def _proj_kernel(x_ref, mod_ref, g_ref, w_ref, o_ref):
    shift = mod_ref[0, 0:1, :]
    scale = mod_ref[0, 1:2, :]
    h = (_rms(x_ref[...]) * g_ref[0] * (1.0 + scale) + shift).astype(BF16)
    for n in range(PROJ_COLS // PROJ_TN):
        cols = slice(n * PROJ_TN, (n + 1) * PROJ_TN)
        o_ref[:, cols] = jnp.dot(h, w_ref[0, :, cols], preferred_element_type=F32)


def _mod_index(ntok, nmod, tile):
    tiles_per_mod = (ntok // tile) // nmod if nmod > 1 else 1
    if nmod > 1:
        return lambda i, *_: (i // tiles_per_mod, 0, 0)
    return lambda i, *_: (0, 0, 0)


def _project(x, mod, norm_g, w_cat, layer):
    ntok = x.shape[0]
    return pl.pallas_call(
        _proj_kernel,
        grid=(ntok // PROJ_TM,),
        in_specs=[pl.BlockSpec((PROJ_TM, D_MODEL), lambda i: (i, 0)),
                  pl.BlockSpec((1, 6, D_MODEL), _mod_index(ntok, mod.shape[0], PROJ_TM)),
                  pl.BlockSpec((1, 1, D_MODEL), lambda i: (layer, 0, 0)),
                  pl.BlockSpec((1, D_MODEL, PROJ_COLS), lambda i: (layer, 0, 0))],
        out_specs=pl.BlockSpec((PROJ_TM, PROJ_COLS), lambda i: (i, 0)),
        out_shape=jax.ShapeDtypeStruct((ntok, PROJ_COLS), F32),
        compiler_params=_params(("arbitrary",)),
        name="norm_in_proj",
    )(x, mod, norm_g.reshape(DEPTH, 1, D_MODEL), w_cat)


def _mlp_kernel(x_ref, oa_ref, ob_ref, oc_ref, od_ref, mod_ref, g2_ref, fg_ref, wout_ref,
                w1_ref, w2_ref, out_ref, h2_scr, *, final_norm):
    j = pl.program_id(1)

    @pl.when(j == 0)
    def _():
        mixed = jnp.concatenate([oa_ref[...], ob_ref[...], oc_ref[...], od_ref[...]], axis=1)
        mix = jnp.dot(mixed, wout_ref[0].astype(BF16), preferred_element_type=F32)
        x1 = x_ref[...] + mod_ref[0, 2:3, :] * mix
        out_ref[...] = x1
        h2 = _rms(x1) * g2_ref[0] * (1.0 + mod_ref[0, 4:5, :]) + mod_ref[0, 3:4, :]
        h2_scr[...] = h2.astype(BF16)

    w1 = w1_ref[0].astype(BF16)
    w2 = w2_ref[0].astype(BF16)
    for r in range(MLP_ROW_SPLIT):
        rows = slice(r * (TOK_TILE // MLP_ROW_SPLIT), (r + 1) * (TOK_TILE // MLP_ROW_SPLIT))
        t = jnp.dot(h2_scr[rows, :], w1, preferred_element_type=F32)
        t = jnp.square(jnp.maximum(t, 0.0)).astype(BF16)
        out_ref[rows, :] += mod_ref[0, 5:6, :] * jnp.dot(t, w2, preferred_element_type=F32)

    if final_norm:
        @pl.when(j == pl.num_programs(1) - 1)
        def _():
            out_ref[...] = _rms(out_ref[...]) * fg_ref[...]


def _out_mlp(x, o_a, o_b, o_c, o_d, mod, norm2_g, final_g, w_out, w1, w2, layer, final_norm):
    ntok = x.shape[0]
    tok = lambda i, j: (i, 0)
    return pl.pallas_call(
        functools.partial(_mlp_kernel, final_norm=final_norm),
        grid=(ntok // TOK_TILE, D_FF // FF_TILE),
        in_specs=[pl.BlockSpec((TOK_TILE, D_MODEL), tok),
                  pl.BlockSpec((TOK_TILE, W_GRP), tok),
                  pl.BlockSpec((TOK_TILE, W_GRP), tok),
                  pl.BlockSpec((TOK_TILE, W_GRP), tok),
                  pl.BlockSpec((TOK_TILE, W_GRP), tok),
                  pl.BlockSpec((1, 6, D_MODEL), _mod_index(ntok, mod.shape[0], TOK_TILE)),
                  pl.BlockSpec((1, 1, D_MODEL), lambda i, j: (layer, 0, 0)),
                  pl.BlockSpec((1, D_MODEL), lambda i, j: (0, 0)),
                  pl.BlockSpec((1, D_MODEL, D_MODEL), lambda i, j: (layer, 0, 0)),
                  pl.BlockSpec((1, D_MODEL, FF_TILE), lambda i, j: (layer, 0, j)),
                  pl.BlockSpec((1, FF_TILE, D_MODEL), lambda i, j: (layer, j, 0))],
        out_specs=pl.BlockSpec((TOK_TILE, D_MODEL), tok),
        out_shape=jax.ShapeDtypeStruct((ntok, D_MODEL), F32),
        scratch_shapes=[pltpu.VMEM((TOK_TILE, D_MODEL), BF16)],
        compiler_params=_params(("arbitrary", "arbitrary")),
        name="out_proj_mlp",
    )(x, o_a, o_b, o_c, o_d, mod, norm2_g.reshape(DEPTH, 1, D_MODEL), final_g.reshape(1, D_MODEL),
      w_out, w1, w2)


def _split3(x):
    hi = x.astype(BF16)
    r1 = x - hi.astype(F32)
    mid = r1.astype(BF16)
    lo = (r1 - mid.astype(F32)).astype(BF16)
    return hi, mid, lo


def _dot_exact_rhs(x, m):
    hi, mid, lo = _split3(x)
    s = jnp.dot(hi, m, preferred_element_type=F32)
    s += jnp.dot(mid, m, preferred_element_type=F32)
    s += jnp.dot(lo, m, preferred_element_type=F32)
    return s


def _block_cumsum(tri, g):
    hi = g.astype(BF16)
    lo = (g - hi.astype(F32)).astype(BF16)
    return jnp.dot(tri, hi, preferred_element_type=F32) + jnp.dot(tri, lo, preferred_element_type=F32)


def _run_prepass(pre, nblock, width):
    rmax = jnp.zeros((1, width), F32)
    if nblock % 2:
        for bi in range(nblock):
            rmax = pre(bi, rmax)
        return rmax
    if nblock == 2:
        return pre(1, pre(0, rmax))
    return lax.fori_loop(0, nblock // 2, lambda i, rm: pre(2 * i + 1, pre(2 * i, rm)), rmax)


def _half_ranges(cum, reverse):
    if reverse:
        second = -cum[HALF:HALF + 1, :]
        first = -(cum[0:1, :] - cum[HALF:HALF + 1, :])
    else:
        first = -cum[HALF - 1:HALF, :]
        second = -(cum[CHUNK - 1:CHUNK, :] - cum[HALF - 1:HALF, :])
    return jnp.maximum(first, second)


def _block_ranges(cum, reverse, rm):
    for c in range(GROUP):
        rm = jnp.maximum(rm, _half_ranges(cum[c * CHUNK:(c + 1) * CHUNK, :], reverse))
    return rm


def _row_start(c):
    r0 = c * CHUNK
    return r0 if isinstance(r0, int) else pl.multiple_of(r0, CHUNK)


class _Rec:
    def __init__(self, q_at, k_at, v_ref, cum_scr, st_scr, oacc, consts, width, seq_len, nsub):
        self.q_at, self.k_at, self.v_ref = q_at, k_at, v_ref
        self.cum_scr, self.st_scr, self.oacc = cum_scr, st_scr, oacc
        (self.tri_ref, self.bdw_b, self.bdw_f, self.bdv_b, self.cmask, self.bmat,
         self.tile_t) = consts
        self.width, self.seq_len, self.nsub = width, seq_len, nsub


def _chunk_step(rec, sub, d, c, fast):
    reverse = d == 1
    mid_row = HALF if reverse else HALF - 1
    last_row = 0 if reverse else CHUNK - 1
    sd = sub * 2 + d
    r0 = sub * rec.seq_len + c * CHUNK
    if not isinstance(r0, int):
        r0 = pl.multiple_of(r0, CHUNK)
    rows = pl.ds(r0, CHUNK)
    q = rec.q_at(r0)
    k = rec.k_at(d, r0)
    v_b = rec.v_ref[rows, :].astype(BF16)
    cum = rec.cum_scr[d, rows, :]
    tot = cum[last_row:last_row + 1, :]
    st = rec.st_scr[sd]
    o = _dot_nt(q * jnp.exp(cum), st)
    if fast:
        cm = cum[mid_row:mid_row + 1, :]
        qm = q * jnp.exp(cum - cm)
        km_b = (k * jnp.exp(cm - cum)).astype(BF16)
        kbd = jnp.concatenate([km_b] * N_HEADS, axis=0) * rec.bdw_b[...]
        sc = lax.dot_general(qm.astype(BF16), kbd, _NT, preferred_element_type=F32)
        a = jnp.where(rec.cmask[d] > 0.5, sc, 0.0)
        vexp = jnp.concatenate([v_b] * N_HEADS, axis=0) * rec.bdv_b[...]
        o = o + jnp.dot(a.astype(BF16), vexp, preferred_element_type=F32)
    else:
        row_id = lax.broadcasted_iota(jnp.int32, (CHUNK, rec.width), 0)

        def key_row(j, acc):
            krow = rec.k_at(d, r0 + j, 1)
            crow = rec.cum_scr[d, pl.ds(r0 + j, 1), :]
            vrow = rec.v_ref[pl.ds(r0 + j, 1), :]
            keep = (row_id <= j) if reverse else (row_id >= j)
            e = jnp.exp(jnp.minimum(cum - crow, 0.0))
            p = jnp.where(keep, q * krow * e, 0.0)
            return acc + _dot(p, rec.bmat[...]) * vrow

        o = lax.fori_loop(0, CHUNK, key_row, o)
    ke_b = (k * jnp.exp(tot - cum)).astype(BF16)
    ds = lax.dot_general(v_b, ke_b, _TN, preferred_element_type=F32)
    rec.st_scr[sd] = st * jnp.exp(tot) + ds * rec.bdw_f[...]
    rec.oacc[d, rows, :] = o


def _load_state(rec, s0_ref, sub, d):
    x = jnp.concatenate([s0_ref[sub, 0, d, h] for h in range(N_HEADS)], axis=0)
    y = _dot_exact_rhs(x, rec.tile_t[...]) * rec.bmat[...].astype(F32)
    rec.st_scr[sub * 2 + d] = y.T


def _store_state(rec, sfin_ref, sub, d):
    y = rec.st_scr[sub * 2 + d].T
    half = y[:, 0:LANES] + y[:, LANES:2 * LANES]
    x = (half + pltpu.roll(half, HEAD_V, 1))[:, 0:HEAD_V]
    dk = rec.width // N_HEADS
    for h in range(N_HEADS):
        sfin_ref[sub, d, h] = x[h * dk:(h + 1) * dk, :]


def _run_recurrence(rec, ok, s0_ref, sfin_ref):
    nsub = rec.nsub
    nchunk = rec.seq_len // CHUNK
    gsize = min(MAIN_GROUP, nchunk)
    ngroup = nchunk // gsize
    spp = max(1, min(nsub, MAIN_GROUP // gsize))
    nsp = nsub // spp
    for sub in range(nsub):
        for d in (0, 1):
            if s0_ref is None:
                rec.st_scr[sub * 2 + d] = jnp.zeros(rec.st_scr.shape[1:], F32)
            else:
                _load_state(rec, s0_ref, sub, d)

    def fast_all():
        def step(i, carry):
            sp = i // ngroup if (nsp > 1 and ngroup > 1) else (i if nsp > 1 else 0)
            gi = i % ngroup if (nsp > 1 and ngroup > 1) else (i if ngroup > 1 else 0)
            for u in range(gsize):
                s = gi * gsize + u
                for j in range(spp):
                    sub = sp * spp + j
                    _chunk_step(rec, sub, 0, s, True)
                    _chunk_step(rec, sub, 1, nchunk - 1 - s, True)
            return carry

        if nsp * ngroup == 1:
            step(0, 0)
        else:
            lax.fori_loop(0, nsp * ngroup, step, 0)

    def direct_all():
        def one(i, carry):
            sub = i // nchunk if nsub > 1 else 0
            s = i % nchunk if nsub > 1 else i
            _chunk_step(rec, sub, 0, s, False)
            _chunk_step(rec, sub, 1, nchunk - 1 - s, False)
            return carry

        lax.fori_loop(0, nsub * nchunk, one, 0)

    lax.cond(ok, fast_all, direct_all)
    if sfin_ref is not None:
        for sub in range(nsub):
            for d in (0, 1):
                _store_state(rec, sfin_ref, sub, d)


def _head_norm_gate(rec, gate_ref, gain_ref, o_ref):
    for rb in range(rec.nsub * rec.seq_len // ROW_BLOCK):
        rows = slice(rb * ROW_BLOCK, (rb + 1) * ROW_BLOCK)
        o = rec.oacc[0, rows, :] + rec.oacc[1, rows, :]
        ms = jnp.dot((o * o).astype(BF16), rec.bdv_b[...], preferred_element_type=F32) * (1.0 / HEAD_V)
        y = o * lax.rsqrt(ms + EPS) * gain_ref[0] * _silu(gate_ref[rows, :])
        o_ref[rows, :] = y.astype(o_ref.dtype)


def _split_refs(refs, n_in, has_s0, want_final):
    ins = refs[:n_in]
    pos = n_in
    s0_ref = None
    if has_s0:
        s0_ref = refs[pos]
        pos += 1
    o_ref = refs[pos]
    pos += 1
    sfin_ref = None
    if want_final:
        sfin_ref = refs[pos]
        pos += 1
    return ins, s0_ref, o_ref, sfin_ref, refs[pos:]


def _hgrn_kernel(*refs, seq_len, nsub, layer, has_s0, want_final):
    ins, s0_ref, o_ref, sfin_ref, scr = _split_refs(refs, 14, has_s0, want_final)
    q_ref, i_ref, ff_ref, fb_ref, g_ref, lbl_ref, gain_ref = ins[:7]
    k_scr, cum_scr, oacc, st_scr = scr
    q_at = lambda r0: q_ref[pl.ds(r0, CHUNK), :]
    k_at = lambda d, r0, n=CHUNK: k_scr[d, pl.ds(r0, n), :]
    rec = _Rec(q_at, k_at, i_ref, cum_scr, st_scr, oacc, ins[7:], W_GRP, seq_len, nsub)

    lg = lbl_ref[...]
    mx = lg[0]
    for i in range(1, DEPTH):
        mx = jnp.maximum(mx, lg[i])
    ex = [jnp.exp(lg[i] - mx) for i in range(DEPTH)]
    den = ex[0]
    for i in range(1, DEPTH):
        den = den + ex[i]
    sm = [e / den for e in ex]
    csum = sm[0]
    for i in range(1, layer + 1):
        csum = csum + sm[i]
    lb = csum - sm[0]

    def pre(bi, rm):
        r0 = bi * GROWS if isinstance(bi, int) else pl.multiple_of(bi * GROWS, GROWS)
        rows = pl.ds(r0, GROWS)
        for d, f_ref in ((0, ff_ref), (1, fb_ref)):
            lbd = lb[d:d + 1, :]
            f = lbd + (1.0 - lbd) * _sigmoid(f_ref[rows, :])
            g = jnp.log(jnp.maximum(f, F_FLOOR))
            k_scr[d, rows, :] = 1.0 - f
            cum = _block_cumsum(rec.tri_ref[d], g)
            cum_scr[d, rows, :] = cum
            rm = _block_ranges(cum, d == 1, rm)
        return rm

    rmax = _run_prepass(pre, nsub * seq_len // GROWS, W_GRP)
    ok = jnp.max(rmax) < SAFE_RANGE

    _run_recurrence(rec, ok, s0_ref, sfin_ref)
    _head_norm_gate(rec, g_ref, gain_ref, o_ref)


def _gla_kernel(*refs, seq_len, nsub, has_s0, want_final):
    ins, s0_ref, o_ref, sfin_ref, scr = _split_refs(refs, 15, has_s0, want_final)
    q_ref, k_ref, v_ref, g_ref, zf_ref, zb_ref, ba_ref, gain_ref = ins[:8]
    cum_scr, oacc, st_scr = scr
    width = N_HEADS * GLA_DK
    q_at = lambda r0: q_ref[pl.ds(r0, CHUNK), :] * (GLA_DK ** -0.5)
    k_at = lambda d, r0, n=CHUNK: k_ref[pl.ds(r0, n), :]
    rec = _Rec(q_at, k_at, v_ref, cum_scr, st_scr, oacc, ins[8:], width, seq_len, nsub)

    def pre(bi, rm):
        r0 = bi * GROWS if isinstance(bi, int) else pl.multiple_of(bi * GROWS, GROWS)
        rows = pl.ds(r0, GROWS)
        for d, z_ref in ((0, zf_ref), (1, zb_ref)):
            z = z_ref[rows, :] + ba_ref[0, d:d + 1, :]
            g = -_softplus(-z) * (1.0 / GLA_NORMALIZER)
            cum = _block_cumsum(rec.tri_ref[d], g)
            cum_scr[d, rows, :] = cum
            rm = _block_ranges(cum, d == 1, rm)
        return rm

    rmax = _run_prepass(pre, nsub * seq_len // GROWS, width)
    ok = jnp.max(rmax) < SAFE_RANGE

    _run_recurrence(rec, ok, s0_ref, sfin_ref)
    _head_norm_gate(rec, g_ref, gain_ref, o_ref)


def _recurrence_constants(width):
    dk = width // N_HEADS
    r = np.arange(N_HEADS * CHUNK)[:, None]
    tri = np.kron(np.eye(GROUP, dtype=np.float32), np.tril(np.ones((CHUNK, CHUNK), np.float32)))
    tri = np.stack([tri, tri.T])
    bdw = (r // CHUNK == np.arange(width)[None, :] // dk).astype(np.float32)
    bdv = (r // CHUNK == np.arange(W_GRP)[None, :] // HEAD_V).astype(np.float32)
    t = np.arange(CHUNK)[:, None]
    s = np.arange(N_HEADS * CHUNK)[None, :] % CHUNK
    cmask = np.stack([(s <= t), (s >= t)]).astype(np.float32)
    bmat = (np.arange(width)[:, None] // dk == np.arange(W_GRP)[None, :] // HEAD_V).astype(np.float32)
    tile_t = np.tile(np.eye(HEAD_V, dtype=np.float32), (1, N_HEADS))
    return (jnp.asarray(tri, BF16), jnp.asarray(bdw, BF16), jnp.asarray(bdw), jnp.asarray(bdv, BF16),
            jnp.asarray(cmask), jnp.asarray(bmat, BF16), jnp.asarray(tile_t, BF16))


def _recurrent_mixer(kernel_fn, name, width, args, in_specs, state, layer, nseq, seq_len, want_final,
                     extra_scratch):
    dk = width // N_HEADS
    nsub = _seqs_per_block(nseq, seq_len)
    rows = nsub * seq_len
    consts = _recurrence_constants(width)
    args = list(args) + list(consts)
    in_specs = list(in_specs) + [_full_spec(c) for c in consts]
    if state is not None:
        args.append(state)
        in_specs.append(pl.BlockSpec((nsub, 1, 2, N_HEADS, dk, HEAD_V), lambda b: (b, layer, 0, 0, 0, 0)))
    out_shape = [jax.ShapeDtypeStruct((nseq * seq_len, W_GRP), BF16)]
    out_specs = [pl.BlockSpec((rows, W_GRP), lambda b: (b, 0))]
    if want_final:
        out_shape.append(jax.ShapeDtypeStruct((nseq, 2, N_HEADS, dk, HEAD_V), F32))
        out_specs.append(pl.BlockSpec((nsub, 2, N_HEADS, dk, HEAD_V), lambda b: (b, 0, 0, 0, 0)))
    res = pl.pallas_call(
        functools.partial(kernel_fn, seq_len=seq_len, nsub=nsub, has_s0=state is not None,
                          want_final=want_final),
        grid=(nseq // nsub,),
        in_specs=in_specs,
        out_specs=out_specs,
        out_shape=out_shape,
        scratch_shapes=[pltpu.VMEM((2, rows, W_GRP), F32)] * extra_scratch + [
            pltpu.VMEM((2, rows, width), F32), pltpu.VMEM((2, rows, W_GRP), F32),
            pltpu.VMEM((2 * nsub, W_GRP, width), F32)],
        compiler_params=_params(("arbitrary",)),
        name=name,
    )(*args)
    return res[0], (res[1] if want_final else None)


def _seqs_per_block(nseq, seq_len):
    return max(1, min(nseq, BLOCK_ROWS // seq_len))


def _hgrn_mixer(proj, lb_logits, gain, state, layer, nseq, seq_len, want_final):
    rows = _seqs_per_block(nseq, seq_len) * seq_len
    col = lambda c: pl.BlockSpec((rows, W_GRP), lambda b: (b, c))
    args = [proj] * 5 + [lb_logits, gain.reshape(DEPTH, 1, W_GRP)]
    in_specs = [col(0), col(1), col(2), col(3), col(4), _full_spec(lb_logits),
                pl.BlockSpec((1, 1, W_GRP), lambda b: (layer, 0, 0))]
    return _recurrent_mixer(functools.partial(_hgrn_kernel, layer=layer), "hgrn2_mixer", W_GRP, args,
                            in_specs, state, layer, nseq, seq_len, want_final, 1)


def _gla_mixer(proj, ba2, gain, state, layer, nseq, seq_len, want_final):
    width = N_HEADS * GLA_DK
    rows = _seqs_per_block(nseq, seq_len) * seq_len
    col256 = lambda c: pl.BlockSpec((rows, W_GRP), lambda b: (b, c))
    col128 = lambda c: pl.BlockSpec((rows, width), lambda b: (b, c))
    args = [proj] * 6 + [ba2, gain.reshape(DEPTH, 1, W_GRP)]
    in_specs = [col128(10), col128(11), col256(6), col256(7), col128(16), col128(17),
                pl.BlockSpec((1, 2, width), lambda b: (layer, 0, 0)),
                pl.BlockSpec((1, 1, W_GRP), lambda b: (layer, 0, 0))]
    return _recurrent_mixer(_gla_kernel, "gla_mixer", width, args, in_specs, state, layer, nseq, seq_len,
                            want_final, 0)


PAD = 8
SCAN_TILE = 8
SCAN_UNROLL = 4


def _rgsc_kernel(*refs, seq_len, nsub, seg, has_h0, want_final):
    (cx_ref, cg_ref, db_ref, dc_ref, dv_ref, convw_ref, convb_ref, wr_ref, br_ref, wi_ref,
     bi_ref, lam_ref, sw_ref) = refs[:13]
    pos = 13
    h0_ref = None
    if has_h0:
        h0_ref = refs[pos]
        pos += 1
    oc_ref, od_ref = refs[pos:pos + 2]
    pos += 2
    hfin_ref = None
    if want_final:
        hfin_ref = refs[pos]
        pos += 1
    upad, a_scr, b_scr, h_scr = refs[pos:]

    slot = seq_len + 2 * PAD
    zpad = jnp.zeros((PAD, W_GRP), F32)
    for sub in range(nsub):
        upad[sub * slot:sub * slot + PAD, :] = zpad
        upad[sub * slot + PAD + seq_len:(sub + 1) * slot, :] = zpad
        upad[sub * slot + PAD:sub * slot + PAD + seq_len, :] = cx_ref[sub * seq_len:(sub + 1) * seq_len, :]
    rows8 = lax.broadcasted_iota(jnp.int32, (SCAN_TILE, W_GRP), 0)
    ntile = seq_len // SCAN_TILE

    for d in (0, 1):
        reverse = d == 1
        w = convw_ref[0, d]
        offs = [PAD + 3, PAD + 2, PAD + 1, PAD] if reverse else [PAD - 3, PAD - 2, PAD - 1, PAD]
        sp = _softplus(-lam_ref[0, d:d + 1, :])
        for sub in range(nsub):
            for rb in range(seq_len // ROW_BLOCK):
                src = sub * slot + rb * ROW_BLOCK
                base = sub * seq_len + rb * ROW_BLOCK
                xc = convb_ref[0, d:d + 1, :]
                for j in range(4):
                    xc = xc + w[j:j + 1, :] * upad[offs[j] + src:offs[j] + src + ROW_BLOCK, :]
                r = _sigmoid(_dot(xc, wr_ref[d]) + br_ref[0, d:d + 1, :])
                ig = _sigmoid(_dot(xc, wi_ref[d]) + bi_ref[0, d:d + 1, :])
                log_a = -RG_C * r * sp
                a = jnp.exp(log_a)
                a_scr[base:base + ROW_BLOCK, :] = a
                b_scr[base:base + ROW_BLOCK, :] = jnp.sqrt(1.0 - a * a) * (ig * xc)

        if has_h0:
            h0 = tuple(h0_ref[sub, 0, d:d + 1, :] for sub in range(nsub))
        else:
            h0 = tuple(jnp.zeros((1, W_GRP), F32) for _ in range(nsub))

        def step(ti, carry, d=d, reverse=reverse):
            t = (ntile - 1 - ti) if reverse else ti
            out = []
            for sub in range(nsub):
                r0 = pl.multiple_of(sub * seq_len + t * SCAN_TILE, SCAN_TILE)
                a = a_scr[pl.ds(r0, SCAN_TILE), :]
                b = b_scr[pl.ds(r0, SCAN_TILE), :]
                for s in (1, 2, 4):
                    if reverse:
                        keep = rows8 <= SCAN_TILE - 1 - s
                        shift = SCAN_TILE - s
                    else:
                        keep = rows8 >= s
                        shift = s
                    a_s = jnp.where(keep, pltpu.roll(a, shift, 0), 1.0)
                    b_s = jnp.where(keep, pltpu.roll(b, shift, 0), 0.0)
                    b = b + a * b_s
                    a = a * a_s
                h = a * carry[sub] + b
                if d == 0:
                    h_scr[pl.ds(r0, SCAN_TILE), :] = h
                else:
                    h_scr[pl.ds(r0, SCAN_TILE), :] += h
                out.append(h[0:1, :] if reverse else h[SCAN_TILE - 1:SCAN_TILE, :])
            return tuple(out)

        hlast = lax.fori_loop(0, ntile, step, h0, unroll=SCAN_UNROLL // nsub if nsub < SCAN_UNROLL else 1)
        if want_final:
            for sub in range(nsub):
                hfin_ref[sub, d:d + 1, :] = hlast[sub]

    sw = sw_ref[0]
    nblock = nsub * seq_len // ROW_BLOCK
    for rb in range(nblock):
        rows = slice(rb * ROW_BLOCK, (rb + 1) * ROW_BLOCK)
        x = cg_ref[rows, :]
        gelu = 0.5 * x * (1.0 + jnp.tanh(0.7978845608028654 * (x + 0.044715 * (x * x * x))))
        oc_ref[rows, :] = (h_scr[rows, :] * gelu).astype(oc_ref.dtype)
        upad[PAD + rb * ROW_BLOCK:PAD + (rb + 1) * ROW_BLOCK, :] = dc_ref[rows, :] * dv_ref[rows, :]
    for rb in range(nblock):
        base = rb * ROW_BLOCK
        rows = slice(base, base + ROW_BLOCK)
        posn = (lax.broadcasted_iota(jnp.int32, (ROW_BLOCK, W_GRP), 0) + base) % seg
        left = jnp.where(posn != 0, upad[PAD - 1 + base:PAD - 1 + base + ROW_BLOCK, :], 0.0)
        right = jnp.where(posn != seg - 1, upad[PAD + 1 + base:PAD + 1 + base + ROW_BLOCK, :], 0.0)
        y = sw[0:1, :] * left + sw[1:2, :] * upad[PAD + base:PAD + base + ROW_BLOCK, :] + sw[2:3, :] * right
        od_ref[rows, :] = (db_ref[rows, :] * y).astype(od_ref.dtype)


def _rgsc_mixer(proj, conv_w, conv_b, wr_bd, b_r, wi_bd, b_i, lam, sconv_w, h0, layer, nseq, seq_len,
                seg, want_final):
    nsub = _seqs_per_block(nseq, seq_len)
    rows = nsub * seq_len
    col = lambda c: pl.BlockSpec((rows, W_GRP), lambda b: (b, c))
    lay3 = lambda a: pl.BlockSpec((1,) + a.shape[1:], lambda b: (layer, 0, 0))
    lay4 = lambda a: pl.BlockSpec((1,) + a.shape[1:], lambda b: (layer, 0, 0, 0))
    params = [conv_w, conv_b, wr_bd, b_r, wi_bd, b_i, lam, sconv_w]
    args = [proj] * 5 + params
    in_specs = [col(9), col(10), col(11), col(12), col(13), lay4(conv_w), lay3(conv_b), _full_spec(wr_bd),
                lay3(b_r), _full_spec(wi_bd), lay3(b_i), lay3(lam), lay3(sconv_w)]
    if h0 is not None:
        args.append(h0)
        in_specs.append(pl.BlockSpec((nsub, 1, 2, W_GRP), lambda b: (b, layer, 0, 0)))
    out_shape = [jax.ShapeDtypeStruct((nseq * seq_len, W_GRP), BF16)] * 2
    out_specs = [pl.BlockSpec((rows, W_GRP), lambda b: (b, 0))] * 2
    if want_final:
        out_shape.append(jax.ShapeDtypeStruct((nseq, 2, W_GRP), F32))
        out_specs.append(pl.BlockSpec((nsub, 2, W_GRP), lambda b: (b, 0, 0)))
    res = pl.pallas_call(
        functools.partial(_rgsc_kernel, seq_len=seq_len, nsub=nsub, seg=seg, has_h0=h0 is not None,
                          want_final=want_final),
        grid=(nseq // nsub,),
        in_specs=in_specs,
        out_specs=out_specs,
        out_shape=out_shape,
        scratch_shapes=[pltpu.VMEM((nsub * (seq_len + 2 * PAD), W_GRP), F32), pltpu.VMEM((rows, W_GRP), F32),
                        pltpu.VMEM((rows, W_GRP), F32), pltpu.VMEM((rows, W_GRP), F32)],
        compiler_params=_params(("arbitrary",)),
        name="rglru_sconv_mixer",
    )(*args)
    return res[0], res[1], (res[2] if want_final else None)


def _block_diag(w):
    two, h, c, _ = w.shape
    eye = jnp.eye(h, dtype=w.dtype)
    full = w[:, :, :, None, :] * eye[None, :, None, :, None]
    return full.reshape(two, h * c, h * c)


def kernel(x_prompt, x_sample, state_hgrn, state_gla, state_rglru, c, c_ctx, norm1_g, norm2_g, ada_w, ada_b, w_in, w_out, hgrn_lb_logits, hgrn_norm_g, gla_wa2, gla_ba2, gla_norm_g, rg_conv_w, rg_conv_b, rg_w_r, rg_b_r, rg_w_i, rg_b_i, rg_lambda, sconv_w, mlp_w1, mlp_w2, final_norm_g):
    b_ctx, l_ctx, _ = x_prompt.shape
    b_lat, l_lat, _ = x_sample.shape

    cvec = jnp.concatenate([c, c_ctx[None, :], jnp.zeros((8 - b_lat - 1, D_MODEL), F32)], axis=0)
    mod = _modulation(cvec, ada_w, ada_b).reshape(DEPTH, 8, 6, D_MODEL)

    a0 = 2048
    w_a = jnp.stack([w_in[:, :, a0:a0 + GLA_RANK], w_in[:, :, a0 + GLA_RANK:a0 + 2 * GLA_RANK]], axis=1)
    w_z = _fold_lowrank(w_a, gla_wa2)
    w_cat = jnp.concatenate([w_in[:, :, :a0], w_z[:, 0], w_z[:, 1], w_in[:, :, a0 + 2 * GLA_RANK:]],
                            axis=-1).astype(BF16)

    xp = x_prompt.reshape(b_ctx * l_ctx, D_MODEL)
    xs = x_sample.reshape(b_lat * l_lat, D_MODEL)
    new_h, new_g, new_r = [], [], []
    for l in range(DEPTH):
        wr_bd = _block_diag(rg_w_r[l]).astype(BF16)
        wi_bd = _block_diag(rg_w_i[l]).astype(BF16)
        mod_ctx = mod[l, b_lat:b_lat + 1]
        mod_lat = mod[l, :b_lat]
        last = l == DEPTH - 1
        streams = (
            (xp, mod_ctx, b_ctx, l_ctx, l_ctx, None, None, None, True),
            (xs, mod_lat, b_lat, l_lat, GRID_W, state_hgrn, state_gla, state_rglru, False),
        )
        outs = []
        for (x, m, nseq, seq_len, seg, s_h, s_g, s_r, want_final) in streams:
            proj = _project(x, m, norm1_g, w_cat, l)
            o_a, f_h = _hgrn_mixer(proj, hgrn_lb_logits, hgrn_norm_g, s_h, l, nseq, seq_len, want_final)
            o_b, f_g = _gla_mixer(proj, gla_ba2, gla_norm_g, s_g, l, nseq, seq_len, want_final)
            o_c, o_d, f_r = _rgsc_mixer(proj, rg_conv_w, rg_conv_b, wr_bd, rg_b_r, wi_bd, rg_b_i,
                                        rg_lambda, sconv_w, s_r, l, nseq, seq_len, seg, want_final)
            x_new = _out_mlp(x, o_a, o_b, o_c, o_d, m, norm2_g, final_norm_g, w_out, mlp_w1, mlp_w2, l, last)
            outs.append((x_new, f_h, f_g, f_r))
        xp, f_h, f_g, f_r = outs[0]
        xs = outs[1][0]
        new_h.append(f_h)
        new_g.append(f_g)
        new_r.append(f_r)
    y_prompt = xp.reshape(b_ctx, l_ctx, D_MODEL)
    y_sample = xs.reshape(b_lat, l_lat, D_MODEL)
    return (y_prompt, y_sample, jnp.stack(new_h, axis=1), jnp.stack(new_g, axis=1),
            jnp.stack(new_r, axis=1))
```

```python
import functools

import numpy as np
import jax
import jax.numpy as jnp
from jax import lax
from jax.experimental import pallas as pl
from jax.experimental.pallas import tpu as pltpu

F32 = jnp.float32
BF16 = jnp.bfloat16

D_MODEL = 1024
DEPTH = 2
GRID_W = 64
N_HEADS = 4
W_GRP = 256
HEAD_V = 64
HGRN_DK = 64
GLA_DK = 32
GLA_RANK = 16
GLA_NORMALIZER = 16.0
RG_C = 8.0
D_FF = 4 * D_MODEL
EPS = 1e-6
F_FLOOR = 1e-20

PROJ_COLS = 3584

CHUNK = 64
HALF = CHUNK // 2
assert CHUNK == HEAD_V
GROUP = 4
GROWS = GROUP * CHUNK
MAIN_GROUP = 8
BLOCK_ROWS = 1024
LANES = 128
SAFE_RANGE = 80.0
TOK_TILE = 1024
PROJ_TM = 1024
PROJ_TN = 512
FF_TILE = 1024
MLP_ROW_SPLIT = 2
ROW_BLOCK = 256
VMEM_LIMIT = 56 * 1024 * 1024

_NT = (((1,), (1,)), ((), ()))
_TN = (((0,), (0,)), ((), ()))


def _dot(a, b):
    return jnp.dot(a.astype(BF16), b.astype(BF16), preferred_element_type=F32)


def _dot_nt(a, b):
    return lax.dot_general(a.astype(BF16), b.astype(BF16), _NT, preferred_element_type=F32)


def _dot_tn(a, b):
    return lax.dot_general(a.astype(BF16), b.astype(BF16), _TN, preferred_element_type=F32)


def _sigmoid(x):
    return 1.0 / (1.0 + jnp.exp(-x))


def _silu(x):
    return x * _sigmoid(x)


def _log1p(y):
    u = 1.0 + y
    return jnp.where(u == 1.0, y, jnp.log(u) * (y / (u - 1.0)))


def _softplus(x):
    return jnp.maximum(x, 0.0) + _log1p(jnp.exp(-jnp.abs(x)))


def _rms(x):
    return x * lax.rsqrt(jnp.mean(x * x, axis=-1, keepdims=True) + EPS)


def _params(sem):
    return pltpu.CompilerParams(dimension_semantics=sem, vmem_limit_bytes=VMEM_LIMIT)


def _full_spec(a):
    zeros = (0,) * a.ndim
    return pl.BlockSpec(a.shape, lambda *_: zeros)


def _mod_kernel(c_ref, w_ref, b_ref, o_ref):
    s = _silu(c_ref[...])
    o_ref[0] = _dot(s, w_ref[0]) + b_ref[0]


def _modulation(cvec, ada_w, ada_b):
    tn = 2048
    return pl.pallas_call(
        _mod_kernel,
        grid=(DEPTH, 6 * D_MODEL // tn),
        in_specs=[pl.BlockSpec((8, D_MODEL), lambda l, j: (0, 0)),
                  pl.BlockSpec((1, D_MODEL, tn), lambda l, j: (l, 0, j)),
                  pl.BlockSpec((1, 1, tn), lambda l, j: (l, 0, j))],
        out_specs=pl.BlockSpec((1, 8, tn), lambda l, j: (l, 0, j)),
        out_shape=jax.ShapeDtypeStruct((DEPTH, 8, 6 * D_MODEL), F32),
        compiler_params=_params(("arbitrary", "arbitrary")),
        name="adaln_modulation",
    )(cvec, ada_w, ada_b.reshape(DEPTH, 1, 6 * D_MODEL))


LOWRANK_COL = 2048
TAIL_COL = LOWRANK_COL + 2 * GLA_RANK
TAIL_WIDTH = 1280
WCAT_ROWS = 256


def _wcat_kernel(w_ref, wpad_ref, shift_ref, o_ref):
    o_ref[0, :, 0:LOWRANK_COL] = w_ref[0, :, 0:LOWRANK_COL].astype(BF16)
    lowrank_tile = w_ref[0, :, LOWRANK_COL:LOWRANK_COL + LANES]
    z = jnp.dot(lowrank_tile, wpad_ref[0], preferred_element_type=F32, precision=lax.Precision.HIGHEST)
    o_ref[0, :, LOWRANK_COL:LOWRANK_COL + 2 * LANES] = z.astype(BF16)
    dst0 = LOWRANK_COL + 2 * LANES
    ntile = TAIL_WIDTH // LANES
    for t in range(ntile):
        lo = LOWRANK_COL + t * LANES
        hi = min(lo + 2 * LANES, w_ref.shape[2])
        src = w_ref[0, :, lo:hi].astype(BF16)
        moved = jnp.dot(src, shift_ref[0:hi - lo, :], preferred_element_type=F32)
        o_ref[0, :, dst0 + t * LANES:dst0 + (t + 1) * LANES] = moved.astype(BF16)


def _build_proj_weight(w_in, wa2):
    ncol = w_in.shape[2]
    assert ncol == TAIL_COL + TAIL_WIDTH and LOWRANK_COL + 2 * LANES + TAIL_WIDTH == PROJ_COLS
    n = N_HEADS * GLA_DK
    wpad = jnp.zeros((DEPTH, LANES, 2 * n), F32)
    wpad = wpad.at[:, 0:GLA_RANK, 0:n].set(wa2[:, 0]).at[:, GLA_RANK:2 * GLA_RANK, n:2 * n].set(wa2[:, 1])
    shift = (np.arange(2 * LANES)[:, None] == np.arange(LANES)[None, :] + 2 * GLA_RANK)
    return pl.pallas_call(
        _wcat_kernel,
        grid=(DEPTH, D_MODEL // WCAT_ROWS),
        in_specs=[pl.BlockSpec((1, WCAT_ROWS, ncol), lambda l, i: (l, i, 0)),
                  pl.BlockSpec((1, LANES, 2 * n), lambda l, i: (l, 0, 0)),
                  pl.BlockSpec((2 * LANES, LANES), lambda l, i: (0, 0))],
        out_specs=pl.BlockSpec((1, WCAT_ROWS, PROJ_COLS), lambda l, i: (l, i, 0)),
        out_shape=jax.ShapeDtypeStruct((DEPTH, D_MODEL, PROJ_COLS), BF16),
        compiler_params=_params(("arbitrary", "arbitrary")),
        name="build_proj_weight",
    )(w_in, wpad, jnp.asarray(shift.astype(np.float32), BF16))


def _proj_kernel(x_ref, mod_ref, g_ref, w_ref, o_ref):
    shift = mod_ref[0, 0:1, :]
    scale = mod_ref[0, 1:2, :]
    h = (_rms(x_ref[...]) * g_ref[0] * (1.0 + scale) + shift).astype(BF16)
    for n in range(PROJ_COLS // PROJ_TN):
        cols = slice(n * PROJ_TN, (n + 1) * PROJ_TN)
        o_ref[:, cols] = jnp.dot(h, w_ref[0, :, cols], preferred_element_type=F32)


def _mod_index(ntok, nmod, tile):
    tiles_per_mod = (ntok // tile) // nmod if nmod > 1 else 1
    if nmod > 1:
        return lambda i, *_: (i // tiles_per_mod, 0, 0)
    return lambda i, *_: (0, 0, 0)


def _project(x, mod, norm_g, w_cat, layer):
    ntok = x.shape[0]
    return pl.pallas_call(
        _proj_kernel,
        grid=(ntok // PROJ_TM,),
        in_specs=[pl.BlockSpec((PROJ_TM, D_MODEL), lambda i: (i, 0)),
                  pl.BlockSpec((1, 6, D_MODEL), _mod_index(ntok, mod.shape[0], PROJ_TM)),
                  pl.BlockSpec((1, 1, D_MODEL), lambda i: (layer, 0, 0)),
                  pl.BlockSpec((1, D_MODEL, PROJ_COLS), lambda i: (layer, 0, 0),
                               pipeline_mode=pl.Buffered(1))],
        out_specs=pl.BlockSpec((PROJ_TM, PROJ_COLS), lambda i: (i, 0)),
        out_shape=jax.ShapeDtypeStruct((ntok, PROJ_COLS), F32),
        compiler_params=_params(("arbitrary",)),
        name="norm_in_proj",
    )(x, mod, norm_g.reshape(DEPTH, 1, D_MODEL), w_cat)


def _mlp_kernel(x_ref, oa_ref, ob_ref, oc_ref, od_ref, mod_ref, g2_ref, fg_ref, wout_ref,
                w1_ref, w2_ref, out_ref, h2_scr, *, final_norm):
    j = pl.program_id(1)

    @pl.when(j == 0)
    def _():
        mixed = jnp.concatenate([oa_ref[...], ob_ref[...], oc_ref[...], od_ref[...]], axis=1)
        mix = jnp.dot(mixed, wout_ref[0].astype(BF16), preferred_element_type=F32)
        x1 = x_ref[...] + mod_ref[0, 2:3, :] * mix
        out_ref[...] = x1
        h2 = _rms(x1) * g2_ref[0] * (1.0 + mod_ref[0, 4:5, :]) + mod_ref[0, 3:4, :]
        h2_scr[...] = h2.astype(BF16)

    w1 = w1_ref[0].astype(BF16)
    w2 = w2_ref[0].astype(BF16)
    for r in range(MLP_ROW_SPLIT):
        rows = slice(r * (TOK_TILE // MLP_ROW_SPLIT), (r + 1) * (TOK_TILE // MLP_ROW_SPLIT))
        t = jnp.dot(h2_scr[rows, :], w1, preferred_element_type=F32)
        t = jnp.square(jnp.maximum(t, 0.0)).astype(BF16)
        out_ref[rows, :] += mod_ref[0, 5:6, :] * jnp.dot(t, w2, preferred_element_type=F32)

    if final_norm:
        @pl.when(j == pl.num_programs(1) - 1)
        def _():
            out_ref[...] = _rms(out_ref[...]) * fg_ref[...]


def _out_mlp(x, o_a, o_b, o_c, o_d, mod, norm2_g, final_g, w_out, w1, w2, layer, final_norm):
    ntok = x.shape[0]
    tok = lambda i, j: (i, 0)
    return pl.pallas_call(
        functools.partial(_mlp_kernel, final_norm=final_norm),
        grid=(ntok // TOK_TILE, D_FF // FF_TILE),
        in_specs=[pl.BlockSpec((TOK_TILE, D_MODEL), tok),
                  pl.BlockSpec((TOK_TILE, W_GRP), tok),
                  pl.BlockSpec((TOK_TILE, W_GRP), tok),
                  pl.BlockSpec((TOK_TILE, W_GRP), tok),
                  pl.BlockSpec((TOK_TILE, W_GRP), tok),
                  pl.BlockSpec((1, 6, D_MODEL), _mod_index(ntok, mod.shape[0], TOK_TILE)),
                  pl.BlockSpec((1, 1, D_MODEL), lambda i, j: (layer, 0, 0)),
                  pl.BlockSpec((1, D_MODEL), lambda i, j: (0, 0)),
                  pl.BlockSpec((1, D_MODEL, D_MODEL), lambda i, j: (layer, 0, 0)),
                  pl.BlockSpec((1, D_MODEL, FF_TILE), lambda i, j: (layer, 0, j)),
                  pl.BlockSpec((1, FF_TILE, D_MODEL), lambda i, j: (layer, j, 0))],
        out_specs=pl.BlockSpec((TOK_TILE, D_MODEL), tok),
        out_shape=jax.ShapeDtypeStruct((ntok, D_MODEL), F32),
        scratch_shapes=[pltpu.VMEM((TOK_TILE, D_MODEL), BF16)],
        compiler_params=_params(("arbitrary", "arbitrary")),
        name="out_proj_mlp",
    )(x, o_a, o_b, o_c, o_d, mod, norm2_g.reshape(DEPTH, 1, D_MODEL), final_g.reshape(1, D_MODEL),
      w_out, w1, w2)


def _split3(x):
    hi = x.astype(BF16)
    r1 = x - hi.astype(F32)
    mid = r1.astype(BF16)
    lo = (r1 - mid.astype(F32)).astype(BF16)
    return hi, mid, lo


def _dot_exact_rhs(x, m):
    hi, mid, lo = _split3(x)
    s = jnp.dot(hi, m, preferred_element_type=F32)
    s += jnp.dot(mid, m, preferred_element_type=F32)
    s += jnp.dot(lo, m, preferred_element_type=F32)
    return s


def _block_cumsum(tri, g):
    hi = g.astype(BF16)
    lo = (g - hi.astype(F32)).astype(BF16)
    return jnp.dot(tri, hi, preferred_element_type=F32) + jnp.dot(tri, lo, preferred_element_type=F32)


def _run_prepass(pre, nblock, width):
    rmax = jnp.zeros((1, width), F32)
    if nblock % 2:
        for bi in range(nblock):
            rmax = pre(bi, rmax)
        return rmax
    if nblock == 2:
        return pre(1, pre(0, rmax))
    return lax.fori_loop(0, nblock // 2, lambda i, rm: pre(2 * i + 1, pre(2 * i, rm)), rmax)


def _half_ranges(cum, reverse):
    if reverse:
        second = -cum[HALF:HALF + 1, :]
        first = -(cum[0:1, :] - cum[HALF:HALF + 1, :])
    else:
        first = -cum[HALF - 1:HALF, :]
        second = -(cum[CHUNK - 1:CHUNK, :] - cum[HALF - 1:HALF, :])
    return jnp.maximum(first, second)


def _block_ranges(cum, reverse, rm):
    for c in range(GROUP):
        rm = jnp.maximum(rm, _half_ranges(cum[c * CHUNK:(c + 1) * CHUNK, :], reverse))
    return rm


def _row_start(c):
    r0 = c * CHUNK
    return r0 if isinstance(r0, int) else pl.multiple_of(r0, CHUNK)


class _Rec:
    def __init__(self, q_at, k_at, v_ref, cum_scr, st_scr, oacc, consts, width, seq_len, nsub):
        self.q_at, self.k_at, self.v_ref = q_at, k_at, v_ref
        self.cum_scr, self.st_scr, self.oacc = cum_scr, st_scr, oacc
        (self.tri_ref, self.bdw_b, self.bdw_f, self.bdv_b, self.cmask, self.bmat,
         self.tile_t) = consts
        self.width, self.seq_len, self.nsub = width, seq_len, nsub


def _chunk_step(rec, sub, d, c, fast):
    reverse = d == 1
    mid_row = HALF if reverse else HALF - 1
    last_row = 0 if reverse else CHUNK - 1
    sd = sub * 2 + d
    r0 = sub * rec.seq_len + c * CHUNK
    if not isinstance(r0, int):
        r0 = pl.multiple_of(r0, CHUNK)
    rows = pl.ds(r0, CHUNK)
    q = rec.q_at(r0)
    k = rec.k_at(d, r0)
    v_b = rec.v_ref[rows, :].astype(BF16)
    cum = rec.cum_scr[d, rows, :]
    tot = cum[last_row:last_row + 1, :]
    st = rec.st_scr[sd]
    o = _dot_nt(q * jnp.exp(cum), st)
    if fast:
        cm = cum[mid_row:mid_row + 1, :]
        qm = q * jnp.exp(cum - cm)
        km_b = (k * jnp.exp(cm - cum)).astype(BF16)
        kbd = jnp.concatenate([km_b] * N_HEADS, axis=0) * rec.bdw_b[...]
        sc = lax.dot_general(qm.astype(BF16), kbd, _NT, preferred_element_type=F32)
        a = jnp.where(rec.cmask[d] > 0.5, sc, 0.0)
        vexp = jnp.concatenate([v_b] * N_HEADS, axis=0) * rec.bdv_b[...]
        o = o + jnp.dot(a.astype(BF16), vexp, preferred_element_type=F32)
    else:
        row_id = lax.broadcasted_iota(jnp.int32, (CHUNK, rec.width), 0)

        def key_row(j, acc):
            krow = rec.k_at(d, r0 + j, 1)
            crow = rec.cum_scr[d, pl.ds(r0 + j, 1), :]
            vrow = rec.v_ref[pl.ds(r0 + j, 1), :]
            keep = (row_id <= j) if reverse else (row_id >= j)
            e = jnp.exp(jnp.minimum(cum - crow, 0.0))
            p = jnp.where(keep, q * krow * e, 0.0)
            return acc + _dot(p, rec.bmat[...]) * vrow

        o = lax.fori_loop(0, CHUNK, key_row, o)
    ke_b = (k * jnp.exp(tot - cum)).astype(BF16)
    ds = lax.dot_general(v_b, ke_b, _TN, preferred_element_type=F32)
    rec.st_scr[sd] = st * jnp.exp(tot) + ds * rec.bdw_f[...]
    rec.oacc[d, rows, :] = o


def _load_state(rec, s0_ref, sub, d):
    x = jnp.concatenate([s0_ref[sub, 0, d, h] for h in range(N_HEADS)], axis=0)
    y = _dot_exact_rhs(x, rec.tile_t[...]) * rec.bmat[...].astype(F32)
    rec.st_scr[sub * 2 + d] = y.T


def _store_state(rec, sfin_ref, sub, d):
    y = rec.st_scr[sub * 2 + d].T
    half = y[:, 0:LANES] + y[:, LANES:2 * LANES]
    x = (half + pltpu.roll(half, HEAD_V, 1))[:, 0:HEAD_V]
    dk = rec.width // N_HEADS
    for h in range(N_HEADS):
        sfin_ref[sub, d, h] = x[h * dk:(h + 1) * dk, :]


def _run_recurrence(rec, ok, s0_ref, sfin_ref):
    nsub = rec.nsub
    nchunk = rec.seq_len // CHUNK
    gsize = min(MAIN_GROUP, nchunk)
    ngroup = nchunk // gsize
    spp = max(1, min(nsub, MAIN_GROUP // gsize))
    nsp = nsub // spp
    for sub in range(nsub):
        for d in (0, 1):
            if s0_ref is None:
                rec.st_scr[sub * 2 + d] = jnp.zeros(rec.st_scr.shape[1:], F32)
            else:
                _load_state(rec, s0_ref, sub, d)

    def fast_all():
        def step(i, carry):
            sp = i // ngroup if (nsp > 1 and ngroup > 1) else (i if nsp > 1 else 0)
            gi = i % ngroup if (nsp > 1 and ngroup > 1) else (i if ngroup > 1 else 0)
            for u in range(gsize):
                s = gi * gsize + u
                for j in range(spp):
                    sub = sp * spp + j
                    _chunk_step(rec, sub, 0, s, True)
                    _chunk_step(rec, sub, 1, nchunk - 1 - s, True)
            return carry

        if nsp * ngroup == 1:
            step(0, 0)
        else:
            lax.fori_loop(0, nsp * ngroup, step, 0)

    def direct_all():
        def one(i, carry):
            sub = i // nchunk if nsub > 1 else 0
            s = i % nchunk if nsub > 1 else i
            _chunk_step(rec, sub, 0, s, False)
            _chunk_step(rec, sub, 1, nchunk - 1 - s, False)
            return carry

        lax.fori_loop(0, nsub * nchunk, one, 0)

    lax.cond(ok, fast_all, direct_all)
    if sfin_ref is not None:
        for sub in range(nsub):
            for d in (0, 1):
                _store_state(rec, sfin_ref, sub, d)


def _head_norm_gate(rec, gate_ref, gain_ref, o_ref):
    for rb in range(rec.nsub * rec.seq_len // ROW_BLOCK):
        rows = slice(rb * ROW_BLOCK, (rb + 1) * ROW_BLOCK)
        o = rec.oacc[0, rows, :] + rec.oacc[1, rows, :]
        ms = jnp.dot((o * o).astype(BF16), rec.bdv_b[...], preferred_element_type=F32) * (1.0 / HEAD_V)
        y = o * lax.rsqrt(ms + EPS) * gain_ref[0] * _silu(gate_ref[rows, :])
        o_ref[rows, :] = y.astype(o_ref.dtype)


def _split_refs(refs, n_in, has_s0, want_final):
    ins = refs[:n_in]
    pos = n_in
    s0_ref = None
    if has_s0:
        s0_ref = refs[pos]
        pos += 1
    o_ref = refs[pos]
    pos += 1
    sfin_ref = None
    if want_final:
        sfin_ref = refs[pos]
        pos += 1
    return ins, s0_ref, o_ref, sfin_ref, refs[pos:]


def _hgrn_kernel(*refs, seq_len, nsub, layer, has_s0, want_final):
    ins, s0_ref, o_ref, sfin_ref, scr = _split_refs(refs, 14, has_s0, want_final)
    q_ref, i_ref, ff_ref, fb_ref, g_ref, lbl_ref, gain_ref = ins[:7]
    k_scr, cum_scr, oacc, st_scr = scr
    q_at = lambda r0: q_ref[pl.ds(r0, CHUNK), :]
    k_at = lambda d, r0, n=CHUNK: k_scr[d, pl.ds(r0, n), :]
    rec = _Rec(q_at, k_at, i_ref, cum_scr, st_scr, oacc, ins[7:], W_GRP, seq_len, nsub)

    lg = lbl_ref[...]
    mx = lg[0]
    for i in range(1, DEPTH):
        mx = jnp.maximum(mx, lg[i])
    ex = [jnp.exp(lg[i] - mx) for i in range(DEPTH)]
    den = ex[0]
    for i in range(1, DEPTH):
        den = den + ex[i]
    sm = [e / den for e in ex]
    csum = sm[0]
    for i in range(1, layer + 1):
        csum = csum + sm[i]
    lb = csum - sm[0]

    def pre(bi, rm):
        r0 = bi * GROWS if isinstance(bi, int) else pl.multiple_of(bi * GROWS, GROWS)
        rows = pl.ds(r0, GROWS)
        for d, f_ref in ((0, ff_ref), (1, fb_ref)):
            lbd = lb[d:d + 1, :]
            f = lbd + (1.0 - lbd) * _sigmoid(f_ref[rows, :])
            g = jnp.log(jnp.maximum(f, F_FLOOR))
            k_scr[d, rows, :] = 1.0 - f
            cum = _block_cumsum(rec.tri_ref[d], g)
            cum_scr[d, rows, :] = cum
            rm = _block_ranges(cum, d == 1, rm)
        return rm

    rmax = _run_prepass(pre, nsub * seq_len // GROWS, W_GRP)
    ok = jnp.max(rmax) < SAFE_RANGE

    _run_recurrence(rec, ok, s0_ref, sfin_ref)
    _head_norm_gate(rec, g_ref, gain_ref, o_ref)


def _gla_kernel(*refs, seq_len, nsub, has_s0, want_final):
    ins, s0_ref, o_ref, sfin_ref, scr = _split_refs(refs, 15, has_s0, want_final)
    q_ref, k_ref, v_ref, g_ref, zf_ref, zb_ref, ba_ref, gain_ref = ins[:8]
    cum_scr, oacc, st_scr = scr
    width = N_HEADS * GLA_DK
    q_at = lambda r0: q_ref[pl.ds(r0, CHUNK), :] * (GLA_DK ** -0.5)
    k_at = lambda d, r0, n=CHUNK: k_ref[pl.ds(r0, n), :]
    rec = _Rec(q_at, k_at, v_ref, cum_scr, st_scr, oacc, ins[8:], width, seq_len, nsub)

    def pre(bi, rm):
        r0 = bi * GROWS if isinstance(bi, int) else pl.multiple_of(bi * GROWS, GROWS)
        rows = pl.ds(r0, GROWS)
        for d, z_ref in ((0, zf_ref), (1, zb_ref)):
            z = z_ref[rows, :] + ba_ref[0, d:d + 1, :]
            g = -_softplus(-z) * (1.0 / GLA_NORMALIZER)
            cum = _block_cumsum(rec.tri_ref[d], g)
            cum_scr[d, rows, :] = cum
            rm = _block_ranges(cum, d == 1, rm)
        return rm

    rmax = _run_prepass(pre, nsub * seq_len // GROWS, width)
    ok = jnp.max(rmax) < SAFE_RANGE

    _run_recurrence(rec, ok, s0_ref, sfin_ref)
    _head_norm_gate(rec, g_ref, gain_ref, o_ref)


def _recurrence_constants(width):
    dk = width // N_HEADS
    r = np.arange(N_HEADS * CHUNK)[:, None]
    tri = np.kron(np.eye(GROUP, dtype=np.float32), np.tril(np.ones((CHUNK, CHUNK), np.float32)))
    tri = np.stack([tri, tri.T])
    bdw = (r // CHUNK == np.arange(width)[None, :] // dk).astype(np.float32)
    bdv = (r // CHUNK == np.arange(W_GRP)[None, :] // HEAD_V).astype(np.float32)
    t = np.arange(CHUNK)[:, None]
    s = np.arange(N_HEADS * CHUNK)[None, :] % CHUNK
    cmask = np.stack([(s <= t), (s >= t)]).astype(np.float32)
    bmat = (np.arange(width)[:, None] // dk == np.arange(W_GRP)[None, :] // HEAD_V).astype(np.float32)
    tile_t = np.tile(np.eye(HEAD_V, dtype=np.float32), (1, N_HEADS))
    return (jnp.asarray(tri, BF16), jnp.asarray(bdw, BF16), jnp.asarray(bdw), jnp.asarray(bdv, BF16),
            jnp.asarray(cmask), jnp.asarray(bmat, BF16), jnp.asarray(tile_t, BF16))


def _recurrent_mixer(kernel_fn, name, width, args, in_specs, state, layer, nseq, seq_len, want_final,
                     extra_scratch):
    dk = width // N_HEADS
    nsub = _seqs_per_block(nseq, seq_len)
    rows = nsub * seq_len
    consts = _recurrence_constants(width)
    args = list(args) + list(consts)
    in_specs = list(in_specs) + [_full_spec(c) for c in consts]
    if state is not None:
        args.append(state)
        in_specs.append(pl.BlockSpec((nsub, 1, 2, N_HEADS, dk, HEAD_V), lambda b: (b, layer, 0, 0, 0, 0)))
    out_shape = [jax.ShapeDtypeStruct((nseq * seq_len, W_GRP), BF16)]
    out_specs = [pl.BlockSpec((rows, W_GRP), lambda b: (b, 0))]
    if want_final:
        out_shape.append(jax.ShapeDtypeStruct((nseq, 2, N_HEADS, dk, HEAD_V), F32))
        out_specs.append(pl.BlockSpec((nsub, 2, N_HEADS, dk, HEAD_V), lambda b: (b, 0, 0, 0, 0)))
    res = pl.pallas_call(
        functools.partial(kernel_fn, seq_len=seq_len, nsub=nsub, has_s0=state is not None,
                          want_final=want_final),
        grid=(nseq // nsub,),
        in_specs=in_specs,
        out_specs=out_specs,
        out_shape=out_shape,
        scratch_shapes=[pltpu.VMEM((2, rows, W_GRP), F32)] * extra_scratch + [
            pltpu.VMEM((2, rows, width), F32), pltpu.VMEM((2, rows, W_GRP), F32),
            pltpu.VMEM((2 * nsub, W_GRP, width), F32)],
        compiler_params=_params(("arbitrary",)),
        name=name,
    )(*args)
    return res[0], (res[1] if want_final else None)


def _seqs_per_block(nseq, seq_len):
    return max(1, min(nseq, BLOCK_ROWS // seq_len))


def _hgrn_mixer(proj, lb_logits, gain, state, layer, nseq, seq_len, want_final):
    rows = _seqs_per_block(nseq, seq_len) * seq_len
    col = lambda c: pl.BlockSpec((rows, W_GRP), lambda b: (b, c))
    args = [proj] * 5 + [lb_logits, gain.reshape(DEPTH, 1, W_GRP)]
    in_specs = [col(0), col(1), col(2), col(3), col(4), _full_spec(lb_logits),
                pl.BlockSpec((1, 1, W_GRP), lambda b: (layer, 0, 0))]
    return _recurrent_mixer(functools.partial(_hgrn_kernel, layer=layer), "hgrn2_mixer", W_GRP, args,
                            in_specs, state, layer, nseq, seq_len, want_final, 1)


def _gla_mixer(proj, ba2, gain, state, layer, nseq, seq_len, want_final):
    width = N_HEADS * GLA_DK
    rows = _seqs_per_block(nseq, seq_len) * seq_len
    col256 = lambda c: pl.BlockSpec((rows, W_GRP), lambda b: (b, c))
    col128 = lambda c: pl.BlockSpec((rows, width), lambda b: (b, c))
    args = [proj] * 6 + [ba2, gain.reshape(DEPTH, 1, W_GRP)]
    in_specs = [col128(10), col128(11), col256(6), col256(7), col128(16), col128(17),
                pl.BlockSpec((1, 2, width), lambda b: (layer, 0, 0)),
                pl.BlockSpec((1, 1, W_GRP), lambda b: (layer, 0, 0))]
    return _recurrent_mixer(_gla_kernel, "gla_mixer", width, args, in_specs, state, layer, nseq, seq_len,
                            want_final, 0)


PAD = 8
SCAN_TILE = 8
SCAN_UNROLL = 4


def _rgsc_kernel(*refs, seq_len, nsub, seg, has_h0, want_final):
    (cx_ref, cg_ref, db_ref, dc_ref, dv_ref, convw_ref, convb_ref, wr_ref, br_ref, wi_ref,
     bi_ref, lam_ref, sw_ref) = refs[:13]
    pos = 13
    h0_ref = None
    if has_h0:
        h0_ref = refs[pos]
        pos += 1
    oc_ref, od_ref = refs[pos:pos + 2]
    pos += 2
    hfin_ref = None
    if want_final:
        hfin_ref = refs[pos]
        pos += 1
    upad, a_scr, b_scr, h_scr = refs[pos:]

    slot = seq_len + 2 * PAD
    zpad = jnp.zeros((PAD, W_GRP), F32)
    for sub in range(nsub):
        upad[sub * slot:sub * slot + PAD, :] = zpad
        upad[sub * slot + PAD + seq_len:(sub + 1) * slot, :] = zpad
        upad[sub * slot + PAD:sub * slot + PAD + seq_len, :] = cx_ref[sub * seq_len:(sub + 1) * seq_len, :]
    rows8 = lax.broadcasted_iota(jnp.int32, (SCAN_TILE, W_GRP), 0)
    ntile = seq_len // SCAN_TILE

    for d in (0, 1):
        reverse = d == 1
        w = convw_ref[0, d]
        offs = [PAD + 3, PAD + 2, PAD + 1, PAD] if reverse else [PAD - 3, PAD - 2, PAD - 1, PAD]
        sp = _softplus(-lam_ref[0, d:d + 1, :])
        for sub in range(nsub):
            for rb in range(seq_len // ROW_BLOCK):
                src = sub * slot + rb * ROW_BLOCK
                base = sub * seq_len + rb * ROW_BLOCK
                xc = convb_ref[0, d:d + 1, :]
                for j in range(4):
                    xc = xc + w[j:j + 1, :] * upad[offs[j] + src:offs[j] + src + ROW_BLOCK, :]
                r = _sigmoid(_dot(xc, wr_ref[d]) + br_ref[0, d:d + 1, :])
                ig = _sigmoid(_dot(xc, wi_ref[d]) + bi_ref[0, d:d + 1, :])
                log_a = -RG_C * r * sp
                a = jnp.exp(log_a)
                a_scr[base:base + ROW_BLOCK, :] = a
                b_scr[base:base + ROW_BLOCK, :] = jnp.sqrt(1.0 - a * a) * (ig * xc)

        if has_h0:
            h0 = tuple(h0_ref[sub, 0, d:d + 1, :] for sub in range(nsub))
        else:
            h0 = tuple(jnp.zeros((1, W_GRP), F32) for _ in range(nsub))

        def step(ti, carry, d=d, reverse=reverse):
            t = (ntile - 1 - ti) if reverse else ti
            out = []
            for sub in range(nsub):
                r0 = pl.multiple_of(sub * seq_len + t * SCAN_TILE, SCAN_TILE)
                a = a_scr[pl.ds(r0, SCAN_TILE), :]
                b = b_scr[pl.ds(r0, SCAN_TILE), :]
                for s in (1, 2, 4):
                    if reverse:
                        keep = rows8 <= SCAN_TILE - 1 - s
                        shift = SCAN_TILE - s
                    else:
                        keep = rows8 >= s
                        shift = s
                    a_s = jnp.where(keep, pltpu.roll(a, shift, 0), 1.0)
                    b_s = jnp.where(keep, pltpu.roll(b, shift, 0), 0.0)
                    b = b + a * b_s
                    a = a * a_s
                h = a * carry[sub] + b
                if d == 0:
                    h_scr[pl.ds(r0, SCAN_TILE), :] = h
                else:
                    h_scr[pl.ds(r0, SCAN_TILE), :] += h
                out.append(h[0:1, :] if reverse else h[SCAN_TILE - 1:SCAN_TILE, :])
            return tuple(out)

        hlast = lax.fori_loop(0, ntile, step, h0, unroll=SCAN_UNROLL // nsub if nsub < SCAN_UNROLL else 1)
        if want_final:
            for sub in range(nsub):
                hfin_ref[sub, d:d + 1, :] = hlast[sub]

    sw = sw_ref[0]
    nblock = nsub * seq_len // ROW_BLOCK
    for rb in range(nblock):
        rows = slice(rb * ROW_BLOCK, (rb + 1) * ROW_BLOCK)
        x = cg_ref[rows, :]
        gelu = 0.5 * x * (1.0 + jnp.tanh(0.7978845608028654 * (x + 0.044715 * (x * x * x))))
        oc_ref[rows, :] = (h_scr[rows, :] * gelu).astype(oc_ref.dtype)
        upad[PAD + rb * ROW_BLOCK:PAD + (rb + 1) * ROW_BLOCK, :] = dc_ref[rows, :] * dv_ref[rows, :]
    for rb in range(nblock):
        base = rb * ROW_BLOCK
        rows = slice(base, base + ROW_BLOCK)
        posn = (lax.broadcasted_iota(jnp.int32, (ROW_BLOCK, W_GRP), 0) + base) % seg
        left = jnp.where(posn != 0, upad[PAD - 1 + base:PAD - 1 + base + ROW_BLOCK, :], 0.0)
        right = jnp.where(posn != seg - 1, upad[PAD + 1 + base:PAD + 1 + base + ROW_BLOCK, :], 0.0)
        y = sw[0:1, :] * left + sw[1:2, :] * upad[PAD + base:PAD + base + ROW_BLOCK, :] + sw[2:3, :] * right
        od_ref[rows, :] = (db_ref[rows, :] * y).astype(od_ref.dtype)


def _rgsc_mixer(proj, conv_w, conv_b, wr_bd, b_r, wi_bd, b_i, lam, sconv_w, h0, layer, nseq, seq_len,
                seg, want_final):
    nsub = _seqs_per_block(nseq, seq_len)
    rows = nsub * seq_len
    col = lambda c: pl.BlockSpec((rows, W_GRP), lambda b: (b, c))
    lay3 = lambda a: pl.BlockSpec((1,) + a.shape[1:], lambda b: (layer, 0, 0))
    lay4 = lambda a: pl.BlockSpec((1,) + a.shape[1:], lambda b: (layer, 0, 0, 0))
    params = [conv_w, conv_b, wr_bd, b_r, wi_bd, b_i, lam, sconv_w]
    args = [proj] * 5 + params
    in_specs = [col(9), col(10), col(11), col(12), col(13), lay4(conv_w), lay3(conv_b), _full_spec(wr_bd),
                lay3(b_r), _full_spec(wi_bd), lay3(b_i), lay3(lam), lay3(sconv_w)]
    if h0 is not None:
        args.append(h0)
        in_specs.append(pl.BlockSpec((nsub, 1, 2, W_GRP), lambda b: (b, layer, 0, 0)))
    out_shape = [jax.ShapeDtypeStruct((nseq * seq_len, W_GRP), BF16)] * 2
    out_specs = [pl.BlockSpec((rows, W_GRP), lambda b: (b, 0))] * 2
    if want_final:
        out_shape.append(jax.ShapeDtypeStruct((nseq, 2, W_GRP), F32))
        out_specs.append(pl.BlockSpec((nsub, 2, W_GRP), lambda b: (b, 0, 0)))
    res = pl.pallas_call(
        functools.partial(_rgsc_kernel, seq_len=seq_len, nsub=nsub, seg=seg, has_h0=h0 is not None,
                          want_final=want_final),
        grid=(nseq // nsub,),
        in_specs=in_specs,
        out_specs=out_specs,
        out_shape=out_shape,
        scratch_shapes=[pltpu.VMEM((nsub * (seq_len + 2 * PAD), W_GRP), F32), pltpu.VMEM((rows, W_GRP), F32),
                        pltpu.VMEM((rows, W_GRP), F32), pltpu.VMEM((rows, W_GRP), F32)],
        compiler_params=_params(("arbitrary",)),
        name="rglru_sconv_mixer",
    )(*args)
    return res[0], res[1], (res[2] if want_final else None)


def _block_diag(w):
    two, h, c, _ = w.shape
    eye = jnp.eye(h, dtype=w.dtype)
    full = w[:, :, :, None, :] * eye[None, :, None, :, None]
    return full.reshape(two, h * c, h * c)


def kernel(x_prompt, x_sample, state_hgrn, state_gla, state_rglru, c, c_ctx, norm1_g, norm2_g, ada_w, ada_b, w_in, w_out, hgrn_lb_logits, hgrn_norm_g, gla_wa2, gla_ba2, gla_norm_g, rg_conv_w, rg_conv_b, rg_w_r, rg_b_r, rg_w_i, rg_b_i, rg_lambda, sconv_w, mlp_w1, mlp_w2, final_norm_g):
    b_ctx, l_ctx, _ = x_prompt.shape
    b_lat, l_lat, _ = x_sample.shape

    cvec = jnp.concatenate([c, c_ctx[None, :], jnp.zeros((8 - b_lat - 1, D_MODEL), F32)], axis=0)
    mod = _modulation(cvec, ada_w, ada_b).reshape(DEPTH, 8, 6, D_MODEL)

    w_cat = _build_proj_weight(w_in, gla_wa2)

    xp = x_prompt.reshape(b_ctx * l_ctx, D_MODEL)
    xs = x_sample.reshape(b_lat * l_lat, D_MODEL)
    new_h, new_g, new_r = [], [], []
    for l in range(DEPTH):
        wr_bd = _block_diag(rg_w_r[l]).astype(BF16)
        wi_bd = _block_diag(rg_w_i[l]).astype(BF16)
        mod_ctx = mod[l, b_lat:b_lat + 1]
        mod_lat = mod[l, :b_lat]
        last = l == DEPTH - 1
        streams = (
            (xp, mod_ctx, b_ctx, l_ctx, l_ctx, None, None, None, True),
            (xs, mod_lat, b_lat, l_lat, GRID_W, state_hgrn, state_gla, state_rglru, False),
        )
        outs = []
        for (x, m, nseq, seq_len, seg, s_h, s_g, s_r, want_final) in streams:
            proj = _project(x, m, norm1_g, w_cat, l)
            o_a, f_h = _hgrn_mixer(proj, hgrn_lb_logits, hgrn_norm_g, s_h, l, nseq, seq_len, want_final)
            o_b, f_g = _gla_mixer(proj, gla_ba2, gla_norm_g, s_g, l, nseq, seq_len, want_final)
            o_c, o_d, f_r = _rgsc_mixer(proj, rg_conv_w, rg_conv_b, wr_bd, rg_b_r, wi_bd, rg_b_i,
                                        rg_lambda, sconv_w, s_r, l, nseq, seq_len, seg, want_final)
            x_new = _out_mlp(x, o_a, o_b, o_c, o_d, m, norm2_g, final_norm_g, w_out, mlp_w1, mlp_w2, l, last)
            outs.append((x_new, f_h, f_g, f_r))
        xp, f_h, f_g, f_r = outs[0]
        xs = outs[1][0]
        new_h.append(f_h)
        new_g.append(f_g)
        new_r.append(f_r)
    y_prompt = xp.reshape(b_ctx, l_ctx, D_MODEL)
    y_sample = xs.reshape(b_lat, l_lat, D_MODEL)
    return (y_prompt, y_sample, jnp.stack(new_h, axis=1), jnp.stack(new_g, axis=1),
            jnp.stack(new_r, axis=1))
```

```python
import functools

import numpy as np
import jax
import jax.numpy as jnp
from jax import lax
from jax.experimental import pallas as pl
from jax.experimental.pallas import tpu as pltpu

F32 = jnp.float32
BF16 = jnp.bfloat16

D_MODEL = 1024
DEPTH = 2
GRID_W = 64
N_HEADS = 4
W_GRP = 256
HEAD_V = 64
HGRN_DK = 64
GLA_DK = 32
GLA_RANK = 16
GLA_NORMALIZER = 16.0
RG_C = 8.0
D_FF = 4 * D_MODEL
EPS = 1e-6
F_FLOOR = 1e-20

PROJ_COLS = 3584

CHUNK = 64
HALF = CHUNK // 2
assert CHUNK == HEAD_V
GROUP = 4
GROWS = GROUP * CHUNK
MAIN_GROUP = 8
BLOCK_ROWS = 1024
LANES = 128
SAFE_RANGE = 80.0
TOK_TILE = 1024
PROJ_TM = 512
PROJ_TN = 512
FF_TILE = 1024
MLP_ROW_SPLIT = 2
ROW_BLOCK = 256
VMEM_LIMIT = 56 * 1024 * 1024

_NT = (((1,), (1,)), ((), ()))
_TN = (((0,), (0,)), ((), ()))


def _dot(a, b):
    return jnp.dot(a.astype(BF16), b.astype(BF16), preferred_element_type=F32)


def _dot_nt(a, b):
    return lax.dot_general(a.astype(BF16), b.astype(BF16), _NT, preferred_element_type=F32)


def _dot_tn(a, b):
    return lax.dot_general(a.astype(BF16), b.astype(BF16), _TN, preferred_element_type=F32)


def _sigmoid(x):
    return 1.0 / (1.0 + jnp.exp(-x))


def _silu(x):
    return x * _sigmoid(x)


def _log1p(y):
    u = 1.0 + y
    return jnp.where(u == 1.0, y, jnp.log(u) * (y / (u - 1.0)))


def _softplus(x):
    return jnp.maximum(x, 0.0) + _log1p(jnp.exp(-jnp.abs(x)))


def _rms(x):
    return x * lax.rsqrt(jnp.mean(x * x, axis=-1, keepdims=True) + EPS)


def _params(sem):
    return pltpu.CompilerParams(dimension_semantics=sem, vmem_limit_bytes=VMEM_LIMIT)


def _full_spec(a):
    zeros = (0,) * a.ndim
    return pl.BlockSpec(a.shape, lambda *_: zeros)


def _mod_kernel(c_ref, w_ref, b_ref, o_ref):
    s = _silu(c_ref[...])
    o_ref[0] = _dot(s, w_ref[0]) + b_ref[0]


def _modulation(cvec, ada_w, ada_b):
    tn = 2048
    return pl.pallas_call(
        _mod_kernel,
        grid=(DEPTH, 6 * D_MODEL // tn),
        in_specs=[pl.BlockSpec((8, D_MODEL), lambda l, j: (0, 0)),
                  pl.BlockSpec((1, D_MODEL, tn), lambda l, j: (l, 0, j)),
                  pl.BlockSpec((1, 1, tn), lambda l, j: (l, 0, j))],
        out_specs=pl.BlockSpec((1, 8, tn), lambda l, j: (l, 0, j)),
        out_shape=jax.ShapeDtypeStruct((DEPTH, 8, 6 * D_MODEL), F32),
        compiler_params=_params(("arbitrary", "arbitrary")),
        name="adaln_modulation",
    )(cvec, ada_w, ada_b.reshape(DEPTH, 1, 6 * D_MODEL))


LOWRANK_COL = 2048
TAIL_COL = LOWRANK_COL + 2 * GLA_RANK
TAIL_WIDTH = 1280
WCAT_LANES = 256


def _wcat_kernel(w_ref, wa2t_ref, o_ref):
    o_ref[0, 0:LOWRANK_COL, :] = w_ref[0, 0:LOWRANK_COL, :].astype(BF16)
    z = jnp.dot(wa2t_ref[0], w_ref[0, LOWRANK_COL:TAIL_COL, :], preferred_element_type=F32,
                precision=lax.Precision.HIGHEST)
    o_ref[0, LOWRANK_COL:LOWRANK_COL + 2 * LANES, :] = z.astype(BF16)
    o_ref[0, LOWRANK_COL + 2 * LANES:PROJ_COLS, :] = w_ref[0, TAIL_COL:TAIL_COL + TAIL_WIDTH, :].astype(BF16)


def _build_proj_weight(w_in, wa2):
    ncol = w_in.shape[2]
    assert ncol == TAIL_COL + TAIL_WIDTH and LOWRANK_COL + 2 * LANES + TAIL_WIDTH == PROJ_COLS
    n = N_HEADS * GLA_DK
    wa2t = jnp.zeros((DEPTH, 2 * n, 2 * GLA_RANK), F32)
    wa2t = wa2t.at[:, 0:n, 0:GLA_RANK].set(jnp.swapaxes(wa2[:, 0], 1, 2))
    wa2t = wa2t.at[:, n:2 * n, GLA_RANK:2 * GLA_RANK].set(jnp.swapaxes(wa2[:, 1], 1, 2))
    w_t = jnp.swapaxes(w_in, 1, 2)
    return pl.pallas_call(
        _wcat_kernel,
        grid=(DEPTH, D_MODEL // WCAT_LANES),
        in_specs=[pl.BlockSpec((1, ncol, WCAT_LANES), lambda l, i: (l, 0, i)),
                  pl.BlockSpec((1, 2 * n, 2 * GLA_RANK), lambda l, i: (l, 0, 0))],
        out_specs=pl.BlockSpec((1, PROJ_COLS, WCAT_LANES), lambda l, i: (l, 0, i)),
        out_shape=jax.ShapeDtypeStruct((DEPTH, PROJ_COLS, D_MODEL), BF16),
        compiler_params=_params(("arbitrary", "arbitrary")),
        name="build_proj_weight",
    )(w_t, wa2t)


def _proj_kernel(x_ref, mod_ref, g_ref, w_ref, o_ref):
    shift = mod_ref[0, 0:1, :]
    scale = mod_ref[0, 1:2, :]
    h = (_rms(x_ref[...]) * g_ref[0] * (1.0 + scale) + shift).astype(BF16)
    for n in range(PROJ_COLS // PROJ_TN):
        cols = slice(n * PROJ_TN, (n + 1) * PROJ_TN)
        o_ref[:, cols] = lax.dot_general(h, w_ref[0, cols, :], _NT, preferred_element_type=F32)


def _mod_index(ntok, nmod, tile):
    tiles_per_mod = (ntok // tile) // nmod if nmod > 1 else 1
    if nmod > 1:
        return lambda i, *_: (i // tiles_per_mod, 0, 0)
    return lambda i, *_: (0, 0, 0)


def _project(x, mod, norm_g, w_cat, layer):
    ntok = x.shape[0]
    return pl.pallas_call(
        _proj_kernel,
        grid=(ntok // PROJ_TM,),
        in_specs=[pl.BlockSpec((PROJ_TM, D_MODEL), lambda i: (i, 0)),
                  pl.BlockSpec((1, 6, D_MODEL), _mod_index(ntok, mod.shape[0], PROJ_TM)),
                  pl.BlockSpec((1, 1, D_MODEL), lambda i: (layer, 0, 0)),
                  pl.BlockSpec((1, PROJ_COLS, D_MODEL), lambda i: (layer, 0, 0),
                               pipeline_mode=pl.Buffered(1))],
        out_specs=pl.BlockSpec((PROJ_TM, PROJ_COLS), lambda i: (i, 0)),
        out_shape=jax.ShapeDtypeStruct((ntok, PROJ_COLS), F32),
        compiler_params=_params(("arbitrary",)),
        name="norm_in_proj",
    )(x, mod, norm_g.reshape(DEPTH, 1, D_MODEL), w_cat)


def _mlp_kernel(x_ref, oa_ref, ob_ref, oc_ref, od_ref, mod_ref, g2_ref, fg_ref, wout_ref,
                w1_ref, w2_ref, out_ref, h2_scr, *, final_norm):
    j = pl.program_id(1)

    @pl.when(j == 0)
    def _():
        mixed = jnp.concatenate([oa_ref[...], ob_ref[...], oc_ref[...], od_ref[...]], axis=1)
        mix = jnp.dot(mixed, wout_ref[0].astype(BF16), preferred_element_type=F32)
        x1 = x_ref[...] + mod_ref[0, 2:3, :] * mix
        out_ref[...] = x1
        h2 = _rms(x1) * g2_ref[0] * (1.0 + mod_ref[0, 4:5, :]) + mod_ref[0, 3:4, :]
        h2_scr[...] = h2.astype(BF16)

    w1 = w1_ref[0].astype(BF16)
    w2 = w2_ref[0].astype(BF16)
    for r in range(MLP_ROW_SPLIT):
        rows = slice(r * (TOK_TILE // MLP_ROW_SPLIT), (r + 1) * (TOK_TILE // MLP_ROW_SPLIT))
        t = jnp.dot(h2_scr[rows, :], w1, preferred_element_type=F32)
        t = jnp.square(jnp.maximum(t, 0.0)).astype(BF16)
        out_ref[rows, :] += mod_ref[0, 5:6, :] * jnp.dot(t, w2, preferred_element_type=F32)

    if final_norm:
        @pl.when(j == pl.num_programs(1) - 1)
        def _():
            out_ref[...] = _rms(out_ref[...]) * fg_ref[...]


def _out_mlp(x, o_a, o_b, o_c, o_d, mod, norm2_g, final_g, w_out, w1, w2, layer, final_norm):
    ntok = x.shape[0]
    tok = lambda i, j: (i, 0)
    return pl.pallas_call(
        functools.partial(_mlp_kernel, final_norm=final_norm),
        grid=(ntok // TOK_TILE, D_FF // FF_TILE),
        in_specs=[pl.BlockSpec((TOK_TILE, D_MODEL), tok),
                  pl.BlockSpec((TOK_TILE, W_GRP), tok),
                  pl.BlockSpec((TOK_TILE, W_GRP), tok),
                  pl.BlockSpec((TOK_TILE, W_GRP), tok),
                  pl.BlockSpec((TOK_TILE, W_GRP), tok),
                  pl.BlockSpec((1, 6, D_MODEL), _mod_index(ntok, mod.shape[0], TOK_TILE)),
                  pl.BlockSpec((1, 1, D_MODEL), lambda i, j: (layer, 0, 0)),
                  pl.BlockSpec((1, D_MODEL), lambda i, j: (0, 0)),
                  pl.BlockSpec((1, D_MODEL, D_MODEL), lambda i, j: (layer, 0, 0)),
                  pl.BlockSpec((1, D_MODEL, FF_TILE), lambda i, j: (layer, 0, j)),
                  pl.BlockSpec((1, FF_TILE, D_MODEL), lambda i, j: (layer, j, 0))],
        out_specs=pl.BlockSpec((TOK_TILE, D_MODEL), tok),
        out_shape=jax.ShapeDtypeStruct((ntok, D_MODEL), F32),
        scratch_shapes=[pltpu.VMEM((TOK_TILE, D_MODEL), BF16)],
        compiler_params=_params(("arbitrary", "arbitrary")),
        name="out_proj_mlp",
    )(x, o_a, o_b, o_c, o_d, mod, norm2_g.reshape(DEPTH, 1, D_MODEL), final_g.reshape(1, D_MODEL),
      w_out, w1, w2)


def _split3(x):
    hi = x.astype(BF16)
    r1 = x - hi.astype(F32)
    mid = r1.astype(BF16)
    lo = (r1 - mid.astype(F32)).astype(BF16)
    return hi, mid, lo


def _dot_exact_rhs(x, m):
    hi, mid, lo = _split3(x)
    s = jnp.dot(hi, m, preferred_element_type=F32)
    s += jnp.dot(mid, m, preferred_element_type=F32)
    s += jnp.dot(lo, m, preferred_element_type=F32)
    return s


def _block_cumsum(tri, g):
    hi = g.astype(BF16)
    lo = (g - hi.astype(F32)).astype(BF16)
    return jnp.dot(tri, hi, preferred_element_type=F32) + jnp.dot(tri, lo, preferred_element_type=F32)


def _run_prepass(pre, nblock, width):
    rmax = jnp.zeros((1, width), F32)
    if nblock % 2:
        for bi in range(nblock):
            rmax = pre(bi, rmax)
        return rmax
    if nblock == 2:
        return pre(1, pre(0, rmax))
    return lax.fori_loop(0, nblock // 2, lambda i, rm: pre(2 * i + 1, pre(2 * i, rm)), rmax)


def _half_ranges(cum, reverse):
    if reverse:
        second = -cum[HALF:HALF + 1, :]
        first = -(cum[0:1, :] - cum[HALF:HALF + 1, :])
    else:
        first = -cum[HALF - 1:HALF, :]
        second = -(cum[CHUNK - 1:CHUNK, :] - cum[HALF - 1:HALF, :])
    return jnp.maximum(first, second)


def _block_ranges(cum, reverse, rm):
    for c in range(GROUP):
        rm = jnp.maximum(rm, _half_ranges(cum[c * CHUNK:(c + 1) * CHUNK, :], reverse))
    return rm


def _row_start(c):
    r0 = c * CHUNK
    return r0 if isinstance(r0, int) else pl.multiple_of(r0, CHUNK)


class _Rec:
    def __init__(self, q_at, k_at, v_ref, cum_scr, st_scr, oacc, consts, width, seq_len, nsub):
        self.q_at, self.k_at, self.v_ref = q_at, k_at, v_ref
        self.cum_scr, self.st_scr, self.oacc = cum_scr, st_scr, oacc
        (self.tri_ref, self.bdw_b, self.bdw_f, self.bdv_b, self.cmask, self.bmat,
         self.tile_t) = consts
        self.width, self.seq_len, self.nsub = width, seq_len, nsub


def _chunk_step(rec, sub, d, c, fast):
    reverse = d == 1
    mid_row = HALF if reverse else HALF - 1
    last_row = 0 if reverse else CHUNK - 1
    sd = sub * 2 + d
    r0 = sub * rec.seq_len + c * CHUNK
    if not isinstance(r0, int):
        r0 = pl.multiple_of(r0, CHUNK)
    rows = pl.ds(r0, CHUNK)
    q = rec.q_at(r0)
    k = rec.k_at(d, r0)
    v_b = rec.v_ref[rows, :].astype(BF16)
    cum = rec.cum_scr[d, rows, :]
    tot = cum[last_row:last_row + 1, :]
    st = rec.st_scr[sd]
    o = _dot_nt(q * jnp.exp(cum), st)
    if fast:
        cm = cum[mid_row:mid_row + 1, :]
        qm = q * jnp.exp(cum - cm)
        km_b = (k * jnp.exp(cm - cum)).astype(BF16)
        kbd = jnp.concatenate([km_b] * N_HEADS, axis=0) * rec.bdw_b[...]
        sc = lax.dot_general(qm.astype(BF16), kbd, _NT, preferred_element_type=F32)
        a = jnp.where(rec.cmask[d] > 0.5, sc, 0.0)
        vexp = jnp.concatenate([v_b] * N_HEADS, axis=0) * rec.bdv_b[...]
        o = o + jnp.dot(a.astype(BF16), vexp, preferred_element_type=F32)
    else:
        row_id = lax.broadcasted_iota(jnp.int32, (CHUNK, rec.width), 0)

        def key_row(j, acc):
            krow = rec.k_at(d, r0 + j, 1)
            crow = rec.cum_scr[d, pl.ds(r0 + j, 1), :]
            vrow = rec.v_ref[pl.ds(r0 + j, 1), :]
            keep = (row_id <= j) if reverse else (row_id >= j)
            e = jnp.exp(jnp.minimum(cum - crow, 0.0))
            p = jnp.where(keep, q * krow * e, 0.0)
            return acc + _dot(p, rec.bmat[...]) * vrow

        o = lax.fori_loop(0, CHUNK, key_row, o)
    ke_b = (k * jnp.exp(tot - cum)).astype(BF16)
    ds = lax.dot_general(v_b, ke_b, _TN, preferred_element_type=F32)
    rec.st_scr[sd] = st * jnp.exp(tot) + ds * rec.bdw_f[...]
    rec.oacc[d, rows, :] = o


def _load_state(rec, s0_ref, sub, d):
    x = jnp.concatenate([s0_ref[sub, 0, d, h] for h in range(N_HEADS)], axis=0)
    y = _dot_exact_rhs(x, rec.tile_t[...]) * rec.bmat[...].astype(F32)
    rec.st_scr[sub * 2 + d] = y.T


def _store_state(rec, sfin_ref, sub, d):
    y = rec.st_scr[sub * 2 + d].T
    half = y[:, 0:LANES] + y[:, LANES:2 * LANES]
    x = (half + pltpu.roll(half, HEAD_V, 1))[:, 0:HEAD_V]
    dk = rec.width // N_HEADS
    for h in range(N_HEADS):
        sfin_ref[sub, d, h] = x[h * dk:(h + 1) * dk, :]


def _run_recurrence(rec, ok, s0_ref, sfin_ref):
    nsub = rec.nsub
    nchunk = rec.seq_len // CHUNK
    gsize = min(MAIN_GROUP, nchunk)
    ngroup = nchunk // gsize
    spp = max(1, min(nsub, MAIN_GROUP // gsize))
    nsp = nsub // spp
    for sub in range(nsub):
        for d in (0, 1):
            if s0_ref is None:
                rec.st_scr[sub * 2 + d] = jnp.zeros(rec.st_scr.shape[1:], F32)
            else:
                _load_state(rec, s0_ref, sub, d)

    def fast_all():
        def step(i, carry):
            sp = i // ngroup if (nsp > 1 and ngroup > 1) else (i if nsp > 1 else 0)
            gi = i % ngroup if (nsp > 1 and ngroup > 1) else (i if ngroup > 1 else 0)
            for u in range(gsize):
                s = gi * gsize + u
                for j in range(spp):
                    sub = sp * spp + j
                    _chunk_step(rec, sub, 0, s, True)
                    _chunk_step(rec, sub, 1, nchunk - 1 - s, True)
            return carry

        if nsp * ngroup == 1:
            step(0, 0)
        else:
            lax.fori_loop(0, nsp * ngroup, step, 0)

    def direct_all():
        def one(i, carry):
            sub = i // nchunk if nsub > 1 else 0
            s = i % nchunk if nsub > 1 else i
            _chunk_step(rec, sub, 0, s, False)
            _chunk_step(rec, sub, 1, nchunk - 1 - s, False)
            return carry

        lax.fori_loop(0, nsub * nchunk, one, 0)

    lax.cond(ok, fast_all, direct_all)
    if sfin_ref is not None:
        for sub in range(nsub):
            for d in (0, 1):
                _store_state(rec, sfin_ref, sub, d)


def _head_norm_gate(rec, gate_ref, gain_ref, o_ref):
    for rb in range(rec.nsub * rec.seq_len // ROW_BLOCK):
        rows = slice(rb * ROW_BLOCK, (rb + 1) * ROW_BLOCK)
        o = rec.oacc[0, rows, :] + rec.oacc[1, rows, :]
        ms = jnp.dot((o * o).astype(BF16), rec.bdv_b[...], preferred_element_type=F32) * (1.0 / HEAD_V)
        y = o * lax.rsqrt(ms + EPS) * gain_ref[0] * _silu(gate_ref[rows, :])
        o_ref[rows, :] = y.astype(o_ref.dtype)


def _split_refs(refs, n_in, has_s0, want_final):
    ins = refs[:n_in]
    pos = n_in
    s0_ref = None
    if has_s0:
        s0_ref = refs[pos]
        pos += 1
    o_ref = refs[pos]
    pos += 1
    sfin_ref = None
    if want_final:
        sfin_ref = refs[pos]
        pos += 1
    return ins, s0_ref, o_ref, sfin_ref, refs[pos:]


def _hgrn_kernel(*refs, seq_len, nsub, layer, has_s0, want_final):
    ins, s0_ref, o_ref, sfin_ref, scr = _split_refs(refs, 14, has_s0, want_final)
    q_ref, i_ref, ff_ref, fb_ref, g_ref, lbl_ref, gain_ref = ins[:7]
    k_scr, cum_scr, oacc, st_scr = scr
    q_at = lambda r0: q_ref[pl.ds(r0, CHUNK), :]
    k_at = lambda d, r0, n=CHUNK: k_scr[d, pl.ds(r0, n), :]
    rec = _Rec(q_at, k_at, i_ref, cum_scr, st_scr, oacc, ins[7:], W_GRP, seq_len, nsub)

    lg = lbl_ref[...]
    mx = lg[0]
    for i in range(1, DEPTH):
        mx = jnp.maximum(mx, lg[i])
    ex = [jnp.exp(lg[i] - mx) for i in range(DEPTH)]
    den = ex[0]
    for i in range(1, DEPTH):
        den = den + ex[i]
    sm = [e / den for e in ex]
    csum = sm[0]
    for i in range(1, layer + 1):
        csum = csum + sm[i]
    lb = csum - sm[0]

    def pre(bi, rm):
        r0 = bi * GROWS if isinstance(bi, int) else pl.multiple_of(bi * GROWS, GROWS)
        rows = pl.ds(r0, GROWS)
        for d, f_ref in ((0, ff_ref), (1, fb_ref)):
            lbd = lb[d:d + 1, :]
            f = lbd + (1.0 - lbd) * _sigmoid(f_ref[rows, :])
            g = jnp.log(jnp.maximum(f, F_FLOOR))
            k_scr[d, rows, :] = 1.0 - f
            cum = _block_cumsum(rec.tri_ref[d], g)
            cum_scr[d, rows, :] = cum
            rm = _block_ranges(cum, d == 1, rm)
        return rm

    rmax = _run_prepass(pre, nsub * seq_len // GROWS, W_GRP)
    ok = jnp.max(rmax) < SAFE_RANGE

    _run_recurrence(rec, ok, s0_ref, sfin_ref)
    _head_norm_gate(rec, g_ref, gain_ref, o_ref)


def _gla_kernel(*refs, seq_len, nsub, has_s0, want_final):
    ins, s0_ref, o_ref, sfin_ref, scr = _split_refs(refs, 15, has_s0, want_final)
    q_ref, k_ref, v_ref, g_ref, zf_ref, zb_ref, ba_ref, gain_ref = ins[:8]
    cum_scr, oacc, st_scr = scr
    width = N_HEADS * GLA_DK
    q_at = lambda r0: q_ref[pl.ds(r0, CHUNK), :] * (GLA_DK ** -0.5)
    k_at = lambda d, r0, n=CHUNK: k_ref[pl.ds(r0, n), :]
    rec = _Rec(q_at, k_at, v_ref, cum_scr, st_scr, oacc, ins[8:], width, seq_len, nsub)

    def pre(bi, rm):
        r0 = bi * GROWS if isinstance(bi, int) else pl.multiple_of(bi * GROWS, GROWS)
        rows = pl.ds(r0, GROWS)
        for d, z_ref in ((0, zf_ref), (1, zb_ref)):
            z = z_ref[rows, :] + ba_ref[0, d:d + 1, :]
            g = -_softplus(-z) * (1.0 / GLA_NORMALIZER)
            cum = _block_cumsum(rec.tri_ref[d], g)
            cum_scr[d, rows, :] = cum
            rm = _block_ranges(cum, d == 1, rm)
        return rm

    rmax = _run_prepass(pre, nsub * seq_len // GROWS, width)
    ok = jnp.max(rmax) < SAFE_RANGE

    _run_recurrence(rec, ok, s0_ref, sfin_ref)
    _head_norm_gate(rec, g_ref, gain_ref, o_ref)


def _recurrence_constants(width):
    dk = width // N_HEADS
    r = np.arange(N_HEADS * CHUNK)[:, None]
    tri = np.kron(np.eye(GROUP, dtype=np.float32), np.tril(np.ones((CHUNK, CHUNK), np.float32)))
    tri = np.stack([tri, tri.T])
    bdw = (r // CHUNK == np.arange(width)[None, :] // dk).astype(np.float32)
    bdv = (r // CHUNK == np.arange(W_GRP)[None, :] // HEAD_V).astype(np.float32)
    t = np.arange(CHUNK)[:, None]
    s = np.arange(N_HEADS * CHUNK)[None, :] % CHUNK
    cmask = np.stack([(s <= t), (s >= t)]).astype(np.float32)
    bmat = (np.arange(width)[:, None] // dk == np.arange(W_GRP)[None, :] // HEAD_V).astype(np.float32)
    tile_t = np.tile(np.eye(HEAD_V, dtype=np.float32), (1, N_HEADS))
    return (jnp.asarray(tri, BF16), jnp.asarray(bdw, BF16), jnp.asarray(bdw), jnp.asarray(bdv, BF16),
            jnp.asarray(cmask), jnp.asarray(bmat, BF16), jnp.asarray(tile_t, BF16))


def _recurrent_mixer(kernel_fn, name, width, args, in_specs, state, layer, nseq, seq_len, want_final,
                     extra_scratch):
    dk = width // N_HEADS
    nsub = _seqs_per_block(nseq, seq_len)
    rows = nsub * seq_len
    consts = _recurrence_constants(width)
    args = list(args) + list(consts)
    in_specs = list(in_specs) + [_full_spec(c) for c in consts]
    if state is not None:
        args.append(state)
        in_specs.append(pl.BlockSpec((nsub, 1, 2, N_HEADS, dk, HEAD_V), lambda b: (b, layer, 0, 0, 0, 0)))
    out_shape = [jax.ShapeDtypeStruct((nseq * seq_len, W_GRP), BF16)]
    out_specs = [pl.BlockSpec((rows, W_GRP), lambda b: (b, 0))]
    if want_final:
        out_shape.append(jax.ShapeDtypeStruct((nseq, 2, N_HEADS, dk, HEAD_V), F32))
        out_specs.append(pl.BlockSpec((nsub, 2, N_HEADS, dk, HEAD_V), lambda b: (b, 0, 0, 0, 0)))
    res = pl.pallas_call(
        functools.partial(kernel_fn, seq_len=seq_len, nsub=nsub, has_s0=state is not None,
                          want_final=want_final),
        grid=(nseq // nsub,),
        in_specs=in_specs,
        out_specs=out_specs,
        out_shape=out_shape,
        scratch_shapes=[pltpu.VMEM((2, rows, W_GRP), F32)] * extra_scratch + [
            pltpu.VMEM((2, rows, width), F32), pltpu.VMEM((2, rows, W_GRP), F32),
            pltpu.VMEM((2 * nsub, W_GRP, width), F32)],
        compiler_params=_params(("arbitrary",)),
        name=name,
    )(*args)
    return res[0], (res[1] if want_final else None)


def _seqs_per_block(nseq, seq_len):
    return max(1, min(nseq, BLOCK_ROWS // seq_len))


def _hgrn_mixer(proj, lb_logits, gain, state, layer, nseq, seq_len, want_final):
    rows = _seqs_per_block(nseq, seq_len) * seq_len
    col = lambda c: pl.BlockSpec((rows, W_GRP), lambda b: (b, c))
    args = [proj] * 5 + [lb_logits, gain.reshape(DEPTH, 1, W_GRP)]
    in_specs = [col(0), col(1), col(2), col(3), col(4), _full_spec(lb_logits),
                pl.BlockSpec((1, 1, W_GRP), lambda b: (layer, 0, 0))]
    return _recurrent_mixer(functools.partial(_hgrn_kernel, layer=layer), "hgrn2_mixer", W_GRP, args,
                            in_specs, state, layer, nseq, seq_len, want_final, 1)


def _gla_mixer(proj, ba2, gain, state, layer, nseq, seq_len, want_final):
    width = N_HEADS * GLA_DK
    rows = _seqs_per_block(nseq, seq_len) * seq_len
    col256 = lambda c: pl.BlockSpec((rows, W_GRP), lambda b: (b, c))
    col128 = lambda c: pl.BlockSpec((rows, width), lambda b: (b, c))
    args = [proj] * 6 + [ba2, gain.reshape(DEPTH, 1, W_GRP)]
    in_specs = [col128(10), col128(11), col256(6), col256(7), col128(16), col128(17),
                pl.BlockSpec((1, 2, width), lambda b: (layer, 0, 0)),
                pl.BlockSpec((1, 1, W_GRP), lambda b: (layer, 0, 0))]
    return _recurrent_mixer(_gla_kernel, "gla_mixer", width, args, in_specs, state, layer, nseq, seq_len,
                            want_final, 0)


PAD = 8
SCAN_TILE = 8
SCAN_UNROLL = 4


def _rgsc_kernel(*refs, seq_len, nsub, seg, has_h0, want_final):
    (cx_ref, cg_ref, db_ref, dc_ref, dv_ref, convw_ref, convb_ref, wr_ref, br_ref, wi_ref,
     bi_ref, lam_ref, sw_ref) = refs[:13]
    pos = 13
    h0_ref = None
    if has_h0:
        h0_ref = refs[pos]
        pos += 1
    oc_ref, od_ref = refs[pos:pos + 2]
    pos += 2
    hfin_ref = None
    if want_final:
        hfin_ref = refs[pos]
        pos += 1
    upad, a_scr, b_scr, h_scr = refs[pos:]

    slot = seq_len + 2 * PAD
    zpad = jnp.zeros((PAD, W_GRP), F32)
    for sub in range(nsub):
        upad[sub * slot:sub * slot + PAD, :] = zpad
        upad[sub * slot + PAD + seq_len:(sub + 1) * slot, :] = zpad
        upad[sub * slot + PAD:sub * slot + PAD + seq_len, :] = cx_ref[sub * seq_len:(sub + 1) * seq_len, :]
    rows8 = lax.broadcasted_iota(jnp.int32, (SCAN_TILE, W_GRP), 0)
    ntile = seq_len // SCAN_TILE

    for d in (0, 1):
        reverse = d == 1
        w = convw_ref[0, d]
        offs = [PAD + 3, PAD + 2, PAD + 1, PAD] if reverse else [PAD - 3, PAD - 2, PAD - 1, PAD]
        sp = _softplus(-lam_ref[0, d:d + 1, :])
        for sub in range(nsub):
            for rb in range(seq_len // ROW_BLOCK):
                src = sub * slot + rb * ROW_BLOCK
                base = sub * seq_len + rb * ROW_BLOCK
                xc = convb_ref[0, d:d + 1, :]
                for j in range(4):
                    xc = xc + w[j:j + 1, :] * upad[offs[j] + src:offs[j] + src + ROW_BLOCK, :]
                r = _sigmoid(_dot(xc, wr_ref[d]) + br_ref[0, d:d + 1, :])
                ig = _sigmoid(_dot(xc, wi_ref[d]) + bi_ref[0, d:d + 1, :])
                log_a = -RG_C * r * sp
                a = jnp.exp(log_a)
                a_scr[base:base + ROW_BLOCK, :] = a
                b_scr[base:base + ROW_BLOCK, :] = jnp.sqrt(1.0 - a * a) * (ig * xc)

        if has_h0:
            h0 = tuple(h0_ref[sub, 0, d:d + 1, :] for sub in range(nsub))
        else:
            h0 = tuple(jnp.zeros((1, W_GRP), F32) for _ in range(nsub))

        def step(ti, carry, d=d, reverse=reverse):
            t = (ntile - 1 - ti) if reverse else ti
            out = []
            for sub in range(nsub):
                r0 = pl.multiple_of(sub * seq_len + t * SCAN_TILE, SCAN_TILE)
                a = a_scr[pl.ds(r0, SCAN_TILE), :]
                b = b_scr[pl.ds(r0, SCAN_TILE), :]
                for s in (1, 2, 4):
                    if reverse:
                        keep = rows8 <= SCAN_TILE - 1 - s
                        shift = SCAN_TILE - s
                    else:
                        keep = rows8 >= s
                        shift = s
                    a_s = jnp.where(keep, pltpu.roll(a, shift, 0), 1.0)
                    b_s = jnp.where(keep, pltpu.roll(b, shift, 0), 0.0)
                    b = b + a * b_s
                    a = a * a_s
                h = a * carry[sub] + b
                if d == 0:
                    h_scr[pl.ds(r0, SCAN_TILE), :] = h
                else:
                    h_scr[pl.ds(r0, SCAN_TILE), :] += h
                out.append(h[0:1, :] if reverse else h[SCAN_TILE - 1:SCAN_TILE, :])
            return tuple(out)

        hlast = lax.fori_loop(0, ntile, step, h0, unroll=SCAN_UNROLL // nsub if nsub < SCAN_UNROLL else 1)
        if want_final:
            for sub in range(nsub):
                hfin_ref[sub, d:d + 1, :] = hlast[sub]

    sw = sw_ref[0]
    nblock = nsub * seq_len // ROW_BLOCK
    for rb in range(nblock):
        rows = slice(rb * ROW_BLOCK, (rb + 1) * ROW_BLOCK)
        x = cg_ref[rows, :]
        gelu = 0.5 * x * (1.0 + jnp.tanh(0.7978845608028654 * (x + 0.044715 * (x * x * x))))
        oc_ref[rows, :] = (h_scr[rows, :] * gelu).astype(oc_ref.dtype)
        upad[PAD + rb * ROW_BLOCK:PAD + (rb + 1) * ROW_BLOCK, :] = dc_ref[rows, :] * dv_ref[rows, :]
    for rb in range(nblock):
        base = rb * ROW_BLOCK
        rows = slice(base, base + ROW_BLOCK)
        posn = (lax.broadcasted_iota(jnp.int32, (ROW_BLOCK, W_GRP), 0) + base) % seg
        left = jnp.where(posn != 0, upad[PAD - 1 + base:PAD - 1 + base + ROW_BLOCK, :], 0.0)
        right = jnp.where(posn != seg - 1, upad[PAD + 1 + base:PAD + 1 + base + ROW_BLOCK, :], 0.0)
        y = sw[0:1, :] * left + sw[1:2, :] * upad[PAD + base:PAD + base + ROW_BLOCK, :] + sw[2:3, :] * right
        od_ref[rows, :] = (db_ref[rows, :] * y).astype(od_ref.dtype)


def _rgsc_mixer(proj, conv_w, conv_b, wr_bd, b_r, wi_bd, b_i, lam, sconv_w, h0, layer, nseq, seq_len,
                seg, want_final):
    nsub = _seqs_per_block(nseq, seq_len)
    rows = nsub * seq_len
    col = lambda c: pl.BlockSpec((rows, W_GRP), lambda b: (b, c))
    lay3 = lambda a: pl.BlockSpec((1,) + a.shape[1:], lambda b: (layer, 0, 0))
    lay4 = lambda a: pl.BlockSpec((1,) + a.shape[1:], lambda b: (layer, 0, 0, 0))
    params = [conv_w, conv_b, wr_bd, b_r, wi_bd, b_i, lam, sconv_w]
    args = [proj] * 5 + params
    in_specs = [col(9), col(10), col(11), col(12), col(13), lay4(conv_w), lay3(conv_b), _full_spec(wr_bd),
                lay3(b_r), _full_spec(wi_bd), lay3(b_i), lay3(lam), lay3(sconv_w)]
    if h0 is not None:
        args.append(h0)
        in_specs.append(pl.BlockSpec((nsub, 1, 2, W_GRP), lambda b: (b, layer, 0, 0)))
    out_shape = [jax.ShapeDtypeStruct((nseq * seq_len, W_GRP), BF16)] * 2
    out_specs = [pl.BlockSpec((rows, W_GRP), lambda b: (b, 0))] * 2
    if want_final:
        out_shape.append(jax.ShapeDtypeStruct((nseq, 2, W_GRP), F32))
        out_specs.append(pl.BlockSpec((nsub, 2, W_GRP), lambda b: (b, 0, 0)))
    res = pl.pallas_call(
        functools.partial(_rgsc_kernel, seq_len=seq_len, nsub=nsub, seg=seg, has_h0=h0 is not None,
                          want_final=want_final),
        grid=(nseq // nsub,),
        in_specs=in_specs,
        out_specs=out_specs,
        out_shape=out_shape,
        scratch_shapes=[pltpu.VMEM((nsub * (seq_len + 2 * PAD), W_GRP), F32), pltpu.VMEM((rows, W_GRP), F32),
                        pltpu.VMEM((rows, W_GRP), F32), pltpu.VMEM((rows, W_GRP), F32)],
        compiler_params=_params(("arbitrary",)),
        name="rglru_sconv_mixer",
    )(*args)
    return res[0], res[1], (res[2] if want_final else None)


def _block_diag(w):
    two, h, c, _ = w.shape
    eye = jnp.eye(h, dtype=w.dtype)
    full = w[:, :, :, None, :] * eye[None, :, None, :, None]
    return full.reshape(two, h * c, h * c)


def kernel(x_prompt, x_sample, state_hgrn, state_gla, state_rglru, c, c_ctx, norm1_g, norm2_g, ada_w, ada_b, w_in, w_out, hgrn_lb_logits, hgrn_norm_g, gla_wa2, gla_ba2, gla_norm_g, rg_conv_w, rg_conv_b, rg_w_r, rg_b_r, rg_w_i, rg_b_i, rg_lambda, sconv_w, mlp_w1, mlp_w2, final_norm_g):
    b_ctx, l_ctx, _ = x_prompt.shape
    b_lat, l_lat, _ = x_sample.shape

    cvec = jnp.concatenate([c, c_ctx[None, :], jnp.zeros((8 - b_lat - 1, D_MODEL), F32)], axis=0)
    mod = _modulation(cvec, ada_w, ada_b).reshape(DEPTH, 8, 6, D_MODEL)

    w_cat = _build_proj_weight(w_in, gla_wa2)

    xp = x_prompt.reshape(b_ctx * l_ctx, D_MODEL)
    xs = x_sample.reshape(b_lat * l_lat, D_MODEL)
    new_h, new_g, new_r = [], [], []
    for l in range(DEPTH):
        wr_bd = _block_diag(rg_w_r[l]).astype(BF16)
        wi_bd = _block_diag(rg_w_i[l]).astype(BF16)
        mod_ctx = mod[l, b_lat:b_lat + 1]
        mod_lat = mod[l, :b_lat]
        last = l == DEPTH - 1
        streams = (
            (xp, mod_ctx, b_ctx, l_ctx, l_ctx, None, None, None, True),
            (xs, mod_lat, b_lat, l_lat, GRID_W, state_hgrn, state_gla, state_rglru, False),
        )
        outs = []
        for (x, m, nseq, seq_len, seg, s_h, s_g, s_r, want_final) in streams:
            proj = _project(x, m, norm1_g, w_cat, l)
            o_a, f_h = _hgrn_mixer(proj, hgrn_lb_logits, hgrn_norm_g, s_h, l, nseq, seq_len, want_final)
            o_b, f_g = _gla_mixer(proj, gla_ba2, gla_norm_g, s_g, l, nseq, seq_len, want_final)
            o_c, o_d, f_r = _rgsc_mixer(proj, rg_conv_w, rg_conv_b, wr_bd, rg_b_r, wi_bd, rg_b_i,
                                        rg_lambda, sconv_w, s_r, l, nseq, seq_len, seg, want_final)
            x_new = _out_mlp(x, o_a, o_b, o_c, o_d, m, norm2_g, final_norm_g, w_out, mlp_w1, mlp_w2, l, last)
            outs.append((x_new, f_h, f_g, f_r))
        xp, f_h, f_g, f_r = outs[0]
        xs = outs[1][0]
        new_h.append(f_h)
        new_g.append(f_g)
        new_r.append(f_r)
    y_prompt = xp.reshape(b_ctx, l_ctx, D_MODEL)
    y_sample = xs.reshape(b_lat, l_lat, D_MODEL)
    return (y_prompt, y_sample, jnp.stack(new_h, axis=1), jnp.stack(new_g, axis=1),
            jnp.stack(new_r, axis=1))
```

```python
import functools

import numpy as np
import jax
import jax.numpy as jnp
from jax import lax
from jax.experimental import pallas as pl
from jax.experimental.pallas import tpu as pltpu

F32 = jnp.float32
BF16 = jnp.bfloat16

D_MODEL = 1024
DEPTH = 2
GRID_W = 64
N_HEADS = 4
W_GRP = 256
HEAD_V = 64
HGRN_DK = 64
GLA_DK = 32
GLA_RANK = 16
GLA_NORMALIZER = 16.0
RG_C = 8.0
D_FF = 4 * D_MODEL
EPS = 1e-6
F_FLOOR = 1e-20

PROJ_COLS = 3584

CHUNK = 64
HALF = CHUNK // 2
assert CHUNK == HEAD_V
GROUP = 4
GROWS = GROUP * CHUNK
MAIN_GROUP = 8
BLOCK_ROWS = 1024
LANES = 128
SAFE_RANGE = 80.0
TOK_TILE = 1024
PROJ_TM = 512
PROJ_TN = 512
FF_TILE = 1024
MLP_ROW_SPLIT = 2
ROW_BLOCK = 256
VMEM_LIMIT = 56 * 1024 * 1024

_NT = (((1,), (1,)), ((), ()))
_TN = (((0,), (0,)), ((), ()))


def _dot(a, b):
    return jnp.dot(a.astype(BF16), b.astype(BF16), preferred_element_type=F32)


def _dot_nt(a, b):
    return lax.dot_general(a.astype(BF16), b.astype(BF16), _NT, preferred_element_type=F32)


def _dot_tn(a, b):
    return lax.dot_general(a.astype(BF16), b.astype(BF16), _TN, preferred_element_type=F32)


def _sigmoid(x):
    return 1.0 / (1.0 + jnp.exp(-x))


def _silu(x):
    return x * _sigmoid(x)


def _log1p(y):
    u = 1.0 + y
    return jnp.where(u == 1.0, y, jnp.log(u) * (y / (u - 1.0)))


def _softplus(x):
    return jnp.maximum(x, 0.0) + _log1p(jnp.exp(-jnp.abs(x)))


def _rms(x):
    return x * lax.rsqrt(jnp.mean(x * x, axis=-1, keepdims=True) + EPS)


def _params(sem):
    return pltpu.CompilerParams(dimension_semantics=sem, vmem_limit_bytes=VMEM_LIMIT)


def _full_spec(a):
    zeros = (0,) * a.ndim
    return pl.BlockSpec(a.shape, lambda *_: zeros)


def _mod_kernel(c_ref, w_ref, b_ref, o_ref):
    s = _silu(c_ref[...])
    o_ref[0] = _dot(s, w_ref[0]) + b_ref[0]


def _modulation(cvec, ada_w, ada_b):
    tn = 2048
    return pl.pallas_call(
        _mod_kernel,
        grid=(DEPTH, 6 * D_MODEL // tn),
        in_specs=[pl.BlockSpec((8, D_MODEL), lambda l, j: (0, 0)),
                  pl.BlockSpec((1, D_MODEL, tn), lambda l, j: (l, 0, j)),
                  pl.BlockSpec((1, 1, tn), lambda l, j: (l, 0, j))],
        out_specs=pl.BlockSpec((1, 8, tn), lambda l, j: (l, 0, j)),
        out_shape=jax.ShapeDtypeStruct((DEPTH, 8, 6 * D_MODEL), F32),
        compiler_params=_params(("arbitrary", "arbitrary")),
        name="adaln_modulation",
    )(cvec, ada_w, ada_b.reshape(DEPTH, 1, 6 * D_MODEL))


LOWRANK_COL = 2048
TAIL_COL = LOWRANK_COL + 2 * GLA_RANK
TAIL_WIDTH = 1280
WCAT_LANES = 256


def _wcat_kernel(w_ref, wa2t_ref, o_ref):
    o_ref[0, 0:LOWRANK_COL, :] = w_ref[0, 0:LOWRANK_COL, :].astype(BF16)
    z = jnp.dot(wa2t_ref[0], w_ref[0, LOWRANK_COL:TAIL_COL, :], preferred_element_type=F32,
                precision=lax.Precision.HIGHEST)
    o_ref[0, LOWRANK_COL:LOWRANK_COL + 2 * LANES, :] = z.astype(BF16)
    o_ref[0, LOWRANK_COL + 2 * LANES:PROJ_COLS, :] = w_ref[0, TAIL_COL:TAIL_COL + TAIL_WIDTH, :].astype(BF16)


def _build_proj_weight(w_in, wa2):
    ncol = w_in.shape[2]
    assert ncol == TAIL_COL + TAIL_WIDTH and LOWRANK_COL + 2 * LANES + TAIL_WIDTH == PROJ_COLS
    n = N_HEADS * GLA_DK
    wa2t = jnp.zeros((DEPTH, 2 * n, 2 * GLA_RANK), F32)
    wa2t = wa2t.at[:, 0:n, 0:GLA_RANK].set(jnp.swapaxes(wa2[:, 0], 1, 2))
    wa2t = wa2t.at[:, n:2 * n, GLA_RANK:2 * GLA_RANK].set(jnp.swapaxes(wa2[:, 1], 1, 2))
    w_t = jnp.swapaxes(w_in, 1, 2)
    return pl.pallas_call(
        _wcat_kernel,
        grid=(DEPTH, D_MODEL // WCAT_LANES),
        in_specs=[pl.BlockSpec((1, ncol, WCAT_LANES), lambda l, i: (l, 0, i)),
                  pl.BlockSpec((1, 2 * n, 2 * GLA_RANK), lambda l, i: (l, 0, 0))],
        out_specs=pl.BlockSpec((1, PROJ_COLS, WCAT_LANES), lambda l, i: (l, 0, i)),
        out_shape=jax.ShapeDtypeStruct((DEPTH, PROJ_COLS, D_MODEL), BF16),
        compiler_params=_params(("arbitrary", "arbitrary")),
        name="build_proj_weight",
    )(w_t, wa2t)


HGRN_GATE_COL = 2 * W_GRP
GLA_GATE_COL = LOWRANK_COL
assert HGRN_GATE_COL % PROJ_TN == 0 and GLA_GATE_COL % PROJ_TN == 0 and PROJ_TN == 2 * W_GRP


def _hgrn_lower_bound(logits, layer):
    mx = logits[0]
    for i in range(1, DEPTH):
        mx = jnp.maximum(mx, logits[i])
    ex = [jnp.exp(logits[i] - mx) for i in range(DEPTH)]
    den = ex[0]
    for i in range(1, DEPTH):
        den = den + ex[i]
    sm = [e / den for e in ex]
    csum = sm[0]
    for i in range(1, layer + 1):
        csum = csum + sm[i]
    return csum - sm[0]


def _proj_kernel(x_ref, mod_ref, g_ref, w_ref, lbl_ref, ba_ref, o_ref, *, layer):
    shift = mod_ref[0, 0:1, :]
    scale = mod_ref[0, 1:2, :]
    h = (_rms(x_ref[...]) * g_ref[0] * (1.0 + scale) + shift).astype(BF16)
    lb = _hgrn_lower_bound(lbl_ref[...], layer)
    lb_row = jnp.concatenate([lb[0:1, :], lb[1:2, :]], axis=1)
    ba_row = jnp.concatenate([ba_ref[0, 0:1, :], ba_ref[0, 1:2, :]], axis=1)
    for n in range(PROJ_COLS // PROJ_TN):
        cols = slice(n * PROJ_TN, (n + 1) * PROJ_TN)
        acc = lax.dot_general(h, w_ref[0, cols, :], _NT, preferred_element_type=F32)
        if n * PROJ_TN == HGRN_GATE_COL:
            acc = lb_row + (1.0 - lb_row) * _sigmoid(acc)
        if n * PROJ_TN == GLA_GATE_COL:
            z = acc[:, 0:W_GRP] + ba_row
            acc = jnp.concatenate([-_softplus(-z) * (1.0 / GLA_NORMALIZER), acc[:, W_GRP:]], axis=1)
        o_ref[:, cols] = acc


def _mod_index(ntok, nmod, tile):
    tiles_per_mod = (ntok // tile) // nmod if nmod > 1 else 1
    if nmod > 1:
        return lambda i, *_: (i // tiles_per_mod, 0, 0)
    return lambda i, *_: (0, 0, 0)


def _project(x, mod, norm_g, w_cat, lb_logits, ba2, layer):
    ntok = x.shape[0]
    return pl.pallas_call(
        functools.partial(_proj_kernel, layer=layer),
        grid=(ntok // PROJ_TM,),
        in_specs=[pl.BlockSpec((PROJ_TM, D_MODEL), lambda i: (i, 0)),
                  pl.BlockSpec((1, 6, D_MODEL), _mod_index(ntok, mod.shape[0], PROJ_TM)),
                  pl.BlockSpec((1, 1, D_MODEL), lambda i: (layer, 0, 0)),
                  pl.BlockSpec((1, PROJ_COLS, D_MODEL), lambda i: (layer, 0, 0),
                               pipeline_mode=pl.Buffered(1)),
                  _full_spec(lb_logits),
                  pl.BlockSpec((1, 2, N_HEADS * GLA_DK), lambda i: (layer, 0, 0))],
        out_specs=pl.BlockSpec((PROJ_TM, PROJ_COLS), lambda i: (i, 0)),
        out_shape=jax.ShapeDtypeStruct((ntok, PROJ_COLS), F32),
        compiler_params=_params(("arbitrary",)),
        name="norm_in_proj",
    )(x, mod, norm_g.reshape(DEPTH, 1, D_MODEL), w_cat, lb_logits, ba2)


def _mlp_kernel(x_ref, oa_ref, ob_ref, oc_ref, od_ref, mod_ref, g2_ref, fg_ref, wout_ref,
                w1_ref, w2_ref, out_ref, h2_scr, *, final_norm):
    j = pl.program_id(1)

    @pl.when(j == 0)
    def _():
        mixed = jnp.concatenate([oa_ref[...], ob_ref[...], oc_ref[...], od_ref[...]], axis=1)
        mix = jnp.dot(mixed, wout_ref[0].astype(BF16), preferred_element_type=F32)
        x1 = x_ref[...] + mod_ref[0, 2:3, :] * mix
        out_ref[...] = x1
        h2 = _rms(x1) * g2_ref[0] * (1.0 + mod_ref[0, 4:5, :]) + mod_ref[0, 3:4, :]
        h2_scr[...] = h2.astype(BF16)

    w1 = w1_ref[0].astype(BF16)
    w2 = w2_ref[0].astype(BF16)
    for r in range(MLP_ROW_SPLIT):
        rows = slice(r * (TOK_TILE // MLP_ROW_SPLIT), (r + 1) * (TOK_TILE // MLP_ROW_SPLIT))
        t = jnp.dot(h2_scr[rows, :], w1, preferred_element_type=F32)
        t = jnp.square(jnp.maximum(t, 0.0)).astype(BF16)
        out_ref[rows, :] += mod_ref[0, 5:6, :] * jnp.dot(t, w2, preferred_element_type=F32)

    if final_norm:
        @pl.when(j == pl.num_programs(1) - 1)
        def _():
            out_ref[...] = _rms(out_ref[...]) * fg_ref[...]


def _out_mlp(x, o_a, o_b, o_c, o_d, mod, norm2_g, final_g, w_out, w1, w2, layer, final_norm):
    ntok = x.shape[0]
    tok = lambda i, j: (i, 0)
    return pl.pallas_call(
        functools.partial(_mlp_kernel, final_norm=final_norm),
        grid=(ntok // TOK_TILE, D_FF // FF_TILE),
        in_specs=[pl.BlockSpec((TOK_TILE, D_MODEL), tok),
                  pl.BlockSpec((TOK_TILE, W_GRP), tok),
                  pl.BlockSpec((TOK_TILE, W_GRP), tok),
                  pl.BlockSpec((TOK_TILE, W_GRP), tok),
                  pl.BlockSpec((TOK_TILE, W_GRP), tok),
                  pl.BlockSpec((1, 6, D_MODEL), _mod_index(ntok, mod.shape[0], TOK_TILE)),
                  pl.BlockSpec((1, 1, D_MODEL), lambda i, j: (layer, 0, 0)),
                  pl.BlockSpec((1, D_MODEL), lambda i, j: (0, 0)),
                  pl.BlockSpec((1, D_MODEL, D_MODEL), lambda i, j: (layer, 0, 0)),
                  pl.BlockSpec((1, D_MODEL, FF_TILE), lambda i, j: (layer, 0, j)),
                  pl.BlockSpec((1, FF_TILE, D_MODEL), lambda i, j: (layer, j, 0))],
        out_specs=pl.BlockSpec((TOK_TILE, D_MODEL), tok),
        out_shape=jax.ShapeDtypeStruct((ntok, D_MODEL), F32),
        scratch_shapes=[pltpu.VMEM((TOK_TILE, D_MODEL), BF16)],
        compiler_params=_params(("arbitrary", "arbitrary")),
        name="out_proj_mlp",
    )(x, o_a, o_b, o_c, o_d, mod, norm2_g.reshape(DEPTH, 1, D_MODEL), final_g.reshape(1, D_MODEL),
      w_out, w1, w2)


def _split3(x):
    hi = x.astype(BF16)
    r1 = x - hi.astype(F32)
    mid = r1.astype(BF16)
    lo = (r1 - mid.astype(F32)).astype(BF16)
    return hi, mid, lo


def _dot_exact_rhs(x, m):
    hi, mid, lo = _split3(x)
    s = jnp.dot(hi, m, preferred_element_type=F32)
    s += jnp.dot(mid, m, preferred_element_type=F32)
    s += jnp.dot(lo, m, preferred_element_type=F32)
    return s


def _block_cumsum(tri, g):
    hi = g.astype(BF16)
    lo = (g - hi.astype(F32)).astype(BF16)
    return jnp.dot(tri, hi, preferred_element_type=F32) + jnp.dot(tri, lo, preferred_element_type=F32)


def _run_prepass(pre, nblock, width):
    rmax = jnp.zeros((1, width), F32)
    if nblock % 2:
        for bi in range(nblock):
            rmax = pre(bi, rmax)
        return rmax
    if nblock == 2:
        return pre(1, pre(0, rmax))
    return lax.fori_loop(0, nblock // 2, lambda i, rm: pre(2 * i + 1, pre(2 * i, rm)), rmax)


def _half_ranges(cum, reverse):
    if reverse:
        second = -cum[HALF:HALF + 1, :]
        first = -(cum[0:1, :] - cum[HALF:HALF + 1, :])
    else:
        first = -cum[HALF - 1:HALF, :]
        second = -(cum[CHUNK - 1:CHUNK, :] - cum[HALF - 1:HALF, :])
    return jnp.maximum(first, second)


def _block_ranges(cum, reverse, rm):
    for c in range(GROUP):
        rm = jnp.maximum(rm, _half_ranges(cum[c * CHUNK:(c + 1) * CHUNK, :], reverse))
    return rm


def _row_start(c):
    r0 = c * CHUNK
    return r0 if isinstance(r0, int) else pl.multiple_of(r0, CHUNK)


class _Rec:
    def __init__(self, q_at, k_at, v_ref, cum_scr, st_scr, oacc, consts, width, seq_len, nsub):
        self.q_at, self.k_at, self.v_ref = q_at, k_at, v_ref
        self.cum_scr, self.st_scr, self.oacc = cum_scr, st_scr, oacc
        (self.tri_ref, self.bdw_b, self.bdw_f, self.bdv_b, self.cmask, self.bmat,
         self.tile_t) = consts
        self.width, self.seq_len, self.nsub = width, seq_len, nsub


def _chunk_step(rec, sub, d, c, fast):
    reverse = d == 1
    mid_row = HALF if reverse else HALF - 1
    last_row = 0 if reverse else CHUNK - 1
    sd = sub * 2 + d
    r0 = sub * rec.seq_len + c * CHUNK
    if not isinstance(r0, int):
        r0 = pl.multiple_of(r0, CHUNK)
    rows = pl.ds(r0, CHUNK)
    q = rec.q_at(r0)
    k = rec.k_at(d, r0)
    v_b = rec.v_ref[rows, :].astype(BF16)
    cum = rec.cum_scr[d, rows, :]
    tot = cum[last_row:last_row + 1, :]
    st = rec.st_scr[sd]
    o = _dot_nt(q * jnp.exp(cum), st)
    if fast:
        cm = cum[mid_row:mid_row + 1, :]
        qm = q * jnp.exp(cum - cm)
        km_b = (k * jnp.exp(cm - cum)).astype(BF16)
        kbd = jnp.concatenate([km_b] * N_HEADS, axis=0) * rec.bdw_b[...]
        sc = lax.dot_general(qm.astype(BF16), kbd, _NT, preferred_element_type=F32)
        a = jnp.where(rec.cmask[d] > 0.5, sc, 0.0)
        vexp = jnp.concatenate([v_b] * N_HEADS, axis=0) * rec.bdv_b[...]
        o = o + jnp.dot(a.astype(BF16), vexp, preferred_element_type=F32)
    else:
        row_id = lax.broadcasted_iota(jnp.int32, (CHUNK, rec.width), 0)

        def key_row(j, acc):
            krow = rec.k_at(d, r0 + j, 1)
            crow = rec.cum_scr[d, pl.ds(r0 + j, 1), :]
            vrow = rec.v_ref[pl.ds(r0 + j, 1), :]
            keep = (row_id <= j) if reverse else (row_id >= j)
            e = jnp.exp(jnp.minimum(cum - crow, 0.0))
            p = jnp.where(keep, q * krow * e, 0.0)
            return acc + _dot(p, rec.bmat[...]) * vrow

        o = lax.fori_loop(0, CHUNK, key_row, o)
    ke_b = (k * jnp.exp(tot - cum)).astype(BF16)
    ds = lax.dot_general(v_b, ke_b, _TN, preferred_element_type=F32)
    rec.st_scr[sd] = st * jnp.exp(tot) + ds * rec.bdw_f[...]
    rec.oacc[d, rows, :] = o


def _load_state(rec, s0_ref, sub, d):
    x = jnp.concatenate([s0_ref[sub, 0, d, h] for h in range(N_HEADS)], axis=0)
    y = _dot_exact_rhs(x, rec.tile_t[...]) * rec.bmat[...].astype(F32)
    rec.st_scr[sub * 2 + d] = y.T


def _store_state(rec, sfin_ref, sub, d):
    y = rec.st_scr[sub * 2 + d].T
    half = y[:, 0:LANES] + y[:, LANES:2 * LANES]
    x = (half + pltpu.roll(half, HEAD_V, 1))[:, 0:HEAD_V]
    dk = rec.width // N_HEADS
    for h in range(N_HEADS):
        sfin_ref[sub, d, h] = x[h * dk:(h + 1) * dk, :]


def _run_recurrence(rec, ok, s0_ref, sfin_ref):
    nsub = rec.nsub
    nchunk = rec.seq_len // CHUNK
    gsize = min(MAIN_GROUP, nchunk)
    ngroup = nchunk // gsize
    spp = max(1, min(nsub, MAIN_GROUP // gsize))
    nsp = nsub // spp
    for sub in range(nsub):
        for d in (0, 1):
            if s0_ref is None:
                rec.st_scr[sub * 2 + d] = jnp.zeros(rec.st_scr.shape[1:], F32)
            else:
                _load_state(rec, s0_ref, sub, d)

    def fast_all():
        def step(i, carry):
            sp = i // ngroup if (nsp > 1 and ngroup > 1) else (i if nsp > 1 else 0)
            gi = i % ngroup if (nsp > 1 and ngroup > 1) else (i if ngroup > 1 else 0)
            for u in range(gsize):
                s = gi * gsize + u
                for j in range(spp):
                    sub = sp * spp + j
                    _chunk_step(rec, sub, 0, s, True)
                    _chunk_step(rec, sub, 1, nchunk - 1 - s, True)
            return carry

        if nsp * ngroup == 1:
            step(0, 0)
        else:
            lax.fori_loop(0, nsp * ngroup, step, 0)

    def direct_all():
        def one(i, carry):
            sub = i // nchunk if nsub > 1 else 0
            s = i % nchunk if nsub > 1 else i
            _chunk_step(rec, sub, 0, s, False)
            _chunk_step(rec, sub, 1, nchunk - 1 - s, False)
            return carry

        lax.fori_loop(0, nsub * nchunk, one, 0)

    lax.cond(ok, fast_all, direct_all)
    if sfin_ref is not None:
        for sub in range(nsub):
            for d in (0, 1):
                _store_state(rec, sfin_ref, sub, d)


def _head_norm_gate(rec, gate_ref, gain_ref, o_ref):
    for rb in range(rec.nsub * rec.seq_len // ROW_BLOCK):
        rows = slice(rb * ROW_BLOCK, (rb + 1) * ROW_BLOCK)
        o = rec.oacc[0, rows, :] + rec.oacc[1, rows, :]
        ms = jnp.dot((o * o).astype(BF16), rec.bdv_b[...], preferred_element_type=F32) * (1.0 / HEAD_V)
        y = o * lax.rsqrt(ms + EPS) * gain_ref[0] * _silu(gate_ref[rows, :])
        o_ref[rows, :] = y.astype(o_ref.dtype)


def _split_refs(refs, n_in, has_s0, want_final):
    ins = refs[:n_in]
    pos = n_in
    s0_ref = None
    if has_s0:
        s0_ref = refs[pos]
        pos += 1
    o_ref = refs[pos]
    pos += 1
    sfin_ref = None
    if want_final:
        sfin_ref = refs[pos]
        pos += 1
    return ins, s0_ref, o_ref, sfin_ref, refs[pos:]


def _hgrn_kernel(*refs, seq_len, nsub, has_s0, want_final):
    ins, s0_ref, o_ref, sfin_ref, scr = _split_refs(refs, 13, has_s0, want_final)
    q_ref, i_ref, ff_ref, fb_ref, g_ref, gain_ref = ins[:6]
    k_scr, cum_scr, oacc, st_scr = scr
    q_at = lambda r0: q_ref[pl.ds(r0, CHUNK), :]
    k_at = lambda d, r0, n=CHUNK: k_scr[d, pl.ds(r0, n), :]
    rec = _Rec(q_at, k_at, i_ref, cum_scr, st_scr, oacc, ins[6:], W_GRP, seq_len, nsub)

    def pre(bi, rm):
        r0 = bi * GROWS if isinstance(bi, int) else pl.multiple_of(bi * GROWS, GROWS)
        rows = pl.ds(r0, GROWS)
        for d, f_ref in ((0, ff_ref), (1, fb_ref)):
            f = f_ref[rows, :]
            g = jnp.log(jnp.maximum(f, F_FLOOR))
            k_scr[d, rows, :] = 1.0 - f
            cum = _block_cumsum(rec.tri_ref[d], g)
            cum_scr[d, rows, :] = cum
            rm = _block_ranges(cum, d == 1, rm)
        return rm

    rmax = _run_prepass(pre, nsub * seq_len // GROWS, W_GRP)
    ok = jnp.max(rmax) < SAFE_RANGE

    _run_recurrence(rec, ok, s0_ref, sfin_ref)
    _head_norm_gate(rec, g_ref, gain_ref, o_ref)


def _gla_kernel(*refs, seq_len, nsub, has_s0, want_final):
    ins, s0_ref, o_ref, sfin_ref, scr = _split_refs(refs, 14, has_s0, want_final)
    q_ref, k_ref, v_ref, g_ref, gf_ref, gb_ref, gain_ref = ins[:7]
    cum_scr, oacc, st_scr = scr
    width = N_HEADS * GLA_DK
    q_at = lambda r0: q_ref[pl.ds(r0, CHUNK), :] * (GLA_DK ** -0.5)
    k_at = lambda d, r0, n=CHUNK: k_ref[pl.ds(r0, n), :]
    rec = _Rec(q_at, k_at, v_ref, cum_scr, st_scr, oacc, ins[7:], width, seq_len, nsub)

    def pre(bi, rm):
        r0 = bi * GROWS if isinstance(bi, int) else pl.multiple_of(bi * GROWS, GROWS)
        rows = pl.ds(r0, GROWS)
        for d, logdecay_ref in ((0, gf_ref), (1, gb_ref)):
            g = logdecay_ref[rows, :]
            cum = _block_cumsum(rec.tri_ref[d], g)
            cum_scr[d, rows, :] = cum
            rm = _block_ranges(cum, d == 1, rm)
        return rm

    rmax = _run_prepass(pre, nsub * seq_len // GROWS, width)
    ok = jnp.max(rmax) < SAFE_RANGE

    _run_recurrence(rec, ok, s0_ref, sfin_ref)
    _head_norm_gate(rec, g_ref, gain_ref, o_ref)


def _recurrence_constants(width):
    dk = width // N_HEADS
    r = np.arange(N_HEADS * CHUNK)[:, None]
    tri = np.kron(np.eye(GROUP, dtype=np.float32), np.tril(np.ones((CHUNK, CHUNK), np.float32)))
    tri = np.stack([tri, tri.T])
    bdw = (r // CHUNK == np.arange(width)[None, :] // dk).astype(np.float32)
    bdv = (r // CHUNK == np.arange(W_GRP)[None, :] // HEAD_V).astype(np.float32)
    t = np.arange(CHUNK)[:, None]
    s = np.arange(N_HEADS * CHUNK)[None, :] % CHUNK
    cmask = np.stack([(s <= t), (s >= t)]).astype(np.float32)
    bmat = (np.arange(width)[:, None] // dk == np.arange(W_GRP)[None, :] // HEAD_V).astype(np.float32)
    tile_t = np.tile(np.eye(HEAD_V, dtype=np.float32), (1, N_HEADS))
    return (jnp.asarray(tri, BF16), jnp.asarray(bdw, BF16), jnp.asarray(bdw), jnp.asarray(bdv, BF16),
            jnp.asarray(cmask), jnp.asarray(bmat, BF16), jnp.asarray(tile_t, BF16))


def _recurrent_mixer(kernel_fn, name, width, args, in_specs, state, layer, nseq, seq_len, want_final,
                     extra_scratch):
    dk = width // N_HEADS
    nsub = _seqs_per_block(nseq, seq_len)
    rows = nsub * seq_len
    consts = _recurrence_constants(width)
    args = list(args) + list(consts)
    in_specs = list(in_specs) + [_full_spec(c) for c in consts]
    if state is not None:
        args.append(state)
        in_specs.append(pl.BlockSpec((nsub, 1, 2, N_HEADS, dk, HEAD_V), lambda b: (b, layer, 0, 0, 0, 0)))
    out_shape = [jax.ShapeDtypeStruct((nseq * seq_len, W_GRP), BF16)]
    out_specs = [pl.BlockSpec((rows, W_GRP), lambda b: (b, 0))]
    if want_final:
        out_shape.append(jax.ShapeDtypeStruct((nseq, 2, N_HEADS, dk, HEAD_V), F32))
        out_specs.append(pl.BlockSpec((nsub, 2, N_HEADS, dk, HEAD_V), lambda b: (b, 0, 0, 0, 0)))
    res = pl.pallas_call(
        functools.partial(kernel_fn, seq_len=seq_len, nsub=nsub, has_s0=state is not None,
                          want_final=want_final),
        grid=(nseq // nsub,),
        in_specs=in_specs,
        out_specs=out_specs,
        out_shape=out_shape,
        scratch_shapes=[pltpu.VMEM((2, rows, W_GRP), F32)] * extra_scratch + [
            pltpu.VMEM((2, rows, width), F32), pltpu.VMEM((2, rows, W_GRP), F32),
            pltpu.VMEM((2 * nsub, W_GRP, width), F32)],
        compiler_params=_params(("arbitrary",)),
        name=name,
    )(*args)
    return res[0], (res[1] if want_final else None)


def _seqs_per_block(nseq, seq_len):
    return max(1, min(nseq, BLOCK_ROWS // seq_len))


def _hgrn_mixer(proj, gain, state, layer, nseq, seq_len, want_final):
    rows = _seqs_per_block(nseq, seq_len) * seq_len
    col = lambda c: pl.BlockSpec((rows, W_GRP), lambda b: (b, c))
    args = [proj] * 5 + [gain.reshape(DEPTH, 1, W_GRP)]
    in_specs = [col(0), col(1), col(2), col(3), col(4),
                pl.BlockSpec((1, 1, W_GRP), lambda b: (layer, 0, 0))]
    return _recurrent_mixer(_hgrn_kernel, "hgrn2_mixer", W_GRP, args, in_specs, state, layer, nseq,
                            seq_len, want_final, 1)


def _gla_mixer(proj, gain, state, layer, nseq, seq_len, want_final):
    width = N_HEADS * GLA_DK
    rows = _seqs_per_block(nseq, seq_len) * seq_len
    col256 = lambda c: pl.BlockSpec((rows, W_GRP), lambda b: (b, c))
    col128 = lambda c: pl.BlockSpec((rows, width), lambda b: (b, c))
    args = [proj] * 6 + [gain.reshape(DEPTH, 1, W_GRP)]
    in_specs = [col128(10), col128(11), col256(6), col256(7), col128(16), col128(17),
                pl.BlockSpec((1, 1, W_GRP), lambda b: (layer, 0, 0))]
    return _recurrent_mixer(_gla_kernel, "gla_mixer", width, args, in_specs, state, layer, nseq, seq_len,
                            want_final, 0)


PAD = 8
SCAN_TILE = 8
SCAN_UNROLL = 4


def _rgsc_kernel(*refs, seq_len, nsub, seg, has_h0, want_final):
    (cx_ref, cg_ref, db_ref, dc_ref, dv_ref, convw_ref, convb_ref, wr_ref, br_ref, wi_ref,
     bi_ref, lam_ref, sw_ref) = refs[:13]
    pos = 13
    h0_ref = None
    if has_h0:
        h0_ref = refs[pos]
        pos += 1
    oc_ref, od_ref = refs[pos:pos + 2]
    pos += 2
    hfin_ref = None
    if want_final:
        hfin_ref = refs[pos]
        pos += 1
    upad, a_scr, b_scr, h_scr = refs[pos:]

    slot = seq_len + 2 * PAD
    zpad = jnp.zeros((PAD, W_GRP), F32)
    for sub in range(nsub):
        upad[sub * slot:sub * slot + PAD, :] = zpad
        upad[sub * slot + PAD + seq_len:(sub + 1) * slot, :] = zpad
        upad[sub * slot + PAD:sub * slot + PAD + seq_len, :] = cx_ref[sub * seq_len:(sub + 1) * seq_len, :]
    rows8 = lax.broadcasted_iota(jnp.int32, (SCAN_TILE, W_GRP), 0)
    ntile = seq_len // SCAN_TILE

    for d in (0, 1):
        reverse = d == 1
        w = convw_ref[0, d]
        offs = [PAD + 3, PAD + 2, PAD + 1, PAD] if reverse else [PAD - 3, PAD - 2, PAD - 1, PAD]
        sp = _softplus(-lam_ref[0, d:d + 1, :])
        for sub in range(nsub):
            for rb in range(seq_len // ROW_BLOCK):
                src = sub * slot + rb * ROW_BLOCK
                base = sub * seq_len + rb * ROW_BLOCK
                xc = convb_ref[0, d:d + 1, :]
                for j in range(4):
                    xc = xc + w[j:j + 1, :] * upad[offs[j] + src:offs[j] + src + ROW_BLOCK, :]
                r = _sigmoid(_dot(xc, wr_ref[d]) + br_ref[0, d:d + 1, :])
                ig = _sigmoid(_dot(xc, wi_ref[d]) + bi_ref[0, d:d + 1, :])
                log_a = -RG_C * r * sp
                a = jnp.exp(log_a)
                a_scr[base:base + ROW_BLOCK, :] = a
                y = 1.0 - a * a
                root = jnp.where(y > 0.0, y * lax.rsqrt(y), 0.0)
                b_scr[base:base + ROW_BLOCK, :] = root * (ig * xc)

        if has_h0:
            h0 = tuple(h0_ref[sub, 0, d:d + 1, :] for sub in range(nsub))
        else:
            h0 = tuple(jnp.zeros((1, W_GRP), F32) for _ in range(nsub))

        def step(ti, carry, d=d, reverse=reverse):
            t = (ntile - 1 - ti) if reverse else ti
            out = []
            for sub in range(nsub):
                r0 = pl.multiple_of(sub * seq_len + t * SCAN_TILE, SCAN_TILE)
                a = a_scr[pl.ds(r0, SCAN_TILE), :]
                b = b_scr[pl.ds(r0, SCAN_TILE), :]
                for s in (1, 2, 4):
                    if reverse:
                        keep = rows8 <= SCAN_TILE - 1 - s
                        shift = SCAN_TILE - s
                    else:
                        keep = rows8 >= s
                        shift = s
                    a_s = jnp.where(keep, pltpu.roll(a, shift, 0), 1.0)
                    b_s = jnp.where(keep, pltpu.roll(b, shift, 0), 0.0)
                    b = b + a * b_s
                    a = a * a_s
                h = a * carry[sub] + b
                if d == 0:
                    h_scr[pl.ds(r0, SCAN_TILE), :] = h
                else:
                    h_scr[pl.ds(r0, SCAN_TILE), :] += h
                out.append(h[0:1, :] if reverse else h[SCAN_TILE - 1:SCAN_TILE, :])
            return tuple(out)

        hlast = lax.fori_loop(0, ntile, step, h0, unroll=SCAN_UNROLL // nsub if nsub < SCAN_UNROLL else 1)
        if want_final:
            for sub in range(nsub):
                hfin_ref[sub, d:d + 1, :] = hlast[sub]

    sw = sw_ref[0]
    nblock = nsub * seq_len // ROW_BLOCK
    for rb in range(nblock):
        rows = slice(rb * ROW_BLOCK, (rb + 1) * ROW_BLOCK)
        x = cg_ref[rows, :]
        gelu = 0.5 * x * (1.0 + jnp.tanh(0.7978845608028654 * (x + 0.044715 * (x * x * x))))
        oc_ref[rows, :] = (h_scr[rows, :] * gelu).astype(oc_ref.dtype)
        upad[PAD + rb * ROW_BLOCK:PAD + (rb + 1) * ROW_BLOCK, :] = dc_ref[rows, :] * dv_ref[rows, :]
    for rb in range(nblock):
        base = rb * ROW_BLOCK
        rows = slice(base, base + ROW_BLOCK)
        posn = (lax.broadcasted_iota(jnp.int32, (ROW_BLOCK, W_GRP), 0) + base) % seg
        left = jnp.where(posn != 0, upad[PAD - 1 + base:PAD - 1 + base + ROW_BLOCK, :], 0.0)
        right = jnp.where(posn != seg - 1, upad[PAD + 1 + base:PAD + 1 + base + ROW_BLOCK, :], 0.0)
        y = sw[0:1, :] * left + sw[1:2, :] * upad[PAD + base:PAD + base + ROW_BLOCK, :] + sw[2:3, :] * right
        od_ref[rows, :] = (db_ref[rows, :] * y).astype(od_ref.dtype)


def _rgsc_mixer(proj, conv_w, conv_b, wr_bd, b_r, wi_bd, b_i, lam, sconv_w, h0, layer, nseq, seq_len,
                seg, want_final):
    nsub = _seqs_per_block(nseq, seq_len)
    rows = nsub * seq_len
    col = lambda c: pl.BlockSpec((rows, W_GRP), lambda b: (b, c))
    lay3 = lambda a: pl.BlockSpec((1,) + a.shape[1:], lambda b: (layer, 0, 0))
    lay4 = lambda a: pl.BlockSpec((1,) + a.shape[1:], lambda b: (layer, 0, 0, 0))
    params = [conv_w, conv_b, wr_bd, b_r, wi_bd, b_i, lam, sconv_w]
    args = [proj] * 5 + params
    in_specs = [col(9), col(10), col(11), col(12), col(13), lay4(conv_w), lay3(conv_b), _full_spec(wr_bd),
                lay3(b_r), _full_spec(wi_bd), lay3(b_i), lay3(lam), lay3(sconv_w)]
    if h0 is not None:
        args.append(h0)
        in_specs.append(pl.BlockSpec((nsub, 1, 2, W_GRP), lambda b: (b, layer, 0, 0)))
    out_shape = [jax.ShapeDtypeStruct((nseq * seq_len, W_GRP), BF16)] * 2
    out_specs = [pl.BlockSpec((rows, W_GRP), lambda b: (b, 0))] * 2
    if want_final:
        out_shape.append(jax.ShapeDtypeStruct((nseq, 2, W_GRP), F32))
        out_specs.append(pl.BlockSpec((nsub, 2, W_GRP), lambda b: (b, 0, 0)))
    res = pl.pallas_call(
        functools.partial(_rgsc_kernel, seq_len=seq_len, nsub=nsub, seg=seg, has_h0=h0 is not None,
                          want_final=want_final),
        grid=(nseq // nsub,),
        in_specs=in_specs,
        out_specs=out_specs,
        out_shape=out_shape,
        scratch_shapes=[pltpu.VMEM((nsub * (seq_len + 2 * PAD), W_GRP), F32), pltpu.VMEM((rows, W_GRP), F32),
                        pltpu.VMEM((rows, W_GRP), F32), pltpu.VMEM((rows, W_GRP), F32)],
        compiler_params=_params(("arbitrary",)),
        name="rglru_sconv_mixer",
    )(*args)
    return res[0], res[1], (res[2] if want_final else None)


def _block_diag(w):
    two, h, c, _ = w.shape
    eye = jnp.eye(h, dtype=w.dtype)
    full = w[:, :, :, None, :] * eye[None, :, None, :, None]
    return full.reshape(two, h * c, h * c)


def kernel(x_prompt, x_sample, state_hgrn, state_gla, state_rglru, c, c_ctx, norm1_g, norm2_g, ada_w, ada_b, w_in, w_out, hgrn_lb_logits, hgrn_norm_g, gla_wa2, gla_ba2, gla_norm_g, rg_conv_w, rg_conv_b, rg_w_r, rg_b_r, rg_w_i, rg_b_i, rg_lambda, sconv_w, mlp_w1, mlp_w2, final_norm_g):
    b_ctx, l_ctx, _ = x_prompt.shape
    b_lat, l_lat, _ = x_sample.shape

    cvec = jnp.concatenate([c, c_ctx[None, :], jnp.zeros((8 - b_lat - 1, D_MODEL), F32)], axis=0)
    mod = _modulation(cvec, ada_w, ada_b).reshape(DEPTH, 8, 6, D_MODEL)

    w_cat = _build_proj_weight(w_in, gla_wa2)

    xp = x_prompt.reshape(b_ctx * l_ctx, D_MODEL)
    xs = x_sample.reshape(b_lat * l_lat, D_MODEL)
    new_h, new_g, new_r = [], [], []
    for l in range(DEPTH):
        wr_bd = _block_diag(rg_w_r[l]).astype(BF16)
        wi_bd = _block_diag(rg_w_i[l]).astype(BF16)
        mod_ctx = mod[l, b_lat:b_lat + 1]
        mod_lat = mod[l, :b_lat]
        last = l == DEPTH - 1
        streams = (
            (xp, mod_ctx, b_ctx, l_ctx, l_ctx, None, None, None, True),
            (xs, mod_lat, b_lat, l_lat, GRID_W, state_hgrn, state_gla, state_rglru, False),
        )
        outs = []
        for (x, m, nseq, seq_len, seg, s_h, s_g, s_r, want_final) in streams:
            proj = _project(x, m, norm1_g, w_cat, hgrn_lb_logits, gla_ba2, l)
            o_a, f_h = _hgrn_mixer(proj, hgrn_norm_g, s_h, l, nseq, seq_len, want_final)
            o_b, f_g = _gla_mixer(proj, gla_norm_g, s_g, l, nseq, seq_len, want_final)
            o_c, o_d, f_r = _rgsc_mixer(proj, rg_conv_w, rg_conv_b, wr_bd, rg_b_r, wi_bd, rg_b_i,
                                        rg_lambda, sconv_w, s_r, l, nseq, seq_len, seg, want_final)
            x_new = _out_mlp(x, o_a, o_b, o_c, o_d, m, norm2_g, final_norm_g, w_out, mlp_w1, mlp_w2, l, last)
            outs.append((x_new, f_h, f_g, f_r))
        xp, f_h, f_g, f_r = outs[0]
        xs = outs[1][0]
        new_h.append(f_h)
        new_g.append(f_g)
        new_r.append(f_r)
    y_prompt = xp.reshape(b_ctx, l_ctx, D_MODEL)
    y_sample = xs.reshape(b_lat, l_lat, D_MODEL)
    return (y_prompt, y_sample, jnp.stack(new_h, axis=1), jnp.stack(new_g, axis=1),
            jnp.stack(new_r, axis=1))
```

```python
import functools

import numpy as np
import jax
import jax.numpy as jnp
from jax import lax
from jax.experimental import pallas as pl
from jax.experimental.pallas import tpu as pltpu

F32 = jnp.float32
BF16 = jnp.bfloat16

D_MODEL = 1024
DEPTH = 2
GRID_W = 64
N_HEADS = 4
W_GRP = 256
HEAD_V = 64
HGRN_DK = 64
GLA_DK = 32
GLA_RANK = 16
GLA_NORMALIZER = 16.0
RG_C = 8.0
D_FF = 4 * D_MODEL
EPS = 1e-6
F_FLOOR = 1e-20

PROJ_COLS = 3584

CHUNK = 64
HALF = CHUNK // 2
assert CHUNK == HEAD_V
GROUP = 4
GROWS = GROUP * CHUNK
MAIN_GROUP = 16
BLOCK_ROWS = 1024
LANES = 128
SAFE_RANGE = 80.0
TOK_TILE = 1024
PROJ_TM = 512
PROJ_TN = 512
FF_TILE = 1024
MLP_ROW_SPLIT = 2
ROW_BLOCK = 256
VMEM_LIMIT = 56 * 1024 * 1024

_NT = (((1,), (1,)), ((), ()))
_TN = (((0,), (0,)), ((), ()))


def _dot(a, b):
    return jnp.dot(a.astype(BF16), b.astype(BF16), preferred_element_type=F32)


def _dot_nt(a, b):
    return lax.dot_general(a.astype(BF16), b.astype(BF16), _NT, preferred_element_type=F32)


def _dot_tn(a, b):
    return lax.dot_general(a.astype(BF16), b.astype(BF16), _TN, preferred_element_type=F32)


def _sigmoid(x):
    return 1.0 / (1.0 + jnp.exp(-x))


def _silu(x):
    return x * _sigmoid(x)


def _log1p(y):
    u = 1.0 + y
    return jnp.where(u == 1.0, y, jnp.log(u) * (y / (u - 1.0)))


def _softplus(x):
    return jnp.maximum(x, 0.0) + _log1p(jnp.exp(-jnp.abs(x)))


def _log_sigmoid(z):
    return jnp.minimum(z, 0.0) - jnp.log(1.0 + jnp.exp(-jnp.abs(z)))


def _rms(x):
    return x * lax.rsqrt(jnp.mean(x * x, axis=-1, keepdims=True) + EPS)


def _params(sem):
    return pltpu.CompilerParams(dimension_semantics=sem, vmem_limit_bytes=VMEM_LIMIT)


def _full_spec(a):
    zeros = (0,) * a.ndim
    return pl.BlockSpec(a.shape, lambda *_: zeros)


def _mod_kernel(c_ref, w_ref, b_ref, o_ref):
    s = _silu(c_ref[...])
    o_ref[0] = _dot(s, w_ref[0]) + b_ref[0]


def _modulation(cvec, ada_w, ada_b):
    tn = 2048
    return pl.pallas_call(
        _mod_kernel,
        grid=(DEPTH, 6 * D_MODEL // tn),
        in_specs=[pl.BlockSpec((8, D_MODEL), lambda l, j: (0, 0)),
                  pl.BlockSpec((1, D_MODEL, tn), lambda l, j: (l, 0, j)),
                  pl.BlockSpec((1, 1, tn), lambda l, j: (l, 0, j))],
        out_specs=pl.BlockSpec((1, 8, tn), lambda l, j: (l, 0, j)),
        out_shape=jax.ShapeDtypeStruct((DEPTH, 8, 6 * D_MODEL), F32),
        compiler_params=_params(("arbitrary", "arbitrary")),
        name="adaln_modulation",
    )(cvec, ada_w, ada_b.reshape(DEPTH, 1, 6 * D_MODEL))


LOWRANK_COL = 2048
TAIL_COL = LOWRANK_COL + 2 * GLA_RANK
TAIL_WIDTH = 1280
WCAT_LANES = 256


def _wcat_kernel(w_ref, wa2t_ref, o_ref):
    o_ref[0, 0:LOWRANK_COL, :] = w_ref[0, 0:LOWRANK_COL, :].astype(BF16)
    z = jnp.dot(wa2t_ref[0], w_ref[0, LOWRANK_COL:TAIL_COL, :], preferred_element_type=F32,
                precision=lax.Precision.HIGHEST)
    o_ref[0, LOWRANK_COL:LOWRANK_COL + 2 * LANES, :] = z.astype(BF16)
    o_ref[0, LOWRANK_COL + 2 * LANES:PROJ_COLS, :] = w_ref[0, TAIL_COL:TAIL_COL + TAIL_WIDTH, :].astype(BF16)


def _build_proj_weight(w_in, wa2):
    ncol = w_in.shape[2]
    assert ncol == TAIL_COL + TAIL_WIDTH and LOWRANK_COL + 2 * LANES + TAIL_WIDTH == PROJ_COLS
    n = N_HEADS * GLA_DK
    wa2t = jnp.zeros((DEPTH, 2 * n, 2 * GLA_RANK), F32)
    wa2t = wa2t.at[:, 0:n, 0:GLA_RANK].set(jnp.swapaxes(wa2[:, 0], 1, 2))
    wa2t = wa2t.at[:, n:2 * n, GLA_RANK:2 * GLA_RANK].set(jnp.swapaxes(wa2[:, 1], 1, 2))
    w_t = jnp.swapaxes(w_in, 1, 2)
    return pl.pallas_call(
        _wcat_kernel,
        grid=(DEPTH, D_MODEL // WCAT_LANES),
        in_specs=[pl.BlockSpec((1, ncol, WCAT_LANES), lambda l, i: (l, 0, i)),
                  pl.BlockSpec((1, 2 * n, 2 * GLA_RANK), lambda l, i: (l, 0, 0))],
        out_specs=pl.BlockSpec((1, PROJ_COLS, WCAT_LANES), lambda l, i: (l, 0, i)),
        out_shape=jax.ShapeDtypeStruct((DEPTH, PROJ_COLS, D_MODEL), BF16),
        compiler_params=_params(("arbitrary", "arbitrary")),
        name="build_proj_weight",
    )(w_t, wa2t)


HGRN_GATE_COL = 2 * W_GRP
GLA_GATE_COL = LOWRANK_COL
assert HGRN_GATE_COL % PROJ_TN == 0 and GLA_GATE_COL % PROJ_TN == 0 and PROJ_TN == 2 * W_GRP


def _hgrn_lower_bound(logits, layer):
    mx = logits[0]
    for i in range(1, DEPTH):
        mx = jnp.maximum(mx, logits[i])
    ex = [jnp.exp(logits[i] - mx) for i in range(DEPTH)]
    den = ex[0]
    for i in range(1, DEPTH):
        den = den + ex[i]
    sm = [e / den for e in ex]
    csum = sm[0]
    for i in range(1, layer + 1):
        csum = csum + sm[i]
    return csum - sm[0]


def _proj_kernel(x_ref, mod_ref, g_ref, w_ref, lbl_ref, ba_ref, o_ref, *, layer):
    shift = mod_ref[0, 0:1, :]
    scale = mod_ref[0, 1:2, :]
    h = (_rms(x_ref[...]) * g_ref[0] * (1.0 + scale) + shift).astype(BF16)
    lb = _hgrn_lower_bound(lbl_ref[...], layer)
    lb_row = jnp.concatenate([lb[0:1, :], lb[1:2, :]], axis=1)
    ba_row = jnp.concatenate([ba_ref[0, 0:1, :], ba_ref[0, 1:2, :]], axis=1)
    for n in range(PROJ_COLS // PROJ_TN):
        cols = slice(n * PROJ_TN, (n + 1) * PROJ_TN)
        acc = lax.dot_general(h, w_ref[0, cols, :], _NT, preferred_element_type=F32)
        if n * PROJ_TN == HGRN_GATE_COL:
            acc = lb_row + (1.0 - lb_row) * _sigmoid(acc)
        if n * PROJ_TN == GLA_GATE_COL:
            z = acc[:, 0:W_GRP] + ba_row
            acc = jnp.concatenate([_log_sigmoid(z) * (1.0 / GLA_NORMALIZER), acc[:, W_GRP:]], axis=1)
        o_ref[:, cols] = acc


def _mod_index(ntok, nmod, tile):
    tiles_per_mod = (ntok // tile) // nmod if nmod > 1 else 1
    if nmod > 1:
        return lambda i, *_: (i // tiles_per_mod, 0, 0)
    return lambda i, *_: (0, 0, 0)


def _project(x, mod, norm_g, w_cat, lb_logits, ba2, layer):
    ntok = x.shape[0]
    return pl.pallas_call(
        functools.partial(_proj_kernel, layer=layer),
        grid=(ntok // PROJ_TM,),
        in_specs=[pl.BlockSpec((PROJ_TM, D_MODEL), lambda i: (i, 0)),
                  pl.BlockSpec((1, 6, D_MODEL), _mod_index(ntok, mod.shape[0], PROJ_TM)),
                  pl.BlockSpec((1, 1, D_MODEL), lambda i: (layer, 0, 0)),
                  pl.BlockSpec((1, PROJ_COLS, D_MODEL), lambda i: (layer, 0, 0),
                               pipeline_mode=pl.Buffered(1)),
                  _full_spec(lb_logits),
                  pl.BlockSpec((1, 2, N_HEADS * GLA_DK), lambda i: (layer, 0, 0))],
        out_specs=pl.BlockSpec((PROJ_TM, PROJ_COLS), lambda i: (i, 0)),
        out_shape=jax.ShapeDtypeStruct((ntok, PROJ_COLS), F32),
        compiler_params=_params(("arbitrary",)),
        name="norm_in_proj",
    )(x, mod, norm_g.reshape(DEPTH, 1, D_MODEL), w_cat, lb_logits, ba2)


def _mlp_kernel(x_ref, oa_ref, ob_ref, oc_ref, od_ref, mod_ref, g2_ref, fg_ref, wout_ref,
                w1_ref, w2_ref, out_ref, h2_scr, *, final_norm):
    j = pl.program_id(1)

    @pl.when(j == 0)
    def _():
        mixed = jnp.concatenate([oa_ref[...], ob_ref[...], oc_ref[...], od_ref[...]], axis=1)
        mix = jnp.dot(mixed, wout_ref[0].astype(BF16), preferred_element_type=F32)
        x1 = x_ref[...] + mod_ref[0, 2:3, :] * mix
        out_ref[...] = x1
        h2 = _rms(x1) * g2_ref[0] * (1.0 + mod_ref[0, 4:5, :]) + mod_ref[0, 3:4, :]
        h2_scr[...] = h2.astype(BF16)

    w1 = w1_ref[0].astype(BF16)
    w2 = w2_ref[0].astype(BF16)
    for r in range(MLP_ROW_SPLIT):
        rows = slice(r * (TOK_TILE // MLP_ROW_SPLIT), (r + 1) * (TOK_TILE // MLP_ROW_SPLIT))
        t = jnp.dot(h2_scr[rows, :], w1, preferred_element_type=F32)
        t = jnp.square(jnp.maximum(t, 0.0)).astype(BF16)
        out_ref[rows, :] += mod_ref[0, 5:6, :] * jnp.dot(t, w2, preferred_element_type=F32)

    if final_norm:
        @pl.when(j == pl.num_programs(1) - 1)
        def _():
            out_ref[...] = _rms(out_ref[...]) * fg_ref[...]


def _out_mlp(x, o_a, o_b, o_c, o_d, mod, norm2_g, final_g, w_out, w1, w2, layer, final_norm):
    ntok = x.shape[0]
    tok = lambda i, j: (i, 0)
    return pl.pallas_call(
        functools.partial(_mlp_kernel, final_norm=final_norm),
        grid=(ntok // TOK_TILE, D_FF // FF_TILE),
        in_specs=[pl.BlockSpec((TOK_TILE, D_MODEL), tok),
                  pl.BlockSpec((TOK_TILE, W_GRP), tok),
                  pl.BlockSpec((TOK_TILE, W_GRP), tok),
                  pl.BlockSpec((TOK_TILE, W_GRP), tok),
                  pl.BlockSpec((TOK_TILE, W_GRP), tok),
                  pl.BlockSpec((1, 6, D_MODEL), _mod_index(ntok, mod.shape[0], TOK_TILE)),
                  pl.BlockSpec((1, 1, D_MODEL), lambda i, j: (layer, 0, 0)),
                  pl.BlockSpec((1, D_MODEL), lambda i, j: (0, 0)),
                  pl.BlockSpec((1, D_MODEL, D_MODEL), lambda i, j: (layer, 0, 0)),
                  pl.BlockSpec((1, D_MODEL, FF_TILE), lambda i, j: (layer, 0, j)),
                  pl.BlockSpec((1, FF_TILE, D_MODEL), lambda i, j: (layer, j, 0))],
        out_specs=pl.BlockSpec((TOK_TILE, D_MODEL), tok),
        out_shape=jax.ShapeDtypeStruct((ntok, D_MODEL), F32),
        scratch_shapes=[pltpu.VMEM((TOK_TILE, D_MODEL), BF16)],
        compiler_params=_params(("arbitrary", "arbitrary")),
        name="out_proj_mlp",
    )(x, o_a, o_b, o_c, o_d, mod, norm2_g.reshape(DEPTH, 1, D_MODEL), final_g.reshape(1, D_MODEL),
      w_out, w1, w2)


def _split3(x):
    hi = x.astype(BF16)
    r1 = x - hi.astype(F32)
    mid = r1.astype(BF16)
    lo = (r1 - mid.astype(F32)).astype(BF16)
    return hi, mid, lo


def _dot_exact_rhs(x, m):
    hi, mid, lo = _split3(x)
    s = jnp.dot(hi, m, preferred_element_type=F32)
    s += jnp.dot(mid, m, preferred_element_type=F32)
    s += jnp.dot(lo, m, preferred_element_type=F32)
    return s


def _block_cumsum(tri, g):
    hi = g.astype(BF16)
    lo = (g - hi.astype(F32)).astype(BF16)
    return jnp.dot(tri, hi, preferred_element_type=F32) + jnp.dot(tri, lo, preferred_element_type=F32)


def _run_prepass(pre, nblock, width):
    rmax = jnp.zeros((1, width), F32)
    if nblock % 2:
        for bi in range(nblock):
            rmax = pre(bi, rmax)
        return rmax
    if nblock == 2:
        return pre(1, pre(0, rmax))
    return lax.fori_loop(0, nblock // 2, lambda i, rm: pre(2 * i + 1, pre(2 * i, rm)), rmax)


def _half_ranges(cum, reverse):
    if reverse:
        second = -cum[HALF:HALF + 1, :]
        first = -(cum[0:1, :] - cum[HALF:HALF + 1, :])
    else:
        first = -cum[HALF - 1:HALF, :]
        second = -(cum[CHUNK - 1:CHUNK, :] - cum[HALF - 1:HALF, :])
    return jnp.maximum(first, second)


def _block_ranges(cum, reverse, rm):
    for c in range(GROUP):
        rm = jnp.maximum(rm, _half_ranges(cum[c * CHUNK:(c + 1) * CHUNK, :], reverse))
    return rm


def _row_start(c):
    r0 = c * CHUNK
    return r0 if isinstance(r0, int) else pl.multiple_of(r0, CHUNK)


class _Rec:
    def __init__(self, q_at, k_at, v_ref, cum_scr, st_scr, oacc, consts, width, seq_len, nsub):
        self.q_at, self.k_at, self.v_ref = q_at, k_at, v_ref
        self.cum_scr, self.st_scr, self.oacc = cum_scr, st_scr, oacc
        (self.tri_ref, self.bdw_b, self.bdw_f, self.bdv_b, self.cmask, self.bmat,
         self.tile_t) = consts
        self.width, self.seq_len, self.nsub = width, seq_len, nsub


def _chunk_step(rec, sub, d, c, fast):
    reverse = d == 1
    mid_row = HALF if reverse else HALF - 1
    last_row = 0 if reverse else CHUNK - 1
    sd = sub * 2 + d
    r0 = sub * rec.seq_len + c * CHUNK
    if not isinstance(r0, int):
        r0 = pl.multiple_of(r0, CHUNK)
    rows = pl.ds(r0, CHUNK)
    q = rec.q_at(r0)
    k = rec.k_at(d, r0)
    v_b = rec.v_ref[rows, :].astype(BF16)
    cum = rec.cum_scr[d, rows, :]
    tot = cum[last_row:last_row + 1, :]
    st = rec.st_scr[sd]
    o = _dot_nt(q * jnp.exp(cum), st)
    if fast:
        cm = cum[mid_row:mid_row + 1, :]
        qm = q * jnp.exp(cum - cm)
        km_b = (k * jnp.exp(cm - cum)).astype(BF16)
        kbd = jnp.concatenate([km_b] * N_HEADS, axis=0) * rec.bdw_b[...]
        sc = lax.dot_general(qm.astype(BF16), kbd, _NT, preferred_element_type=F32)
        a = jnp.where(rec.cmask[d] > 0.5, sc, 0.0)
        vexp = jnp.concatenate([v_b] * N_HEADS, axis=0) * rec.bdv_b[...]
        o = o + jnp.dot(a.astype(BF16), vexp, preferred_element_type=F32)
    else:
        row_id = lax.broadcasted_iota(jnp.int32, (CHUNK, rec.width), 0)

        def key_row(j, acc):
            krow = rec.k_at(d, r0 + j, 1)
            crow = rec.cum_scr[d, pl.ds(r0 + j, 1), :]
            vrow = rec.v_ref[pl.ds(r0 + j, 1), :]
            keep = (row_id <= j) if reverse else (row_id >= j)
            e = jnp.exp(jnp.minimum(cum - crow, 0.0))
            p = jnp.where(keep, q * krow * e, 0.0)
            return acc + _dot(p, rec.bmat[...]) * vrow

        o = lax.fori_loop(0, CHUNK, key_row, o)
    ke_b = (k * jnp.exp(tot - cum)).astype(BF16)
    ds = lax.dot_general(v_b, ke_b, _TN, preferred_element_type=F32)
    rec.st_scr[sd] = st * jnp.exp(tot) + ds * rec.bdw_f[...]
    rec.oacc[d, rows, :] = o


def _load_state(rec, s0_ref, sub, d):
    x = jnp.concatenate([s0_ref[sub, 0, d, h] for h in range(N_HEADS)], axis=0)
    y = _dot_exact_rhs(x, rec.tile_t[...]) * rec.bmat[...].astype(F32)
    rec.st_scr[sub * 2 + d] = y.T


def _store_state(rec, sfin_ref, sub, d):
    y = rec.st_scr[sub * 2 + d].T
    half = y[:, 0:LANES] + y[:, LANES:2 * LANES]
    x = (half + pltpu.roll(half, HEAD_V, 1))[:, 0:HEAD_V]
    dk = rec.width // N_HEADS
    for h in range(N_HEADS):
        sfin_ref[sub, d, h] = x[h * dk:(h + 1) * dk, :]


def _run_recurrence(rec, ok, s0_ref, sfin_ref):
    nsub = rec.nsub
    nchunk = rec.seq_len // CHUNK
    gsize = min(MAIN_GROUP, nchunk)
    ngroup = nchunk // gsize
    spp = max(1, min(nsub, MAIN_GROUP // gsize))
    nsp = nsub // spp
    for sub in range(nsub):
        for d in (0, 1):
            if s0_ref is None:
                rec.st_scr[sub * 2 + d] = jnp.zeros(rec.st_scr.shape[1:], F32)
            else:
                _load_state(rec, s0_ref, sub, d)

    def fast_all():
        def step(i, carry):
            sp = i // ngroup if (nsp > 1 and ngroup > 1) else (i if nsp > 1 else 0)
            gi = i % ngroup if (nsp > 1 and ngroup > 1) else (i if ngroup > 1 else 0)
            for u in range(gsize):
                s = gi * gsize + u
                for j in range(spp):
                    sub = sp * spp + j
                    _chunk_step(rec, sub, 0, s, True)
                    _chunk_step(rec, sub, 1, nchunk - 1 - s, True)
            return carry

        if nsp * ngroup == 1:
            step(0, 0)
        else:
            lax.fori_loop(0, nsp * ngroup, step, 0)

    def direct_all():
        def one(i, carry):
            sub = i // nchunk if nsub > 1 else 0
            s = i % nchunk if nsub > 1 else i
            _chunk_step(rec, sub, 0, s, False)
            _chunk_step(rec, sub, 1, nchunk - 1 - s, False)
            return carry

        lax.fori_loop(0, nsub * nchunk, one, 0)

    lax.cond(ok, fast_all, direct_all)
    if sfin_ref is not None:
        for sub in range(nsub):
            for d in (0, 1):
                _store_state(rec, sfin_ref, sub, d)


def _head_norm_gate(rec, gate_ref, gain_ref, o_ref):
    for rb in range(rec.nsub * rec.seq_len // ROW_BLOCK):
        rows = slice(rb * ROW_BLOCK, (rb + 1) * ROW_BLOCK)
        o = rec.oacc[0, rows, :] + rec.oacc[1, rows, :]
        ms = jnp.dot((o * o).astype(BF16), rec.bdv_b[...], preferred_element_type=F32) * (1.0 / HEAD_V)
        y = o * lax.rsqrt(ms + EPS) * gain_ref[0] * _silu(gate_ref[rows, :])
        o_ref[rows, :] = y.astype(o_ref.dtype)


def _split_refs(refs, n_in, has_s0, want_final):
    ins = refs[:n_in]
    pos = n_in
    s0_ref = None
    if has_s0:
        s0_ref = refs[pos]
        pos += 1
    o_ref = refs[pos]
    pos += 1
    sfin_ref = None
    if want_final:
        sfin_ref = refs[pos]
        pos += 1
    return ins, s0_ref, o_ref, sfin_ref, refs[pos:]


def _hgrn_kernel(*refs, seq_len, nsub, has_s0, want_final):
    ins, s0_ref, o_ref, sfin_ref, scr = _split_refs(refs, 13, has_s0, want_final)
    q_ref, i_ref, ff_ref, fb_ref, g_ref, gain_ref = ins[:6]
    k_scr, cum_scr, oacc, st_scr = scr
    q_at = lambda r0: q_ref[pl.ds(r0, CHUNK), :]
    k_at = lambda d, r0, n=CHUNK: k_scr[d, pl.ds(r0, n), :]
    rec = _Rec(q_at, k_at, i_ref, cum_scr, st_scr, oacc, ins[6:], W_GRP, seq_len, nsub)

    def pre(bi, rm):
        r0 = bi * GROWS if isinstance(bi, int) else pl.multiple_of(bi * GROWS, GROWS)
        rows = pl.ds(r0, GROWS)
        for d, f_ref in ((0, ff_ref), (1, fb_ref)):
            f = f_ref[rows, :]
            g = jnp.log(jnp.maximum(f, F_FLOOR))
            k_scr[d, rows, :] = 1.0 - f
            cum = _block_cumsum(rec.tri_ref[d], g)
            cum_scr[d, rows, :] = cum
            rm = _block_ranges(cum, d == 1, rm)
        return rm

    rmax = _run_prepass(pre, nsub * seq_len // GROWS, W_GRP)
    ok = jnp.max(rmax) < SAFE_RANGE

    _run_recurrence(rec, ok, s0_ref, sfin_ref)
    _head_norm_gate(rec, g_ref, gain_ref, o_ref)


def _gla_kernel(*refs, seq_len, nsub, has_s0, want_final):
    ins, s0_ref, o_ref, sfin_ref, scr = _split_refs(refs, 14, has_s0, want_final)
    q_ref, k_ref, v_ref, g_ref, gf_ref, gb_ref, gain_ref = ins[:7]
    cum_scr, oacc, st_scr = scr
    width = N_HEADS * GLA_DK
    q_at = lambda r0: q_ref[pl.ds(r0, CHUNK), :] * (GLA_DK ** -0.5)
    k_at = lambda d, r0, n=CHUNK: k_ref[pl.ds(r0, n), :]
    rec = _Rec(q_at, k_at, v_ref, cum_scr, st_scr, oacc, ins[7:], width, seq_len, nsub)

    def pre(bi, rm):
        r0 = bi * GROWS if isinstance(bi, int) else pl.multiple_of(bi * GROWS, GROWS)
        rows = pl.ds(r0, GROWS)
        for d, logdecay_ref in ((0, gf_ref), (1, gb_ref)):
            g = logdecay_ref[rows, :]
            cum = _block_cumsum(rec.tri_ref[d], g)
            cum_scr[d, rows, :] = cum
            rm = _block_ranges(cum, d == 1, rm)
        return rm

    rmax = _run_prepass(pre, nsub * seq_len // GROWS, width)
    ok = jnp.max(rmax) < SAFE_RANGE

    _run_recurrence(rec, ok, s0_ref, sfin_ref)
    _head_norm_gate(rec, g_ref, gain_ref, o_ref)


def _recurrence_constants(width):
    dk = width // N_HEADS
    r = np.arange(N_HEADS * CHUNK)[:, None]
    tri = np.kron(np.eye(GROUP, dtype=np.float32), np.tril(np.ones((CHUNK, CHUNK), np.float32)))
    tri = np.stack([tri, tri.T])
    bdw = (r // CHUNK == np.arange(width)[None, :] // dk).astype(np.float32)
    bdv = (r // CHUNK == np.arange(W_GRP)[None, :] // HEAD_V).astype(np.float32)
    t = np.arange(CHUNK)[:, None]
    s = np.arange(N_HEADS * CHUNK)[None, :] % CHUNK
    cmask = np.stack([(s <= t), (s >= t)]).astype(np.float32)
    bmat = (np.arange(width)[:, None] // dk == np.arange(W_GRP)[None, :] // HEAD_V).astype(np.float32)
    tile_t = np.tile(np.eye(HEAD_V, dtype=np.float32), (1, N_HEADS))
    return (jnp.asarray(tri, BF16), jnp.asarray(bdw, BF16), jnp.asarray(bdw), jnp.asarray(bdv, BF16),
            jnp.asarray(cmask), jnp.asarray(bmat, BF16), jnp.asarray(tile_t, BF16))


def _recurrent_mixer(kernel_fn, name, width, args, in_specs, state, layer, nseq, seq_len, want_final,
                     extra_scratch):
    dk = width // N_HEADS
    nsub = _seqs_per_block(nseq, seq_len)
    rows = nsub * seq_len
    consts = _recurrence_constants(width)
    args = list(args) + list(consts)
    in_specs = list(in_specs) + [_full_spec(c) for c in consts]
    if state is not None:
        args.append(state)
        in_specs.append(pl.BlockSpec((nsub, 1, 2, N_HEADS, dk, HEAD_V), lambda b: (b, layer, 0, 0, 0, 0)))
    out_shape = [jax.ShapeDtypeStruct((nseq * seq_len, W_GRP), BF16)]
    out_specs = [pl.BlockSpec((rows, W_GRP), lambda b: (b, 0))]
    if want_final:
        out_shape.append(jax.ShapeDtypeStruct((nseq, 2, N_HEADS, dk, HEAD_V), F32))
        out_specs.append(pl.BlockSpec((nsub, 2, N_HEADS, dk, HEAD_V), lambda b: (b, 0, 0, 0, 0)))
    res = pl.pallas_call(
        functools.partial(kernel_fn, seq_len=seq_len, nsub=nsub, has_s0=state is not None,
                          want_final=want_final),
        grid=(nseq // nsub,),
        in_specs=in_specs,
        out_specs=out_specs,
        out_shape=out_shape,
        scratch_shapes=[pltpu.VMEM((2, rows, W_GRP), F32)] * extra_scratch + [
            pltpu.VMEM((2, rows, width), F32), pltpu.VMEM((2, rows, W_GRP), F32),
            pltpu.VMEM((2 * nsub, W_GRP, width), F32)],
        compiler_params=_params(("arbitrary",)),
        name=name,
    )(*args)
    return res[0], (res[1] if want_final else None)


def _seqs_per_block(nseq, seq_len):
    return max(1, min(nseq, BLOCK_ROWS // seq_len))


def _hgrn_mixer(proj, gain, state, layer, nseq, seq_len, want_final):
    rows = _seqs_per_block(nseq, seq_len) * seq_len
    col = lambda c: pl.BlockSpec((rows, W_GRP), lambda b: (b, c))
    args = [proj] * 5 + [gain.reshape(DEPTH, 1, W_GRP)]
    in_specs = [col(0), col(1), col(2), col(3), col(4),
                pl.BlockSpec((1, 1, W_GRP), lambda b: (layer, 0, 0))]
    return _recurrent_mixer(_hgrn_kernel, "hgrn2_mixer", W_GRP, args, in_specs, state, layer, nseq,
                            seq_len, want_final, 1)


def _gla_mixer(proj, gain, state, layer, nseq, seq_len, want_final):
    width = N_HEADS * GLA_DK
    rows = _seqs_per_block(nseq, seq_len) * seq_len
    col256 = lambda c: pl.BlockSpec((rows, W_GRP), lambda b: (b, c))
    col128 = lambda c: pl.BlockSpec((rows, width), lambda b: (b, c))
    args = [proj] * 6 + [gain.reshape(DEPTH, 1, W_GRP)]
    in_specs = [col128(10), col128(11), col256(6), col256(7), col128(16), col128(17),
                pl.BlockSpec((1, 1, W_GRP), lambda b: (layer, 0, 0))]
    return _recurrent_mixer(_gla_kernel, "gla_mixer", width, args, in_specs, state, layer, nseq, seq_len,
                            want_final, 0)


PAD = 8
SCAN_TILE = 8
SCAN_UNROLL = 4
GELU_C = 0.7978845608028654


def _rgsc_kernel(*refs, seq_len, nsub, seg, has_h0, want_final):
    (cx_ref, cg_ref, db_ref, dc_ref, dv_ref, convw_ref, convb_ref, wr_ref, br_ref, wi_ref,
     bi_ref, lam_ref, sw_ref) = refs[:13]
    pos = 13
    h0_ref = None
    if has_h0:
        h0_ref = refs[pos]
        pos += 1
    oc_ref, od_ref = refs[pos:pos + 2]
    pos += 2
    hfin_ref = None
    if want_final:
        hfin_ref = refs[pos]
        pos += 1
    upad, a_scr, b_scr, h_scr = refs[pos:]

    slot = seq_len + 2 * PAD
    zpad = jnp.zeros((PAD, W_GRP), F32)
    for sub in range(nsub):
        upad[sub * slot:sub * slot + PAD, :] = zpad
        upad[sub * slot + PAD + seq_len:(sub + 1) * slot, :] = zpad
        upad[sub * slot + PAD:sub * slot + PAD + seq_len, :] = cx_ref[sub * seq_len:(sub + 1) * seq_len, :]
    rows8 = lax.broadcasted_iota(jnp.int32, (SCAN_TILE, W_GRP), 0)
    ntile = seq_len // SCAN_TILE

    for d in (0, 1):
        reverse = d == 1
        w = convw_ref[0, d]
        offs = [PAD + 3, PAD + 2, PAD + 1, PAD] if reverse else [PAD - 3, PAD - 2, PAD - 1, PAD]
        sp = _softplus(-lam_ref[0, d:d + 1, :])
        for sub in range(nsub):
            for rb in range(seq_len // ROW_BLOCK):
                src = sub * slot + rb * ROW_BLOCK
                base = sub * seq_len + rb * ROW_BLOCK
                xc = convb_ref[0, d:d + 1, :]
                for j in range(4):
                    xc = xc + w[j:j + 1, :] * upad[offs[j] + src:offs[j] + src + ROW_BLOCK, :]
                r = _sigmoid(_dot(xc, wr_ref[d]) + br_ref[0, d:d + 1, :])
                ig = _sigmoid(_dot(xc, wi_ref[d]) + bi_ref[0, d:d + 1, :])
                log_a = -RG_C * r * sp
                a = jnp.exp(log_a)
                a_scr[base:base + ROW_BLOCK, :] = a
                y = 1.0 - a * a
                root = jnp.where(y > 0.0, y * lax.rsqrt(y), 0.0)
                b_scr[base:base + ROW_BLOCK, :] = root * (ig * xc)

        if has_h0:
            h0 = tuple(h0_ref[sub, 0, d:d + 1, :] for sub in range(nsub))
        else:
            h0 = tuple(jnp.zeros((1, W_GRP), F32) for _ in range(nsub))

        def step(ti, carry, d=d, reverse=reverse):
            t = (ntile - 1 - ti) if reverse else ti
            out = []
            for sub in range(nsub):
                r0 = pl.multiple_of(sub * seq_len + t * SCAN_TILE, SCAN_TILE)
                a = a_scr[pl.ds(r0, SCAN_TILE), :]
                b = b_scr[pl.ds(r0, SCAN_TILE), :]
                for s in (1, 2, 4):
                    if reverse:
                        keep = rows8 <= SCAN_TILE - 1 - s
                        shift = SCAN_TILE - s
                    else:
                        keep = rows8 >= s
                        shift = s
                    a_s = jnp.where(keep, pltpu.roll(a, shift, 0), 1.0)
                    b_s = jnp.where(keep, pltpu.roll(b, shift, 0), 0.0)
                    b = b + a * b_s
                    a = a * a_s
                h = a * carry[sub] + b
                if d == 0:
                    h_scr[pl.ds(r0, SCAN_TILE), :] = h
                else:
                    h_scr[pl.ds(r0, SCAN_TILE), :] += h
                out.append(h[0:1, :] if reverse else h[SCAN_TILE - 1:SCAN_TILE, :])
            return tuple(out)

        hlast = lax.fori_loop(0, ntile, step, h0, unroll=SCAN_UNROLL // nsub if nsub < SCAN_UNROLL else 1)
        if want_final:
            for sub in range(nsub):
                hfin_ref[sub, d:d + 1, :] = hlast[sub]

    sw = sw_ref[0]
    nblock = nsub * seq_len // ROW_BLOCK
    for rb in range(nblock):
        rows = slice(rb * ROW_BLOCK, (rb + 1) * ROW_BLOCK)
        x = cg_ref[rows, :]
        gelu = x * (0.5 + 0.5 * jnp.tanh(x * (GELU_C + (GELU_C * 0.044715) * (x * x))))
        oc_ref[rows, :] = (h_scr[rows, :] * gelu).astype(oc_ref.dtype)
        upad[PAD + rb * ROW_BLOCK:PAD + (rb + 1) * ROW_BLOCK, :] = dc_ref[rows, :] * dv_ref[rows, :]
    for rb in range(nblock):
        base = rb * ROW_BLOCK
        rows = slice(base, base + ROW_BLOCK)
        posn = (lax.broadcasted_iota(jnp.int32, (ROW_BLOCK, W_GRP), 0) + base) % seg
        left = jnp.where(posn != 0, upad[PAD - 1 + base:PAD - 1 + base + ROW_BLOCK, :], 0.0)
        right = jnp.where(posn != seg - 1, upad[PAD + 1 + base:PAD + 1 + base + ROW_BLOCK, :], 0.0)
        y = sw[0:1, :] * left + sw[1:2, :] * upad[PAD + base:PAD + base + ROW_BLOCK, :] + sw[2:3, :] * right
        od_ref[rows, :] = (db_ref[rows, :] * y).astype(od_ref.dtype)


def _rgsc_mixer(proj, conv_w, conv_b, wr_bd, b_r, wi_bd, b_i, lam, sconv_w, h0, layer, nseq, seq_len,
                seg, want_final):
    nsub = _seqs_per_block(nseq, seq_len)
    rows = nsub * seq_len
    col = lambda c: pl.BlockSpec((rows, W_GRP), lambda b: (b, c))
    lay3 = lambda a: pl.BlockSpec((1,) + a.shape[1:], lambda b: (layer, 0, 0))
    lay4 = lambda a: pl.BlockSpec((1,) + a.shape[1:], lambda b: (layer, 0, 0, 0))
    params = [conv_w, conv_b, wr_bd, b_r, wi_bd, b_i, lam, sconv_w]
    args = [proj] * 5 + params
    in_specs = [col(9), col(10), col(11), col(12), col(13), lay4(conv_w), lay3(conv_b), _full_spec(wr_bd),
                lay3(b_r), _full_spec(wi_bd), lay3(b_i), lay3(lam), lay3(sconv_w)]
    if h0 is not None:
        args.append(h0)
        in_specs.append(pl.BlockSpec((nsub, 1, 2, W_GRP), lambda b: (b, layer, 0, 0)))
    out_shape = [jax.ShapeDtypeStruct((nseq * seq_len, W_GRP), BF16)] * 2
    out_specs = [pl.BlockSpec((rows, W_GRP), lambda b: (b, 0))] * 2
    if want_final:
        out_shape.append(jax.ShapeDtypeStruct((nseq, 2, W_GRP), F32))
        out_specs.append(pl.BlockSpec((nsub, 2, W_GRP), lambda b: (b, 0, 0)))
    res = pl.pallas_call(
        functools.partial(_rgsc_kernel, seq_len=seq_len, nsub=nsub, seg=seg, has_h0=h0 is not None,
                          want_final=want_final),
        grid=(nseq // nsub,),
        in_specs=in_specs,
        out_specs=out_specs,
        out_shape=out_shape,
        scratch_shapes=[pltpu.VMEM((nsub * (seq_len + 2 * PAD), W_GRP), F32), pltpu.VMEM((rows, W_GRP), F32),
                        pltpu.VMEM((rows, W_GRP), F32), pltpu.VMEM((rows, W_GRP), F32)],
        compiler_params=_params(("arbitrary",)),
        name="rglru_sconv_mixer",
    )(*args)
    return res[0], res[1], (res[2] if want_final else None)


def _block_diag(w):
    two, h, c, _ = w.shape
    eye = jnp.eye(h, dtype=w.dtype)
    full = w[:, :, :, None, :] * eye[None, :, None, :, None]
    return full.reshape(two, h * c, h * c)


def kernel(x_prompt, x_sample, state_hgrn, state_gla, state_rglru, c, c_ctx, norm1_g, norm2_g, ada_w, ada_b, w_in, w_out, hgrn_lb_logits, hgrn_norm_g, gla_wa2, gla_ba2, gla_norm_g, rg_conv_w, rg_conv_b, rg_w_r, rg_b_r, rg_w_i, rg_b_i, rg_lambda, sconv_w, mlp_w1, mlp_w2, final_norm_g):
    b_ctx, l_ctx, _ = x_prompt.shape
    b_lat, l_lat, _ = x_sample.shape

    cvec = jnp.concatenate([c, c_ctx[None, :], jnp.zeros((8 - b_lat - 1, D_MODEL), F32)], axis=0)
    mod = _modulation(cvec, ada_w, ada_b).reshape(DEPTH, 8, 6, D_MODEL)

    w_cat = _build_proj_weight(w_in, gla_wa2)

    xp = x_prompt.reshape(b_ctx * l_ctx, D_MODEL)
    xs = x_sample.reshape(b_lat * l_lat, D_MODEL)
    new_h, new_g, new_r = [], [], []
    for l in range(DEPTH):
        wr_bd = _block_diag(rg_w_r[l]).astype(BF16)
        wi_bd = _block_diag(rg_w_i[l]).astype(BF16)
        mod_ctx = mod[l, b_lat:b_lat + 1]
        mod_lat = mod[l, :b_lat]
        last = l == DEPTH - 1
        streams = (
            (xp, mod_ctx, b_ctx, l_ctx, l_ctx, None, None, None, True),
            (xs, mod_lat, b_lat, l_lat, GRID_W, state_hgrn, state_gla, state_rglru, False),
        )
        outs = []
        for (x, m, nseq, seq_len, seg, s_h, s_g, s_r, want_final) in streams:
            proj = _project(x, m, norm1_g, w_cat, hgrn_lb_logits, gla_ba2, l)
            o_a, f_h = _hgrn_mixer(proj, hgrn_norm_g, s_h, l, nseq, seq_len, want_final)
            o_b, f_g = _gla_mixer(proj, gla_norm_g, s_g, l, nseq, seq_len, want_final)
            o_c, o_d, f_r = _rgsc_mixer(proj, rg_conv_w, rg_conv_b, wr_bd, rg_b_r, wi_bd, rg_b_i,
                                        rg_lambda, sconv_w, s_r, l, nseq, seq_len, seg, want_final)
            x_new = _out_mlp(x, o_a, o_b, o_c, o_d, m, norm2_g, final_norm_g, w_out, mlp_w1, mlp_w2, l, last)
            outs.append((x_new, f_h, f_g, f_r))
        xp, f_h, f_g, f_r = outs[0]
        xs = outs[1][0]
        new_h.append(f_h)
        new_g.append(f_g)
        new_r.append(f_r)
    y_prompt = xp.reshape(b_ctx, l_ctx, D_MODEL)
    y_sample = xs.reshape(b_lat, l_lat, D_MODEL)
    return (y_prompt, y_sample, jnp.stack(new_h, axis=1), jnp.stack(new_g, axis=1),
            jnp.stack(new_r, axis=1))
```

```python
import functools

import numpy as np
import jax
import jax.numpy as jnp
from jax import lax
from jax.experimental import pallas as pl
from jax.experimental.pallas import tpu as pltpu

F32 = jnp.float32
BF16 = jnp.bfloat16

D_MODEL = 1024
DEPTH = 2
GRID_W = 64
N_HEADS = 4
W_GRP = 256
HEAD_V = 64
HGRN_DK = 64
GLA_DK = 32
GLA_RANK = 16
GLA_NORMALIZER = 16.0
RG_C = 8.0
D_FF = 4 * D_MODEL
EPS = 1e-6
F_FLOOR = 1e-20

PROJ_COLS = 3584

CHUNK = 64
HALF = CHUNK // 2
assert CHUNK == HEAD_V
GROUP = 4
GROWS = GROUP * CHUNK
MAIN_GROUP = 16
BLOCK_ROWS = 1024
LANES = 128
SAFE_RANGE = 80.0
TOK_TILE = 1024
PROJ_TM = 512
PROJ_TN = 512
FF_TILE = 1024
MLP_ROW_SPLIT = 2
PROLOGUE_ROWS = 256
ROW_BLOCK = 256
VMEM_LIMIT = 56 * 1024 * 1024

_NT = (((1,), (1,)), ((), ()))
_TN = (((0,), (0,)), ((), ()))


def _dot(a, b):
    return jnp.dot(a.astype(BF16), b.astype(BF16), preferred_element_type=F32)


def _dot_nt(a, b):
    return lax.dot_general(a.astype(BF16), b.astype(BF16), _NT, preferred_element_type=F32)


def _dot_tn(a, b):
    return lax.dot_general(a.astype(BF16), b.astype(BF16), _TN, preferred_element_type=F32)


def _sigmoid(x):
    return 1.0 / (1.0 + jnp.exp(-x))


def _silu(x):
    return x * _sigmoid(x)


def _log1p(y):
    u = 1.0 + y
    return jnp.where(u == 1.0, y, jnp.log(u) * (y / (u - 1.0)))


def _softplus(x):
    return jnp.maximum(x, 0.0) + _log1p(jnp.exp(-jnp.abs(x)))


def _log_sigmoid(z):
    return jnp.minimum(z, 0.0) - jnp.log(1.0 + jnp.exp(-jnp.abs(z)))


def _rms(x):
    return x * lax.rsqrt(jnp.mean(x * x, axis=-1, keepdims=True) + EPS)


def _params(sem):
    return pltpu.CompilerParams(dimension_semantics=sem, vmem_limit_bytes=VMEM_LIMIT)


def _full_spec(a):
    zeros = (0,) * a.ndim
    return pl.BlockSpec(a.shape, lambda *_: zeros)


def _mod_kernel(c_ref, w_ref, b_ref, o_ref):
    s = _silu(c_ref[...])
    o_ref[0] = _dot(s, w_ref[0]) + b_ref[0]


def _modulation(cvec, ada_w, ada_b):
    tn = 2048
    return pl.pallas_call(
        _mod_kernel,
        grid=(DEPTH, 6 * D_MODEL // tn),
        in_specs=[pl.BlockSpec((8, D_MODEL), lambda l, j: (0, 0)),
                  pl.BlockSpec((1, D_MODEL, tn), lambda l, j: (l, 0, j)),
                  pl.BlockSpec((1, 1, tn), lambda l, j: (l, 0, j))],
        out_specs=pl.BlockSpec((1, 8, tn), lambda l, j: (l, 0, j)),
        out_shape=jax.ShapeDtypeStruct((DEPTH, 8, 6 * D_MODEL), F32),
        compiler_params=_params(("arbitrary", "arbitrary")),
        name="adaln_modulation",
    )(cvec, ada_w, ada_b.reshape(DEPTH, 1, 6 * D_MODEL))


LOWRANK_COL = 2048
TAIL_COL = LOWRANK_COL + 2 * GLA_RANK
TAIL_WIDTH = 1280
WCAT_LANES = 256


def _wcat_kernel(w_ref, wa2t_ref, o_ref):
    o_ref[0, 0:LOWRANK_COL, :] = w_ref[0, 0:LOWRANK_COL, :].astype(BF16)
    z = jnp.dot(wa2t_ref[0], w_ref[0, LOWRANK_COL:TAIL_COL, :], preferred_element_type=F32,
                precision=lax.Precision.HIGHEST)
    o_ref[0, LOWRANK_COL:LOWRANK_COL + 2 * LANES, :] = z.astype(BF16)
    o_ref[0, LOWRANK_COL + 2 * LANES:PROJ_COLS, :] = w_ref[0, TAIL_COL:TAIL_COL + TAIL_WIDTH, :].astype(BF16)


def _build_proj_weight(w_in, wa2):
    ncol = w_in.shape[2]
    assert ncol == TAIL_COL + TAIL_WIDTH and LOWRANK_COL + 2 * LANES + TAIL_WIDTH == PROJ_COLS
    n = N_HEADS * GLA_DK
    wa2t = jnp.zeros((DEPTH, 2 * n, 2 * GLA_RANK), F32)
    wa2t = wa2t.at[:, 0:n, 0:GLA_RANK].set(jnp.swapaxes(wa2[:, 0], 1, 2))
    wa2t = wa2t.at[:, n:2 * n, GLA_RANK:2 * GLA_RANK].set(jnp.swapaxes(wa2[:, 1], 1, 2))
    w_t = jnp.swapaxes(w_in, 1, 2)
    return pl.pallas_call(
        _wcat_kernel,
        grid=(DEPTH, D_MODEL // WCAT_LANES),
        in_specs=[pl.BlockSpec((1, ncol, WCAT_LANES), lambda l, i: (l, 0, i)),
                  pl.BlockSpec((1, 2 * n, 2 * GLA_RANK), lambda l, i: (l, 0, 0))],
        out_specs=pl.BlockSpec((1, PROJ_COLS, WCAT_LANES), lambda l, i: (l, 0, i)),
        out_shape=jax.ShapeDtypeStruct((DEPTH, PROJ_COLS, D_MODEL), BF16),
        compiler_params=_params(("arbitrary", "arbitrary")),
        name="build_proj_weight",
    )(w_t, wa2t)


HGRN_GATE_COL = 2 * W_GRP
GLA_GATE_COL = LOWRANK_COL
assert HGRN_GATE_COL % PROJ_TN == 0 and GLA_GATE_COL % PROJ_TN == 0 and PROJ_TN == 2 * W_GRP


def _hgrn_lower_bound(logits, layer):
    mx = logits[0]
    for i in range(1, DEPTH):
        mx = jnp.maximum(mx, logits[i])
    ex = [jnp.exp(logits[i] - mx) for i in range(DEPTH)]
    den = ex[0]
    for i in range(1, DEPTH):
        den = den + ex[i]
    sm = [e / den for e in ex]
    csum = sm[0]
    for i in range(1, layer + 1):
        csum = csum + sm[i]
    return csum - sm[0]


def _proj_kernel(x_ref, mod_ref, g_ref, w_ref, lbl_ref, ba_ref, o_ref, *, layer):
    shift = mod_ref[0, 0:1, :]
    scale = mod_ref[0, 1:2, :]
    h = (_rms(x_ref[...]) * g_ref[0] * (1.0 + scale) + shift).astype(BF16)
    lb = _hgrn_lower_bound(lbl_ref[...], layer)
    lb_row = jnp.concatenate([lb[0:1, :], lb[1:2, :]], axis=1)
    ba_row = jnp.concatenate([ba_ref[0, 0:1, :], ba_ref[0, 1:2, :]], axis=1)
    for n in range(PROJ_COLS // PROJ_TN):
        cols = slice(n * PROJ_TN, (n + 1) * PROJ_TN)
        acc = lax.dot_general(h, w_ref[0, cols, :], _NT, preferred_element_type=F32)
        if n * PROJ_TN == HGRN_GATE_COL:
            acc = lb_row + (1.0 - lb_row) * _sigmoid(acc)
        if n * PROJ_TN == GLA_GATE_COL:
            z = acc[:, 0:W_GRP] + ba_row
            acc = jnp.concatenate([_log_sigmoid(z) * (1.0 / GLA_NORMALIZER), acc[:, W_GRP:]], axis=1)
        o_ref[:, cols] = acc


def _mod_index(ntok, nmod, tile):
    tiles_per_mod = (ntok // tile) // nmod if nmod > 1 else 1
    if nmod > 1:
        return lambda i, *_: (i // tiles_per_mod, 0, 0)
    return lambda i, *_: (0, 0, 0)


def _project(x, mod, norm_g, w_cat, lb_logits, ba2, layer):
    ntok = x.shape[0]
    return pl.pallas_call(
        functools.partial(_proj_kernel, layer=layer),
        grid=(ntok // PROJ_TM,),
        in_specs=[pl.BlockSpec((PROJ_TM, D_MODEL), lambda i: (i, 0)),
                  pl.BlockSpec((1, 6, D_MODEL), _mod_index(ntok, mod.shape[0], PROJ_TM)),
                  pl.BlockSpec((1, 1, D_MODEL), lambda i: (layer, 0, 0)),
                  pl.BlockSpec((1, PROJ_COLS, D_MODEL), lambda i: (layer, 0, 0),
                               pipeline_mode=pl.Buffered(1)),
                  _full_spec(lb_logits),
                  pl.BlockSpec((1, 2, N_HEADS * GLA_DK), lambda i: (layer, 0, 0))],
        out_specs=pl.BlockSpec((PROJ_TM, PROJ_COLS), lambda i: (i, 0)),
        out_shape=jax.ShapeDtypeStruct((ntok, PROJ_COLS), F32),
        compiler_params=_params(("arbitrary",)),
        name="norm_in_proj",
    )(x, mod, norm_g.reshape(DEPTH, 1, D_MODEL), w_cat, lb_logits, ba2)


def _mlp_kernel(x_ref, oa_ref, ob_ref, oc_ref, od_ref, mod_ref, g2_ref, fg_ref, wout_ref,
                w1_ref, w2_ref, out_ref, h2_scr, *, final_norm):
    j = pl.program_id(1)

    @pl.when(j == 0)
    def _():
        wout = wout_ref[0].astype(BF16)
        nslab = TOK_TILE // PROLOGUE_ROWS
        for s in range(nslab):
            rows = slice(s * PROLOGUE_ROWS, (s + 1) * PROLOGUE_ROWS)
            mixed = jnp.concatenate([oa_ref[rows, :], ob_ref[rows, :], oc_ref[rows, :], od_ref[rows, :]],
                                    axis=1)
            mix = jnp.dot(mixed, wout, preferred_element_type=F32)
            x1 = x_ref[rows, :] + mod_ref[0, 2:3, :] * mix
            out_ref[rows, :] = x1
            h2 = _rms(x1) * g2_ref[0] * (1.0 + mod_ref[0, 4:5, :]) + mod_ref[0, 3:4, :]
            h2_scr[rows, :] = h2.astype(BF16)

    w1 = w1_ref[0].astype(BF16)
    w2 = w2_ref[0].astype(BF16)
    for r in range(MLP_ROW_SPLIT):
        rows = slice(r * (TOK_TILE // MLP_ROW_SPLIT), (r + 1) * (TOK_TILE // MLP_ROW_SPLIT))
        t = jnp.dot(h2_scr[rows, :], w1, preferred_element_type=F32)
        t = jnp.square(jnp.maximum(t, 0.0)).astype(BF16)
        out_ref[rows, :] += mod_ref[0, 5:6, :] * jnp.dot(t, w2, preferred_element_type=F32)

    if final_norm:
        @pl.when(j == pl.num_programs(1) - 1)
        def _():
            out_ref[...] = _rms(out_ref[...]) * fg_ref[...]


def _out_mlp(x, o_a, o_b, o_c, o_d, mod, norm2_g, final_g, w_out, w1, w2, layer, final_norm):
    ntok = x.shape[0]
    tok = lambda i, j: (i, 0)
    return pl.pallas_call(
        functools.partial(_mlp_kernel, final_norm=final_norm),
        grid=(ntok // TOK_TILE, D_FF // FF_TILE),
        in_specs=[pl.BlockSpec((TOK_TILE, D_MODEL), tok),
                  pl.BlockSpec((TOK_TILE, W_GRP), tok),
                  pl.BlockSpec((TOK_TILE, W_GRP), tok),
                  pl.BlockSpec((TOK_TILE, W_GRP), tok),
                  pl.BlockSpec((TOK_TILE, W_GRP), tok),
                  pl.BlockSpec((1, 6, D_MODEL), _mod_index(ntok, mod.shape[0], TOK_TILE)),
                  pl.BlockSpec((1, 1, D_MODEL), lambda i, j: (layer, 0, 0)),
                  pl.BlockSpec((1, D_MODEL), lambda i, j: (0, 0)),
                  pl.BlockSpec((1, D_MODEL, D_MODEL), lambda i, j: (layer, 0, 0)),
                  pl.BlockSpec((1, D_MODEL, FF_TILE), lambda i, j: (layer, 0, j)),
                  pl.BlockSpec((1, FF_TILE, D_MODEL), lambda i, j: (layer, j, 0))],
        out_specs=pl.BlockSpec((TOK_TILE, D_MODEL), tok),
        out_shape=jax.ShapeDtypeStruct((ntok, D_MODEL), F32),
        scratch_shapes=[pltpu.VMEM((TOK_TILE, D_MODEL), BF16)],
        compiler_params=_params(("arbitrary", "arbitrary")),
        name="out_proj_mlp",
    )(x, o_a, o_b, o_c, o_d, mod, norm2_g.reshape(DEPTH, 1, D_MODEL), final_g.reshape(1, D_MODEL),
      w_out, w1, w2)


def _split3(x):
    hi = x.astype(BF16)
    r1 = x - hi.astype(F32)
    mid = r1.astype(BF16)
    lo = (r1 - mid.astype(F32)).astype(BF16)
    return hi, mid, lo


def _dot_exact_rhs(x, m):
    hi, mid, lo = _split3(x)
    s = jnp.dot(hi, m, preferred_element_type=F32)
    s += jnp.dot(mid, m, preferred_element_type=F32)
    s += jnp.dot(lo, m, preferred_element_type=F32)
    return s


def _block_cumsum(tri, g):
    hi = g.astype(BF16)
    lo = (g - hi.astype(F32)).astype(BF16)
    return jnp.dot(tri, hi, preferred_element_type=F32) + jnp.dot(tri, lo, preferred_element_type=F32)


def _run_prepass(pre, nblock, width):
    rmax = jnp.zeros((1, width), F32)
    if nblock % 2:
        for bi in range(nblock):
            rmax = pre(bi, rmax)
        return rmax
    if nblock == 2:
        return pre(1, pre(0, rmax))
    return lax.fori_loop(0, nblock // 2, lambda i, rm: pre(2 * i + 1, pre(2 * i, rm)), rmax)


def _half_ranges(cum, reverse):
    if reverse:
        second = -cum[HALF:HALF + 1, :]
        first = -(cum[0:1, :] - cum[HALF:HALF + 1, :])
    else:
        first = -cum[HALF - 1:HALF, :]
        second = -(cum[CHUNK - 1:CHUNK, :] - cum[HALF - 1:HALF, :])
    return jnp.maximum(first, second)


def _block_ranges(cum, reverse, rm):
    for c in range(GROUP):
        rm = jnp.maximum(rm, _half_ranges(cum[c * CHUNK:(c + 1) * CHUNK, :], reverse))
    return rm


def _row_start(c):
    r0 = c * CHUNK
    return r0 if isinstance(r0, int) else pl.multiple_of(r0, CHUNK)


class _Rec:
    def __init__(self, q_at, k_at, v_ref, cum_scr, st_scr, oacc, consts, width, seq_len, nsub):
        self.q_at, self.k_at, self.v_ref = q_at, k_at, v_ref
        self.cum_scr, self.st_scr, self.oacc = cum_scr, st_scr, oacc
        (self.tri_ref, self.bdw_b, self.bdw_f, self.bdv_b, self.cmask, self.bmat,
         self.tile_t) = consts
        self.width, self.seq_len, self.nsub = width, seq_len, nsub


def _chunk_step(rec, sub, d, c, fast):
    reverse = d == 1
    mid_row = HALF if reverse else HALF - 1
    last_row = 0 if reverse else CHUNK - 1
    sd = sub * 2 + d
    r0 = sub * rec.seq_len + c * CHUNK
    if not isinstance(r0, int):
        r0 = pl.multiple_of(r0, CHUNK)
    rows = pl.ds(r0, CHUNK)
    q = rec.q_at(r0)
    k = rec.k_at(d, r0)
    v_b = rec.v_ref[rows, :].astype(BF16)
    cum = rec.cum_scr[d, rows, :]
    tot = cum[last_row:last_row + 1, :]
    st = rec.st_scr[sd]
    o = _dot_nt(q * jnp.exp(cum), st)
    if fast:
        cm = cum[mid_row:mid_row + 1, :]
        qm = q * jnp.exp(cum - cm)
        km_b = (k * jnp.exp(cm - cum)).astype(BF16)
        kbd = jnp.concatenate([km_b] * N_HEADS, axis=0) * rec.bdw_b[...]
        sc = lax.dot_general(qm.astype(BF16), kbd, _NT, preferred_element_type=F32)
        a = jnp.where(rec.cmask[d] > 0.5, sc, 0.0)
        vexp = jnp.concatenate([v_b] * N_HEADS, axis=0) * rec.bdv_b[...]
        o = o + jnp.dot(a.astype(BF16), vexp, preferred_element_type=F32)
    else:
        row_id = lax.broadcasted_iota(jnp.int32, (CHUNK, rec.width), 0)

        def key_row(j, acc):
            krow = rec.k_at(d, r0 + j, 1)
            crow = rec.cum_scr[d, pl.ds(r0 + j, 1), :]
            vrow = rec.v_ref[pl.ds(r0 + j, 1), :]
            keep = (row_id <= j) if reverse else (row_id >= j)
            e = jnp.exp(jnp.minimum(cum - crow, 0.0))
            p = jnp.where(keep, q * krow * e, 0.0)
            return acc + _dot(p, rec.bmat[...]) * vrow

        o = lax.fori_loop(0, CHUNK, key_row, o)
    ke_b = (k * jnp.exp(tot - cum)).astype(BF16)
    ds = lax.dot_general(v_b, ke_b, _TN, preferred_element_type=F32)
    rec.st_scr[sd] = st * jnp.exp(tot) + ds * rec.bdw_f[...]
    rec.oacc[d, rows, :] = o


def _load_state(rec, s0_ref, sub, d):
    x = jnp.concatenate([s0_ref[sub, 0, d, h] for h in range(N_HEADS)], axis=0)
    y = _dot_exact_rhs(x, rec.tile_t[...]) * rec.bmat[...].astype(F32)
    rec.st_scr[sub * 2 + d] = y.T


def _store_state(rec, sfin_ref, sub, d):
    y = rec.st_scr[sub * 2 + d].T
    half = y[:, 0:LANES] + y[:, LANES:2 * LANES]
    x = (half + pltpu.roll(half, HEAD_V, 1))[:, 0:HEAD_V]
    dk = rec.width // N_HEADS
    for h in range(N_HEADS):
        sfin_ref[sub, d, h] = x[h * dk:(h + 1) * dk, :]


def _run_recurrence(rec, ok, s0_ref, sfin_ref):
    nsub = rec.nsub
    nchunk = rec.seq_len // CHUNK
    gsize = min(MAIN_GROUP, nchunk)
    ngroup = nchunk // gsize
    spp = max(1, min(nsub, MAIN_GROUP // gsize))
    nsp = nsub // spp
    for sub in range(nsub):
        for d in (0, 1):
            if s0_ref is None:
                rec.st_scr[sub * 2 + d] = jnp.zeros(rec.st_scr.shape[1:], F32)
            else:
                _load_state(rec, s0_ref, sub, d)

    def fast_all():
        def step(i, carry):
            sp = i // ngroup if (nsp > 1 and ngroup > 1) else (i if nsp > 1 else 0)
            gi = i % ngroup if (nsp > 1 and ngroup > 1) else (i if ngroup > 1 else 0)
            for u in range(gsize):
                s = gi * gsize + u
                for j in range(spp):
                    sub = sp * spp + j
                    _chunk_step(rec, sub, 0, s, True)
                    _chunk_step(rec, sub, 1, nchunk - 1 - s, True)
            return carry

        if nsp * ngroup == 1:
            step(0, 0)
        else:
            lax.fori_loop(0, nsp * ngroup, step, 0)

    def direct_all():
        def one(i, carry):
            sub = i // nchunk if nsub > 1 else 0
            s = i % nchunk if nsub > 1 else i
            _chunk_step(rec, sub, 0, s, False)
            _chunk_step(rec, sub, 1, nchunk - 1 - s, False)
            return carry

        lax.fori_loop(0, nsub * nchunk, one, 0)

    lax.cond(ok, fast_all, direct_all)
    if sfin_ref is not None:
        for sub in range(nsub):
            for d in (0, 1):
                _store_state(rec, sfin_ref, sub, d)


def _head_norm_gate(rec, gate_ref, gain_ref, o_ref):
    for rb in range(rec.nsub * rec.seq_len // ROW_BLOCK):
        rows = slice(rb * ROW_BLOCK, (rb + 1) * ROW_BLOCK)
        o = rec.oacc[0, rows, :] + rec.oacc[1, rows, :]
        ms = jnp.dot((o * o).astype(BF16), rec.bdv_b[...], preferred_element_type=F32) * (1.0 / HEAD_V)
        y = o * lax.rsqrt(ms + EPS) * gain_ref[0] * _silu(gate_ref[rows, :])
        o_ref[rows, :] = y.astype(o_ref.dtype)


def _split_refs(refs, n_in, has_s0, want_final):
    ins = refs[:n_in]
    pos = n_in
    s0_ref = None
    if has_s0:
        s0_ref = refs[pos]
        pos += 1
    o_ref = refs[pos]
    pos += 1
    sfin_ref = None
    if want_final:
        sfin_ref = refs[pos]
        pos += 1
    return ins, s0_ref, o_ref, sfin_ref, refs[pos:]


def _hgrn_kernel(*refs, seq_len, nsub, has_s0, want_final):
    ins, s0_ref, o_ref, sfin_ref, scr = _split_refs(refs, 13, has_s0, want_final)
    q_ref, i_ref, ff_ref, fb_ref, g_ref, gain_ref = ins[:6]
    k_scr, cum_scr, oacc, st_scr = scr
    q_at = lambda r0: q_ref[pl.ds(r0, CHUNK), :]
    k_at = lambda d, r0, n=CHUNK: k_scr[d, pl.ds(r0, n), :]
    rec = _Rec(q_at, k_at, i_ref, cum_scr, st_scr, oacc, ins[6:], W_GRP, seq_len, nsub)

    def pre(bi, rm):
        r0 = bi * GROWS if isinstance(bi, int) else pl.multiple_of(bi * GROWS, GROWS)
        rows = pl.ds(r0, GROWS)
        for d, f_ref in ((0, ff_ref), (1, fb_ref)):
            f = f_ref[rows, :]
            g = jnp.log(jnp.maximum(f, F_FLOOR))
            k_scr[d, rows, :] = 1.0 - f
            cum = _block_cumsum(rec.tri_ref[d], g)
            cum_scr[d, rows, :] = cum
            rm = _block_ranges(cum, d == 1, rm)
        return rm

    rmax = _run_prepass(pre, nsub * seq_len // GROWS, W_GRP)
    ok = jnp.max(rmax) < SAFE_RANGE

    _run_recurrence(rec, ok, s0_ref, sfin_ref)
    _head_norm_gate(rec, g_ref, gain_ref, o_ref)


def _gla_kernel(*refs, seq_len, nsub, has_s0, want_final):
    ins, s0_ref, o_ref, sfin_ref, scr = _split_refs(refs, 14, has_s0, want_final)
    q_ref, k_ref, v_ref, g_ref, gf_ref, gb_ref, gain_ref = ins[:7]
    cum_scr, oacc, st_scr = scr
    width = N_HEADS * GLA_DK
    q_at = lambda r0: q_ref[pl.ds(r0, CHUNK), :] * (GLA_DK ** -0.5)
    k_at = lambda d, r0, n=CHUNK: k_ref[pl.ds(r0, n), :]
    rec = _Rec(q_at, k_at, v_ref, cum_scr, st_scr, oacc, ins[7:], width, seq_len, nsub)

    def pre(bi, rm):
        r0 = bi * GROWS if isinstance(bi, int) else pl.multiple_of(bi * GROWS, GROWS)
        rows = pl.ds(r0, GROWS)
        for d, logdecay_ref in ((0, gf_ref), (1, gb_ref)):
            g = logdecay_ref[rows, :]
            cum = _block_cumsum(rec.tri_ref[d], g)
            cum_scr[d, rows, :] = cum
            rm = _block_ranges(cum, d == 1, rm)
        return rm

    rmax = _run_prepass(pre, nsub * seq_len // GROWS, width)
    ok = jnp.max(rmax) < SAFE_RANGE

    _run_recurrence(rec, ok, s0_ref, sfin_ref)
    _head_norm_gate(rec, g_ref, gain_ref, o_ref)


def _recurrence_constants(width):
    dk = width // N_HEADS
    r = np.arange(N_HEADS * CHUNK)[:, None]
    tri = np.kron(np.eye(GROUP, dtype=np.float32), np.tril(np.ones((CHUNK, CHUNK), np.float32)))
    tri = np.stack([tri, tri.T])
    bdw = (r // CHUNK == np.arange(width)[None, :] // dk).astype(np.float32)
    bdv = (r // CHUNK == np.arange(W_GRP)[None, :] // HEAD_V).astype(np.float32)
    t = np.arange(CHUNK)[:, None]
    s = np.arange(N_HEADS * CHUNK)[None, :] % CHUNK
    cmask = np.stack([(s <= t), (s >= t)]).astype(np.float32)
    bmat = (np.arange(width)[:, None] // dk == np.arange(W_GRP)[None, :] // HEAD_V).astype(np.float32)
    tile_t = np.tile(np.eye(HEAD_V, dtype=np.float32), (1, N_HEADS))
    return (jnp.asarray(tri, BF16), jnp.asarray(bdw, BF16), jnp.asarray(bdw), jnp.asarray(bdv, BF16),
            jnp.asarray(cmask), jnp.asarray(bmat, BF16), jnp.asarray(tile_t, BF16))


def _recurrent_mixer(kernel_fn, name, width, args, in_specs, state, layer, nseq, seq_len, want_final,
                     extra_scratch):
    dk = width // N_HEADS
    nsub = _seqs_per_block(nseq, seq_len)
    rows = nsub * seq_len
    consts = _recurrence_constants(width)
    args = list(args) + list(consts)
    in_specs = list(in_specs) + [_full_spec(c) for c in consts]
    if state is not None:
        args.append(state)
        in_specs.append(pl.BlockSpec((nsub, 1, 2, N_HEADS, dk, HEAD_V), lambda b: (b, layer, 0, 0, 0, 0)))
    out_shape = [jax.ShapeDtypeStruct((nseq * seq_len, W_GRP), BF16)]
    out_specs = [pl.BlockSpec((rows, W_GRP), lambda b: (b, 0))]
    if want_final:
        out_shape.append(jax.ShapeDtypeStruct((nseq, 2, N_HEADS, dk, HEAD_V), F32))
        out_specs.append(pl.BlockSpec((nsub, 2, N_HEADS, dk, HEAD_V), lambda b: (b, 0, 0, 0, 0)))
    res = pl.pallas_call(
        functools.partial(kernel_fn, seq_len=seq_len, nsub=nsub, has_s0=state is not None,
                          want_final=want_final),
        grid=(nseq // nsub,),
        in_specs=in_specs,
        out_specs=out_specs,
        out_shape=out_shape,
        scratch_shapes=[pltpu.VMEM((2, rows, W_GRP), F32)] * extra_scratch + [
            pltpu.VMEM((2, rows, width), F32), pltpu.VMEM((2, rows, W_GRP), F32),
            pltpu.VMEM((2 * nsub, W_GRP, width), F32)],
        compiler_params=_params(("arbitrary",)),
        name=name,
    )(*args)
    return res[0], (res[1] if want_final else None)


def _seqs_per_block(nseq, seq_len):
    return max(1, min(nseq, BLOCK_ROWS // seq_len))


def _hgrn_mixer(proj, gain, state, layer, nseq, seq_len, want_final):
    rows = _seqs_per_block(nseq, seq_len) * seq_len
    col = lambda c: pl.BlockSpec((rows, W_GRP), lambda b: (b, c))
    args = [proj] * 5 + [gain.reshape(DEPTH, 1, W_GRP)]
    in_specs = [col(0), col(1), col(2), col(3), col(4),
                pl.BlockSpec((1, 1, W_GRP), lambda b: (layer, 0, 0))]
    return _recurrent_mixer(_hgrn_kernel, "hgrn2_mixer", W_GRP, args, in_specs, state, layer, nseq,
                            seq_len, want_final, 1)


def _gla_mixer(proj, gain, state, layer, nseq, seq_len, want_final):
    width = N_HEADS * GLA_DK
    rows = _seqs_per_block(nseq, seq_len) * seq_len
    col256 = lambda c: pl.BlockSpec((rows, W_GRP), lambda b: (b, c))
    col128 = lambda c: pl.BlockSpec((rows, width), lambda b: (b, c))
    args = [proj] * 6 + [gain.reshape(DEPTH, 1, W_GRP)]
    in_specs = [col128(10), col128(11), col256(6), col256(7), col128(16), col128(17),
                pl.BlockSpec((1, 1, W_GRP), lambda b: (layer, 0, 0))]
    return _recurrent_mixer(_gla_kernel, "gla_mixer", width, args, in_specs, state, layer, nseq, seq_len,
                            want_final, 0)


PAD = 8
SCAN_TILE = 8
SCAN_UNROLL = 4
GELU_C = 0.7978845608028654


def _rgsc_kernel(*refs, seq_len, nsub, seg, has_h0, want_final):
    (cx_ref, cg_ref, db_ref, dc_ref, dv_ref, convw_ref, convb_ref, wr_ref, br_ref, wi_ref,
     bi_ref, lam_ref, sw_ref) = refs[:13]
    pos = 13
    h0_ref = None
    if has_h0:
        h0_ref = refs[pos]
        pos += 1
    oc_ref, od_ref = refs[pos:pos + 2]
    pos += 2
    hfin_ref = None
    if want_final:
        hfin_ref = refs[pos]
        pos += 1
    upad, a_scr, b_scr, h_scr = refs[pos:]

    slot = seq_len + 2 * PAD
    zpad = jnp.zeros((PAD, W_GRP), F32)
    for sub in range(nsub):
        upad[sub * slot:sub * slot + PAD, :] = zpad
        upad[sub * slot + PAD + seq_len:(sub + 1) * slot, :] = zpad
        upad[sub * slot + PAD:sub * slot + PAD + seq_len, :] = cx_ref[sub * seq_len:(sub + 1) * seq_len, :]
    rows8 = lax.broadcasted_iota(jnp.int32, (SCAN_TILE, W_GRP), 0)
    ntile = seq_len // SCAN_TILE

    for d in (0, 1):
        reverse = d == 1
        w = convw_ref[0, d]
        offs = [PAD + 3, PAD + 2, PAD + 1, PAD] if reverse else [PAD - 3, PAD - 2, PAD - 1, PAD]
        sp = _softplus(-lam_ref[0, d:d + 1, :])
        for sub in range(nsub):
            for rb in range(seq_len // ROW_BLOCK):
                src = sub * slot + rb * ROW_BLOCK
                base = sub * seq_len + rb * ROW_BLOCK
                xc = convb_ref[0, d:d + 1, :]
                for j in range(4):
                    xc = xc + w[j:j + 1, :] * upad[offs[j] + src:offs[j] + src + ROW_BLOCK, :]
                r = _sigmoid(_dot(xc, wr_ref[d]) + br_ref[0, d:d + 1, :])
                ig = _sigmoid(_dot(xc, wi_ref[d]) + bi_ref[0, d:d + 1, :])
                log_a = -RG_C * r * sp
                a = jnp.exp(log_a)
                a_scr[base:base + ROW_BLOCK, :] = a
                y = 1.0 - a * a
                root = jnp.where(y > 0.0, y * lax.rsqrt(y), 0.0)
                b_scr[base:base + ROW_BLOCK, :] = root * (ig * xc)

        if has_h0:
            h0 = tuple(h0_ref[sub, 0, d:d + 1, :] for sub in range(nsub))
        else:
            h0 = tuple(jnp.zeros((1, W_GRP), F32) for _ in range(nsub))

        def step(ti, carry, d=d, reverse=reverse):
            t = (ntile - 1 - ti) if reverse else ti
            out = []
            for sub in range(nsub):
                r0 = pl.multiple_of(sub * seq_len + t * SCAN_TILE, SCAN_TILE)
                a = a_scr[pl.ds(r0, SCAN_TILE), :]
                b = b_scr[pl.ds(r0, SCAN_TILE), :]
                for s in (1, 2, 4):
                    if reverse:
                        keep = rows8 <= SCAN_TILE - 1 - s
                        shift = SCAN_TILE - s
                    else:
                        keep = rows8 >= s
                        shift = s
                    a_s = jnp.where(keep, pltpu.roll(a, shift, 0), 1.0)
                    b_s = jnp.where(keep, pltpu.roll(b, shift, 0), 0.0)
                    b = b + a * b_s
                    a = a * a_s
                h = a * carry[sub] + b
                if d == 0:
                    h_scr[pl.ds(r0, SCAN_TILE), :] = h
                else:
                    h_scr[pl.ds(r0, SCAN_TILE), :] += h
                out.append(h[0:1, :] if reverse else h[SCAN_TILE - 1:SCAN_TILE, :])
            return tuple(out)

        hlast = lax.fori_loop(0, ntile, step, h0, unroll=SCAN_UNROLL // nsub if nsub < SCAN_UNROLL else 1)
        if want_final:
            for sub in range(nsub):
                hfin_ref[sub, d:d + 1, :] = hlast[sub]

    sw = sw_ref[0]
    nblock = nsub * seq_len // ROW_BLOCK
    for rb in range(nblock):
        rows = slice(rb * ROW_BLOCK, (rb + 1) * ROW_BLOCK)
        x = cg_ref[rows, :]
        gelu = x * (0.5 + 0.5 * jnp.tanh(x * (GELU_C + (GELU_C * 0.044715) * (x * x))))
        oc_ref[rows, :] = (h_scr[rows, :] * gelu).astype(oc_ref.dtype)
        upad[PAD + rb * ROW_BLOCK:PAD + (rb + 1) * ROW_BLOCK, :] = dc_ref[rows, :] * dv_ref[rows, :]
    for rb in range(nblock):
        base = rb * ROW_BLOCK
        rows = slice(base, base + ROW_BLOCK)
        posn = (lax.broadcasted_iota(jnp.int32, (ROW_BLOCK, W_GRP), 0) + base) % seg
        left = jnp.where(posn != 0, upad[PAD - 1 + base:PAD - 1 + base + ROW_BLOCK, :], 0.0)
        right = jnp.where(posn != seg - 1, upad[PAD + 1 + base:PAD + 1 + base + ROW_BLOCK, :], 0.0)
        y = sw[0:1, :] * left + sw[1:2, :] * upad[PAD + base:PAD + base + ROW_BLOCK, :] + sw[2:3, :] * right
        od_ref[rows, :] = (db_ref[rows, :] * y).astype(od_ref.dtype)


def _rgsc_mixer(proj, conv_w, conv_b, wr_bd, b_r, wi_bd, b_i, lam, sconv_w, h0, layer, nseq, seq_len,
                seg, want_final):
    nsub = _seqs_per_block(nseq, seq_len)
    rows = nsub * seq_len
    col = lambda c: pl.BlockSpec((rows, W_GRP), lambda b: (b, c))
    lay3 = lambda a: pl.BlockSpec((1,) + a.shape[1:], lambda b: (layer, 0, 0))
    lay4 = lambda a: pl.BlockSpec((1,) + a.shape[1:], lambda b: (layer, 0, 0, 0))
    params = [conv_w, conv_b, wr_bd, b_r, wi_bd, b_i, lam, sconv_w]
    args = [proj] * 5 + params
    in_specs = [col(9), col(10), col(11), col(12), col(13), lay4(conv_w), lay3(conv_b), _full_spec(wr_bd),
                lay3(b_r), _full_spec(wi_bd), lay3(b_i), lay3(lam), lay3(sconv_w)]
    if h0 is not None:
        args.append(h0)
        in_specs.append(pl.BlockSpec((nsub, 1, 2, W_GRP), lambda b: (b, layer, 0, 0)))
    out_shape = [jax.ShapeDtypeStruct((nseq * seq_len, W_GRP), BF16)] * 2
    out_specs = [pl.BlockSpec((rows, W_GRP), lambda b: (b, 0))] * 2
    if want_final:
        out_shape.append(jax.ShapeDtypeStruct((nseq, 2, W_GRP), F32))
        out_specs.append(pl.BlockSpec((nsub, 2, W_GRP), lambda b: (b, 0, 0)))
    res = pl.pallas_call(
        functools.partial(_rgsc_kernel, seq_len=seq_len, nsub=nsub, seg=seg, has_h0=h0 is not None,
                          want_final=want_final),
        grid=(nseq // nsub,),
        in_specs=in_specs,
        out_specs=out_specs,
        out_shape=out_shape,
        scratch_shapes=[pltpu.VMEM((nsub * (seq_len + 2 * PAD), W_GRP), F32), pltpu.VMEM((rows, W_GRP), F32),
                        pltpu.VMEM((rows, W_GRP), F32), pltpu.VMEM((rows, W_GRP), F32)],
        compiler_params=_params(("arbitrary",)),
        name="rglru_sconv_mixer",
    )(*args)
    return res[0], res[1], (res[2] if want_final else None)


def _block_diag(w):
    two, h, c, _ = w.shape
    eye = jnp.eye(h, dtype=w.dtype)
    full = w[:, :, :, None, :] * eye[None, :, None, :, None]
    return full.reshape(two, h * c, h * c)


def kernel(x_prompt, x_sample, state_hgrn, state_gla, state_rglru, c, c_ctx, norm1_g, norm2_g, ada_w, ada_b, w_in, w_out, hgrn_lb_logits, hgrn_norm_g, gla_wa2, gla_ba2, gla_norm_g, rg_conv_w, rg_conv_b, rg_w_r, rg_b_r, rg_w_i, rg_b_i, rg_lambda, sconv_w, mlp_w1, mlp_w2, final_norm_g):
    b_ctx, l_ctx, _ = x_prompt.shape
    b_lat, l_lat, _ = x_sample.shape

    cvec = jnp.concatenate([c, c_ctx[None, :], jnp.zeros((8 - b_lat - 1, D_MODEL), F32)], axis=0)
    mod = _modulation(cvec, ada_w, ada_b).reshape(DEPTH, 8, 6, D_MODEL)

    w_cat = _build_proj_weight(w_in, gla_wa2)

    xp = x_prompt.reshape(b_ctx * l_ctx, D_MODEL)
    xs = x_sample.reshape(b_lat * l_lat, D_MODEL)
    new_h, new_g, new_r = [], [], []
    for l in range(DEPTH):
        wr_bd = _block_diag(rg_w_r[l]).astype(BF16)
        wi_bd = _block_diag(rg_w_i[l]).astype(BF16)
        mod_ctx = mod[l, b_lat:b_lat + 1]
        mod_lat = mod[l, :b_lat]
        last = l == DEPTH - 1
        streams = (
            (xp, mod_ctx, b_ctx, l_ctx, l_ctx, None, None, None, True),
            (xs, mod_lat, b_lat, l_lat, GRID_W, state_hgrn, state_gla, state_rglru, False),
        )
        outs = []
        for (x, m, nseq, seq_len, seg, s_h, s_g, s_r, want_final) in streams:
            proj = _project(x, m, norm1_g, w_cat, hgrn_lb_logits, gla_ba2, l)
            o_a, f_h = _hgrn_mixer(proj, hgrn_norm_g, s_h, l, nseq, seq_len, want_final)
            o_b, f_g = _gla_mixer(proj, gla_norm_g, s_g, l, nseq, seq_len, want_final)
            o_c, o_d, f_r = _rgsc_mixer(proj, rg_conv_w, rg_conv_b, wr_bd, rg_b_r, wi_bd, rg_b_i,
                                        rg_lambda, sconv_w, s_r, l, nseq, seq_len, seg, want_final)
            x_new = _out_mlp(x, o_a, o_b, o_c, o_d, m, norm2_g, final_norm_g, w_out, mlp_w1, mlp_w2, l, last)
            outs.append((x_new, f_h, f_g, f_r))
        xp, f_h, f_g, f_r = outs[0]
        xs = outs[1][0]
        new_h.append(f_h)
        new_g.append(f_g)
        new_r.append(f_r)
    y_prompt = xp.reshape(b_ctx, l_ctx, D_MODEL)
    y_sample = xs.reshape(b_lat, l_lat, D_MODEL)
    return (y_prompt, y_sample, jnp.stack(new_h, axis=1), jnp.stack(new_g, axis=1),
            jnp.stack(new_r, axis=1))
```

```python
import functools

import numpy as np
import jax
import jax.numpy as jnp
from jax import lax
from jax.experimental import pallas as pl
from jax.experimental.pallas import tpu as pltpu

F32 = jnp.float32
BF16 = jnp.bfloat16

D_MODEL = 1024
DEPTH = 2
GRID_W = 64
N_HEADS = 4
W_GRP = 256
HEAD_V = 64
HGRN_DK = 64
GLA_DK = 32
GLA_RANK = 16
GLA_NORMALIZER = 16.0
RG_C = 8.0
D_FF = 4 * D_MODEL
EPS = 1e-6
F_FLOOR = 1e-20

PROJ_COLS = 3584

CHUNK = 64
HALF = CHUNK // 2
assert CHUNK == HEAD_V
GROUP = 4
GROWS = GROUP * CHUNK
MAIN_GROUP = 16
BLOCK_ROWS = 1024
LANES = 128
SAFE_RANGE = 80.0
TOK_TILE = 1024
PROJ_TM = 512
PROJ_TN = 512
FF_TILE = 1024
MLP_ROW_SPLIT = 2
PROLOGUE_ROWS = 256
ROW_BLOCK = 256
VMEM_LIMIT = 56 * 1024 * 1024

_NT = (((1,), (1,)), ((), ()))
_TN = (((0,), (0,)), ((), ()))


def _dot(a, b):
    return jnp.dot(a.astype(BF16), b.astype(BF16), preferred_element_type=F32)


def _dot_nt(a, b):
    return lax.dot_general(a.astype(BF16), b.astype(BF16), _NT, preferred_element_type=F32)


def _dot_tn(a, b):
    return lax.dot_general(a.astype(BF16), b.astype(BF16), _TN, preferred_element_type=F32)


def _sigmoid(x):
    return 1.0 / (1.0 + jnp.exp(-x))


def _silu(x):
    return x * _sigmoid(x)


def _log1p(y):
    u = 1.0 + y
    return jnp.where(u == 1.0, y, jnp.log(u) * (y / (u - 1.0)))


def _softplus(x):
    return jnp.maximum(x, 0.0) + _log1p(jnp.exp(-jnp.abs(x)))


def _log_sigmoid(z):
    return jnp.minimum(z, 0.0) - jnp.log(1.0 + jnp.exp(-jnp.abs(z)))


def _rms(x):
    return x * lax.rsqrt(jnp.mean(x * x, axis=-1, keepdims=True) + EPS)


def _params(sem):
    return pltpu.CompilerParams(dimension_semantics=sem, vmem_limit_bytes=VMEM_LIMIT)


def _full_spec(a):
    zeros = (0,) * a.ndim
    return pl.BlockSpec(a.shape, lambda *_: zeros)


def _mod_kernel(c_ref, w_ref, b_ref, o_ref):
    s = _silu(c_ref[...])
    o_ref[0] = _dot(s, w_ref[0]) + b_ref[0]


def _modulation(cvec, ada_w, ada_b):
    tn = 2048
    return pl.pallas_call(
        _mod_kernel,
        grid=(DEPTH, 6 * D_MODEL // tn),
        in_specs=[pl.BlockSpec((8, D_MODEL), lambda l, j: (0, 0)),
                  pl.BlockSpec((1, D_MODEL, tn), lambda l, j: (l, 0, j)),
                  pl.BlockSpec((1, 1, tn), lambda l, j: (l, 0, j))],
        out_specs=pl.BlockSpec((1, 8, tn), lambda l, j: (l, 0, j)),
        out_shape=jax.ShapeDtypeStruct((DEPTH, 8, 6 * D_MODEL), F32),
        compiler_params=_params(("arbitrary", "arbitrary")),
        name="adaln_modulation",
    )(cvec, ada_w, ada_b.reshape(DEPTH, 1, 6 * D_MODEL))


LOWRANK_COL = 2048
TAIL_COL = LOWRANK_COL + 2 * GLA_RANK
TAIL_WIDTH = 1280
WCAT_LANES = 256


def _wcat_kernel(w_ref, wa2t_ref, o_ref):
    o_ref[0, 0:LOWRANK_COL, :] = w_ref[0, 0:LOWRANK_COL, :].astype(BF16)
    z = jnp.dot(wa2t_ref[0], w_ref[0, LOWRANK_COL:TAIL_COL, :], preferred_element_type=F32,
                precision=lax.Precision.HIGHEST)
    o_ref[0, LOWRANK_COL:LOWRANK_COL + 2 * LANES, :] = z.astype(BF16)
    o_ref[0, LOWRANK_COL + 2 * LANES:PROJ_COLS, :] = w_ref[0, TAIL_COL:TAIL_COL + TAIL_WIDTH, :].astype(BF16)


def _build_proj_weight(w_in, wa2):
    ncol = w_in.shape[2]
    assert ncol == TAIL_COL + TAIL_WIDTH and LOWRANK_COL + 2 * LANES + TAIL_WIDTH == PROJ_COLS
    n = N_HEADS * GLA_DK
    wa2t = jnp.zeros((DEPTH, 2 * n, 2 * GLA_RANK), F32)
    wa2t = wa2t.at[:, 0:n, 0:GLA_RANK].set(jnp.swapaxes(wa2[:, 0], 1, 2))
    wa2t = wa2t.at[:, n:2 * n, GLA_RANK:2 * GLA_RANK].set(jnp.swapaxes(wa2[:, 1], 1, 2))
    w_t = jnp.swapaxes(w_in, 1, 2)
    return pl.pallas_call(
        _wcat_kernel,
        grid=(DEPTH, D_MODEL // WCAT_LANES),
        in_specs=[pl.BlockSpec((1, ncol, WCAT_LANES), lambda l, i: (l, 0, i)),
                  pl.BlockSpec((1, 2 * n, 2 * GLA_RANK), lambda l, i: (l, 0, 0))],
        out_specs=pl.BlockSpec((1, PROJ_COLS, WCAT_LANES), lambda l, i: (l, 0, i)),
        out_shape=jax.ShapeDtypeStruct((DEPTH, PROJ_COLS, D_MODEL), BF16),
        compiler_params=_params(("arbitrary", "arbitrary")),
        name="build_proj_weight",
    )(w_t, wa2t)


HGRN_GATE_COL = 2 * W_GRP
GLA_GATE_COL = LOWRANK_COL
assert HGRN_GATE_COL % PROJ_TN == 0 and GLA_GATE_COL % PROJ_TN == 0 and PROJ_TN == 2 * W_GRP


def _hgrn_lower_bound(logits, layer):
    mx = logits[0]
    for i in range(1, DEPTH):
        mx = jnp.maximum(mx, logits[i])
    ex = [jnp.exp(logits[i] - mx) for i in range(DEPTH)]
    den = ex[0]
    for i in range(1, DEPTH):
        den = den + ex[i]
    sm = [e / den for e in ex]
    csum = sm[0]
    for i in range(1, layer + 1):
        csum = csum + sm[i]
    return csum - sm[0]


def _proj_kernel(x_ref, mod_ref, g_ref, w_ref, lbl_ref, ba_ref, o_ref, *, layer):
    shift = mod_ref[0, 0:1, :]
    scale = mod_ref[0, 1:2, :]
    h = (_rms(x_ref[...]) * g_ref[0] * (1.0 + scale) + shift).astype(BF16)
    lb = _hgrn_lower_bound(lbl_ref[...], layer)
    lb_row = jnp.concatenate([lb[0:1, :], lb[1:2, :]], axis=1)
    ba_row = jnp.concatenate([ba_ref[0, 0:1, :], ba_ref[0, 1:2, :]], axis=1)
    for n in range(PROJ_COLS // PROJ_TN):
        cols = slice(n * PROJ_TN, (n + 1) * PROJ_TN)
        acc = lax.dot_general(h, w_ref[0, cols, :], _NT, preferred_element_type=F32)
        if n * PROJ_TN == HGRN_GATE_COL:
            acc = lb_row + (1.0 - lb_row) * _sigmoid(acc)
        if n * PROJ_TN == GLA_GATE_COL:
            z = acc[:, 0:W_GRP] + ba_row
            acc = jnp.concatenate([_log_sigmoid(z) * (1.0 / GLA_NORMALIZER), acc[:, W_GRP:]], axis=1)
        o_ref[:, cols] = acc


def _mod_index(ntok, nmod, tile):
    tiles_per_mod = (ntok // tile) // nmod if nmod > 1 else 1
    if nmod > 1:
        return lambda i, *_: (i // tiles_per_mod, 0, 0)
    return lambda i, *_: (0, 0, 0)


def _project(x, mod, norm_g, w_cat, lb_logits, ba2, layer):
    ntok = x.shape[0]
    return pl.pallas_call(
        functools.partial(_proj_kernel, layer=layer),
        grid=(ntok // PROJ_TM,),
        in_specs=[pl.BlockSpec((PROJ_TM, D_MODEL), lambda i: (i, 0)),
                  pl.BlockSpec((1, 6, D_MODEL), _mod_index(ntok, mod.shape[0], PROJ_TM)),
                  pl.BlockSpec((1, 1, D_MODEL), lambda i: (layer, 0, 0)),
                  pl.BlockSpec((1, PROJ_COLS, D_MODEL), lambda i: (layer, 0, 0),
                               pipeline_mode=pl.Buffered(1)),
                  _full_spec(lb_logits),
                  pl.BlockSpec((1, 2, N_HEADS * GLA_DK), lambda i: (layer, 0, 0))],
        out_specs=pl.BlockSpec((PROJ_TM, PROJ_COLS), lambda i: (i, 0)),
        out_shape=jax.ShapeDtypeStruct((ntok, PROJ_COLS), F32),
        compiler_params=_params(("arbitrary",)),
        name="norm_in_proj",
    )(x, mod, norm_g.reshape(DEPTH, 1, D_MODEL), w_cat, lb_logits, ba2)


def _mlp_kernel(x_ref, oa_ref, ob_ref, oc_ref, od_ref, mod_ref, g2_ref, fg_ref, wout_ref,
                w1_ref, w2_ref, out_ref, h2_scr, *, final_norm):
    j = pl.program_id(1)

    @pl.when(j == 0)
    def _():
        wout = wout_ref[0].astype(BF16)
        nslab = TOK_TILE // PROLOGUE_ROWS
        for s in range(nslab):
            rows = slice(s * PROLOGUE_ROWS, (s + 1) * PROLOGUE_ROWS)
            mixed = jnp.concatenate([oa_ref[rows, :], ob_ref[rows, :], oc_ref[rows, :], od_ref[rows, :]],
                                    axis=1)
            mix = jnp.dot(mixed, wout, preferred_element_type=F32)
            x1 = x_ref[rows, :] + mod_ref[0, 2:3, :] * mix
            out_ref[rows, :] = x1
            h2 = _rms(x1) * g2_ref[0] * (1.0 + mod_ref[0, 4:5, :]) + mod_ref[0, 3:4, :]
            h2_scr[rows, :] = h2.astype(BF16)

    w1 = w1_ref[0].astype(BF16)
    w2 = w2_ref[0].astype(BF16)
    for r in range(MLP_ROW_SPLIT):
        rows = slice(r * (TOK_TILE // MLP_ROW_SPLIT), (r + 1) * (TOK_TILE // MLP_ROW_SPLIT))
        t = jnp.dot(h2_scr[rows, :], w1, preferred_element_type=F32)
        t = jnp.square(jnp.maximum(t, 0.0)).astype(BF16)
        out_ref[rows, :] += mod_ref[0, 5:6, :] * jnp.dot(t, w2, preferred_element_type=F32)

    if final_norm:
        @pl.when(j == pl.num_programs(1) - 1)
        def _():
            out_ref[...] = _rms(out_ref[...]) * fg_ref[...]


def _out_mlp(x, o_a, o_b, o_c, o_d, mod, norm2_g, final_g, w_out, w1, w2, layer, final_norm):
    ntok = x.shape[0]
    tok = lambda i, j: (i, 0)
    return pl.pallas_call(
        functools.partial(_mlp_kernel, final_norm=final_norm),
        grid=(ntok // TOK_TILE, D_FF // FF_TILE),
        in_specs=[pl.BlockSpec((TOK_TILE, D_MODEL), tok),
                  pl.BlockSpec((TOK_TILE, W_GRP), tok),
                  pl.BlockSpec((TOK_TILE, W_GRP), tok),
                  pl.BlockSpec((TOK_TILE, W_GRP), tok),
                  pl.BlockSpec((TOK_TILE, W_GRP), tok),
                  pl.BlockSpec((1, 6, D_MODEL), _mod_index(ntok, mod.shape[0], TOK_TILE)),
                  pl.BlockSpec((1, 1, D_MODEL), lambda i, j: (layer, 0, 0)),
                  pl.BlockSpec((1, D_MODEL), lambda i, j: (0, 0)),
                  pl.BlockSpec((1, D_MODEL, D_MODEL), lambda i, j: (layer, 0, 0)),
                  pl.BlockSpec((1, D_MODEL, FF_TILE), lambda i, j: (layer, 0, j)),
                  pl.BlockSpec((1, FF_TILE, D_MODEL), lambda i, j: (layer, j, 0))],
        out_specs=pl.BlockSpec((TOK_TILE, D_MODEL), tok),
        out_shape=jax.ShapeDtypeStruct((ntok, D_MODEL), F32),
        scratch_shapes=[pltpu.VMEM((TOK_TILE, D_MODEL), BF16)],
        compiler_params=_params(("arbitrary", "arbitrary")),
        name="out_proj_mlp",
    )(x, o_a, o_b, o_c, o_d, mod, norm2_g.reshape(DEPTH, 1, D_MODEL), final_g.reshape(1, D_MODEL),
      w_out, w1, w2)


def _split3(x):
    hi = x.astype(BF16)
    r1 = x - hi.astype(F32)
    mid = r1.astype(BF16)
    lo = (r1 - mid.astype(F32)).astype(BF16)
    return hi, mid, lo


def _dot_exact_rhs(x, m):
    hi, mid, lo = _split3(x)
    s = jnp.dot(hi, m, preferred_element_type=F32)
    s += jnp.dot(mid, m, preferred_element_type=F32)
    s += jnp.dot(lo, m, preferred_element_type=F32)
    return s


def _block_cumsum(tri, g):
    hi = g.astype(BF16)
    lo = (g - hi.astype(F32)).astype(BF16)
    return jnp.dot(tri, hi, preferred_element_type=F32) + jnp.dot(tri, lo, preferred_element_type=F32)


def _run_prepass(pre, nblock, width):
    rmax = jnp.zeros((1, width), F32)
    if nblock % 2:
        for bi in range(nblock):
            rmax = pre(bi, rmax)
        return rmax
    if nblock == 2:
        return pre(1, pre(0, rmax))
    return lax.fori_loop(0, nblock // 2, lambda i, rm: pre(2 * i + 1, pre(2 * i, rm)), rmax)


def _half_ranges(cum, reverse):
    if reverse:
        second = -cum[HALF:HALF + 1, :]
        first = -(cum[0:1, :] - cum[HALF:HALF + 1, :])
    else:
        first = -cum[HALF - 1:HALF, :]
        second = -(cum[CHUNK - 1:CHUNK, :] - cum[HALF - 1:HALF, :])
    return jnp.maximum(first, second)


def _block_ranges(cum, reverse, rm):
    for c in range(GROUP):
        rm = jnp.maximum(rm, _half_ranges(cum[c * CHUNK:(c + 1) * CHUNK, :], reverse))
    return rm


def _row_start(c):
    r0 = c * CHUNK
    return r0 if isinstance(r0, int) else pl.multiple_of(r0, CHUNK)


class _Rec:
    def __init__(self, q_at, k_at, v_ref, cum_scr, st_scr, oacc, consts, width, seq_len, nsub):
        self.q_at, self.k_at, self.v_ref = q_at, k_at, v_ref
        self.cum_scr, self.st_scr, self.oacc = cum_scr, st_scr, oacc
        (self.tri_ref, self.bdw_b, self.bdw_f, self.bdv_b, self.cmask, self.bmat,
         self.tile_t) = consts
        self.width, self.seq_len, self.nsub = width, seq_len, nsub


def _chunk_step(rec, sub, d, c, fast):
    reverse = d == 1
    mid_row = HALF if reverse else HALF - 1
    last_row = 0 if reverse else CHUNK - 1
    sd = sub * 2 + d
    r0 = sub * rec.seq_len + c * CHUNK
    if not isinstance(r0, int):
        r0 = pl.multiple_of(r0, CHUNK)
    rows = pl.ds(r0, CHUNK)
    q = rec.q_at(r0)
    k = rec.k_at(d, r0)
    v_b = rec.v_ref[rows, :].astype(BF16)
    cum = rec.cum_scr[d, rows, :]
    tot = cum[last_row:last_row + 1, :]
    st = rec.st_scr[sd]
    o = _dot_nt(q * jnp.exp(cum), st)
    if fast:
        cm = cum[mid_row:mid_row + 1, :]
        qm = q * jnp.exp(cum - cm)
        km_b = (k * jnp.exp(cm - cum)).astype(BF16)
        kbd = jnp.concatenate([km_b] * N_HEADS, axis=0) * rec.bdw_b[...]
        sc = lax.dot_general(qm.astype(BF16), kbd, _NT, preferred_element_type=F32)
        a = jnp.where(rec.cmask[d] > 0.5, sc, 0.0)
        vexp = jnp.concatenate([v_b] * N_HEADS, axis=0) * rec.bdv_b[...]
        o = o + jnp.dot(a.astype(BF16), vexp, preferred_element_type=F32)
    else:
        row_id = lax.broadcasted_iota(jnp.int32, (CHUNK, rec.width), 0)

        def key_row(j, acc):
            krow = rec.k_at(d, r0 + j, 1)
            crow = rec.cum_scr[d, pl.ds(r0 + j, 1), :]
            vrow = rec.v_ref[pl.ds(r0 + j, 1), :]
            keep = (row_id <= j) if reverse else (row_id >= j)
            e = jnp.exp(jnp.minimum(cum - crow, 0.0))
            p = jnp.where(keep, q * krow * e, 0.0)
            return acc + _dot(p, rec.bmat[...]) * vrow

        o = lax.fori_loop(0, CHUNK, key_row, o)
    ke_b = (k * jnp.exp(tot - cum)).astype(BF16)
    ds = lax.dot_general(v_b, ke_b, _TN, preferred_element_type=F32)
    rec.st_scr[sd] = st * jnp.exp(tot) + ds * rec.bdw_f[...]
    rec.oacc[d, rows, :] = o


def _load_state(rec, s0_ref, sub, d):
    x = jnp.concatenate([s0_ref[sub, 0, d, h] for h in range(N_HEADS)], axis=0)
    y = _dot_exact_rhs(x, rec.tile_t[...]) * rec.bmat[...].astype(F32)
    rec.st_scr[sub * 2 + d] = y.T


def _store_state(rec, sfin_ref, sub, d):
    y = rec.st_scr[sub * 2 + d].T
    half = y[:, 0:LANES] + y[:, LANES:2 * LANES]
    x = (half + pltpu.roll(half, HEAD_V, 1))[:, 0:HEAD_V]
    dk = rec.width // N_HEADS
    for h in range(N_HEADS):
        sfin_ref[sub, d, h] = x[h * dk:(h + 1) * dk, :]


def _run_recurrence(rec, ok, s0_ref, sfin_ref):
    nsub = rec.nsub
    nchunk = rec.seq_len // CHUNK
    gsize = min(MAIN_GROUP, nchunk)
    ngroup = nchunk // gsize
    spp = max(1, min(nsub, MAIN_GROUP // gsize))
    nsp = nsub // spp
    for sub in range(nsub):
        for d in (0, 1):
            if s0_ref is None:
                rec.st_scr[sub * 2 + d] = jnp.zeros(rec.st_scr.shape[1:], F32)
            else:
                _load_state(rec, s0_ref, sub, d)

    def fast_all():
        def step(i, carry):
            sp = i // ngroup if (nsp > 1 and ngroup > 1) else (i if nsp > 1 else 0)
            gi = i % ngroup if (nsp > 1 and ngroup > 1) else (i if ngroup > 1 else 0)
            for u in range(gsize):
                s = gi * gsize + u
                for j in range(spp):
                    sub = sp * spp + j
                    _chunk_step(rec, sub, 0, s, True)
                    _chunk_step(rec, sub, 1, nchunk - 1 - s, True)
            return carry

        if nsp * ngroup == 1:
            step(0, 0)
        else:
            lax.fori_loop(0, nsp * ngroup, step, 0)

    def direct_all():
        def one(i, carry):
            sub = i // nchunk if nsub > 1 else 0
            s = i % nchunk if nsub > 1 else i
            _chunk_step(rec, sub, 0, s, False)
            _chunk_step(rec, sub, 1, nchunk - 1 - s, False)
            return carry

        lax.fori_loop(0, nsub * nchunk, one, 0)

    lax.cond(ok, fast_all, direct_all)
    if sfin_ref is not None:
        for sub in range(nsub):
            for d in (0, 1):
                _store_state(rec, sfin_ref, sub, d)


def _head_norm_gate(rec, gate_ref, gain_ref, o_ref):
    for rb in range(rec.nsub * rec.seq_len // ROW_BLOCK):
        rows = slice(rb * ROW_BLOCK, (rb + 1) * ROW_BLOCK)
        o = rec.oacc[0, rows, :] + rec.oacc[1, rows, :]
        ms = jnp.dot((o * o).astype(BF16), rec.bdv_b[...], preferred_element_type=F32) * (1.0 / HEAD_V)
        y = o * lax.rsqrt(ms + EPS) * gain_ref[0] * _silu(gate_ref[rows, :])
        o_ref[rows, :] = y.astype(o_ref.dtype)


def _split_refs(refs, n_in, has_s0, want_final):
    ins = refs[:n_in]
    pos = n_in
    s0_ref = None
    if has_s0:
        s0_ref = refs[pos]
        pos += 1
    o_ref = refs[pos]
    pos += 1
    sfin_ref = None
    if want_final:
        sfin_ref = refs[pos]
        pos += 1
    return ins, s0_ref, o_ref, sfin_ref, refs[pos:]


def _hgrn_kernel(*refs, seq_len, nsub, has_s0, want_final):
    ins, s0_ref, o_ref, sfin_ref, scr = _split_refs(refs, 13, has_s0, want_final)
    q_ref, i_ref, ff_ref, fb_ref, g_ref, gain_ref = ins[:6]
    k_scr, cum_scr, oacc, st_scr = scr
    q_at = lambda r0: q_ref[pl.ds(r0, CHUNK), :]
    k_at = lambda d, r0, n=CHUNK: k_scr[d, pl.ds(r0, n), :]
    rec = _Rec(q_at, k_at, i_ref, cum_scr, st_scr, oacc, ins[6:], W_GRP, seq_len, nsub)

    def pre(bi, rm):
        r0 = bi * GROWS if isinstance(bi, int) else pl.multiple_of(bi * GROWS, GROWS)
        rows = pl.ds(r0, GROWS)
        for d, f_ref in ((0, ff_ref), (1, fb_ref)):
            f = f_ref[rows, :]
            g = jnp.log(jnp.maximum(f, F_FLOOR))
            k_scr[d, rows, :] = 1.0 - f
            cum = _block_cumsum(rec.tri_ref[d], g)
            cum_scr[d, rows, :] = cum
            rm = _block_ranges(cum, d == 1, rm)
        return rm

    rmax = _run_prepass(pre, nsub * seq_len // GROWS, W_GRP)
    ok = jnp.max(rmax) < SAFE_RANGE

    _run_recurrence(rec, ok, s0_ref, sfin_ref)
    _head_norm_gate(rec, g_ref, gain_ref, o_ref)


def _gla_kernel(*refs, seq_len, nsub, has_s0, want_final):
    ins, s0_ref, o_ref, sfin_ref, scr = _split_refs(refs, 14, has_s0, want_final)
    q_ref, k_ref, v_ref, g_ref, gf_ref, gb_ref, gain_ref = ins[:7]
    cum_scr, oacc, st_scr = scr
    width = N_HEADS * GLA_DK
    q_at = lambda r0: q_ref[pl.ds(r0, CHUNK), :] * (GLA_DK ** -0.5)
    k_at = lambda d, r0, n=CHUNK: k_ref[pl.ds(r0, n), :]
    rec = _Rec(q_at, k_at, v_ref, cum_scr, st_scr, oacc, ins[7:], width, seq_len, nsub)

    def pre(bi, rm):
        r0 = bi * GROWS if isinstance(bi, int) else pl.multiple_of(bi * GROWS, GROWS)
        rows = pl.ds(r0, GROWS)
        for d, logdecay_ref in ((0, gf_ref), (1, gb_ref)):
            g = logdecay_ref[rows, :]
            cum = _block_cumsum(rec.tri_ref[d], g)
            cum_scr[d, rows, :] = cum
            rm = _block_ranges(cum, d == 1, rm)
        return rm

    rmax = _run_prepass(pre, nsub * seq_len // GROWS, width)
    ok = jnp.max(rmax) < SAFE_RANGE

    _run_recurrence(rec, ok, s0_ref, sfin_ref)
    _head_norm_gate(rec, g_ref, gain_ref, o_ref)


def _recurrence_constants(width):
    dk = width // N_HEADS
    r = np.arange(N_HEADS * CHUNK)[:, None]
    tri = np.kron(np.eye(GROUP, dtype=np.float32), np.tril(np.ones((CHUNK, CHUNK), np.float32)))
    tri = np.stack([tri, tri.T])
    bdw = (r // CHUNK == np.arange(width)[None, :] // dk).astype(np.float32)
    bdv = (r // CHUNK == np.arange(W_GRP)[None, :] // HEAD_V).astype(np.float32)
    t = np.arange(CHUNK)[:, None]
    s = np.arange(N_HEADS * CHUNK)[None, :] % CHUNK
    cmask = np.stack([(s <= t), (s >= t)]).astype(np.float32)
    bmat = (np.arange(width)[:, None] // dk == np.arange(W_GRP)[None, :] // HEAD_V).astype(np.float32)
    tile_t = np.tile(np.eye(HEAD_V, dtype=np.float32), (1, N_HEADS))
    return (jnp.asarray(tri, BF16), jnp.asarray(bdw, BF16), jnp.asarray(bdw), jnp.asarray(bdv, BF16),
            jnp.asarray(cmask), jnp.asarray(bmat, BF16), jnp.asarray(tile_t, BF16))


def _recurrent_mixer(kernel_fn, name, width, args, in_specs, state, layer, nseq, seq_len, want_final,
                     extra_scratch):
    dk = width // N_HEADS
    nsub = _seqs_per_block(nseq, seq_len)
    rows = nsub * seq_len
    consts = _recurrence_constants(width)
    args = list(args) + list(consts)
    in_specs = list(in_specs) + [_full_spec(c) for c in consts]
    if state is not None:
        args.append(state)
        in_specs.append(pl.BlockSpec((nsub, 1, 2, N_HEADS, dk, HEAD_V), lambda b: (b, layer, 0, 0, 0, 0)))
    out_shape = [jax.ShapeDtypeStruct((nseq * seq_len, W_GRP), BF16)]
    out_specs = [pl.BlockSpec((rows, W_GRP), lambda b: (b, 0))]
    if want_final:
        out_shape.append(jax.ShapeDtypeStruct((nseq, 2, N_HEADS, dk, HEAD_V), F32))
        out_specs.append(pl.BlockSpec((nsub, 2, N_HEADS, dk, HEAD_V), lambda b: (b, 0, 0, 0, 0)))
    res = pl.pallas_call(
        functools.partial(kernel_fn, seq_len=seq_len, nsub=nsub, has_s0=state is not None,
                          want_final=want_final),
        grid=(nseq // nsub,),
        in_specs=in_specs,
        out_specs=out_specs,
        out_shape=out_shape,
        scratch_shapes=[pltpu.VMEM((2, rows, W_GRP), F32)] * extra_scratch + [
            pltpu.VMEM((2, rows, width), F32), pltpu.VMEM((2, rows, W_GRP), F32),
            pltpu.VMEM((2 * nsub, W_GRP, width), F32)],
        compiler_params=_params(("arbitrary",)),
        name=name,
    )(*args)
    return res[0], (res[1] if want_final else None)


def _seqs_per_block(nseq, seq_len):
    return max(1, min(nseq, BLOCK_ROWS // seq_len))


def _hgrn_mixer(proj, gain, state, layer, nseq, seq_len, want_final):
    rows = _seqs_per_block(nseq, seq_len) * seq_len
    col = lambda c: pl.BlockSpec((rows, W_GRP), lambda b: (b, c))
    args = [proj] * 5 + [gain.reshape(DEPTH, 1, W_GRP)]
    in_specs = [col(0), col(1), col(2), col(3), col(4),
                pl.BlockSpec((1, 1, W_GRP), lambda b: (layer, 0, 0))]
    return _recurrent_mixer(_hgrn_kernel, "hgrn2_mixer", W_GRP, args, in_specs, state, layer, nseq,
                            seq_len, want_final, 1)


def _gla_mixer(proj, gain, state, layer, nseq, seq_len, want_final):
    width = N_HEADS * GLA_DK
    rows = _seqs_per_block(nseq, seq_len) * seq_len
    col256 = lambda c: pl.BlockSpec((rows, W_GRP), lambda b: (b, c))
    col128 = lambda c: pl.BlockSpec((rows, width), lambda b: (b, c))
    args = [proj] * 6 + [gain.reshape(DEPTH, 1, W_GRP)]
    in_specs = [col128(10), col128(11), col256(6), col256(7), col128(16), col128(17),
                pl.BlockSpec((1, 1, W_GRP), lambda b: (layer, 0, 0))]
    return _recurrent_mixer(_gla_kernel, "gla_mixer", width, args, in_specs, state, layer, nseq, seq_len,
                            want_final, 0)


PAD = 8
SCAN_TILE = 8
SCAN_UNROLL = 4
GELU_C = 0.7978845608028654


def _rgsc_kernel(*refs, seq_len, nsub, seg, has_h0, want_final):
    (cx_ref, cg_ref, db_ref, dc_ref, dv_ref, convw_ref, convb_ref, wr_ref, br_ref, wi_ref,
     bi_ref, lam_ref, sw_ref) = refs[:13]
    pos = 13
    h0_ref = None
    if has_h0:
        h0_ref = refs[pos]
        pos += 1
    oc_ref, od_ref = refs[pos:pos + 2]
    pos += 2
    hfin_ref = None
    if want_final:
        hfin_ref = refs[pos]
        pos += 1
    upad, a_scr, b_scr, h_scr = refs[pos:]

    slot = seq_len + 2 * PAD
    zpad = jnp.zeros((PAD, W_GRP), F32)
    for sub in range(nsub):
        upad[sub * slot:sub * slot + PAD, :] = zpad
        upad[sub * slot + PAD + seq_len:(sub + 1) * slot, :] = zpad
        upad[sub * slot + PAD:sub * slot + PAD + seq_len, :] = cx_ref[sub * seq_len:(sub + 1) * seq_len, :]
    rows8 = lax.broadcasted_iota(jnp.int32, (SCAN_TILE, W_GRP), 0)
    ntile = seq_len // SCAN_TILE

    for d in (0, 1):
        reverse = d == 1
        w = convw_ref[0, d]
        offs = [PAD + 3, PAD + 2, PAD + 1, PAD] if reverse else [PAD - 3, PAD - 2, PAD - 1, PAD]
        sp = _softplus(-lam_ref[0, d:d + 1, :])
        for sub in range(nsub):
            for rb in range(seq_len // ROW_BLOCK):
                src = sub * slot + rb * ROW_BLOCK
                base = sub * seq_len + rb * ROW_BLOCK
                win = upad[src:src + ROW_BLOCK + 2 * PAD, :]
                xc = convb_ref[0, d:d + 1, :]
                for j in range(4):
                    shift = (PAD - offs[j]) % (ROW_BLOCK + 2 * PAD)
                    tap = win if shift == 0 else pltpu.roll(win, shift, 0)
                    xc = xc + w[j:j + 1, :] * tap[PAD:PAD + ROW_BLOCK, :]
                r = _sigmoid(_dot(xc, wr_ref[d]) + br_ref[0, d:d + 1, :])
                ig = _sigmoid(_dot(xc, wi_ref[d]) + bi_ref[0, d:d + 1, :])
                log_a = -RG_C * r * sp
                a = jnp.exp(log_a)
                a_scr[base:base + ROW_BLOCK, :] = a
                y = 1.0 - a * a
                root = jnp.where(y > 0.0, y * lax.rsqrt(y), 0.0)
                b_scr[base:base + ROW_BLOCK, :] = root * (ig * xc)

        if has_h0:
            h0 = tuple(h0_ref[sub, 0, d:d + 1, :] for sub in range(nsub))
        else:
            h0 = tuple(jnp.zeros((1, W_GRP), F32) for _ in range(nsub))

        def step(ti, carry, d=d, reverse=reverse):
            t = (ntile - 1 - ti) if reverse else ti
            out = []
            for sub in range(nsub):
                r0 = pl.multiple_of(sub * seq_len + t * SCAN_TILE, SCAN_TILE)
                a = a_scr[pl.ds(r0, SCAN_TILE), :]
                b = b_scr[pl.ds(r0, SCAN_TILE), :]
                for s in (1, 2, 4):
                    if reverse:
                        keep = rows8 <= SCAN_TILE - 1 - s
                        shift = SCAN_TILE - s
                    else:
                        keep = rows8 >= s
                        shift = s
                    a_s = jnp.where(keep, pltpu.roll(a, shift, 0), 1.0)
                    b_s = jnp.where(keep, pltpu.roll(b, shift, 0), 0.0)
                    b = b + a * b_s
                    a = a * a_s
                h = a * carry[sub] + b
                if d == 0:
                    h_scr[pl.ds(r0, SCAN_TILE), :] = h
                else:
                    h_scr[pl.ds(r0, SCAN_TILE), :] += h
                out.append(h[0:1, :] if reverse else h[SCAN_TILE - 1:SCAN_TILE, :])
            return tuple(out)

        hlast = lax.fori_loop(0, ntile, step, h0, unroll=SCAN_UNROLL // nsub if nsub < SCAN_UNROLL else 1)
        if want_final:
            for sub in range(nsub):
                hfin_ref[sub, d:d + 1, :] = hlast[sub]

    sw = sw_ref[0]
    nblock = nsub * seq_len // ROW_BLOCK
    for rb in range(nblock):
        rows = slice(rb * ROW_BLOCK, (rb + 1) * ROW_BLOCK)
        x = cg_ref[rows, :]
        gelu = x * (0.5 + 0.5 * jnp.tanh(x * (GELU_C + (GELU_C * 0.044715) * (x * x))))
        oc_ref[rows, :] = (h_scr[rows, :] * gelu).astype(oc_ref.dtype)
        upad[PAD + rb * ROW_BLOCK:PAD + (rb + 1) * ROW_BLOCK, :] = dc_ref[rows, :] * dv_ref[rows, :]
    for rb in range(nblock):
        base = rb * ROW_BLOCK
        rows = slice(base, base + ROW_BLOCK)
        posn = (lax.broadcasted_iota(jnp.int32, (ROW_BLOCK, W_GRP), 0) + base) % seg
        win = upad[base:base + ROW_BLOCK + 2 * PAD, :]
        nwin = ROW_BLOCK + 2 * PAD
        mid = slice(PAD, PAD + ROW_BLOCK)
        left = jnp.where(posn != 0, pltpu.roll(win, 1, 0)[mid, :], 0.0)
        right = jnp.where(posn != seg - 1, pltpu.roll(win, nwin - 1, 0)[mid, :], 0.0)
        y = sw[0:1, :] * left + sw[1:2, :] * win[mid, :] + sw[2:3, :] * right
        od_ref[rows, :] = (db_ref[rows, :] * y).astype(od_ref.dtype)


def _rgsc_mixer(proj, conv_w, conv_b, wr_bd, b_r, wi_bd, b_i, lam, sconv_w, h0, layer, nseq, seq_len,
                seg, want_final):
    nsub = _seqs_per_block(nseq, seq_len)
    rows = nsub * seq_len
    col = lambda c: pl.BlockSpec((rows, W_GRP), lambda b: (b, c))
    lay3 = lambda a: pl.BlockSpec((1,) + a.shape[1:], lambda b: (layer, 0, 0))
    lay4 = lambda a: pl.BlockSpec((1,) + a.shape[1:], lambda b: (layer, 0, 0, 0))
    params = [conv_w, conv_b, wr_bd, b_r, wi_bd, b_i, lam, sconv_w]
    args = [proj] * 5 + params
    in_specs = [col(9), col(10), col(11), col(12), col(13), lay4(conv_w), lay3(conv_b), _full_spec(wr_bd),
                lay3(b_r), _full_spec(wi_bd), lay3(b_i), lay3(lam), lay3(sconv_w)]
    if h0 is not None:
        args.append(h0)
        in_specs.append(pl.BlockSpec((nsub, 1, 2, W_GRP), lambda b: (b, layer, 0, 0)))
    out_shape = [jax.ShapeDtypeStruct((nseq * seq_len, W_GRP), BF16)] * 2
    out_specs = [pl.BlockSpec((rows, W_GRP), lambda b: (b, 0))] * 2
    if want_final:
        out_shape.append(jax.ShapeDtypeStruct((nseq, 2, W_GRP), F32))
        out_specs.append(pl.BlockSpec((nsub, 2, W_GRP), lambda b: (b, 0, 0)))
    res = pl.pallas_call(
        functools.partial(_rgsc_kernel, seq_len=seq_len, nsub=nsub, seg=seg, has_h0=h0 is not None,
                          want_final=want_final),
        grid=(nseq // nsub,),
        in_specs=in_specs,
        out_specs=out_specs,
        out_shape=out_shape,
        scratch_shapes=[pltpu.VMEM((nsub * (seq_len + 2 * PAD), W_GRP), F32), pltpu.VMEM((rows, W_GRP), F32),
                        pltpu.VMEM((rows, W_GRP), F32), pltpu.VMEM((rows, W_GRP), F32)],
        compiler_params=_params(("arbitrary",)),
        name="rglru_sconv_mixer",
    )(*args)
    return res[0], res[1], (res[2] if want_final else None)


def _block_diag(w):
    two, h, c, _ = w.shape
    eye = jnp.eye(h, dtype=w.dtype)
    full = w[:, :, :, None, :] * eye[None, :, None, :, None]
    return full.reshape(two, h * c, h * c)


def kernel(x_prompt, x_sample, state_hgrn, state_gla, state_rglru, c, c_ctx, norm1_g, norm2_g, ada_w, ada_b, w_in, w_out, hgrn_lb_logits, hgrn_norm_g, gla_wa2, gla_ba2, gla_norm_g, rg_conv_w, rg_conv_b, rg_w_r, rg_b_r, rg_w_i, rg_b_i, rg_lambda, sconv_w, mlp_w1, mlp_w2, final_norm_g):
    b_ctx, l_ctx, _ = x_prompt.shape
    b_lat, l_lat, _ = x_sample.shape

    cvec = jnp.concatenate([c, c_ctx[None, :], jnp.zeros((8 - b_lat - 1, D_MODEL), F32)], axis=0)
    mod = _modulation(cvec, ada_w, ada_b).reshape(DEPTH, 8, 6, D_MODEL)

    w_cat = _build_proj_weight(w_in, gla_wa2)

    xp = x_prompt.reshape(b_ctx * l_ctx, D_MODEL)
    xs = x_sample.reshape(b_lat * l_lat, D_MODEL)
    new_h, new_g, new_r = [], [], []
    for l in range(DEPTH):
        wr_bd = _block_diag(rg_w_r[l]).astype(BF16)
        wi_bd = _block_diag(rg_w_i[l]).astype(BF16)
        mod_ctx = mod[l, b_lat:b_lat + 1]
        mod_lat = mod[l, :b_lat]
        last = l == DEPTH - 1
        streams = (
            (xp, mod_ctx, b_ctx, l_ctx, l_ctx, None, None, None, True),
            (xs, mod_lat, b_lat, l_lat, GRID_W, state_hgrn, state_gla, state_rglru, False),
        )
        outs = []
        for (x, m, nseq, seq_len, seg, s_h, s_g, s_r, want_final) in streams:
            proj = _project(x, m, norm1_g, w_cat, hgrn_lb_logits, gla_ba2, l)
            o_a, f_h = _hgrn_mixer(proj, hgrn_norm_g, s_h, l, nseq, seq_len, want_final)
            o_b, f_g = _gla_mixer(proj, gla_norm_g, s_g, l, nseq, seq_len, want_final)
            o_c, o_d, f_r = _rgsc_mixer(proj, rg_conv_w, rg_conv_b, wr_bd, rg_b_r, wi_bd, rg_b_i,
                                        rg_lambda, sconv_w, s_r, l, nseq, seq_len, seg, want_final)
            x_new = _out_mlp(x, o_a, o_b, o_c, o_d, m, norm2_g, final_norm_g, w_out, mlp_w1, mlp_w2, l, last)
            outs.append((x_new, f_h, f_g, f_r))
        xp, f_h, f_g, f_r = outs[0]
        xs = outs[1][0]
        new_h.append(f_h)
        new_g.append(f_g)
        new_r.append(f_r)
    y_prompt = xp.reshape(b_ctx, l_ctx, D_MODEL)
    y_sample = xs.reshape(b_lat, l_lat, D_MODEL)
    return (y_prompt, y_sample, jnp.stack(new_h, axis=1), jnp.stack(new_g, axis=1),
            jnp.stack(new_r, axis=1))
```

```python
import functools

import numpy as np
import jax
import jax.numpy as jnp
from jax import lax
from jax.experimental import pallas as pl
from jax.experimental.pallas import tpu as pltpu

F32 = jnp.float32
BF16 = jnp.bfloat16

D_MODEL = 1024
DEPTH = 2
GRID_W = 64
N_HEADS = 4
W_GRP = 256
HEAD_V = 64
HGRN_DK = 64
GLA_DK = 32
GLA_RANK = 16
GLA_NORMALIZER = 16.0
RG_C = 8.0
D_FF = 4 * D_MODEL
EPS = 1e-6
F_FLOOR = 1e-20

PROJ_COLS = 3584

CHUNK = 64
HALF = CHUNK // 2
assert CHUNK == HEAD_V
GROUP = 4
GROWS = GROUP * CHUNK
MAIN_GROUP = 16
BLOCK_ROWS = 1024
LANES = 128
SAFE_RANGE = 80.0
TOK_TILE = 1024
PROJ_TM = 512
PROJ_TN = 512
FF_TILE = 1024
MLP_ROW_SPLIT = 2
PROLOGUE_ROWS = 256
ROW_BLOCK = 256
VMEM_LIMIT = 56 * 1024 * 1024

_NT = (((1,), (1,)), ((), ()))
_TN = (((0,), (0,)), ((), ()))


def _dot(a, b):
    return jnp.dot(a.astype(BF16), b.astype(BF16), preferred_element_type=F32)


def _dot_nt(a, b):
    return lax.dot_general(a.astype(BF16), b.astype(BF16), _NT, preferred_element_type=F32)


def _dot_tn(a, b):
    return lax.dot_general(a.astype(BF16), b.astype(BF16), _TN, preferred_element_type=F32)


def _sigmoid(x):
    return 1.0 / (1.0 + jnp.exp(-x))


def _silu(x):
    return x * _sigmoid(x)


def _log1p(y):
    u = 1.0 + y
    return jnp.where(u == 1.0, y, jnp.log(u) * (y / (u - 1.0)))


def _softplus(x):
    return jnp.maximum(x, 0.0) + _log1p(jnp.exp(-jnp.abs(x)))


def _log_sigmoid(z):
    return jnp.minimum(z, 0.0) - jnp.log(1.0 + jnp.exp(-jnp.abs(z)))


def _rms(x):
    return x * lax.rsqrt(jnp.mean(x * x, axis=-1, keepdims=True) + EPS)


def _params(sem):
    return pltpu.CompilerParams(dimension_semantics=sem, vmem_limit_bytes=VMEM_LIMIT)


def _full_spec(a):
    zeros = (0,) * a.ndim
    return pl.BlockSpec(a.shape, lambda *_: zeros)


def _mod_kernel(c_ref, w_ref, b_ref, o_ref):
    s = _silu(c_ref[...])
    o_ref[0] = _dot(s, w_ref[0]) + b_ref[0]


def _modulation(cvec, ada_w, ada_b):
    tn = 2048
    return pl.pallas_call(
        _mod_kernel,
        grid=(DEPTH, 6 * D_MODEL // tn),
        in_specs=[pl.BlockSpec((8, D_MODEL), lambda l, j: (0, 0)),
                  pl.BlockSpec((1, D_MODEL, tn), lambda l, j: (l, 0, j)),
                  pl.BlockSpec((1, 1, tn), lambda l, j: (l, 0, j))],
        out_specs=pl.BlockSpec((1, 8, tn), lambda l, j: (l, 0, j)),
        out_shape=jax.ShapeDtypeStruct((DEPTH, 8, 6 * D_MODEL), F32),
        compiler_params=_params(("arbitrary", "arbitrary")),
        name="adaln_modulation",
    )(cvec, ada_w, ada_b.reshape(DEPTH, 1, 6 * D_MODEL))


LOWRANK_COL = 2048
TAIL_COL = LOWRANK_COL + 2 * GLA_RANK
TAIL_WIDTH = 1280
WCAT_LANES = 256


def _wcat_kernel(w_ref, wa2t_ref, o_ref):
    o_ref[0, 0:LOWRANK_COL, :] = w_ref[0, 0:LOWRANK_COL, :].astype(BF16)
    z = jnp.dot(wa2t_ref[0], w_ref[0, LOWRANK_COL:TAIL_COL, :], preferred_element_type=F32,
                precision=lax.Precision.HIGHEST)
    o_ref[0, LOWRANK_COL:LOWRANK_COL + 2 * LANES, :] = z.astype(BF16)
    o_ref[0, LOWRANK_COL + 2 * LANES:PROJ_COLS, :] = w_ref[0, TAIL_COL:TAIL_COL + TAIL_WIDTH, :].astype(BF16)


def _build_proj_weight(w_in, wa2):
    ncol = w_in.shape[2]
    assert ncol == TAIL_COL + TAIL_WIDTH and LOWRANK_COL + 2 * LANES + TAIL_WIDTH == PROJ_COLS
    n = N_HEADS * GLA_DK
    wa2t = jnp.zeros((DEPTH, 2 * n, 2 * GLA_RANK), F32)
    wa2t = wa2t.at[:, 0:n, 0:GLA_RANK].set(jnp.swapaxes(wa2[:, 0], 1, 2))
    wa2t = wa2t.at[:, n:2 * n, GLA_RANK:2 * GLA_RANK].set(jnp.swapaxes(wa2[:, 1], 1, 2))
    w_t = jnp.swapaxes(w_in, 1, 2)
    return pl.pallas_call(
        _wcat_kernel,
        grid=(DEPTH, D_MODEL // WCAT_LANES),
        in_specs=[pl.BlockSpec((1, ncol, WCAT_LANES), lambda l, i: (l, 0, i)),
                  pl.BlockSpec((1, 2 * n, 2 * GLA_RANK), lambda l, i: (l, 0, 0))],
        out_specs=pl.BlockSpec((1, PROJ_COLS, WCAT_LANES), lambda l, i: (l, 0, i)),
        out_shape=jax.ShapeDtypeStruct((DEPTH, PROJ_COLS, D_MODEL), BF16),
        compiler_params=_params(("arbitrary", "arbitrary")),
        name="build_proj_weight",
    )(w_t, wa2t)


HGRN_GATE_COL = 2 * W_GRP
GLA_GATE_COL = LOWRANK_COL
assert HGRN_GATE_COL % PROJ_TN == 0 and GLA_GATE_COL % PROJ_TN == 0 and PROJ_TN == 2 * W_GRP


def _hgrn_lower_bound(logits, layer):
    mx = logits[0]
    for i in range(1, DEPTH):
        mx = jnp.maximum(mx, logits[i])
    ex = [jnp.exp(logits[i] - mx) for i in range(DEPTH)]
    den = ex[0]
    for i in range(1, DEPTH):
        den = den + ex[i]
    sm = [e / den for e in ex]
    csum = sm[0]
    for i in range(1, layer + 1):
        csum = csum + sm[i]
    return csum - sm[0]


def _proj_kernel(x_ref, mod_ref, g_ref, w_ref, lbl_ref, ba_ref, o_ref, *, layer):
    shift = mod_ref[0, 0:1, :]
    scale = mod_ref[0, 1:2, :]
    h = (_rms(x_ref[...]) * g_ref[0] * (1.0 + scale) + shift).astype(BF16)
    lb = _hgrn_lower_bound(lbl_ref[...], layer)
    lb_row = jnp.concatenate([lb[0:1, :], lb[1:2, :]], axis=1)
    ba_row = jnp.concatenate([ba_ref[0, 0:1, :], ba_ref[0, 1:2, :]], axis=1)
    for n in range(PROJ_COLS // PROJ_TN):
        cols = slice(n * PROJ_TN, (n + 1) * PROJ_TN)
        acc = lax.dot_general(h, w_ref[0, cols, :], _NT, preferred_element_type=F32)
        if n * PROJ_TN == HGRN_GATE_COL:
            acc = lb_row + (1.0 - lb_row) * _sigmoid(acc)
        if n * PROJ_TN == GLA_GATE_COL:
            z = acc[:, 0:W_GRP] + ba_row
            acc = jnp.concatenate([_log_sigmoid(z) * (1.0 / GLA_NORMALIZER), acc[:, W_GRP:]], axis=1)
        o_ref[:, cols] = acc


def _mod_index(ntok, nmod, tile):
    tiles_per_mod = (ntok // tile) // nmod if nmod > 1 else 1
    if nmod > 1:
        return lambda i, *_: (i // tiles_per_mod, 0, 0)
    return lambda i, *_: (0, 0, 0)


def _project(x, mod, norm_g, w_cat, lb_logits, ba2, layer):
    ntok = x.shape[0]
    return pl.pallas_call(
        functools.partial(_proj_kernel, layer=layer),
        grid=(ntok // PROJ_TM,),
        in_specs=[pl.BlockSpec((PROJ_TM, D_MODEL), lambda i: (i, 0)),
                  pl.BlockSpec((1, 6, D_MODEL), _mod_index(ntok, mod.shape[0], PROJ_TM)),
                  pl.BlockSpec((1, 1, D_MODEL), lambda i: (layer, 0, 0)),
                  pl.BlockSpec((1, PROJ_COLS, D_MODEL), lambda i: (layer, 0, 0),
                               pipeline_mode=pl.Buffered(1)),
                  _full_spec(lb_logits),
                  pl.BlockSpec((1, 2, N_HEADS * GLA_DK), lambda i: (layer, 0, 0))],
        out_specs=pl.BlockSpec((PROJ_TM, PROJ_COLS), lambda i: (i, 0)),
        out_shape=jax.ShapeDtypeStruct((ntok, PROJ_COLS), F32),
        compiler_params=_params(("arbitrary",)),
        name="norm_in_proj",
    )(x, mod, norm_g.reshape(DEPTH, 1, D_MODEL), w_cat, lb_logits, ba2)


def _mlp_kernel(x_ref, oa_ref, ob_ref, oc_ref, od_ref, mod_ref, g2_ref, fg_ref, wout_ref,
                w1_ref, w2_ref, out_ref, h2_scr, *, final_norm):
    j = pl.program_id(1)

    @pl.when(j == 0)
    def _():
        wout = wout_ref[0].astype(BF16)
        nslab = TOK_TILE // PROLOGUE_ROWS
        for s in range(nslab):
            rows = slice(s * PROLOGUE_ROWS, (s + 1) * PROLOGUE_ROWS)
            mixed = jnp.concatenate([oa_ref[rows, :], ob_ref[rows, :], oc_ref[rows, :], od_ref[rows, :]],
                                    axis=1)
            mix = jnp.dot(mixed, wout, preferred_element_type=F32)
            x1 = x_ref[rows, :] + mod_ref[0, 2:3, :] * mix
            out_ref[rows, :] = x1
            h2 = _rms(x1) * g2_ref[0] * (1.0 + mod_ref[0, 4:5, :]) + mod_ref[0, 3:4, :]
            h2_scr[rows, :] = h2.astype(BF16)

    w1 = w1_ref[0].astype(BF16)
    w2 = w2_ref[0].astype(BF16)
    for r in range(MLP_ROW_SPLIT):
        rows = slice(r * (TOK_TILE // MLP_ROW_SPLIT), (r + 1) * (TOK_TILE // MLP_ROW_SPLIT))
        t = jnp.dot(h2_scr[rows, :], w1, preferred_element_type=F32)
        t = jnp.square(jnp.maximum(t, 0.0)).astype(BF16)
        out_ref[rows, :] += mod_ref[0, 5:6, :] * jnp.dot(t, w2, preferred_element_type=F32)

    if final_norm:
        @pl.when(j == pl.num_programs(1) - 1)
        def _():
            out_ref[...] = _rms(out_ref[...]) * fg_ref[...]


def _out_mlp(x, o_a, o_b, o_c, o_d, mod, norm2_g, final_g, w_out, w1, w2, layer, final_norm):
    ntok = x.shape[0]
    tok = lambda i, j: (i, 0)
    return pl.pallas_call(
        functools.partial(_mlp_kernel, final_norm=final_norm),
        grid=(ntok // TOK_TILE, D_FF // FF_TILE),
        in_specs=[pl.BlockSpec((TOK_TILE, D_MODEL), tok),
                  pl.BlockSpec((TOK_TILE, W_GRP), tok),
                  pl.BlockSpec((TOK_TILE, W_GRP), tok),
                  pl.BlockSpec((TOK_TILE, W_GRP), tok),
                  pl.BlockSpec((TOK_TILE, W_GRP), tok),
                  pl.BlockSpec((1, 6, D_MODEL), _mod_index(ntok, mod.shape[0], TOK_TILE)),
                  pl.BlockSpec((1, 1, D_MODEL), lambda i, j: (layer, 0, 0)),
                  pl.BlockSpec((1, D_MODEL), lambda i, j: (0, 0)),
                  pl.BlockSpec((1, D_MODEL, D_MODEL), lambda i, j: (layer, 0, 0)),
                  pl.BlockSpec((1, D_MODEL, FF_TILE), lambda i, j: (layer, 0, j)),
                  pl.BlockSpec((1, FF_TILE, D_MODEL), lambda i, j: (layer, j, 0))],
        out_specs=pl.BlockSpec((TOK_TILE, D_MODEL), tok),
        out_shape=jax.ShapeDtypeStruct((ntok, D_MODEL), F32),
        scratch_shapes=[pltpu.VMEM((TOK_TILE, D_MODEL), BF16)],
        compiler_params=_params(("arbitrary", "arbitrary")),
        name="out_proj_mlp",
    )(x, o_a, o_b, o_c, o_d, mod, norm2_g.reshape(DEPTH, 1, D_MODEL), final_g.reshape(1, D_MODEL),
      w_out, w1, w2)


def _split3(x):
    hi = x.astype(BF16)
    r1 = x - hi.astype(F32)
    mid = r1.astype(BF16)
    lo = (r1 - mid.astype(F32)).astype(BF16)
    return hi, mid, lo


def _dot_exact_rhs(x, m):
    hi, mid, lo = _split3(x)
    s = jnp.dot(hi, m, preferred_element_type=F32)
    s += jnp.dot(mid, m, preferred_element_type=F32)
    s += jnp.dot(lo, m, preferred_element_type=F32)
    return s


def _block_cumsum(tri, g):
    hi = g.astype(BF16)
    lo = (g - hi.astype(F32)).astype(BF16)
    return jnp.dot(tri, hi, preferred_element_type=F32) + jnp.dot(tri, lo, preferred_element_type=F32)


def _run_prepass(pre, nblock, width):
    rmax = jnp.zeros((1, width), F32)
    if nblock % 2:
        for bi in range(nblock):
            rmax = pre(bi, rmax)
        return rmax
    if nblock == 2:
        return pre(1, pre(0, rmax))
    return lax.fori_loop(0, nblock // 2, lambda i, rm: pre(2 * i + 1, pre(2 * i, rm)), rmax)


def _half_ranges(cum, reverse):
    if reverse:
        second = -cum[HALF:HALF + 1, :]
        first = -(cum[0:1, :] - cum[HALF:HALF + 1, :])
    else:
        first = -cum[HALF - 1:HALF, :]
        second = -(cum[CHUNK - 1:CHUNK, :] - cum[HALF - 1:HALF, :])
    return jnp.maximum(first, second)


def _block_ranges(cum, reverse, rm):
    for c in range(GROUP):
        rm = jnp.maximum(rm, _half_ranges(cum[c * CHUNK:(c + 1) * CHUNK, :], reverse))
    return rm


class _Rec:
    def __init__(self, q_at, k_at, v_ref, cum_scr, st_scr, oacc, consts, width, seq_len, nsub):
        self.q_at, self.k_at, self.v_ref = q_at, k_at, v_ref
        self.cum_scr, self.st_scr, self.oacc = cum_scr, st_scr, oacc
        (self.tri_ref, self.bdw_b, self.bdw_f, self.bdv_b, self.cmask, self.bmat,
         self.tile_t) = consts
        self.width, self.seq_len, self.nsub = width, seq_len, nsub


def _chunk_step(rec, sub, d, c, fast):
    reverse = d == 1
    mid_row = HALF if reverse else HALF - 1
    last_row = 0 if reverse else CHUNK - 1
    sd = sub * 2 + d
    r0 = sub * rec.seq_len + c * CHUNK
    if not isinstance(r0, int):
        r0 = pl.multiple_of(r0, CHUNK)
    rows = pl.ds(r0, CHUNK)
    q = rec.q_at(r0)
    k = rec.k_at(d, r0)
    v_b = rec.v_ref[rows, :].astype(BF16)
    cum = rec.cum_scr[d, rows, :]
    tot = cum[last_row:last_row + 1, :]
    st = rec.st_scr[sd]
    o = _dot_nt(q * jnp.exp(cum), st)
    if fast:
        cm = cum[mid_row:mid_row + 1, :]
        qm = q * jnp.exp(cum - cm)
        km_b = (k * jnp.exp(cm - cum)).astype(BF16)
        kbd = jnp.concatenate([km_b] * N_HEADS, axis=0) * rec.bdw_b[...]
        sc = lax.dot_general(qm.astype(BF16), kbd, _NT, preferred_element_type=F32)
        a = jnp.where(rec.cmask[d] > 0.5, sc, 0.0)
        vexp = jnp.concatenate([v_b] * N_HEADS, axis=0) * rec.bdv_b[...]
        o = o + jnp.dot(a.astype(BF16), vexp, preferred_element_type=F32)
    else:
        row_id = lax.broadcasted_iota(jnp.int32, (CHUNK, rec.width), 0)

        def key_row(j, acc):
            krow = rec.k_at(d, r0 + j, 1)
            crow = rec.cum_scr[d, pl.ds(r0 + j, 1), :]
            vrow = rec.v_ref[pl.ds(r0 + j, 1), :]
            keep = (row_id <= j) if reverse else (row_id >= j)
            e = jnp.exp(jnp.minimum(cum - crow, 0.0))
            p = jnp.where(keep, q * krow * e, 0.0)
            return acc + _dot(p, rec.bmat[...]) * vrow

        o = lax.fori_loop(0, CHUNK, key_row, o)
    ke_b = (k * jnp.exp(tot - cum)).astype(BF16)
    ds = lax.dot_general(v_b, ke_b, _TN, preferred_element_type=F32)
    rec.st_scr[sd] = st * jnp.exp(tot) + ds * rec.bdw_f[...]
    rec.oacc[d, rows, :] = o


def _load_state(rec, s0_ref, sub, d):
    x = jnp.concatenate([s0_ref[sub, 0, d, h] for h in range(N_HEADS)], axis=0)
    y = _dot_exact_rhs(x, rec.tile_t[...]) * rec.bmat[...].astype(F32)
    rec.st_scr[sub * 2 + d] = y.T


def _store_state(rec, sfin_ref, sub, d):
    y = rec.st_scr[sub * 2 + d].T
    half = y[:, 0:LANES] + y[:, LANES:2 * LANES]
    x = (half + pltpu.roll(half, HEAD_V, 1))[:, 0:HEAD_V]
    dk = rec.width // N_HEADS
    for h in range(N_HEADS):
        sfin_ref[sub, d, h] = x[h * dk:(h + 1) * dk, :]


def _run_recurrence(rec, ok, s0_ref, sfin_ref):
    nsub = rec.nsub
    nchunk = rec.seq_len // CHUNK
    gsize = min(MAIN_GROUP, nchunk)
    ngroup = nchunk // gsize
    spp = max(1, min(nsub, MAIN_GROUP // gsize))
    nsp = nsub // spp
    for sub in range(nsub):
        for d in (0, 1):
            if s0_ref is None:
                rec.st_scr[sub * 2 + d] = jnp.zeros(rec.st_scr.shape[1:], F32)
            else:
                _load_state(rec, s0_ref, sub, d)

    def fast_all():
        def step(i, carry):
            sp = i // ngroup if (nsp > 1 and ngroup > 1) else (i if nsp > 1 else 0)
            gi = i % ngroup if (nsp > 1 and ngroup > 1) else (i if ngroup > 1 else 0)
            for u in range(gsize):
                s = gi * gsize + u
                for j in range(spp):
                    sub = sp * spp + j
                    _chunk_step(rec, sub, 0, s, True)
                    _chunk_step(rec, sub, 1, nchunk - 1 - s, True)
            return carry

        if nsp * ngroup == 1:
            step(0, 0)
        else:
            lax.fori_loop(0, nsp * ngroup, step, 0)

    def direct_all():
        def one(i, carry):
            sub = i // nchunk if nsub > 1 else 0
            s = i % nchunk if nsub > 1 else i
            _chunk_step(rec, sub, 0, s, False)
            _chunk_step(rec, sub, 1, nchunk - 1 - s, False)
            return carry

        lax.fori_loop(0, nsub * nchunk, one, 0)

    lax.cond(ok, fast_all, direct_all)
    if sfin_ref is not None:
        for sub in range(nsub):
            for d in (0, 1):
                _store_state(rec, sfin_ref, sub, d)


def _head_norm_gate(rec, gate_ref, gain_ref, o_ref):
    for rb in range(rec.nsub * rec.seq_len // ROW_BLOCK):
        rows = slice(rb * ROW_BLOCK, (rb + 1) * ROW_BLOCK)
        o = rec.oacc[0, rows, :] + rec.oacc[1, rows, :]
        ms = jnp.dot((o * o).astype(BF16), rec.bdv_b[...], preferred_element_type=F32) * (1.0 / HEAD_V)
        y = o * lax.rsqrt(ms + EPS) * gain_ref[0] * _silu(gate_ref[rows, :])
        o_ref[rows, :] = y.astype(o_ref.dtype)


def _split_refs(refs, n_in, has_s0, want_final):
    ins = refs[:n_in]
    pos = n_in
    s0_ref = None
    if has_s0:
        s0_ref = refs[pos]
        pos += 1
    o_ref = refs[pos]
    pos += 1
    sfin_ref = None
    if want_final:
        sfin_ref = refs[pos]
        pos += 1
    return ins, s0_ref, o_ref, sfin_ref, refs[pos:]


def _hgrn_kernel(*refs, seq_len, nsub, has_s0, want_final):
    ins, s0_ref, o_ref, sfin_ref, scr = _split_refs(refs, 13, has_s0, want_final)
    q_ref, i_ref, ff_ref, fb_ref, g_ref, gain_ref = ins[:6]
    k_scr, cum_scr, oacc, st_scr = scr
    q_at = lambda r0: q_ref[pl.ds(r0, CHUNK), :]
    k_at = lambda d, r0, n=CHUNK: k_scr[d, pl.ds(r0, n), :]
    rec = _Rec(q_at, k_at, i_ref, cum_scr, st_scr, oacc, ins[6:], W_GRP, seq_len, nsub)

    def pre(bi, rm):
        r0 = bi * GROWS if isinstance(bi, int) else pl.multiple_of(bi * GROWS, GROWS)
        rows = pl.ds(r0, GROWS)
        for d, f_ref in ((0, ff_ref), (1, fb_ref)):
            f = f_ref[rows, :]
            g = jnp.log(jnp.maximum(f, F_FLOOR))
            k_scr[d, rows, :] = 1.0 - f
            cum = _block_cumsum(rec.tri_ref[d], g)
            cum_scr[d, rows, :] = cum
            rm = _block_ranges(cum, d == 1, rm)
        return rm

    rmax = _run_prepass(pre, nsub * seq_len // GROWS, W_GRP)
    ok = jnp.max(rmax) < SAFE_RANGE

    _run_recurrence(rec, ok, s0_ref, sfin_ref)
    _head_norm_gate(rec, g_ref, gain_ref, o_ref)


def _gla_kernel(*refs, seq_len, nsub, has_s0, want_final):
    ins, s0_ref, o_ref, sfin_ref, scr = _split_refs(refs, 14, has_s0, want_final)
    q_ref, k_ref, v_ref, g_ref, gf_ref, gb_ref, gain_ref = ins[:7]
    cum_scr, oacc, st_scr = scr
    width = N_HEADS * GLA_DK
    q_at = lambda r0: q_ref[pl.ds(r0, CHUNK), :] * (GLA_DK ** -0.5)
    k_at = lambda d, r0, n=CHUNK: k_ref[pl.ds(r0, n), :]
    rec = _Rec(q_at, k_at, v_ref, cum_scr, st_scr, oacc, ins[7:], width, seq_len, nsub)

    def pre(bi, rm):
        r0 = bi * GROWS if isinstance(bi, int) else pl.multiple_of(bi * GROWS, GROWS)
        rows = pl.ds(r0, GROWS)
        for d, logdecay_ref in ((0, gf_ref), (1, gb_ref)):
            g = logdecay_ref[rows, :]
            cum = _block_cumsum(rec.tri_ref[d], g)
            cum_scr[d, rows, :] = cum
            rm = _block_ranges(cum, d == 1, rm)
        return rm

    rmax = _run_prepass(pre, nsub * seq_len // GROWS, width)
    ok = jnp.max(rmax) < SAFE_RANGE

    _run_recurrence(rec, ok, s0_ref, sfin_ref)
    _head_norm_gate(rec, g_ref, gain_ref, o_ref)


def _recurrence_constants(width):
    dk = width // N_HEADS
    r = np.arange(N_HEADS * CHUNK)[:, None]
    tri = np.kron(np.eye(GROUP, dtype=np.float32), np.tril(np.ones((CHUNK, CHUNK), np.float32)))
    tri = np.stack([tri, tri.T])
    bdw = (r // CHUNK == np.arange(width)[None, :] // dk).astype(np.float32)
    bdv = (r // CHUNK == np.arange(W_GRP)[None, :] // HEAD_V).astype(np.float32)
    t = np.arange(CHUNK)[:, None]
    s = np.arange(N_HEADS * CHUNK)[None, :] % CHUNK
    cmask = np.stack([(s <= t), (s >= t)]).astype(np.float32)
    bmat = (np.arange(width)[:, None] // dk == np.arange(W_GRP)[None, :] // HEAD_V).astype(np.float32)
    tile_t = np.tile(np.eye(HEAD_V, dtype=np.float32), (1, N_HEADS))
    return (jnp.asarray(tri, BF16), jnp.asarray(bdw, BF16), jnp.asarray(bdw), jnp.asarray(bdv, BF16),
            jnp.asarray(cmask), jnp.asarray(bmat, BF16), jnp.asarray(tile_t, BF16))


def _recurrent_mixer(kernel_fn, name, width, args, in_specs, state, layer, nseq, seq_len, want_final,
                     extra_scratch):
    dk = width // N_HEADS
    nsub = _seqs_per_block(nseq, seq_len)
    rows = nsub * seq_len
    consts = _recurrence_constants(width)
    args = list(args) + list(consts)
    in_specs = list(in_specs) + [_full_spec(c) for c in consts]
    if state is not None:
        args.append(state)
        in_specs.append(pl.BlockSpec((nsub, 1, 2, N_HEADS, dk, HEAD_V), lambda b: (b, layer, 0, 0, 0, 0)))
    out_shape = [jax.ShapeDtypeStruct((nseq * seq_len, W_GRP), BF16)]
    out_specs = [pl.BlockSpec((rows, W_GRP), lambda b: (b, 0))]
    if want_final:
        out_shape.append(jax.ShapeDtypeStruct((nseq, 2, N_HEADS, dk, HEAD_V), F32))
        out_specs.append(pl.BlockSpec((nsub, 2, N_HEADS, dk, HEAD_V), lambda b: (b, 0, 0, 0, 0)))
    res = pl.pallas_call(
        functools.partial(kernel_fn, seq_len=seq_len, nsub=nsub, has_s0=state is not None,
                          want_final=want_final),
        grid=(nseq // nsub,),
        in_specs=in_specs,
        out_specs=out_specs,
        out_shape=out_shape,
        scratch_shapes=[pltpu.VMEM((2, rows, W_GRP), F32)] * extra_scratch + [
            pltpu.VMEM((2, rows, width), F32), pltpu.VMEM((2, rows, W_GRP), F32),
            pltpu.VMEM((2 * nsub, W_GRP, width), F32)],
        compiler_params=_params(("arbitrary",)),
        name=name,
    )(*args)
    return res[0], (res[1] if want_final else None)


def _seqs_per_block(nseq, seq_len):
    return max(1, min(nseq, BLOCK_ROWS // seq_len))


def _hgrn_mixer(proj, gain, state, layer, nseq, seq_len, want_final):
    rows = _seqs_per_block(nseq, seq_len) * seq_len
    col = lambda c: pl.BlockSpec((rows, W_GRP), lambda b: (b, c))
    args = [proj] * 5 + [gain.reshape(DEPTH, 1, W_GRP)]
    in_specs = [col(0), col(1), col(2), col(3), col(4),
                pl.BlockSpec((1, 1, W_GRP), lambda b: (layer, 0, 0))]
    return _recurrent_mixer(_hgrn_kernel, "hgrn2_mixer", W_GRP, args, in_specs, state, layer, nseq,
                            seq_len, want_final, 1)


def _gla_mixer(proj, gain, state, layer, nseq, seq_len, want_final):
    width = N_HEADS * GLA_DK
    rows = _seqs_per_block(nseq, seq_len) * seq_len
    col256 = lambda c: pl.BlockSpec((rows, W_GRP), lambda b: (b, c))
    col128 = lambda c: pl.BlockSpec((rows, width), lambda b: (b, c))
    args = [proj] * 6 + [gain.reshape(DEPTH, 1, W_GRP)]
    in_specs = [col128(10), col128(11), col256(6), col256(7), col128(16), col128(17),
                pl.BlockSpec((1, 1, W_GRP), lambda b: (layer, 0, 0))]
    return _recurrent_mixer(_gla_kernel, "gla_mixer", width, args, in_specs, state, layer, nseq, seq_len,
                            want_final, 0)


PAD = 8
SCAN_TILE = 8
SCAN_UNROLL = 4
GELU_C = 0.7978845608028654


def _rgsc_kernel(*refs, seq_len, nsub, seg, has_h0, want_final):
    (cx_ref, cg_ref, db_ref, dc_ref, dv_ref, convw_ref, convb_ref, wr_ref, br_ref, wi_ref,
     bi_ref, lam_ref, sw_ref) = refs[:13]
    pos = 13
    h0_ref = None
    if has_h0:
        h0_ref = refs[pos]
        pos += 1
    oc_ref, od_ref = refs[pos:pos + 2]
    pos += 2
    hfin_ref = None
    if want_final:
        hfin_ref = refs[pos]
        pos += 1
    upad, a_scr, b_scr, h_scr = refs[pos:]

    slot = seq_len + 2 * PAD
    zpad = jnp.zeros((PAD, W_GRP), F32)
    for sub in range(nsub):
        upad[sub * slot:sub * slot + PAD, :] = zpad
        upad[sub * slot + PAD + seq_len:(sub + 1) * slot, :] = zpad
        upad[sub * slot + PAD:sub * slot + PAD + seq_len, :] = cx_ref[sub * seq_len:(sub + 1) * seq_len, :]
    rows8 = lax.broadcasted_iota(jnp.int32, (SCAN_TILE, W_GRP), 0)
    ntile = seq_len // SCAN_TILE

    for d in (0, 1):
        reverse = d == 1
        w = convw_ref[0, d]
        offs = [PAD + 3, PAD + 2, PAD + 1, PAD] if reverse else [PAD - 3, PAD - 2, PAD - 1, PAD]
        sp = _softplus(-lam_ref[0, d:d + 1, :])
        for sub in range(nsub):
            for rb in range(seq_len // ROW_BLOCK):
                src = sub * slot + rb * ROW_BLOCK
                base = sub * seq_len + rb * ROW_BLOCK
                win = upad[src:src + ROW_BLOCK + 2 * PAD, :]
                xc = convb_ref[0, d:d + 1, :]
                for j in range(4):
                    shift = (PAD - offs[j]) % (ROW_BLOCK + 2 * PAD)
                    tap = win if shift == 0 else pltpu.roll(win, shift, 0)
                    xc = xc + w[j:j + 1, :] * tap[PAD:PAD + ROW_BLOCK, :]
                r = _sigmoid(_dot(xc, wr_ref[d]) + br_ref[0, d:d + 1, :])
                ig = _sigmoid(_dot(xc, wi_ref[d]) + bi_ref[0, d:d + 1, :])
                log_a = -RG_C * r * sp
                a = jnp.exp(log_a)
                a_scr[base:base + ROW_BLOCK, :] = a
                y = 1.0 - a * a
                root = jnp.where(y > 0.0, y * lax.rsqrt(y), 0.0)
                b_scr[base:base + ROW_BLOCK, :] = root * (ig * xc)

        if has_h0:
            h0 = tuple(h0_ref[sub, 0, d:d + 1, :] for sub in range(nsub))
        else:
            h0 = tuple(jnp.zeros((1, W_GRP), F32) for _ in range(nsub))

        def step(ti, carry, d=d, reverse=reverse):
            t = (ntile - 1 - ti) if reverse else ti
            out = []
            for sub in range(nsub):
                r0 = pl.multiple_of(sub * seq_len + t * SCAN_TILE, SCAN_TILE)
                a = a_scr[pl.ds(r0, SCAN_TILE), :]
                b = b_scr[pl.ds(r0, SCAN_TILE), :]
                for s in (1, 2, 4):
                    if reverse:
                        keep = rows8 <= SCAN_TILE - 1 - s
                        shift = SCAN_TILE - s
                    else:
                        keep = rows8 >= s
                        shift = s
                    a_s = jnp.where(keep, pltpu.roll(a, shift, 0), 1.0)
                    b_s = jnp.where(keep, pltpu.roll(b, shift, 0), 0.0)
                    b = b + a * b_s
                    a = a * a_s
                h = a * carry[sub] + b
                if d == 0:
                    h_scr[pl.ds(r0, SCAN_TILE), :] = h
                else:
                    h_scr[pl.ds(r0, SCAN_TILE), :] += h
                out.append(h[0:1, :] if reverse else h[SCAN_TILE - 1:SCAN_TILE, :])
            return tuple(out)

        hlast = lax.fori_loop(0, ntile, step, h0, unroll=SCAN_UNROLL // nsub if nsub < SCAN_UNROLL else 1)
        if want_final:
            for sub in range(nsub):
                hfin_ref[sub, d:d + 1, :] = hlast[sub]

    sw = sw_ref[0]
    nblock = nsub * seq_len // ROW_BLOCK
    for rb in range(nblock):
        rows = slice(rb * ROW_BLOCK, (rb + 1) * ROW_BLOCK)
        x = cg_ref[rows, :]
        gelu = x * (0.5 + 0.5 * jnp.tanh(x * (GELU_C + (GELU_C * 0.044715) * (x * x))))
        oc_ref[rows, :] = (h_scr[rows, :] * gelu).astype(oc_ref.dtype)
        upad[PAD + rb * ROW_BLOCK:PAD + (rb + 1) * ROW_BLOCK, :] = dc_ref[rows, :] * dv_ref[rows, :]
    for rb in range(nblock):
        base = rb * ROW_BLOCK
        rows = slice(base, base + ROW_BLOCK)
        posn = (lax.broadcasted_iota(jnp.int32, (ROW_BLOCK, W_GRP), 0) + base) % seg
        win = upad[base:base + ROW_BLOCK + 2 * PAD, :]
        nwin = ROW_BLOCK + 2 * PAD
        mid = slice(PAD, PAD + ROW_BLOCK)
        left = jnp.where(posn != 0, pltpu.roll(win, 1, 0)[mid, :], 0.0)
        right = jnp.where(posn != seg - 1, pltpu.roll(win, nwin - 1, 0)[mid, :], 0.0)
        y = sw[0:1, :] * left + sw[1:2, :] * win[mid, :] + sw[2:3, :] * right
        od_ref[rows, :] = (db_ref[rows, :] * y).astype(od_ref.dtype)


def _rgsc_mixer(proj, conv_w, conv_b, wr_bd, b_r, wi_bd, b_i, lam, sconv_w, h0, layer, nseq, seq_len,
                seg, want_final):
    nsub = _seqs_per_block(nseq, seq_len)
    rows = nsub * seq_len
    col = lambda c: pl.BlockSpec((rows, W_GRP), lambda b: (b, c))
    lay3 = lambda a: pl.BlockSpec((1,) + a.shape[1:], lambda b: (layer, 0, 0))
    lay4 = lambda a: pl.BlockSpec((1,) + a.shape[1:], lambda b: (layer, 0, 0, 0))
    params = [conv_w, conv_b, wr_bd, b_r, wi_bd, b_i, lam, sconv_w]
    args = [proj] * 5 + params
    in_specs = [col(9), col(10), col(11), col(12), col(13), lay4(conv_w), lay3(conv_b), _full_spec(wr_bd),
                lay3(b_r), _full_spec(wi_bd), lay3(b_i), lay3(lam), lay3(sconv_w)]
    if h0 is not None:
        args.append(h0)
        in_specs.append(pl.BlockSpec((nsub, 1, 2, W_GRP), lambda b: (b, layer, 0, 0)))
    out_shape = [jax.ShapeDtypeStruct((nseq * seq_len, W_GRP), BF16)] * 2
    out_specs = [pl.BlockSpec((rows, W_GRP), lambda b: (b, 0))] * 2
    if want_final:
        out_shape.append(jax.ShapeDtypeStruct((nseq, 2, W_GRP), F32))
        out_specs.append(pl.BlockSpec((nsub, 2, W_GRP), lambda b: (b, 0, 0)))
    res = pl.pallas_call(
        functools.partial(_rgsc_kernel, seq_len=seq_len, nsub=nsub, seg=seg, has_h0=h0 is not None,
                          want_final=want_final),
        grid=(nseq // nsub,),
        in_specs=in_specs,
        out_specs=out_specs,
        out_shape=out_shape,
        scratch_shapes=[pltpu.VMEM((nsub * (seq_len + 2 * PAD), W_GRP), F32), pltpu.VMEM((rows, W_GRP), F32),
                        pltpu.VMEM((rows, W_GRP), F32), pltpu.VMEM((rows, W_GRP), F32)],
        compiler_params=_params(("arbitrary",)),
        name="rglru_sconv_mixer",
    )(*args)
    return res[0], res[1], (res[2] if want_final else None)


def _block_diag(w):
    two, h, c, _ = w.shape
    eye = jnp.eye(h, dtype=w.dtype)
    full = w[:, :, :, None, :] * eye[None, :, None, :, None]
    return full.reshape(two, h * c, h * c)


def kernel(x_prompt, x_sample, state_hgrn, state_gla, state_rglru, c, c_ctx, norm1_g, norm2_g, ada_w, ada_b, w_in, w_out, hgrn_lb_logits, hgrn_norm_g, gla_wa2, gla_ba2, gla_norm_g, rg_conv_w, rg_conv_b, rg_w_r, rg_b_r, rg_w_i, rg_b_i, rg_lambda, sconv_w, mlp_w1, mlp_w2, final_norm_g):
    b_ctx, l_ctx, _ = x_prompt.shape
    b_lat, l_lat, _ = x_sample.shape

    cvec = jnp.concatenate([c, c_ctx[None, :], jnp.zeros((8 - b_lat - 1, D_MODEL), F32)], axis=0)
    mod = _modulation(cvec, ada_w, ada_b).reshape(DEPTH, 8, 6, D_MODEL)

    w_cat = _build_proj_weight(w_in, gla_wa2)

    xp = x_prompt.reshape(b_ctx * l_ctx, D_MODEL)
    xs = x_sample.reshape(b_lat * l_lat, D_MODEL)
    new_h, new_g, new_r = [], [], []
    for l in range(DEPTH):
        wr_bd = _block_diag(rg_w_r[l]).astype(BF16)
        wi_bd = _block_diag(rg_w_i[l]).astype(BF16)
        mod_ctx = mod[l, b_lat:b_lat + 1]
        mod_lat = mod[l, :b_lat]
        last = l == DEPTH - 1
        streams = (
            (xp, mod_ctx, b_ctx, l_ctx, l_ctx, None, None, None, True),
            (xs, mod_lat, b_lat, l_lat, GRID_W, state_hgrn, state_gla, state_rglru, False),
        )
        outs = []
        for (x, m, nseq, seq_len, seg, s_h, s_g, s_r, want_final) in streams:
            proj = _project(x, m, norm1_g, w_cat, hgrn_lb_logits, gla_ba2, l)
            o_a, f_h = _hgrn_mixer(proj, hgrn_norm_g, s_h, l, nseq, seq_len, want_final)
            o_b, f_g = _gla_mixer(proj, gla_norm_g, s_g, l, nseq, seq_len, want_final)
            o_c, o_d, f_r = _rgsc_mixer(proj, rg_conv_w, rg_conv_b, wr_bd, rg_b_r, wi_bd, rg_b_i,
                                        rg_lambda, sconv_w, s_r, l, nseq, seq_len, seg, want_final)
            x_new = _out_mlp(x, o_a, o_b, o_c, o_d, m, norm2_g, final_norm_g, w_out, mlp_w1, mlp_w2, l, last)
            outs.append((x_new, f_h, f_g, f_r))
        xp, f_h, f_g, f_r = outs[0]
        xs = outs[1][0]
        new_h.append(f_h)
        new_g.append(f_g)
        new_r.append(f_r)
    y_prompt = xp.reshape(b_ctx, l_ctx, D_MODEL)
    y_sample = xs.reshape(b_lat, l_lat, D_MODEL)
    return (y_prompt, y_sample, jnp.stack(new_h, axis=1), jnp.stack(new_g, axis=1),
            jnp.stack(new_r, axis=1))
```

```python
import functools

import numpy as np
import jax
import jax.numpy as jnp
from jax import lax
from jax.experimental import pallas as pl
from jax.experimental.pallas import tpu as pltpu

F32 = jnp.float32
BF16 = jnp.bfloat16

D_MODEL = 1024
DEPTH = 2
GRID_W = 64
N_HEADS = 4
W_GRP = 256
HEAD_V = 64
HGRN_DK = 64
GLA_DK = 32
GLA_RANK = 16
GLA_NORMALIZER = 16.0
RG_C = 8.0
D_FF = 4 * D_MODEL
EPS = 1e-6
F_FLOOR = 1e-20

PROJ_COLS = 3584

CHUNK = 64
HALF = CHUNK // 2
assert CHUNK == HEAD_V
GROUP = 4
GROWS = GROUP * CHUNK
MAIN_GROUP = 32
BLOCK_ROWS = 2048
LANES = 128
SAFE_RANGE = 80.0
TOK_TILE = 1024
PROJ_TM = 512
PROJ_TN = 512
FF_TILE = 1024
MLP_ROW_SPLIT = 2
PROLOGUE_ROWS = 256
ROW_BLOCK = 256
VMEM_LIMIT = 56 * 1024 * 1024

_NT = (((1,), (1,)), ((), ()))
_TN = (((0,), (0,)), ((), ()))


def _dot(a, b):
    return jnp.dot(a.astype(BF16), b.astype(BF16), preferred_element_type=F32)


def _dot_nt(a, b):
    return lax.dot_general(a.astype(BF16), b.astype(BF16), _NT, preferred_element_type=F32)


def _dot_tn(a, b):
    return lax.dot_general(a.astype(BF16), b.astype(BF16), _TN, preferred_element_type=F32)


def _sigmoid(x):
    return 1.0 / (1.0 + jnp.exp(-x))


def _silu(x):
    return x * _sigmoid(x)


def _log1p(y):
    u = 1.0 + y
    return jnp.where(u == 1.0, y, jnp.log(u) * (y / (u - 1.0)))


def _softplus(x):
    return jnp.maximum(x, 0.0) + _log1p(jnp.exp(-jnp.abs(x)))


def _log_sigmoid(z):
    return jnp.minimum(z, 0.0) - jnp.log(1.0 + jnp.exp(-jnp.abs(z)))


def _rms(x):
    return x * lax.rsqrt(jnp.mean(x * x, axis=-1, keepdims=True) + EPS)


def _params(sem):
    return pltpu.CompilerParams(dimension_semantics=sem, vmem_limit_bytes=VMEM_LIMIT)


def _full_spec(a):
    zeros = (0,) * a.ndim
    return pl.BlockSpec(a.shape, lambda *_: zeros)


def _mod_kernel(c_ref, w_ref, b_ref, o_ref):
    s = _silu(c_ref[...])
    o_ref[0] = _dot(s, w_ref[0]) + b_ref[0]


def _modulation(cvec, ada_w, ada_b):
    tn = 2048
    return pl.pallas_call(
        _mod_kernel,
        grid=(DEPTH, 6 * D_MODEL // tn),
        in_specs=[pl.BlockSpec((8, D_MODEL), lambda l, j: (0, 0)),
                  pl.BlockSpec((1, D_MODEL, tn), lambda l, j: (l, 0, j)),
                  pl.BlockSpec((1, 1, tn), lambda l, j: (l, 0, j))],
        out_specs=pl.BlockSpec((1, 8, tn), lambda l, j: (l, 0, j)),
        out_shape=jax.ShapeDtypeStruct((DEPTH, 8, 6 * D_MODEL), F32),
        compiler_params=_params(("arbitrary", "arbitrary")),
        name="adaln_modulation",
    )(cvec, ada_w, ada_b.reshape(DEPTH, 1, 6 * D_MODEL))


LOWRANK_COL = 2048
TAIL_COL = LOWRANK_COL + 2 * GLA_RANK
TAIL_WIDTH = 1280
WCAT_LANES = 256


def _wcat_kernel(w_ref, wa2t_ref, o_ref):
    o_ref[0, 0:LOWRANK_COL, :] = w_ref[0, 0:LOWRANK_COL, :].astype(BF16)
    z = jnp.dot(wa2t_ref[0], w_ref[0, LOWRANK_COL:TAIL_COL, :], preferred_element_type=F32,
                precision=lax.Precision.HIGHEST)
    o_ref[0, LOWRANK_COL:LOWRANK_COL + 2 * LANES, :] = z.astype(BF16)
    o_ref[0, LOWRANK_COL + 2 * LANES:PROJ_COLS, :] = w_ref[0, TAIL_COL:TAIL_COL + TAIL_WIDTH, :].astype(BF16)


def _build_proj_weight(w_in, wa2):
    ncol = w_in.shape[2]
    assert ncol == TAIL_COL + TAIL_WIDTH and LOWRANK_COL + 2 * LANES + TAIL_WIDTH == PROJ_COLS
    n = N_HEADS * GLA_DK
    wa2t = jnp.zeros((DEPTH, 2 * n, 2 * GLA_RANK), F32)
    wa2t = wa2t.at[:, 0:n, 0:GLA_RANK].set(jnp.swapaxes(wa2[:, 0], 1, 2))
    wa2t = wa2t.at[:, n:2 * n, GLA_RANK:2 * GLA_RANK].set(jnp.swapaxes(wa2[:, 1], 1, 2))
    w_t = jnp.swapaxes(w_in, 1, 2)
    return pl.pallas_call(
        _wcat_kernel,
        grid=(DEPTH, D_MODEL // WCAT_LANES),
        in_specs=[pl.BlockSpec((1, ncol, WCAT_LANES), lambda l, i: (l, 0, i)),
                  pl.BlockSpec((1, 2 * n, 2 * GLA_RANK), lambda l, i: (l, 0, 0))],
        out_specs=pl.BlockSpec((1, PROJ_COLS, WCAT_LANES), lambda l, i: (l, 0, i)),
        out_shape=jax.ShapeDtypeStruct((DEPTH, PROJ_COLS, D_MODEL), BF16),
        compiler_params=_params(("arbitrary", "arbitrary")),
        name="build_proj_weight",
    )(w_t, wa2t)


HGRN_GATE_COL = 2 * W_GRP
GLA_GATE_COL = LOWRANK_COL
assert HGRN_GATE_COL % PROJ_TN == 0 and GLA_GATE_COL % PROJ_TN == 0 and PROJ_TN == 2 * W_GRP


def _hgrn_lower_bound(logits, layer):
    mx = logits[0]
    for i in range(1, DEPTH):
        mx = jnp.maximum(mx, logits[i])
    ex = [jnp.exp(logits[i] - mx) for i in range(DEPTH)]
    den = ex[0]
    for i in range(1, DEPTH):
        den = den + ex[i]
    sm = [e / den for e in ex]
    csum = sm[0]
    for i in range(1, layer + 1):
        csum = csum + sm[i]
    return csum - sm[0]


def _proj_kernel(x_ref, mod_ref, g_ref, w_ref, lbl_ref, ba_ref, o_ref, *, layer):
    shift = mod_ref[0, 0:1, :]
    scale = mod_ref[0, 1:2, :]
    h = (_rms(x_ref[...]) * g_ref[0] * (1.0 + scale) + shift).astype(BF16)
    lb = _hgrn_lower_bound(lbl_ref[...], layer)
    lb_row = jnp.concatenate([lb[0:1, :], lb[1:2, :]], axis=1)
    ba_row = jnp.concatenate([ba_ref[0, 0:1, :], ba_ref[0, 1:2, :]], axis=1)
    for n in range(PROJ_COLS // PROJ_TN):
        cols = slice(n * PROJ_TN, (n + 1) * PROJ_TN)
        acc = lax.dot_general(h, w_ref[0, cols, :], _NT, preferred_element_type=F32)
        if n * PROJ_TN == HGRN_GATE_COL:
            acc = lb_row + (1.0 - lb_row) * _sigmoid(acc)
        if n * PROJ_TN == GLA_GATE_COL:
            z = acc[:, 0:W_GRP] + ba_row
            acc = jnp.concatenate([_log_sigmoid(z) * (1.0 / GLA_NORMALIZER), acc[:, W_GRP:]], axis=1)
        o_ref[:, cols] = acc


def _mod_index(ntok, nmod, tile):
    tiles_per_mod = (ntok // tile) // nmod if nmod > 1 else 1
    if nmod > 1:
        return lambda i, *_: (i // tiles_per_mod, 0, 0)
    return lambda i, *_: (0, 0, 0)


def _project(x, mod, norm_g, w_cat, lb_logits, ba2, layer):
    ntok = x.shape[0]
    return pl.pallas_call(
        functools.partial(_proj_kernel, layer=layer),
        grid=(ntok // PROJ_TM,),
        in_specs=[pl.BlockSpec((PROJ_TM, D_MODEL), lambda i: (i, 0)),
                  pl.BlockSpec((1, 6, D_MODEL), _mod_index(ntok, mod.shape[0], PROJ_TM)),
                  pl.BlockSpec((1, 1, D_MODEL), lambda i: (layer, 0, 0)),
                  pl.BlockSpec((1, PROJ_COLS, D_MODEL), lambda i: (layer, 0, 0),
                               pipeline_mode=pl.Buffered(1)),
                  _full_spec(lb_logits),
                  pl.BlockSpec((1, 2, N_HEADS * GLA_DK), lambda i: (layer, 0, 0))],
        out_specs=pl.BlockSpec((PROJ_TM, PROJ_COLS), lambda i: (i, 0)),
        out_shape=jax.ShapeDtypeStruct((ntok, PROJ_COLS), F32),
        compiler_params=_params(("arbitrary",)),
        name="norm_in_proj",
    )(x, mod, norm_g.reshape(DEPTH, 1, D_MODEL), w_cat, lb_logits, ba2)


def _mlp_kernel(x_ref, oa_ref, ob_ref, oc_ref, od_ref, mod_ref, g2_ref, fg_ref, wout_ref,
                w1_ref, w2_ref, out_ref, h2_scr, *, final_norm):
    j = pl.program_id(1)

    @pl.when(j == 0)
    def _():
        wout = wout_ref[0].astype(BF16)
        nslab = TOK_TILE // PROLOGUE_ROWS
        for s in range(nslab):
            rows = slice(s * PROLOGUE_ROWS, (s + 1) * PROLOGUE_ROWS)
            mixed = jnp.concatenate([oa_ref[rows, :], ob_ref[rows, :], oc_ref[rows, :], od_ref[rows, :]],
                                    axis=1)
            mix = jnp.dot(mixed, wout, preferred_element_type=F32)
            x1 = x_ref[rows, :] + mod_ref[0, 2:3, :] * mix
            out_ref[rows, :] = x1
            h2 = _rms(x1) * g2_ref[0] * (1.0 + mod_ref[0, 4:5, :]) + mod_ref[0, 3:4, :]
            h2_scr[rows, :] = h2.astype(BF16)

    w1 = w1_ref[0].astype(BF16)
    w2 = w2_ref[0].astype(BF16)
    for r in range(MLP_ROW_SPLIT):
        rows = slice(r * (TOK_TILE // MLP_ROW_SPLIT), (r + 1) * (TOK_TILE // MLP_ROW_SPLIT))
        t = jnp.dot(h2_scr[rows, :], w1, preferred_element_type=F32)
        t = jnp.square(jnp.maximum(t, 0.0)).astype(BF16)
        out_ref[rows, :] += mod_ref[0, 5:6, :] * jnp.dot(t, w2, preferred_element_type=F32)

    if final_norm:
        @pl.when(j == pl.num_programs(1) - 1)
        def _():
            out_ref[...] = _rms(out_ref[...]) * fg_ref[...]


def _out_mlp(x, o_a, o_b, o_c, o_d, mod, norm2_g, final_g, w_out, w1, w2, layer, final_norm):
    ntok = x.shape[0]
    tok = lambda i, j: (i, 0)
    return pl.pallas_call(
        functools.partial(_mlp_kernel, final_norm=final_norm),
        grid=(ntok // TOK_TILE, D_FF // FF_TILE),
        in_specs=[pl.BlockSpec((TOK_TILE, D_MODEL), tok),
                  pl.BlockSpec((TOK_TILE, W_GRP), tok),
                  pl.BlockSpec((TOK_TILE, W_GRP), tok),
                  pl.BlockSpec((TOK_TILE, W_GRP), tok),
                  pl.BlockSpec((TOK_TILE, W_GRP), tok),
                  pl.BlockSpec((1, 6, D_MODEL), _mod_index(ntok, mod.shape[0], TOK_TILE)),
                  pl.BlockSpec((1, 1, D_MODEL), lambda i, j: (layer, 0, 0)),
                  pl.BlockSpec((1, D_MODEL), lambda i, j: (0, 0)),
                  pl.BlockSpec((1, D_MODEL, D_MODEL), lambda i, j: (layer, 0, 0)),
                  pl.BlockSpec((1, D_MODEL, FF_TILE), lambda i, j: (layer, 0, j)),
                  pl.BlockSpec((1, FF_TILE, D_MODEL), lambda i, j: (layer, j, 0))],
        out_specs=pl.BlockSpec((TOK_TILE, D_MODEL), tok),
        out_shape=jax.ShapeDtypeStruct((ntok, D_MODEL), F32),
        scratch_shapes=[pltpu.VMEM((TOK_TILE, D_MODEL), BF16)],
        compiler_params=_params(("arbitrary", "arbitrary")),
        name="out_proj_mlp",
    )(x, o_a, o_b, o_c, o_d, mod, norm2_g.reshape(DEPTH, 1, D_MODEL), final_g.reshape(1, D_MODEL),
      w_out, w1, w2)


def _split3(x):
    hi = x.astype(BF16)
    r1 = x - hi.astype(F32)
    mid = r1.astype(BF16)
    lo = (r1 - mid.astype(F32)).astype(BF16)
    return hi, mid, lo


def _dot_exact_rhs(x, m):
    hi, mid, lo = _split3(x)
    s = jnp.dot(hi, m, preferred_element_type=F32)
    s += jnp.dot(mid, m, preferred_element_type=F32)
    s += jnp.dot(lo, m, preferred_element_type=F32)
    return s


def _block_cumsum(tri, g):
    hi = g.astype(BF16)
    lo = (g - hi.astype(F32)).astype(BF16)
    return jnp.dot(tri, hi, preferred_element_type=F32) + jnp.dot(tri, lo, preferred_element_type=F32)


def _run_prepass(pre, nblock, width):
    rmax = jnp.zeros((1, width), F32)
    if nblock % 2:
        for bi in range(nblock):
            rmax = pre(bi, rmax)
        return rmax
    if nblock == 2:
        return pre(1, pre(0, rmax))
    return lax.fori_loop(0, nblock // 2, lambda i, rm: pre(2 * i + 1, pre(2 * i, rm)), rmax)


def _half_ranges(cum, reverse):
    if reverse:
        second = -cum[HALF:HALF + 1, :]
        first = -(cum[0:1, :] - cum[HALF:HALF + 1, :])
    else:
        first = -cum[HALF - 1:HALF, :]
        second = -(cum[CHUNK - 1:CHUNK, :] - cum[HALF - 1:HALF, :])
    return jnp.maximum(first, second)


def _block_ranges(cum, reverse, rm):
    for c in range(GROUP):
        rm = jnp.maximum(rm, _half_ranges(cum[c * CHUNK:(c + 1) * CHUNK, :], reverse))
    return rm


class _Rec:
    def __init__(self, q_at, k_at, v_ref, cum_scr, st_scr, oacc, consts, width, seq_len, nsub):
        self.q_at, self.k_at, self.v_ref = q_at, k_at, v_ref
        self.cum_scr, self.st_scr, self.oacc = cum_scr, st_scr, oacc
        (self.tri_ref, self.bdw_b, self.bdw_f, self.bdv_b, self.cmask, self.bmat,
         self.tile_t) = consts
        self.width, self.seq_len, self.nsub = width, seq_len, nsub


def _chunk_step(rec, sub, d, c, fast):
    reverse = d == 1
    mid_row = HALF if reverse else HALF - 1
    last_row = 0 if reverse else CHUNK - 1
    sd = sub * 2 + d
    r0 = sub * rec.seq_len + c * CHUNK
    if not isinstance(r0, int):
        r0 = pl.multiple_of(r0, CHUNK)
    rows = pl.ds(r0, CHUNK)
    q = rec.q_at(r0)
    k = rec.k_at(d, r0)
    v_b = rec.v_ref[rows, :].astype(BF16)
    cum = rec.cum_scr[d, rows, :]
    tot = cum[last_row:last_row + 1, :]
    st = rec.st_scr[sd]
    o = _dot_nt(q * jnp.exp(cum), st)
    if fast:
        cm = cum[mid_row:mid_row + 1, :]
        qm = q * jnp.exp(cum - cm)
        km_b = (k * jnp.exp(cm - cum)).astype(BF16)
        kbd = jnp.concatenate([km_b] * N_HEADS, axis=0) * rec.bdw_b[...]
        sc = lax.dot_general(qm.astype(BF16), kbd, _NT, preferred_element_type=F32)
        a = jnp.where(rec.cmask[d] > 0.5, sc, 0.0)
        vexp = jnp.concatenate([v_b] * N_HEADS, axis=0) * rec.bdv_b[...]
        o = o + jnp.dot(a.astype(BF16), vexp, preferred_element_type=F32)
    else:
        row_id = lax.broadcasted_iota(jnp.int32, (CHUNK, rec.width), 0)

        def key_row(j, acc):
            krow = rec.k_at(d, r0 + j, 1)
            crow = rec.cum_scr[d, pl.ds(r0 + j, 1), :]
            vrow = rec.v_ref[pl.ds(r0 + j, 1), :]
            keep = (row_id <= j) if reverse else (row_id >= j)
            e = jnp.exp(jnp.minimum(cum - crow, 0.0))
            p = jnp.where(keep, q * krow * e, 0.0)
            return acc + _dot(p, rec.bmat[...]) * vrow

        o = lax.fori_loop(0, CHUNK, key_row, o)
    ke_b = (k * jnp.exp(tot - cum)).astype(BF16)
    ds = lax.dot_general(v_b, ke_b, _TN, preferred_element_type=F32)
    rec.st_scr[sd] = st * jnp.exp(tot) + ds * rec.bdw_f[...]
    rec.oacc[d, rows, :] = o


def _load_state(rec, s0_ref, sub, d):
    x = jnp.concatenate([s0_ref[sub, 0, d, h] for h in range(N_HEADS)], axis=0)
    y = _dot_exact_rhs(x, rec.tile_t[...]) * rec.bmat[...].astype(F32)
    rec.st_scr[sub * 2 + d] = y.T


def _store_state(rec, sfin_ref, sub, d):
    y = rec.st_scr[sub * 2 + d].T
    half = y[:, 0:LANES] + y[:, LANES:2 * LANES]
    x = (half + pltpu.roll(half, HEAD_V, 1))[:, 0:HEAD_V]
    dk = rec.width // N_HEADS
    for h in range(N_HEADS):
        sfin_ref[sub, d, h] = x[h * dk:(h + 1) * dk, :]


def _run_recurrence(rec, ok, s0_ref, sfin_ref):
    nsub = rec.nsub
    nchunk = rec.seq_len // CHUNK
    gsize = min(MAIN_GROUP, nchunk)
    ngroup = nchunk // gsize
    spp = max(1, min(nsub, MAIN_GROUP // gsize))
    nsp = nsub // spp
    for sub in range(nsub):
        for d in (0, 1):
            if s0_ref is None:
                rec.st_scr[sub * 2 + d] = jnp.zeros(rec.st_scr.shape[1:], F32)
            else:
                _load_state(rec, s0_ref, sub, d)

    def fast_all():
        def step(i, carry):
            sp = i // ngroup if (nsp > 1 and ngroup > 1) else (i if nsp > 1 else 0)
            gi = i % ngroup if (nsp > 1 and ngroup > 1) else (i if ngroup > 1 else 0)
            for u in range(gsize):
                s = gi * gsize + u
                for j in range(spp):
                    sub = sp * spp + j
                    _chunk_step(rec, sub, 0, s, True)
                    _chunk_step(rec, sub, 1, nchunk - 1 - s, True)
            return carry

        if nsp * ngroup == 1:
            step(0, 0)
        else:
            lax.fori_loop(0, nsp * ngroup, step, 0)

    def direct_all():
        def one(i, carry):
            sub = i // nchunk if nsub > 1 else 0
            s = i % nchunk if nsub > 1 else i
            _chunk_step(rec, sub, 0, s, False)
            _chunk_step(rec, sub, 1, nchunk - 1 - s, False)
            return carry

        lax.fori_loop(0, nsub * nchunk, one, 0)

    lax.cond(ok, fast_all, direct_all)
    if sfin_ref is not None:
        for sub in range(nsub):
            for d in (0, 1):
                _store_state(rec, sfin_ref, sub, d)


def _head_norm_gate(rec, gate_ref, gain_ref, o_ref):
    for rb in range(rec.nsub * rec.seq_len // ROW_BLOCK):
        rows = slice(rb * ROW_BLOCK, (rb + 1) * ROW_BLOCK)
        o = rec.oacc[0, rows, :] + rec.oacc[1, rows, :]
        ms = jnp.dot((o * o).astype(BF16), rec.bdv_b[...], preferred_element_type=F32) * (1.0 / HEAD_V)
        y = o * lax.rsqrt(ms + EPS) * gain_ref[0] * _silu(gate_ref[rows, :])
        o_ref[rows, :] = y.astype(o_ref.dtype)


def _split_refs(refs, n_in, has_s0, want_final):
    ins = refs[:n_in]
    pos = n_in
    s0_ref = None
    if has_s0:
        s0_ref = refs[pos]
        pos += 1
    o_ref = refs[pos]
    pos += 1
    sfin_ref = None
    if want_final:
        sfin_ref = refs[pos]
        pos += 1
    return ins, s0_ref, o_ref, sfin_ref, refs[pos:]


def _hgrn_kernel(*refs, seq_len, nsub, has_s0, want_final):
    ins, s0_ref, o_ref, sfin_ref, scr = _split_refs(refs, 13, has_s0, want_final)
    q_ref, i_ref, ff_ref, fb_ref, g_ref, gain_ref = ins[:6]
    k_scr, cum_scr, oacc, st_scr = scr
    q_at = lambda r0: q_ref[pl.ds(r0, CHUNK), :]
    k_at = lambda d, r0, n=CHUNK: k_scr[d, pl.ds(r0, n), :]
    rec = _Rec(q_at, k_at, i_ref, cum_scr, st_scr, oacc, ins[6:], W_GRP, seq_len, nsub)

    def pre(bi, rm):
        r0 = bi * GROWS if isinstance(bi, int) else pl.multiple_of(bi * GROWS, GROWS)
        rows = pl.ds(r0, GROWS)
        for d, f_ref in ((0, ff_ref), (1, fb_ref)):
            f = f_ref[rows, :]
            g = jnp.log(jnp.maximum(f, F_FLOOR))
            k_scr[d, rows, :] = 1.0 - f
            cum = _block_cumsum(rec.tri_ref[d], g)
            cum_scr[d, rows, :] = cum
            rm = _block_ranges(cum, d == 1, rm)
        return rm

    rmax = _run_prepass(pre, nsub * seq_len // GROWS, W_GRP)
    ok = jnp.max(rmax) < SAFE_RANGE

    _run_recurrence(rec, ok, s0_ref, sfin_ref)
    _head_norm_gate(rec, g_ref, gain_ref, o_ref)


def _gla_kernel(*refs, seq_len, nsub, has_s0, want_final):
    ins, s0_ref, o_ref, sfin_ref, scr = _split_refs(refs, 14, has_s0, want_final)
    q_ref, k_ref, v_ref, g_ref, gf_ref, gb_ref, gain_ref = ins[:7]
    cum_scr, oacc, st_scr = scr
    width = N_HEADS * GLA_DK
    q_at = lambda r0: q_ref[pl.ds(r0, CHUNK), :] * (GLA_DK ** -0.5)
    k_at = lambda d, r0, n=CHUNK: k_ref[pl.ds(r0, n), :]
    rec = _Rec(q_at, k_at, v_ref, cum_scr, st_scr, oacc, ins[7:], width, seq_len, nsub)

    def pre(bi, rm):
        r0 = bi * GROWS if isinstance(bi, int) else pl.multiple_of(bi * GROWS, GROWS)
        rows = pl.ds(r0, GROWS)
        for d, logdecay_ref in ((0, gf_ref), (1, gb_ref)):
            g = logdecay_ref[rows, :]
            cum = _block_cumsum(rec.tri_ref[d], g)
            cum_scr[d, rows, :] = cum
            rm = _block_ranges(cum, d == 1, rm)
        return rm

    rmax = _run_prepass(pre, nsub * seq_len // GROWS, width)
    ok = jnp.max(rmax) < SAFE_RANGE

    _run_recurrence(rec, ok, s0_ref, sfin_ref)
    _head_norm_gate(rec, g_ref, gain_ref, o_ref)


def _recurrence_constants(width):
    dk = width // N_HEADS
    r = np.arange(N_HEADS * CHUNK)[:, None]
    tri = np.kron(np.eye(GROUP, dtype=np.float32), np.tril(np.ones((CHUNK, CHUNK), np.float32)))
    tri = np.stack([tri, tri.T])
    bdw = (r // CHUNK == np.arange(width)[None, :] // dk).astype(np.float32)
    bdv = (r // CHUNK == np.arange(W_GRP)[None, :] // HEAD_V).astype(np.float32)
    t = np.arange(CHUNK)[:, None]
    s = np.arange(N_HEADS * CHUNK)[None, :] % CHUNK
    cmask = np.stack([(s <= t), (s >= t)]).astype(np.float32)
    bmat = (np.arange(width)[:, None] // dk == np.arange(W_GRP)[None, :] // HEAD_V).astype(np.float32)
    tile_t = np.tile(np.eye(HEAD_V, dtype=np.float32), (1, N_HEADS))
    return (jnp.asarray(tri, BF16), jnp.asarray(bdw, BF16), jnp.asarray(bdw), jnp.asarray(bdv, BF16),
            jnp.asarray(cmask), jnp.asarray(bmat, BF16), jnp.asarray(tile_t, BF16))


def _recurrent_mixer(kernel_fn, name, width, args, in_specs, state, layer, nseq, seq_len, want_final,
                     extra_scratch):
    dk = width // N_HEADS
    nsub = _seqs_per_block(nseq, seq_len)
    rows = nsub * seq_len
    consts = _recurrence_constants(width)
    args = list(args) + list(consts)
    in_specs = list(in_specs) + [_full_spec(c) for c in consts]
    if state is not None:
        args.append(state)
        in_specs.append(pl.BlockSpec((nsub, 1, 2, N_HEADS, dk, HEAD_V), lambda b: (b, layer, 0, 0, 0, 0)))
    out_shape = [jax.ShapeDtypeStruct((nseq * seq_len, W_GRP), BF16)]
    out_specs = [pl.BlockSpec((rows, W_GRP), lambda b: (b, 0))]
    if want_final:
        out_shape.append(jax.ShapeDtypeStruct((nseq, 2, N_HEADS, dk, HEAD_V), F32))
        out_specs.append(pl.BlockSpec((nsub, 2, N_HEADS, dk, HEAD_V), lambda b: (b, 0, 0, 0, 0)))
    res = pl.pallas_call(
        functools.partial(kernel_fn, seq_len=seq_len, nsub=nsub, has_s0=state is not None,
                          want_final=want_final),
        grid=(nseq // nsub,),
        in_specs=in_specs,
        out_specs=out_specs,
        out_shape=out_shape,
        scratch_shapes=[pltpu.VMEM((2, rows, W_GRP), F32)] * extra_scratch + [
            pltpu.VMEM((2, rows, width), F32), pltpu.VMEM((2, rows, W_GRP), F32),
            pltpu.VMEM((2 * nsub, W_GRP, width), F32)],
        compiler_params=_params(("arbitrary",)),
        name=name,
    )(*args)
    return res[0], (res[1] if want_final else None)


def _seqs_per_block(nseq, seq_len):
    return max(1, min(nseq, BLOCK_ROWS // seq_len))


def _hgrn_mixer(proj, gain, state, layer, nseq, seq_len, want_final):
    rows = _seqs_per_block(nseq, seq_len) * seq_len
    col = lambda c: pl.BlockSpec((rows, W_GRP), lambda b: (b, c))
    args = [proj] * 5 + [gain.reshape(DEPTH, 1, W_GRP)]
    in_specs = [col(0), col(1), col(2), col(3), col(4),
                pl.BlockSpec((1, 1, W_GRP), lambda b: (layer, 0, 0))]
    return _recurrent_mixer(_hgrn_kernel, "hgrn2_mixer", W_GRP, args, in_specs, state, layer, nseq,
                            seq_len, want_final, 1)


def _gla_mixer(proj, gain, state, layer, nseq, seq_len, want_final):
    width = N_HEADS * GLA_DK
    rows = _seqs_per_block(nseq, seq_len) * seq_len
    col256 = lambda c: pl.BlockSpec((rows, W_GRP), lambda b: (b, c))
    col128 = lambda c: pl.BlockSpec((rows, width), lambda b: (b, c))
    args = [proj] * 6 + [gain.reshape(DEPTH, 1, W_GRP)]
    in_specs = [col128(10), col128(11), col256(6), col256(7), col128(16), col128(17),
                pl.BlockSpec((1, 1, W_GRP), lambda b: (layer, 0, 0))]
    return _recurrent_mixer(_gla_kernel, "gla_mixer", width, args, in_specs, state, layer, nseq, seq_len,
                            want_final, 0)


PAD = 8
SCAN_TILE = 8
SCAN_UNROLL = 4
GELU_C = 0.7978845608028654


def _rgsc_kernel(*refs, seq_len, nsub, seg, has_h0, want_final):
    (cx_ref, cg_ref, db_ref, dc_ref, dv_ref, convw_ref, convb_ref, wr_ref, br_ref, wi_ref,
     bi_ref, lam_ref, sw_ref) = refs[:13]
    pos = 13
    h0_ref = None
    if has_h0:
        h0_ref = refs[pos]
        pos += 1
    oc_ref, od_ref = refs[pos:pos + 2]
    pos += 2
    hfin_ref = None
    if want_final:
        hfin_ref = refs[pos]
        pos += 1
    upad, a_scr, b_scr, h_scr = refs[pos:]

    slot = seq_len + 2 * PAD
    zpad = jnp.zeros((PAD, W_GRP), F32)
    for sub in range(nsub):
        upad[sub * slot:sub * slot + PAD, :] = zpad
        upad[sub * slot + PAD + seq_len:(sub + 1) * slot, :] = zpad
        upad[sub * slot + PAD:sub * slot + PAD + seq_len, :] = cx_ref[sub * seq_len:(sub + 1) * seq_len, :]
    rows8 = lax.broadcasted_iota(jnp.int32, (SCAN_TILE, W_GRP), 0)
    ntile = seq_len // SCAN_TILE

    for d in (0, 1):
        reverse = d == 1
        w = convw_ref[0, d]
        offs = [PAD + 3, PAD + 2, PAD + 1, PAD] if reverse else [PAD - 3, PAD - 2, PAD - 1, PAD]
        sp = _softplus(-lam_ref[0, d:d + 1, :])
        for sub in range(nsub):
            for rb in range(seq_len // ROW_BLOCK):
                src = sub * slot + rb * ROW_BLOCK
                base = sub * seq_len + rb * ROW_BLOCK
                win = upad[src:src + ROW_BLOCK + 2 * PAD, :]
                xc = convb_ref[0, d:d + 1, :]
                for j in range(4):
                    shift = (PAD - offs[j]) % (ROW_BLOCK + 2 * PAD)
                    tap = win if shift == 0 else pltpu.roll(win, shift, 0)
                    xc = xc + w[j:j + 1, :] * tap[PAD:PAD + ROW_BLOCK, :]
                r = _sigmoid(_dot(xc, wr_ref[d]) + br_ref[0, d:d + 1, :])
                ig = _sigmoid(_dot(xc, wi_ref[d]) + bi_ref[0, d:d + 1, :])
                log_a = -RG_C * r * sp
                a = jnp.exp(log_a)
                a_scr[base:base + ROW_BLOCK, :] = a
                y = 1.0 - a * a
                root = jnp.where(y > 0.0, y * lax.rsqrt(y), 0.0)
                b_scr[base:base + ROW_BLOCK, :] = root * (ig * xc)

        if has_h0:
            h0 = tuple(h0_ref[sub, 0, d:d + 1, :] for sub in range(nsub))
        else:
            h0 = tuple(jnp.zeros((1, W_GRP), F32) for _ in range(nsub))

        def step(ti, carry, d=d, reverse=reverse):
            t = (ntile - 1 - ti) if reverse else ti
            out = []
            for sub in range(nsub):
                r0 = pl.multiple_of(sub * seq_len + t * SCAN_TILE, SCAN_TILE)
                a = a_scr[pl.ds(r0, SCAN_TILE), :]
                b = b_scr[pl.ds(r0, SCAN_TILE), :]
                for s in (1, 2, 4):
                    if reverse:
                        keep = rows8 <= SCAN_TILE - 1 - s
                        shift = SCAN_TILE - s
                    else:
                        keep = rows8 >= s
                        shift = s
                    a_s = jnp.where(keep, pltpu.roll(a, shift, 0), 1.0)
                    b_s = jnp.where(keep, pltpu.roll(b, shift, 0), 0.0)
                    b = b + a * b_s
                    a = a * a_s
                h = a * carry[sub] + b
                if d == 0:
                    h_scr[pl.ds(r0, SCAN_TILE), :] = h
                else:
                    h_scr[pl.ds(r0, SCAN_TILE), :] += h
                out.append(h[0:1, :] if reverse else h[SCAN_TILE - 1:SCAN_TILE, :])
            return tuple(out)

        hlast = lax.fori_loop(0, ntile, step, h0, unroll=SCAN_UNROLL // nsub if nsub < SCAN_UNROLL else 1)
        if want_final:
            for sub in range(nsub):
                hfin_ref[sub, d:d + 1, :] = hlast[sub]

    sw = sw_ref[0]
    nblock = nsub * seq_len // ROW_BLOCK
    for rb in range(nblock):
        rows = slice(rb * ROW_BLOCK, (rb + 1) * ROW_BLOCK)
        x = cg_ref[rows, :]
        gelu = x * (0.5 + 0.5 * jnp.tanh(x * (GELU_C + (GELU_C * 0.044715) * (x * x))))
        oc_ref[rows, :] = (h_scr[rows, :] * gelu).astype(oc_ref.dtype)
        upad[PAD + rb * ROW_BLOCK:PAD + (rb + 1) * ROW_BLOCK, :] = dc_ref[rows, :] * dv_ref[rows, :]
    for rb in range(nblock):
        base = rb * ROW_BLOCK
        rows = slice(base, base + ROW_BLOCK)
        posn = (lax.broadcasted_iota(jnp.int32, (ROW_BLOCK, W_GRP), 0) + base) % seg
        win = upad[base:base + ROW_BLOCK + 2 * PAD, :]
        nwin = ROW_BLOCK + 2 * PAD
        mid = slice(PAD, PAD + ROW_BLOCK)
        left = jnp.where(posn != 0, pltpu.roll(win, 1, 0)[mid, :], 0.0)
        right = jnp.where(posn != seg - 1, pltpu.roll(win, nwin - 1, 0)[mid, :], 0.0)
        y = sw[0:1, :] * left + sw[1:2, :] * win[mid, :] + sw[2:3, :] * right
        od_ref[rows, :] = (db_ref[rows, :] * y).astype(od_ref.dtype)


def _rgsc_mixer(proj, conv_w, conv_b, wr_bd, b_r, wi_bd, b_i, lam, sconv_w, h0, layer, nseq, seq_len,
                seg, want_final):
    nsub = _seqs_per_block(nseq, seq_len)
    rows = nsub * seq_len
    col = lambda c: pl.BlockSpec((rows, W_GRP), lambda b: (b, c))
    lay3 = lambda a: pl.BlockSpec((1,) + a.shape[1:], lambda b: (layer, 0, 0))
    lay4 = lambda a: pl.BlockSpec((1,) + a.shape[1:], lambda b: (layer, 0, 0, 0))
    params = [conv_w, conv_b, wr_bd, b_r, wi_bd, b_i, lam, sconv_w]
    args = [proj] * 5 + params
    in_specs = [col(9), col(10), col(11), col(12), col(13), lay4(conv_w), lay3(conv_b), _full_spec(wr_bd),
                lay3(b_r), _full_spec(wi_bd), lay3(b_i), lay3(lam), lay3(sconv_w)]
    if h0 is not None:
        args.append(h0)
        in_specs.append(pl.BlockSpec((nsub, 1, 2, W_GRP), lambda b: (b, layer, 0, 0)))
    out_shape = [jax.ShapeDtypeStruct((nseq * seq_len, W_GRP), BF16)] * 2
    out_specs = [pl.BlockSpec((rows, W_GRP), lambda b: (b, 0))] * 2
    if want_final:
        out_shape.append(jax.ShapeDtypeStruct((nseq, 2, W_GRP), F32))
        out_specs.append(pl.BlockSpec((nsub, 2, W_GRP), lambda b: (b, 0, 0)))
    res = pl.pallas_call(
        functools.partial(_rgsc_kernel, seq_len=seq_len, nsub=nsub, seg=seg, has_h0=h0 is not None,
                          want_final=want_final),
        grid=(nseq // nsub,),
        in_specs=in_specs,
        out_specs=out_specs,
        out_shape=out_shape,
        scratch_shapes=[pltpu.VMEM((nsub * (seq_len + 2 * PAD), W_GRP), F32), pltpu.VMEM((rows, W_GRP), F32),
                        pltpu.VMEM((rows, W_GRP), F32), pltpu.VMEM((rows, W_GRP), F32)],
        compiler_params=_params(("arbitrary",)),
        name="rglru_sconv_mixer",
    )(*args)
    return res[0], res[1], (res[2] if want_final else None)


def _block_diag(w):
    two, h, c, _ = w.shape
    eye = jnp.eye(h, dtype=w.dtype)
    full = w[:, :, :, None, :] * eye[None, :, None, :, None]
    return full.reshape(two, h * c, h * c)


def kernel(x_prompt, x_sample, state_hgrn, state_gla, state_rglru, c, c_ctx, norm1_g, norm2_g, ada_w, ada_b, w_in, w_out, hgrn_lb_logits, hgrn_norm_g, gla_wa2, gla_ba2, gla_norm_g, rg_conv_w, rg_conv_b, rg_w_r, rg_b_r, rg_w_i, rg_b_i, rg_lambda, sconv_w, mlp_w1, mlp_w2, final_norm_g):
    b_ctx, l_ctx, _ = x_prompt.shape
    b_lat, l_lat, _ = x_sample.shape

    cvec = jnp.concatenate([c, c_ctx[None, :], jnp.zeros((8 - b_lat - 1, D_MODEL), F32)], axis=0)
    mod = _modulation(cvec, ada_w, ada_b).reshape(DEPTH, 8, 6, D_MODEL)

    w_cat = _build_proj_weight(w_in, gla_wa2)

    xp = x_prompt.reshape(b_ctx * l_ctx, D_MODEL)
    xs = x_sample.reshape(b_lat * l_lat, D_MODEL)
    new_h, new_g, new_r = [], [], []
    for l in range(DEPTH):
        wr_bd = _block_diag(rg_w_r[l]).astype(BF16)
        wi_bd = _block_diag(rg_w_i[l]).astype(BF16)
        mod_ctx = mod[l, b_lat:b_lat + 1]
        mod_lat = mod[l, :b_lat]
        last = l == DEPTH - 1
        streams = (
            (xp, mod_ctx, b_ctx, l_ctx, l_ctx, None, None, None, True),
            (xs, mod_lat, b_lat, l_lat, GRID_W, state_hgrn, state_gla, state_rglru, False),
        )
        outs = []
        for (x, m, nseq, seq_len, seg, s_h, s_g, s_r, want_final) in streams:
            proj = _project(x, m, norm1_g, w_cat, hgrn_lb_logits, gla_ba2, l)
            o_a, f_h = _hgrn_mixer(proj, hgrn_norm_g, s_h, l, nseq, seq_len, want_final)
            o_b, f_g = _gla_mixer(proj, gla_norm_g, s_g, l, nseq, seq_len, want_final)
            o_c, o_d, f_r = _rgsc_mixer(proj, rg_conv_w, rg_conv_b, wr_bd, rg_b_r, wi_bd, rg_b_i,
                                        rg_lambda, sconv_w, s_r, l, nseq, seq_len, seg, want_final)
            x_new = _out_mlp(x, o_a, o_b, o_c, o_d, m, norm2_g, final_norm_g, w_out, mlp_w1, mlp_w2, l, last)
            outs.append((x_new, f_h, f_g, f_r))
        xp, f_h, f_g, f_r = outs[0]
        xs = outs[1][0]
        new_h.append(f_h)
        new_g.append(f_g)
        new_r.append(f_r)
    y_prompt = xp.reshape(b_ctx, l_ctx, D_MODEL)
    y_sample = xs.reshape(b_lat, l_lat, D_MODEL)
    return (y_prompt, y_sample, jnp.stack(new_h, axis=1), jnp.stack(new_g, axis=1),
            jnp.stack(new_r, axis=1))
```

```python
import functools

import numpy as np
import jax
import jax.numpy as jnp
from jax import lax
from jax.experimental import pallas as pl
from jax.experimental.pallas import tpu as pltpu

F32 = jnp.float32
BF16 = jnp.bfloat16

D_MODEL = 1024
DEPTH = 2
GRID_W = 64
N_HEADS = 4
W_GRP = 256
HEAD_V = 64
HGRN_DK = 64
GLA_DK = 32
GLA_RANK = 16
GLA_NORMALIZER = 16.0
RG_C = 8.0
D_FF = 4 * D_MODEL
EPS = 1e-6
F_FLOOR = 1e-20

PROJ_COLS = 3584

CHUNK = 64
HALF = CHUNK // 2
assert CHUNK == HEAD_V
GROUP = 4
GROWS = GROUP * CHUNK
MAIN_GROUP = 16
BLOCK_ROWS = 1024
LANES = 128
SAFE_RANGE = 80.0
TOK_TILE = 1024
PROJ_TM = 512
PROJ_TN = 512
FF_TILE = 1024
MLP_ROW_SPLIT = 2
PROLOGUE_ROWS = 256
ROW_BLOCK = 256
VMEM_LIMIT = 56 * 1024 * 1024

_NT = (((1,), (1,)), ((), ()))
_TN = (((0,), (0,)), ((), ()))


def _dot(a, b):
    return jnp.dot(a.astype(BF16), b.astype(BF16), preferred_element_type=F32)


def _dot_nt(a, b):
    return lax.dot_general(a.astype(BF16), b.astype(BF16), _NT, preferred_element_type=F32)


def _dot_tn(a, b):
    return lax.dot_general(a.astype(BF16), b.astype(BF16), _TN, preferred_element_type=F32)


def _sigmoid(x):
    return 1.0 / (1.0 + jnp.exp(-x))


def _silu(x):
    return x * _sigmoid(x)


def _log1p(y):
    u = 1.0 + y
    return jnp.where(u == 1.0, y, jnp.log(u) * (y / (u - 1.0)))


def _softplus(x):
    return jnp.maximum(x, 0.0) + _log1p(jnp.exp(-jnp.abs(x)))


def _log_sigmoid(z):
    return jnp.minimum(z, 0.0) - jnp.log(1.0 + jnp.exp(-jnp.abs(z)))


def _rms(x):
    return x * lax.rsqrt(jnp.mean(x * x, axis=-1, keepdims=True) + EPS)


def _params(sem):
    return pltpu.CompilerParams(dimension_semantics=sem, vmem_limit_bytes=VMEM_LIMIT)


def _full_spec(a):
    zeros = (0,) * a.ndim
    return pl.BlockSpec(a.shape, lambda *_: zeros)


def _mod_kernel(c_ref, w_ref, b_ref, o_ref):
    s = _silu(c_ref[...])
    o_ref[0] = _dot(s, w_ref[0]) + b_ref[0]


def _modulation(cvec, ada_w, ada_b):
    tn = 2048
    return pl.pallas_call(
        _mod_kernel,
        grid=(DEPTH, 6 * D_MODEL // tn),
        in_specs=[pl.BlockSpec((8, D_MODEL), lambda l, j: (0, 0)),
                  pl.BlockSpec((1, D_MODEL, tn), lambda l, j: (l, 0, j)),
                  pl.BlockSpec((1, 1, tn), lambda l, j: (l, 0, j))],
        out_specs=pl.BlockSpec((1, 8, tn), lambda l, j: (l, 0, j)),
        out_shape=jax.ShapeDtypeStruct((DEPTH, 8, 6 * D_MODEL), F32),
        compiler_params=_params(("arbitrary", "arbitrary")),
        name="adaln_modulation",
    )(cvec, ada_w, ada_b.reshape(DEPTH, 1, 6 * D_MODEL))


LOWRANK_COL = 2048
TAIL_COL = LOWRANK_COL + 2 * GLA_RANK
TAIL_WIDTH = 1280
WCAT_LANES = 256


def _wcat_kernel(w_ref, wa2t_ref, o_ref):
    o_ref[0, 0:LOWRANK_COL, :] = w_ref[0, 0:LOWRANK_COL, :].astype(BF16)
    z = jnp.dot(wa2t_ref[0], w_ref[0, LOWRANK_COL:TAIL_COL, :], preferred_element_type=F32,
                precision=lax.Precision.HIGHEST)
    o_ref[0, LOWRANK_COL:LOWRANK_COL + 2 * LANES, :] = z.astype(BF16)
    o_ref[0, LOWRANK_COL + 2 * LANES:PROJ_COLS, :] = w_ref[0, TAIL_COL:TAIL_COL + TAIL_WIDTH, :].astype(BF16)


def _build_proj_weight(w_in, wa2):
    ncol = w_in.shape[2]
    assert ncol == TAIL_COL + TAIL_WIDTH and LOWRANK_COL + 2 * LANES + TAIL_WIDTH == PROJ_COLS
    n = N_HEADS * GLA_DK
    wa2t = jnp.zeros((DEPTH, 2 * n, 2 * GLA_RANK), F32)
    wa2t = wa2t.at[:, 0:n, 0:GLA_RANK].set(jnp.swapaxes(wa2[:, 0], 1, 2))
    wa2t = wa2t.at[:, n:2 * n, GLA_RANK:2 * GLA_RANK].set(jnp.swapaxes(wa2[:, 1], 1, 2))
    w_t = jnp.swapaxes(w_in, 1, 2)
    return pl.pallas_call(
        _wcat_kernel,
        grid=(DEPTH, D_MODEL // WCAT_LANES),
        in_specs=[pl.BlockSpec((1, ncol, WCAT_LANES), lambda l, i: (l, 0, i)),
                  pl.BlockSpec((1, 2 * n, 2 * GLA_RANK), lambda l, i: (l, 0, 0))],
        out_specs=pl.BlockSpec((1, PROJ_COLS, WCAT_LANES), lambda l, i: (l, 0, i)),
        out_shape=jax.ShapeDtypeStruct((DEPTH, PROJ_COLS, D_MODEL), BF16),
        compiler_params=_params(("arbitrary", "arbitrary")),
        name="build_proj_weight",
    )(w_t, wa2t)


HGRN_GATE_COL = 2 * W_GRP
GLA_GATE_COL = LOWRANK_COL
assert HGRN_GATE_COL % PROJ_TN == 0 and GLA_GATE_COL % PROJ_TN == 0 and PROJ_TN == 2 * W_GRP


def _hgrn_lower_bound(logits, layer):
    mx = logits[0]
    for i in range(1, DEPTH):
        mx = jnp.maximum(mx, logits[i])
    ex = [jnp.exp(logits[i] - mx) for i in range(DEPTH)]
    den = ex[0]
    for i in range(1, DEPTH):
        den = den + ex[i]
    sm = [e / den for e in ex]
    csum = sm[0]
    for i in range(1, layer + 1):
        csum = csum + sm[i]
    return csum - sm[0]


def _proj_kernel(xp_ref, xs_ref, mod_ref, g_ref, w_ref, lbl_ref, ba_ref, o_ref, *, layer, n_ctx):
    shift = mod_ref[0, 0, 0:1, :]
    scale = mod_ref[0, 0, 1:2, :]
    x = jnp.where(pl.program_id(0) < n_ctx, xp_ref[...], xs_ref[...])
    h = (_rms(x) * g_ref[0] * (1.0 + scale) + shift).astype(BF16)
    lb = _hgrn_lower_bound(lbl_ref[...], layer)
    lb_row = jnp.concatenate([lb[0:1, :], lb[1:2, :]], axis=1)
    ba_row = jnp.concatenate([ba_ref[0, 0:1, :], ba_ref[0, 1:2, :]], axis=1)
    for n in range(PROJ_COLS // PROJ_TN):
        cols = slice(n * PROJ_TN, (n + 1) * PROJ_TN)
        acc = lax.dot_general(h, w_ref[0, cols, :], _NT, preferred_element_type=F32)
        if n * PROJ_TN == HGRN_GATE_COL:
            acc = lb_row + (1.0 - lb_row) * _sigmoid(acc)
        if n * PROJ_TN == GLA_GATE_COL:
            z = acc[:, 0:W_GRP] + ba_row
            acc = jnp.concatenate([_log_sigmoid(z) * (1.0 / GLA_NORMALIZER), acc[:, W_GRP:]], axis=1)
        o_ref[:, cols] = acc


def _mod_index(ntok, nmod, tile):
    tiles_per_mod = (ntok // tile) // nmod if nmod > 1 else 1
    if nmod > 1:
        return lambda i, *_: (i // tiles_per_mod, 0, 0)
    return lambda i, *_: (0, 0, 0)


def _project(xp, xs, mod, ctx_row, lat_len, norm_g, w_cat, lb_logits, ba2, layer):
    n_ctx = xp.shape[0] // PROJ_TM
    n_lat = xs.shape[0] // PROJ_TM
    tiles_per_seq = lat_len // PROJ_TM
    ntok = xp.shape[0] + xs.shape[0]
    mod_row = lambda i: jnp.where(i < n_ctx, ctx_row, jnp.maximum(i - n_ctx, 0) // tiles_per_seq)
    return pl.pallas_call(
        functools.partial(_proj_kernel, layer=layer, n_ctx=n_ctx),
        grid=(n_ctx + n_lat,),
        in_specs=[pl.BlockSpec((PROJ_TM, D_MODEL), lambda i: (jnp.minimum(i, n_ctx - 1), 0)),
                  pl.BlockSpec((PROJ_TM, D_MODEL), lambda i: (jnp.maximum(i - n_ctx, 0), 0)),
                  pl.BlockSpec((1, 1, 6, D_MODEL), lambda i: (layer, mod_row(i), 0, 0)),
                  pl.BlockSpec((1, 1, D_MODEL), lambda i: (layer, 0, 0)),
                  pl.BlockSpec((1, PROJ_COLS, D_MODEL), lambda i: (layer, 0, 0),
                               pipeline_mode=pl.Buffered(1)),
                  _full_spec(lb_logits),
                  pl.BlockSpec((1, 2, N_HEADS * GLA_DK), lambda i: (layer, 0, 0))],
        out_specs=pl.BlockSpec((PROJ_TM, PROJ_COLS), lambda i: (i, 0)),
        out_shape=jax.ShapeDtypeStruct((ntok, PROJ_COLS), F32),
        compiler_params=_params(("arbitrary",)),
        name="norm_in_proj",
    )(xp, xs, mod, norm_g.reshape(DEPTH, 1, D_MODEL), w_cat, lb_logits, ba2)


def _mlp_kernel(x_ref, oa_ref, ob_ref, oc_ref, od_ref, mod_ref, g2_ref, fg_ref, wout_ref,
                w1_ref, w2_ref, out_ref, h2_scr, *, final_norm):
    j = pl.program_id(1)

    @pl.when(j == 0)
    def _():
        wout = wout_ref[0].astype(BF16)
        nslab = TOK_TILE // PROLOGUE_ROWS
        for s in range(nslab):
            rows = slice(s * PROLOGUE_ROWS, (s + 1) * PROLOGUE_ROWS)
            mixed = jnp.concatenate([oa_ref[rows, :], ob_ref[rows, :], oc_ref[rows, :], od_ref[rows, :]],
                                    axis=1)
            mix = jnp.dot(mixed, wout, preferred_element_type=F32)
            x1 = x_ref[rows, :] + mod_ref[0, 2:3, :] * mix
            out_ref[rows, :] = x1
            h2 = _rms(x1) * g2_ref[0] * (1.0 + mod_ref[0, 4:5, :]) + mod_ref[0, 3:4, :]
            h2_scr[rows, :] = h2.astype(BF16)

    w1 = w1_ref[0].astype(BF16)
    w2 = w2_ref[0].astype(BF16)
    for r in range(MLP_ROW_SPLIT):
        rows = slice(r * (TOK_TILE // MLP_ROW_SPLIT), (r + 1) * (TOK_TILE // MLP_ROW_SPLIT))
        t = jnp.dot(h2_scr[rows, :], w1, preferred_element_type=F32)
        t = jnp.square(jnp.maximum(t, 0.0)).astype(BF16)
        out_ref[rows, :] += mod_ref[0, 5:6, :] * jnp.dot(t, w2, preferred_element_type=F32)

    if final_norm:
        @pl.when(j == pl.num_programs(1) - 1)
        def _():
            out_ref[...] = _rms(out_ref[...]) * fg_ref[...]


def _out_mlp(x, o_a, o_b, o_c, o_d, mod, norm2_g, final_g, w_out, w1, w2, layer, final_norm):
    ntok = x.shape[0]
    tok = lambda i, j: (i, 0)
    return pl.pallas_call(
        functools.partial(_mlp_kernel, final_norm=final_norm),
        grid=(ntok // TOK_TILE, D_FF // FF_TILE),
        in_specs=[pl.BlockSpec((TOK_TILE, D_MODEL), tok),
                  pl.BlockSpec((TOK_TILE, W_GRP), tok),
                  pl.BlockSpec((TOK_TILE, W_GRP), tok),
                  pl.BlockSpec((TOK_TILE, W_GRP), tok),
                  pl.BlockSpec((TOK_TILE, W_GRP), tok),
                  pl.BlockSpec((1, 6, D_MODEL), _mod_index(ntok, mod.shape[0], TOK_TILE)),
                  pl.BlockSpec((1, 1, D_MODEL), lambda i, j: (layer, 0, 0)),
                  pl.BlockSpec((1, D_MODEL), lambda i, j: (0, 0)),
                  pl.BlockSpec((1, D_MODEL, D_MODEL), lambda i, j: (layer, 0, 0)),
                  pl.BlockSpec((1, D_MODEL, FF_TILE), lambda i, j: (layer, 0, j)),
                  pl.BlockSpec((1, FF_TILE, D_MODEL), lambda i, j: (layer, j, 0))],
        out_specs=pl.BlockSpec((TOK_TILE, D_MODEL), tok),
        out_shape=jax.ShapeDtypeStruct((ntok, D_MODEL), F32),
        scratch_shapes=[pltpu.VMEM((TOK_TILE, D_MODEL), BF16)],
        compiler_params=_params(("arbitrary", "arbitrary")),
        name="out_proj_mlp",
    )(x, o_a, o_b, o_c, o_d, mod, norm2_g.reshape(DEPTH, 1, D_MODEL), final_g.reshape(1, D_MODEL),
      w_out, w1, w2)


def _split3(x):
    hi = x.astype(BF16)
    r1 = x - hi.astype(F32)
    mid = r1.astype(BF16)
    lo = (r1 - mid.astype(F32)).astype(BF16)
    return hi, mid, lo


def _dot_exact_rhs(x, m):
    hi, mid, lo = _split3(x)
    s = jnp.dot(hi, m, preferred_element_type=F32)
    s += jnp.dot(mid, m, preferred_element_type=F32)
    s += jnp.dot(lo, m, preferred_element_type=F32)
    return s


def _block_cumsum(tri, g):
    hi = g.astype(BF16)
    lo = (g - hi.astype(F32)).astype(BF16)
    return jnp.dot(tri, hi, preferred_element_type=F32) + jnp.dot(tri, lo, preferred_element_type=F32)


def _run_prepass(pre, nblock, width):
    rmax = jnp.zeros((1, width), F32)
    if nblock % 2:
        for bi in range(nblock):
            rmax = pre(bi, rmax)
        return rmax
    if nblock == 2:
        return pre(1, pre(0, rmax))
    return lax.fori_loop(0, nblock // 2, lambda i, rm: pre(2 * i + 1, pre(2 * i, rm)), rmax)


def _half_ranges(cum, reverse):
    if reverse:
        second = -cum[HALF:HALF + 1, :]
        first = -(cum[0:1, :] - cum[HALF:HALF + 1, :])
    else:
        first = -cum[HALF - 1:HALF, :]
        second = -(cum[CHUNK - 1:CHUNK, :] - cum[HALF - 1:HALF, :])
    return jnp.maximum(first, second)


def _block_ranges(cum, reverse, rm):
    for c in range(GROUP):
        rm = jnp.maximum(rm, _half_ranges(cum[c * CHUNK:(c + 1) * CHUNK, :], reverse))
    return rm


class _Rec:
    def __init__(self, q_at, k_at, v_ref, cum_scr, st_scr, oacc, consts, width, seq_len, nsub):
        self.q_at, self.k_at, self.v_ref = q_at, k_at, v_ref
        self.cum_scr, self.st_scr, self.oacc = cum_scr, st_scr, oacc
        (self.tri_ref, self.bdw_b, self.bdw_f, self.bdv_b, self.cmask, self.bmat,
         self.tile_t) = consts
        self.width, self.seq_len, self.nsub = width, seq_len, nsub


def _chunk_step(rec, sub, d, c, fast):
    reverse = d == 1
    mid_row = HALF if reverse else HALF - 1
    last_row = 0 if reverse else CHUNK - 1
    sd = sub * 2 + d
    r0 = sub * rec.seq_len + c * CHUNK
    if not isinstance(r0, int):
        r0 = pl.multiple_of(r0, CHUNK)
    rows = pl.ds(r0, CHUNK)
    q = rec.q_at(r0)
    k = rec.k_at(d, r0)
    v_b = rec.v_ref[rows, :].astype(BF16)
    cum = rec.cum_scr[d, rows, :]
    tot = cum[last_row:last_row + 1, :]
    st = rec.st_scr[sd]
    o = _dot_nt(q * jnp.exp(cum), st)
    if fast:
        cm = cum[mid_row:mid_row + 1, :]
        qm = q * jnp.exp(cum - cm)
        km_b = (k * jnp.exp(cm - cum)).astype(BF16)
        kbd = jnp.concatenate([km_b] * N_HEADS, axis=0) * rec.bdw_b[...]
        sc = lax.dot_general(qm.astype(BF16), kbd, _NT, preferred_element_type=F32)
        a = jnp.where(rec.cmask[d] > 0.5, sc, 0.0)
        vexp = jnp.concatenate([v_b] * N_HEADS, axis=0) * rec.bdv_b[...]
        o = o + jnp.dot(a.astype(BF16), vexp, preferred_element_type=F32)
    else:
        row_id = lax.broadcasted_iota(jnp.int32, (CHUNK, rec.width), 0)

        def key_row(j, acc):
            krow = rec.k_at(d, r0 + j, 1)
            crow = rec.cum_scr[d, pl.ds(r0 + j, 1), :]
            vrow = rec.v_ref[pl.ds(r0 + j, 1), :]
            keep = (row_id <= j) if reverse else (row_id >= j)
            e = jnp.exp(jnp.minimum(cum - crow, 0.0))
            p = jnp.where(keep, q * krow * e, 0.0)
            return acc + _dot(p, rec.bmat[...]) * vrow

        o = lax.fori_loop(0, CHUNK, key_row, o)
    ke_b = (k * jnp.exp(tot - cum)).astype(BF16)
    ds = lax.dot_general(v_b, ke_b, _TN, preferred_element_type=F32)
    rec.st_scr[sd] = st * jnp.exp(tot) + ds * rec.bdw_f[...]
    rec.oacc[d, rows, :] = o


def _load_state(rec, s0_ref, sub, d):
    x = jnp.concatenate([s0_ref[sub, 0, d, h] for h in range(N_HEADS)], axis=0)
    y = _dot_exact_rhs(x, rec.tile_t[...]) * rec.bmat[...].astype(F32)
    rec.st_scr[sub * 2 + d] = y.T


def _store_state(rec, sfin_ref, sub, d):
    y = rec.st_scr[sub * 2 + d].T
    half = y[:, 0:LANES] + y[:, LANES:2 * LANES]
    x = (half + pltpu.roll(half, HEAD_V, 1))[:, 0:HEAD_V]
    dk = rec.width // N_HEADS
    for h in range(N_HEADS):
        sfin_ref[sub, d, h] = x[h * dk:(h + 1) * dk, :]


def _run_recurrence(rec, ok, s0_ref, sfin_ref):
    nsub = rec.nsub
    nchunk = rec.seq_len // CHUNK
    gsize = min(MAIN_GROUP, nchunk)
    ngroup = nchunk // gsize
    spp = max(1, min(nsub, MAIN_GROUP // gsize))
    nsp = nsub // spp
    for sub in range(nsub):
        for d in (0, 1):
            if s0_ref is None:
                rec.st_scr[sub * 2 + d] = jnp.zeros(rec.st_scr.shape[1:], F32)
            else:
                _load_state(rec, s0_ref, sub, d)

    def fast_all():
        def step(i, carry):
            sp = i // ngroup if (nsp > 1 and ngroup > 1) else (i if nsp > 1 else 0)
            gi = i % ngroup if (nsp > 1 and ngroup > 1) else (i if ngroup > 1 else 0)
            for u in range(gsize):
                s = gi * gsize + u
                for j in range(spp):
                    sub = sp * spp + j
                    _chunk_step(rec, sub, 0, s, True)
                    _chunk_step(rec, sub, 1, nchunk - 1 - s, True)
            return carry

        if nsp * ngroup == 1:
            step(0, 0)
        else:
            lax.fori_loop(0, nsp * ngroup, step, 0)

    def direct_all():
        def one(i, carry):
            sub = i // nchunk if nsub > 1 else 0
            s = i % nchunk if nsub > 1 else i
            _chunk_step(rec, sub, 0, s, False)
            _chunk_step(rec, sub, 1, nchunk - 1 - s, False)
            return carry

        lax.fori_loop(0, nsub * nchunk, one, 0)

    lax.cond(ok, fast_all, direct_all)
    if sfin_ref is not None:
        for sub in range(nsub):
            for d in (0, 1):
                _store_state(rec, sfin_ref, sub, d)


def _head_norm_gate(rec, gate_ref, gain_ref, o_ref):
    for rb in range(rec.nsub * rec.seq_len // ROW_BLOCK):
        rows = slice(rb * ROW_BLOCK, (rb + 1) * ROW_BLOCK)
        o = rec.oacc[0, rows, :] + rec.oacc[1, rows, :]
        ms = jnp.dot((o * o).astype(BF16), rec.bdv_b[...], preferred_element_type=F32) * (1.0 / HEAD_V)
        y = o * lax.rsqrt(ms + EPS) * gain_ref[0] * _silu(gate_ref[rows, :])
        o_ref[rows, :] = y.astype(o_ref.dtype)


def _split_refs(refs, n_in, has_s0, want_final):
    ins = refs[:n_in]
    pos = n_in
    s0_ref = None
    if has_s0:
        s0_ref = refs[pos]
        pos += 1
    o_ref = refs[pos]
    pos += 1
    sfin_ref = None
    if want_final:
        sfin_ref = refs[pos]
        pos += 1
    return ins, s0_ref, o_ref, sfin_ref, refs[pos:]


def _hgrn_kernel(*refs, seq_len, nsub, has_s0, want_final):
    ins, s0_ref, o_ref, sfin_ref, scr = _split_refs(refs, 13, has_s0, want_final)
    q_ref, i_ref, ff_ref, fb_ref, g_ref, gain_ref = ins[:6]
    k_scr, cum_scr, oacc, st_scr = scr
    q_at = lambda r0: q_ref[pl.ds(r0, CHUNK), :]
    k_at = lambda d, r0, n=CHUNK: k_scr[d, pl.ds(r0, n), :]
    rec = _Rec(q_at, k_at, i_ref, cum_scr, st_scr, oacc, ins[6:], W_GRP, seq_len, nsub)

    def pre(bi, rm):
        r0 = bi * GROWS if isinstance(bi, int) else pl.multiple_of(bi * GROWS, GROWS)
        rows = pl.ds(r0, GROWS)
        for d, f_ref in ((0, ff_ref), (1, fb_ref)):
            f = f_ref[rows, :]
            g = jnp.log(jnp.maximum(f, F_FLOOR))
            k_scr[d, rows, :] = 1.0 - f
            cum = _block_cumsum(rec.tri_ref[d], g)
            cum_scr[d, rows, :] = cum
            rm = _block_ranges(cum, d == 1, rm)
        return rm

    rmax = _run_prepass(pre, nsub * seq_len // GROWS, W_GRP)
    ok = jnp.max(rmax) < SAFE_RANGE

    _run_recurrence(rec, ok, s0_ref, sfin_ref)
    _head_norm_gate(rec, g_ref, gain_ref, o_ref)


def _gla_kernel(*refs, seq_len, nsub, has_s0, want_final):
    ins, s0_ref, o_ref, sfin_ref, scr = _split_refs(refs, 14, has_s0, want_final)
    q_ref, k_ref, v_ref, g_ref, gf_ref, gb_ref, gain_ref = ins[:7]
    cum_scr, oacc, st_scr = scr
    width = N_HEADS * GLA_DK
    q_at = lambda r0: q_ref[pl.ds(r0, CHUNK), :] * (GLA_DK ** -0.5)
    k_at = lambda d, r0, n=CHUNK: k_ref[pl.ds(r0, n), :]
    rec = _Rec(q_at, k_at, v_ref, cum_scr, st_scr, oacc, ins[7:], width, seq_len, nsub)

    def pre(bi, rm):
        r0 = bi * GROWS if isinstance(bi, int) else pl.multiple_of(bi * GROWS, GROWS)
        rows = pl.ds(r0, GROWS)
        for d, logdecay_ref in ((0, gf_ref), (1, gb_ref)):
            g = logdecay_ref[rows, :]
            cum = _block_cumsum(rec.tri_ref[d], g)
            cum_scr[d, rows, :] = cum
            rm = _block_ranges(cum, d == 1, rm)
        return rm

    rmax = _run_prepass(pre, nsub * seq_len // GROWS, width)
    ok = jnp.max(rmax) < SAFE_RANGE

    _run_recurrence(rec, ok, s0_ref, sfin_ref)
    _head_norm_gate(rec, g_ref, gain_ref, o_ref)


def _recurrence_constants(width):
    dk = width // N_HEADS
    r = np.arange(N_HEADS * CHUNK)[:, None]
    tri = np.kron(np.eye(GROUP, dtype=np.float32), np.tril(np.ones((CHUNK, CHUNK), np.float32)))
    tri = np.stack([tri, tri.T])
    bdw = (r // CHUNK == np.arange(width)[None, :] // dk).astype(np.float32)
    bdv = (r // CHUNK == np.arange(W_GRP)[None, :] // HEAD_V).astype(np.float32)
    t = np.arange(CHUNK)[:, None]
    s = np.arange(N_HEADS * CHUNK)[None, :] % CHUNK
    cmask = np.stack([(s <= t), (s >= t)]).astype(np.float32)
    bmat = (np.arange(width)[:, None] // dk == np.arange(W_GRP)[None, :] // HEAD_V).astype(np.float32)
    tile_t = np.tile(np.eye(HEAD_V, dtype=np.float32), (1, N_HEADS))
    return (jnp.asarray(tri, BF16), jnp.asarray(bdw, BF16), jnp.asarray(bdw), jnp.asarray(bdv, BF16),
            jnp.asarray(cmask), jnp.asarray(bmat, BF16), jnp.asarray(tile_t, BF16))


def _recurrent_mixer(kernel_fn, name, width, args, in_specs, state, layer, nseq, seq_len, want_final,
                     extra_scratch):
    dk = width // N_HEADS
    nsub = _seqs_per_block(nseq, seq_len)
    rows = nsub * seq_len
    consts = _recurrence_constants(width)
    args = list(args) + list(consts)
    in_specs = list(in_specs) + [_full_spec(c) for c in consts]
    if state is not None:
        args.append(state)
        in_specs.append(pl.BlockSpec((nsub, 1, 2, N_HEADS, dk, HEAD_V), lambda b: (b, layer, 0, 0, 0, 0)))
    out_shape = [jax.ShapeDtypeStruct((nseq * seq_len, W_GRP), BF16)]
    out_specs = [pl.BlockSpec((rows, W_GRP), lambda b: (b, 0))]
    if want_final:
        out_shape.append(jax.ShapeDtypeStruct((nseq, 2, N_HEADS, dk, HEAD_V), F32))
        out_specs.append(pl.BlockSpec((nsub, 2, N_HEADS, dk, HEAD_V), lambda b: (b, 0, 0, 0, 0)))
    res = pl.pallas_call(
        functools.partial(kernel_fn, seq_len=seq_len, nsub=nsub, has_s0=state is not None,
                          want_final=want_final),
        grid=(nseq // nsub,),
        in_specs=in_specs,
        out_specs=out_specs,
        out_shape=out_shape,
        scratch_shapes=[pltpu.VMEM((2, rows, W_GRP), F32)] * extra_scratch + [
            pltpu.VMEM((2, rows, width), F32), pltpu.VMEM((2, rows, W_GRP), F32),
            pltpu.VMEM((2 * nsub, W_GRP, width), F32)],
        compiler_params=_params(("arbitrary",)),
        name=name,
    )(*args)
    return res[0], (res[1] if want_final else None)


def _seqs_per_block(nseq, seq_len):
    return max(1, min(nseq, BLOCK_ROWS // seq_len))


def _hgrn_mixer(proj, row0, gain, state, layer, nseq, seq_len, want_final):
    rows = _seqs_per_block(nseq, seq_len) * seq_len
    blk0 = row0 // rows
    col = lambda c: pl.BlockSpec((rows, W_GRP), lambda b: (b + blk0, c))
    args = [proj] * 5 + [gain.reshape(DEPTH, 1, W_GRP)]
    in_specs = [col(0), col(1), col(2), col(3), col(4),
                pl.BlockSpec((1, 1, W_GRP), lambda b: (layer, 0, 0))]
    return _recurrent_mixer(_hgrn_kernel, "hgrn2_mixer", W_GRP, args, in_specs, state, layer, nseq,
                            seq_len, want_final, 1)


def _gla_mixer(proj, row0, gain, state, layer, nseq, seq_len, want_final):
    width = N_HEADS * GLA_DK
    rows = _seqs_per_block(nseq, seq_len) * seq_len
    blk0 = row0 // rows
    col256 = lambda c: pl.BlockSpec((rows, W_GRP), lambda b: (b + blk0, c))
    col128 = lambda c: pl.BlockSpec((rows, width), lambda b: (b + blk0, c))
    args = [proj] * 6 + [gain.reshape(DEPTH, 1, W_GRP)]
    in_specs = [col128(10), col128(11), col256(6), col256(7), col128(16), col128(17),
                pl.BlockSpec((1, 1, W_GRP), lambda b: (layer, 0, 0))]
    return _recurrent_mixer(_gla_kernel, "gla_mixer", width, args, in_specs, state, layer, nseq, seq_len,
                            want_final, 0)


PAD = 8
SCAN_TILE = 8
SCAN_UNROLL = 4
GELU_C = 0.7978845608028654


def _rgsc_kernel(*refs, seq_len, nsub, seg, has_h0, want_final):
    (cx_ref, cg_ref, db_ref, dc_ref, dv_ref, convw_ref, convb_ref, wr_ref, br_ref, wi_ref,
     bi_ref, lam_ref, sw_ref) = refs[:13]
    pos = 13
    h0_ref = None
    if has_h0:
        h0_ref = refs[pos]
        pos += 1
    oc_ref, od_ref = refs[pos:pos + 2]
    pos += 2
    hfin_ref = None
    if want_final:
        hfin_ref = refs[pos]
        pos += 1
    upad, a_scr, b_scr, h_scr = refs[pos:]

    slot = seq_len + 2 * PAD
    zpad = jnp.zeros((PAD, W_GRP), F32)
    for sub in range(nsub):
        upad[sub * slot:sub * slot + PAD, :] = zpad
        upad[sub * slot + PAD + seq_len:(sub + 1) * slot, :] = zpad
        upad[sub * slot + PAD:sub * slot + PAD + seq_len, :] = cx_ref[sub * seq_len:(sub + 1) * seq_len, :]
    rows8 = lax.broadcasted_iota(jnp.int32, (SCAN_TILE, W_GRP), 0)
    ntile = seq_len // SCAN_TILE

    for d in (0, 1):
        reverse = d == 1
        w = convw_ref[0, d]
        offs = [PAD + 3, PAD + 2, PAD + 1, PAD] if reverse else [PAD - 3, PAD - 2, PAD - 1, PAD]
        sp = _softplus(-lam_ref[0, d:d + 1, :])
        for sub in range(nsub):
            for rb in range(seq_len // ROW_BLOCK):
                src = sub * slot + rb * ROW_BLOCK
                base = sub * seq_len + rb * ROW_BLOCK
                win = upad[src:src + ROW_BLOCK + 2 * PAD, :]
                xc = convb_ref[0, d:d + 1, :]
                for j in range(4):
                    shift = (PAD - offs[j]) % (ROW_BLOCK + 2 * PAD)
                    tap = win if shift == 0 else pltpu.roll(win, shift, 0)
                    xc = xc + w[j:j + 1, :] * tap[PAD:PAD + ROW_BLOCK, :]
                r = _sigmoid(_dot(xc, wr_ref[d]) + br_ref[0, d:d + 1, :])
                ig = _sigmoid(_dot(xc, wi_ref[d]) + bi_ref[0, d:d + 1, :])
                log_a = -RG_C * r * sp
                a = jnp.exp(log_a)
                a_scr[base:base + ROW_BLOCK, :] = a
                y = 1.0 - a * a
                root = jnp.where(y > 0.0, y * lax.rsqrt(y), 0.0)
                b_scr[base:base + ROW_BLOCK, :] = root * (ig * xc)

        if has_h0:
            h0 = tuple(h0_ref[sub, 0, d:d + 1, :] for sub in range(nsub))
        else:
            h0 = tuple(jnp.zeros((1, W_GRP), F32) for _ in range(nsub))

        def step(ti, carry, d=d, reverse=reverse):
            t = (ntile - 1 - ti) if reverse else ti
            out = []
            for sub in range(nsub):
                r0 = pl.multiple_of(sub * seq_len + t * SCAN_TILE, SCAN_TILE)
                a = a_scr[pl.ds(r0, SCAN_TILE), :]
                b = b_scr[pl.ds(r0, SCAN_TILE), :]
                for s in (1, 2, 4):
                    if reverse:
                        keep = rows8 <= SCAN_TILE - 1 - s
                        shift = SCAN_TILE - s
                    else:
                        keep = rows8 >= s
                        shift = s
                    a_s = jnp.where(keep, pltpu.roll(a, shift, 0), 1.0)
                    b_s = jnp.where(keep, pltpu.roll(b, shift, 0), 0.0)
                    b = b + a * b_s
                    a = a * a_s
                h = a * carry[sub] + b
                if d == 0:
                    h_scr[pl.ds(r0, SCAN_TILE), :] = h
                else:
                    h_scr[pl.ds(r0, SCAN_TILE), :] += h
                out.append(h[0:1, :] if reverse else h[SCAN_TILE - 1:SCAN_TILE, :])
            return tuple(out)

        hlast = lax.fori_loop(0, ntile, step, h0, unroll=SCAN_UNROLL // nsub if nsub < SCAN_UNROLL else 1)
        if want_final:
            for sub in range(nsub):
                hfin_ref[sub, d:d + 1, :] = hlast[sub]

    sw = sw_ref[0]
    nblock = nsub * seq_len // ROW_BLOCK
    for rb in range(nblock):
        rows = slice(rb * ROW_BLOCK, (rb + 1) * ROW_BLOCK)
        x = cg_ref[rows, :]
        gelu = x * (0.5 + 0.5 * jnp.tanh(x * (GELU_C + (GELU_C * 0.044715) * (x * x))))
        oc_ref[rows, :] = (h_scr[rows, :] * gelu).astype(oc_ref.dtype)
        upad[PAD + rb * ROW_BLOCK:PAD + (rb + 1) * ROW_BLOCK, :] = dc_ref[rows, :] * dv_ref[rows, :]
    for rb in range(nblock):
        base = rb * ROW_BLOCK
        rows = slice(base, base + ROW_BLOCK)
        posn = (lax.broadcasted_iota(jnp.int32, (ROW_BLOCK, W_GRP), 0) + base) % seg
        win = upad[base:base + ROW_BLOCK + 2 * PAD, :]
        nwin = ROW_BLOCK + 2 * PAD
        mid = slice(PAD, PAD + ROW_BLOCK)
        left = jnp.where(posn != 0, pltpu.roll(win, 1, 0)[mid, :], 0.0)
        right = jnp.where(posn != seg - 1, pltpu.roll(win, nwin - 1, 0)[mid, :], 0.0)
        y = sw[0:1, :] * left + sw[1:2, :] * win[mid, :] + sw[2:3, :] * right
        od_ref[rows, :] = (db_ref[rows, :] * y).astype(od_ref.dtype)


def _rgsc_mixer(proj, row0, conv_w, conv_b, wr_bd, b_r, wi_bd, b_i, lam, sconv_w, h0, layer, nseq, seq_len,
                seg, want_final):
    nsub = _seqs_per_block(nseq, seq_len)
    rows = nsub * seq_len
    blk0 = row0 // rows
    col = lambda c: pl.BlockSpec((rows, W_GRP), lambda b: (b + blk0, c))
    lay3 = lambda a: pl.BlockSpec((1,) + a.shape[1:], lambda b: (layer, 0, 0))
    lay4 = lambda a: pl.BlockSpec((1,) + a.shape[1:], lambda b: (layer, 0, 0, 0))
    params = [conv_w, conv_b, wr_bd, b_r, wi_bd, b_i, lam, sconv_w]
    args = [proj] * 5 + params
    in_specs = [col(9), col(10), col(11), col(12), col(13), lay4(conv_w), lay3(conv_b), _full_spec(wr_bd),
                lay3(b_r), _full_spec(wi_bd), lay3(b_i), lay3(lam), lay3(sconv_w)]
    if h0 is not None:
        args.append(h0)
        in_specs.append(pl.BlockSpec((nsub, 1, 2, W_GRP), lambda b: (b, layer, 0, 0)))
    out_shape = [jax.ShapeDtypeStruct((nseq * seq_len, W_GRP), BF16)] * 2
    out_specs = [pl.BlockSpec((rows, W_GRP), lambda b: (b, 0))] * 2
    if want_final:
        out_shape.append(jax.ShapeDtypeStruct((nseq, 2, W_GRP), F32))
        out_specs.append(pl.BlockSpec((nsub, 2, W_GRP), lambda b: (b, 0, 0)))
    res = pl.pallas_call(
        functools.partial(_rgsc_kernel, seq_len=seq_len, nsub=nsub, seg=seg, has_h0=h0 is not None,
                          want_final=want_final),
        grid=(nseq // nsub,),
        in_specs=in_specs,
        out_specs=out_specs,
        out_shape=out_shape,
        scratch_shapes=[pltpu.VMEM((nsub * (seq_len + 2 * PAD), W_GRP), F32), pltpu.VMEM((rows, W_GRP), F32),
                        pltpu.VMEM((rows, W_GRP), F32), pltpu.VMEM((rows, W_GRP), F32)],
        compiler_params=_params(("arbitrary",)),
        name="rglru_sconv_mixer",
    )(*args)
    return res[0], res[1], (res[2] if want_final else None)


def _block_diag(w):
    two, h, c, _ = w.shape
    eye = jnp.eye(h, dtype=w.dtype)
    full = w[:, :, :, None, :] * eye[None, :, None, :, None]
    return full.reshape(two, h * c, h * c)


def kernel(x_prompt, x_sample, state_hgrn, state_gla, state_rglru, c, c_ctx, norm1_g, norm2_g, ada_w, ada_b, w_in, w_out, hgrn_lb_logits, hgrn_norm_g, gla_wa2, gla_ba2, gla_norm_g, rg_conv_w, rg_conv_b, rg_w_r, rg_b_r, rg_w_i, rg_b_i, rg_lambda, sconv_w, mlp_w1, mlp_w2, final_norm_g):
    b_ctx, l_ctx, _ = x_prompt.shape
    b_lat, l_lat, _ = x_sample.shape

    cvec = jnp.concatenate([c, c_ctx[None, :], jnp.zeros((8 - b_lat - 1, D_MODEL), F32)], axis=0)
    mod = _modulation(cvec, ada_w, ada_b).reshape(DEPTH, 8, 6, D_MODEL)

    w_cat = _build_proj_weight(w_in, gla_wa2)

    xp = x_prompt.reshape(b_ctx * l_ctx, D_MODEL)
    xs = x_sample.reshape(b_lat * l_lat, D_MODEL)
    new_h, new_g, new_r = [], [], []
    for l in range(DEPTH):
        wr_bd = _block_diag(rg_w_r[l]).astype(BF16)
        wi_bd = _block_diag(rg_w_i[l]).astype(BF16)
        mod_ctx = mod[l, b_lat:b_lat + 1]
        mod_lat = mod[l, :b_lat]
        last = l == DEPTH - 1
        streams = (
            (xp, 0, mod_ctx, b_ctx, l_ctx, l_ctx, None, None, None, True),
            (xs, xp.shape[0], mod_lat, b_lat, l_lat, GRID_W, state_hgrn, state_gla, state_rglru, False),
        )
        proj = _project(xp, xs, mod, b_lat, l_lat, norm1_g, w_cat, hgrn_lb_logits, gla_ba2, l)
        outs = []
        for (x, row0, m, nseq, seq_len, seg, s_h, s_g, s_r, want_final) in streams:
            o_a, f_h = _hgrn_mixer(proj, row0, hgrn_norm_g, s_h, l, nseq, seq_len, want_final)
            o_b, f_g = _gla_mixer(proj, row0, gla_norm_g, s_g, l, nseq, seq_len, want_final)
            o_c, o_d, f_r = _rgsc_mixer(proj, row0, rg_conv_w, rg_conv_b, wr_bd, rg_b_r, wi_bd, rg_b_i,
                                        rg_lambda, sconv_w, s_r, l, nseq, seq_len, seg, want_final)
            x_new = _out_mlp(x, o_a, o_b, o_c, o_d, m, norm2_g, final_norm_g, w_out, mlp_w1, mlp_w2, l, last)
            outs.append((x_new, f_h, f_g, f_r))
        xp, f_h, f_g, f_r = outs[0]
        xs = outs[1][0]
        new_h.append(f_h)
        new_g.append(f_g)
        new_r.append(f_r)
    y_prompt = xp.reshape(b_ctx, l_ctx, D_MODEL)
    y_sample = xs.reshape(b_lat, l_lat, D_MODEL)
    return (y_prompt, y_sample, jnp.stack(new_h, axis=1), jnp.stack(new_g, axis=1),
            jnp.stack(new_r, axis=1))
```
